```python
import math
import jax, jax.numpy as jnp
from jax import lax
import numpy as np

D_MODEL = 1024
BATCH = 32
SEQ = 256
DEPTH = 4
DEC_BATCH = 2
DEC_SEQ = 2048
PAST_LEN = 256

GRID_W = 64
N_RET_LAYERS = (DEPTH + 1) // 2
N_ATT_LAYERS = DEPTH // 2
HY_W = D_MODEL // 2
HY_EMB = 33
HY_BANDS = (HY_EMB - 1) // 2
HY_FF = 64
HY_TARGET = 1e-2
HY_FAST = 0.3
HY_SLOW = 1.5
RET_W = D_MODEL // 2
RET_HEADS = 4
RET_HEAD_DIM = RET_W // RET_HEADS
RET_CHUNK = 128
DIFF_HEADS = 8
DIFF_HEAD_DIM = D_MODEL // (2 * DIFF_HEADS)
Q_BLOCK = 128
ROPE_BASE = 10000.0
N_EXPERTS = 16
EC_FACTOR = 2
EXPERT_FF = 1024
EVEN_IN = 3 * HY_W + 4 * RET_W
EPS = 1e-6

kernel_name = "hyena_retnet_diffattn_ec_diffusion_step"


def rms_norm(x, g):
    xf = x.astype(jnp.float32)
    y = xf * lax.rsqrt(jnp.mean(xf * xf, axis=-1, keepdims=True) + EPS)
    return (y * g).astype(x.dtype)


def ada_modulation(cond, w, b):
    m = jax.nn.silu(cond) @ w + b
    m = m.reshape(cond.shape[:-1] + (1, 6, D_MODEL))
    return [m[..., i, :] for i in range(6)]


def short_conv3(x, w, b):
    xp = jnp.pad(x, ((0, 0), (1, 1), (0, 0)))
    return xp[:, :-2] * w[0] + xp[:, 1:-1] * w[1] + xp[:, 2:] * w[2] + b


def hyena_filters(L, f1w, f1b, fr1, f2w, f2b, fr2, f3w):
    pos = jnp.arange(L, dtype=jnp.float32)
    t = pos / (L - 1)
    w = 2.0 * math.pi * pos / L
    f = jnp.linspace(1e-4, HY_BANDS - 1, HY_BANDS, dtype=jnp.float32)
    wf = w[:, None] * f[None, :]
    feat = jnp.concatenate([t[:, None], jnp.cos(wf), -jnp.sin(wf)], axis=-1)
    h = jnp.sin(fr1 * (feat @ f1w + f1b))
    h = jnp.sin(fr2 * (h @ f2w + f2b))
    h = (h @ f3w).astype(jnp.float32).reshape(L, 2, 2, HY_W)
    deltas = jnp.linspace(math.log(HY_TARGET) / HY_SLOW, math.log(HY_TARGET) / HY_FAST, HY_W, dtype=jnp.float32)
    window = jnp.exp(-t[:, None] * jnp.abs(deltas)[None, :])
    h = h * window[:, None, None, :]
    h = h / (jnp.sum(jnp.abs(h), axis=(0, 2), keepdims=True) + EPS)
    fwd, bwd = h[:, :, 0], h[:, :, 1]
    kern = jnp.concatenate([fwd, jnp.zeros((1, 2, HY_W), jnp.float32), bwd[1:][::-1]], axis=0)
    return jnp.transpose(kern, (1, 0, 2))


def long_conv(z, kern, bias):
    L = z.shape[1]
    zf = z.astype(jnp.float32)
    Z = jnp.fft.rfft(zf, n=2 * L, axis=1)
    K = jnp.fft.rfft(kern, n=2 * L, axis=0)
    y = jnp.fft.irfft(Z * K[None], n=2 * L, axis=1)[:, :L]
    return (y + zf * bias).astype(z.dtype)


def hyena(u, short_w, short_b, f1w, f1b, fr1, f2w, f2b, fr2, f3w, hbias):
    L = u.shape[1]
    u = short_conv3(u, short_w, short_b)
    v, x1, x2 = jnp.split(u, 3, axis=-1)
    kern = hyena_filters(L, f1w, f1b, fr1, f2w, f2b, fr2, f3w)
    z = long_conv(v, kern[0], hbias[0]) * x1
    z = long_conv(z, kern[1], hbias[1]) * x2
    return z


def retention_chunked(q, k, v, log_g, s0):
    B, L, H, _ = q.shape
    nc = L // RET_CHUNK

    def to_chunks(t):
        return t.reshape(B, nc, RET_CHUNK, H, t.shape[-1]).transpose(1, 0, 3, 2, 4)

    qc, kc, vc = to_chunks(q), to_chunks(k), to_chunks(v)
    i = jnp.arange(RET_CHUNK, dtype=jnp.float32)
    diff = i[:, None] - i[None, :]
    decay_mask = jnp.where(diff[None] >= 0, jnp.exp(log_g[:, None, None] * jnp.maximum(diff, 0.0)[None]), 0.0)
    q_decay = jnp.exp(log_g[:, None] * (i + 1.0))[..., None]
    k_decay = jnp.exp(log_g[:, None] * (RET_CHUNK - 1.0 - i))[..., None]
    chunk_decay = jnp.exp(log_g * RET_CHUNK)[:, None, None]

    def step(s, qkv):
        qi, ki, vi = qkv
        scores = jnp.einsum("bhid,bhjd->bhij", qi, ki) * decay_mask
        inner = jnp.einsum("bhij,bhjv->bhiv", scores, vi)
        cross = jnp.einsum("bhid,bhdv->bhiv", qi * q_decay, s)
        s_new = s * chunk_decay + jnp.einsum("bhjd,bhjv->bhdv", ki * k_decay, vi)
        return s_new, inner + cross

    s_fin, out = lax.scan(step, s0, (qc, kc, vc))
    out = out.transpose(1, 0, 3, 2, 4).reshape(B, L, H, -1)
    return out, s_fin


def head_group_norm(o):
    mu = jnp.mean(o, axis=-1, keepdims=True)
    var = jnp.mean(jnp.square(o - mu), axis=-1, keepdims=True)
    return (o - mu) * lax.rsqrt(var + EPS)


def even_mixer(h, s0, w_in, short_w, short_b, f1w, f1b, fr1, f2w, f2b, fr2, f3w, hbias, decay_logit, w_out):
    B, L, _ = h.shape
    proj = h @ w_in
    hy_in, q, k, v, g = jnp.split(proj, [3 * HY_W, 3 * HY_W + RET_W, 3 * HY_W + 2 * RET_W, 3 * HY_W + 3 * RET_W], axis=-1)
    y_hy = hyena(hy_in, short_w, short_b, f1w, f1b, fr1, f2w, f2b, fr2, f3w, hbias)

    def heads(t):
        return t.reshape(B, L, RET_HEADS, RET_HEAD_DIM).astype(jnp.float32)

    q, k, v = heads(q), heads(k) * (RET_HEAD_DIM ** -0.5), heads(v)
    log_g = jax.nn.log_sigmoid(decay_logit.astype(jnp.float32))
    o_f, s_f = retention_chunked(q, k, v, log_g[0], s0[:, 0])
    o_b, s_b = retention_chunked(q[:, ::-1], k[:, ::-1], v[:, ::-1], log_g[1], s0[:, 1])
    o = head_group_norm(o_f + o_b[:, ::-1])
    y_ret = (jax.nn.silu(g.astype(jnp.float32)) * o.reshape(B, L, RET_W)).astype(h.dtype)
    y = jnp.concatenate([y_hy, y_ret], axis=-1) @ w_out
    return y, jnp.stack([s_f, s_b], axis=1)


def axial_rope(x):
    L = x.shape[1]
    rows = L // GRID_W
    row = jnp.repeat(jnp.arange(rows, dtype=jnp.float32), GRID_W)
    col = jnp.tile(jnp.arange(GRID_W, dtype=jnp.float32), rows)
    half = x.shape[-1] // 2
    quarter = half // 2
    freqs = ROPE_BASE ** (-jnp.arange(quarter, dtype=jnp.float32) / quarter)

    def rot(xh, p):
        ang = p[:, None] * freqs[None, :]
        cos = jnp.cos(ang)[None, :, None, None, :]
        sin = jnp.sin(ang)[None, :, None, None, :]
        a, b = xh[..., :quarter].astype(jnp.float32), xh[..., quarter:].astype(jnp.float32)
        return jnp.concatenate([a * cos - b * sin, a * sin + b * cos], axis=-1)

    return jnp.concatenate([rot(x[..., :half], row), rot(x[..., half:], col)], axis=-1).astype(x.dtype)


def diff_project(h, w_in):
    B, L, _ = h.shape
    q, k, v = jnp.split(h @ w_in, 3, axis=-1)
    return (q.reshape(B, L, DIFF_HEADS, 2, DIFF_HEAD_DIM),
            k.reshape(B, L, DIFF_HEADS, 2, DIFF_HEAD_DIM),
            v.reshape(B, L, DIFF_HEADS, 2 * DIFF_HEAD_DIM))


def diff_attention(q, k, v, lam):
    B, Lq, H, _, d = q.shape
    nb = Lq // Q_BLOCK
    qb = q.reshape(B, nb, Q_BLOCK, H, 2, d).transpose(1, 0, 2, 3, 4, 5)
    scale = d ** -0.5

    def block(qi):
        s = jnp.einsum("bqhmd,bkhmd->bhmqk", qi, k).astype(jnp.float32) * scale
        p = jax.nn.softmax(s, axis=-1)
        a = p[:, :, 0] - lam * p[:, :, 1]
        return jnp.einsum("bhqk,bkhe->bqhe", a.astype(v.dtype), v)

    o = lax.map(block, qb)
    return o.transpose(1, 0, 2, 3, 4).reshape(B, Lq, H, 2 * d)


def diff_lambda(lq1, lk1, lq2, lk2, lam_init):
    return (jnp.exp(jnp.sum(lq1 * lk1).astype(jnp.float32))
            - jnp.exp(jnp.sum(lq2 * lk2).astype(jnp.float32)) + lam_init)


def diff_output(o, subln_g, lam_init, w_out):
    B, L = o.shape[:2]
    o = rms_norm(o, subln_g) * (1.0 - lam_init)
    return o.reshape(B, L, D_MODEL).astype(w_out.dtype) @ w_out


def odd_context(h, w_in, lam, lam_init, subln_g, w_out):
    B, L, _ = h.shape
    q, k, v = diff_project(h, w_in)
    o = diff_attention(q, k, v, lam)
    return diff_output(o, subln_g, lam_init, w_out), k.reshape(B, L, DIFF_HEADS, 2 * DIFF_HEAD_DIM), v


def odd_latent(h, ck, cv, w_in, lam, lam_init, subln_g, w_out):
    q, k, v = diff_project(h, w_in)
    q, k = axial_rope(q), axial_rope(k)
    B, Lc = ck.shape[:2]
    keys = jnp.concatenate([ck.reshape(B, Lc, DIFF_HEADS, 2, DIFF_HEAD_DIM).astype(k.dtype), k], axis=1)
    vals = jnp.concatenate([cv.astype(v.dtype), v], axis=1)
    o = diff_attention(q, keys, vals, lam)
    return diff_output(o, subln_g, lam_init, w_out)


def expert_choice_ffn(h, w_router, wg, wu, wd):
    N, D = h.shape
    cap = EC_FACTOR * N // N_EXPERTS
    aff = jax.nn.softmax((h @ w_router).astype(jnp.float32), axis=-1)
    gate, idx = lax.top_k(aff.T, cap)
    xs = h[idx]
    hid = jax.nn.silu(jnp.einsum("ecd,edf->ecf", xs, wg)) * jnp.einsum("ecd,edf->ecf", xs, wu)
    out = jnp.einsum("ecf,efd->ecd", hid, wd) * gate[..., None].astype(h.dtype)
    return jnp.zeros_like(h).at[idx.reshape(-1)].add(out.reshape(-1, D))


def setup_inputs(seed: int = 0) -> dict:
    key = jax.random.key(seed)
    ks = iter(jax.random.split(key, 48))

    def nrm(shape, scale):
        return jax.random.normal(next(ks), shape, jnp.float32) * scale

    D, R, A = D_MODEL, N_RET_LAYERS, N_ATT_LAYERS
    base_logit = jnp.asarray(np.log(2.0 ** (5 + np.arange(RET_HEADS)) - 1.0), jnp.float32)
    return {
        "x_prompt": nrm((BATCH, SEQ, D), 1.0),
        "x_sample": nrm((DEC_BATCH, DEC_SEQ, D), 1.0),
        "state_ret": nrm((DEC_BATCH, R, 2, RET_HEADS, RET_HEAD_DIM, RET_HEAD_DIM), 1.0),
        "cache_k": nrm((DEC_BATCH, A, PAST_LEN, DIFF_HEADS, 2 * DIFF_HEAD_DIM), 1.0),
        "cache_v": nrm((DEC_BATCH, A, PAST_LEN, DIFF_HEADS, 2 * DIFF_HEAD_DIM), 1.0),
        "c": nrm((DEC_BATCH, D), 1.0),
        "c_ctx": nrm((D,), 1.0),
        "w_mod": nrm((DEPTH, D, 6 * D), D ** -0.5),
        "b_mod": nrm((DEPTH, 6 * D), 0.01),
        "norm1_g": 1.0 + nrm((DEPTH, D), 0.05),
        "norm2_g": 1.0 + nrm((DEPTH, D), 0.05),
        "w_in_even": nrm((R, D, EVEN_IN), D ** -0.5),
        "hy_short_w": nrm((R, 3, 3 * HY_W), 3 ** -0.5),
        "hy_short_b": nrm((R, 3 * HY_W), 0.01),
        "hy_f1_w": nrm((R, HY_EMB, HY_FF), HY_EMB ** -0.5),
        "hy_f1_b": nrm((R, HY_FF), 0.01),
        "hy_freq1": 1.0 + nrm((R, HY_FF), 0.1),
        "hy_f2_w": nrm((R, HY_FF, HY_FF), HY_FF ** -0.5),
        "hy_f2_b": nrm((R, HY_FF), 0.01),
        "hy_freq2": 1.0 + nrm((R, HY_FF), 0.1),
        "hy_f3_w": nrm((R, HY_FF, 4 * HY_W), HY_FF ** -0.5),
        "hy_bias": nrm((R, 2, HY_W), 0.1),
        "ret_decay": base_logit + nrm((R, 2, RET_HEADS), 0.1),
        "w_out_even": nrm((R, D, D), D ** -0.5),
        "w_in_odd": nrm((A, D, 3 * D), D ** -0.5),
        "lam_q1": nrm((A, DIFF_HEAD_DIM), 0.1),
        "lam_k1": nrm((A, DIFF_HEAD_DIM), 0.1),
        "lam_q2": nrm((A, DIFF_HEAD_DIM), 0.1),
        "lam_k2": nrm((A, DIFF_HEAD_DIM), 0.1),
        "subln_g": 1.0 + nrm((A, 2 * DIFF_HEAD_DIM), 0.05),
        "w_out_odd": nrm((A, D, D), D ** -0.5),
        "moe_router": nrm((DEPTH, D, N_EXPERTS), D ** -0.5),
        "moe_wg": nrm((DEPTH, N_EXPERTS, D, EXPERT_FF), D ** -0.5),
        "moe_wu": nrm((DEPTH, N_EXPERTS, D, EXPERT_FF), D ** -0.5),
        "moe_wd": nrm((DEPTH, N_EXPERTS, EXPERT_FF, D), EXPERT_FF ** -0.5),
        "final_g": 1.0 + nrm((D,), 0.05),
    }


def reference(x_prompt, x_sample, state_ret, cache_k, cache_v, c, c_ctx, w_mod, b_mod, norm1_g, norm2_g,
              w_in_even, hy_short_w, hy_short_b, hy_f1_w, hy_f1_b, hy_freq1, hy_f2_w, hy_f2_b, hy_freq2,
              hy_f3_w, hy_bias, ret_decay, w_out_even, w_in_odd, lam_q1, lam_k1, lam_q2, lam_k2, subln_g,
              w_out_odd, moe_router, moe_wg, moe_wu, moe_wd, final_g):
    xp, xs = x_prompt, x_sample
    Bp, Lp, _ = xp.shape
    Bs, Ls, _ = xs.shape
    new_ret, new_k, new_v = [], [], []
    for l in range(DEPTH):
        sh1p, sc1p, g1p, sh2p, sc2p, g2p = ada_modulation(c_ctx, w_mod[l], b_mod[l])
        sh1s, sc1s, g1s, sh2s, sc2s, g2s = ada_modulation(c, w_mod[l], b_mod[l])
        hp = rms_norm(xp, norm1_g[l]) * (1.0 + sc1p) + sh1p
        hs = rms_norm(xs, norm1_g[l]) * (1.0 + sc1s) + sh1s
        i = l // 2
        if l % 2 == 0:
            params = (w_in_even[i], hy_short_w[i], hy_short_b[i], hy_f1_w[i], hy_f1_b[i], hy_freq1[i],
                      hy_f2_w[i], hy_f2_b[i], hy_freq2[i], hy_f3_w[i], hy_bias[i], ret_decay[i], w_out_even[i])
            s0 = jnp.zeros((Bp, 2, RET_HEADS, RET_HEAD_DIM, RET_HEAD_DIM), jnp.float32)
            yp, st = even_mixer(hp, s0, *params)
            ys, _ = even_mixer(hs, state_ret[:, i].astype(jnp.float32), *params)
            new_ret.append(st)
        else:
            lam_init = 0.8 - 0.6 * math.exp(-0.3 * l)
            lam = diff_lambda(lam_q1[i], lam_k1[i], lam_q2[i], lam_k2[i], lam_init)
            yp, kc, vc = odd_context(hp, w_in_odd[i], lam, lam_init, subln_g[i], w_out_odd[i])
            ys = odd_latent(hs, cache_k[:, i], cache_v[:, i], w_in_odd[i], lam, lam_init, subln_g[i], w_out_odd[i])
            new_k.append(kc)
            new_v.append(vc)
        xp = xp + g1p * yp
        xs = xs + g1s * ys
        hp = rms_norm(xp, norm2_g[l]) * (1.0 + sc2p) + sh2p
        hs = rms_norm(xs, norm2_g[l]) * (1.0 + sc2s) + sh2s
        fp = expert_choice_ffn(hp.reshape(Bp * Lp, D_MODEL), moe_router[l], moe_wg[l], moe_wu[l], moe_wd[l])
        fs = expert_choice_ffn(hs.reshape(Bs * Ls, D_MODEL), moe_router[l], moe_wg[l], moe_wu[l], moe_wd[l])
        xp = xp + g2p * fp.reshape(Bp, Lp, D_MODEL)
        xs = xs + g2s * fs.reshape(Bs, Ls, D_MODEL)
    y_prompt = rms_norm(xp, final_g)
    y_sample = rms_norm(xs, final_g)
    new_state_ret = jnp.stack(new_ret, axis=1)
    new_cache_k = jnp.stack(new_k, axis=1)
    new_cache_v = jnp.stack(new_v, axis=1)
    return (y_prompt, y_sample, new_state_ret, new_cache_k, new_cache_v)
```

```python
import functools
import math

import numpy as np
import jax
import jax.numpy as jnp
from jax import lax
from jax.experimental import pallas as pl
from jax.experimental.pallas import tpu as pltpu

F32 = jnp.float32
BF16 = jnp.bfloat16

D_MODEL = 1024
BATCH = 32
SEQ = 256
DEPTH = 4
DEC_BATCH = 2
DEC_SEQ = 2048
PAST_LEN = 256
GRID_W = 64
HY_W = 512
HY_EMB = 33
HY_BANDS = 16
HY_FF = 64
HY_TARGET = 1e-2
HY_FAST = 0.3
HY_SLOW = 1.5
RET_W = 512
RET_HEADS = 4
RET_HEAD_DIM = 128
RET_CHUNK = 128
DIFF_HEADS = 8
DIFF_HEAD_DIM = 64
ROPE_BASE = 10000.0
N_EXPERTS = 16
EC_FACTOR = 2
EXPERT_FF = 1024
EVEN_IN = 3 * HY_W + 4 * RET_W
EPS = 1e-6

N_PROMPT = BATCH * SEQ
N_SAMPLE = DEC_BATCH * DEC_SEQ
N_TOK = N_PROMPT + N_SAMPLE
N_GROUPS = 1 + DEC_BATCH
CAP_P = EC_FACTOR * N_PROMPT // N_EXPERTS
CAP_S = EC_FACTOR * N_SAMPLE // N_EXPERTS
CAP_T = CAP_P + CAP_S

LANES = 128
SUBLANES = 8
VMEM_LIMIT = 56 * 1024 * 1024

SH1, SC1, G1, SH2, SC2, G2 = range(6)


def _params(sem, vmem=VMEM_LIMIT):
    return pltpu.CompilerParams(dimension_semantics=sem, vmem_limit_bytes=vmem)


def _group_of_block(i, tm):
    pb = N_PROMPT // tm
    return jnp.where(i < pb, 0, 1 + (i - pb) // (DEC_SEQ // tm))


MOD_TN = 1024


MOD_UNROLL = 4


def _mod_kernel(cb_ref, w_ref, b_ref, o_ref, a_sc):
    nchunk = MOD_TN // LANES
    cv = cb_ref[...]
    a_sc[...] = cv * (1.0 / (1.0 + jnp.exp(-cv)))

    def body(kb, accs):
        accs = list(accs)
        for u in range(MOD_UNROLL):
            k0 = pl.multiple_of((kb * MOD_UNROLL + u) * SUBLANES, SUBLANES)
            a = [a_sc[r, pl.ds(k0, SUBLANES), :] for r in range(N_GROUPS)]
            for ci in range(nchunk):
                wv = w_ref[0, pl.ds(k0, SUBLANES), ci * LANES:(ci + 1) * LANES]
                for r in range(N_GROUPS):
                    accs[ci * N_GROUPS + r] = accs[ci * N_GROUPS + r] + wv * a[r]
        return tuple(accs)

    init = tuple(jnp.zeros((SUBLANES, LANES), F32) for _ in range(N_GROUPS * nchunk))
    accs = lax.fori_loop(0, D_MODEL // (SUBLANES * MOD_UNROLL), body, init)
    o_ref[...] = jnp.zeros(o_ref.shape, F32)
    for r in range(N_GROUPS):
        for ci in range(nchunk):
            row = jnp.sum(accs[ci * N_GROUPS + r], axis=0, keepdims=True)
            o_ref[0, r:r + 1, ci * LANES:(ci + 1) * LANES] = row + b_ref[0, :, ci * LANES:(ci + 1) * LANES]


def _modulation(c, c_ctx, w_mod, b_mod):
    cond = jnp.concatenate([c_ctx[None, :], c], axis=0)
    cb = jnp.broadcast_to(cond[:, :, None], (N_GROUPS, D_MODEL, LANES))
    out = pl.pallas_call(
        _mod_kernel,
        grid=(DEPTH, 6 * D_MODEL // MOD_TN),
        in_specs=[
            pl.BlockSpec((N_GROUPS, D_MODEL, LANES), lambda l, j: (0, 0, 0)),
            pl.BlockSpec((1, D_MODEL, MOD_TN), lambda l, j: (l, 0, j)),
            pl.BlockSpec((1, 1, MOD_TN), lambda l, j: (l, 0, j)),
        ],
        out_specs=pl.BlockSpec((1, SUBLANES, MOD_TN), lambda l, j: (l, 0, j)),
        out_shape=jax.ShapeDtypeStruct((DEPTH, SUBLANES, 6 * D_MODEL), F32),
        scratch_shapes=[pltpu.VMEM((N_GROUPS, D_MODEL, LANES), F32)],
        compiler_params=_params(("parallel", "parallel")),
        name="ada_mod",
    )(cb, w_mod, b_mod.reshape(DEPTH, 1, 6 * D_MODEL))
    return out[:, :N_GROUPS].reshape(DEPTH, N_GROUPS, 6, D_MODEL)


def _normed(x, g_ref, m_ref, shift, scale):
    ms = jnp.mean(x * x, axis=-1, keepdims=True)
    y = x * lax.rsqrt(ms + EPS) * g_ref[...]
    return y * (1.0 + m_ref[0, scale:scale + 1, :]) + m_ref[0, shift:shift + 1, :]


def _norm_mm_kernel(x_ref, m_ref, g_ref, w_ref, o_ref, *, tn):
    h = _normed(x_ref[...], g_ref, m_ref, SH1, SC1).astype(BF16)
    for c0 in range(0, o_ref.shape[1], tn):
        o_ref[:, c0:c0 + tn] = jnp.dot(h, w_ref[:, c0:c0 + tn], preferred_element_type=F32)


def _norm_project(x, mods_l, g, w_bf16, tm=512, tn=512):
    nout = w_bf16.shape[1]
    return pl.pallas_call(
        functools.partial(_norm_mm_kernel, tn=tn),
        grid=(N_TOK // tm,),
        in_specs=[
            pl.BlockSpec((tm, D_MODEL), lambda i: (i, 0)),
            pl.BlockSpec((1, 6, D_MODEL), lambda i: (_group_of_block(i, tm), 0, 0)),
            pl.BlockSpec((1, D_MODEL), lambda i: (0, 0)),
            pl.BlockSpec((D_MODEL, nout), lambda i: (0, 0)),
        ],
        out_specs=pl.BlockSpec((tm, nout), lambda i: (i, 0)),
        out_shape=jax.ShapeDtypeStruct((N_TOK, nout), F32),
        compiler_params=_params(("parallel",)),
        name="norm_project",
    )(x, mods_l, g.reshape(1, D_MODEL), w_bf16)


def _proj_res_kernel(*refs, n_in, gate):
    a_refs = refs[:n_in]
    w_refs = refs[n_in:2 * n_in]
    x_ref, m_ref, o_ref = refs[2 * n_in:]
    acc = None
    for a_ref, w_ref in zip(a_refs, w_refs):
        t = jnp.dot(a_ref[...].astype(BF16), w_ref[...], preferred_element_type=F32)
        acc = t if acc is None else acc + t
    o_ref[...] = x_ref[...] + m_ref[0, gate:gate + 1, :] * acc


def _project_residual(acts, ws_bf16, x, mods_l, gate, tm=512):
    n_in = len(acts)
    in_specs = [pl.BlockSpec((tm, a.shape[1]), lambda i: (i, 0)) for a in acts]
    in_specs += [pl.BlockSpec(w.shape, lambda i: (0, 0)) for w in ws_bf16]
    in_specs += [
        pl.BlockSpec((tm, D_MODEL), lambda i: (i, 0)),
        pl.BlockSpec((1, 6, D_MODEL), lambda i: (_group_of_block(i, tm), 0, 0)),
    ]
    return pl.pallas_call(
        functools.partial(_proj_res_kernel, n_in=n_in, gate=gate),
        grid=(N_TOK // tm,),
        in_specs=in_specs,
        out_specs=pl.BlockSpec((tm, D_MODEL), lambda i: (i, 0)),
        out_shape=jax.ShapeDtypeStruct((N_TOK, D_MODEL), F32),
        input_output_aliases={2 * n_in: 0},
        compiler_params=_params(("parallel",)),
        name="project_residual",
    )(*acts, *ws_bf16, x, mods_l)


@functools.lru_cache(maxsize=None)
def _dft_mats(L):
    n = 2 * L
    ft = (np.arange(L, dtype=np.int64)[:, None] * np.arange(L, dtype=np.int64)[None, :]) % n
    ang = ft.astype(np.float64) * (2.0 * np.pi / n)
    return np.cos(ang).astype(np.float32), np.sin(ang).astype(np.float32)


def _dft_bf16(L):
    c, s = _dft_mats(L)
    return jnp.asarray(c).astype(BF16), jnp.asarray(s).astype(BF16)


def _alt_sign(shape, row0):
    t = lax.broadcasted_iota(jnp.int32, shape, 0) + row0
    return (1 - 2 * (t & 1)).astype(F32)


def _filter_dft_kernel(s_ref, d_ref, c_ref, sn_ref, ka_ref, ki_ref, kn_ref, *, L, fb):
    f0 = pl.program_id(0) * fb
    n = 2.0 * L
    s = s_ref[...]
    r = jnp.dot(c_ref[...], s, preferred_element_type=F32)
    im = jnp.dot(sn_ref[...], d_ref[...], preferred_element_type=F32)
    fidx = lax.broadcasted_iota(jnp.int32, r.shape, 0) + f0
    scale = jnp.where(fidx == 0, 1.0 / n, 2.0 / n)
    ka_ref[...] = r * scale
    ki_ref[...] = im * (2.0 / n)
    nyq = jnp.sum(s.astype(F32) * _alt_sign(s.shape, 0), axis=0, keepdims=True) * (1.0 / n)
    kn_ref[...] = jnp.broadcast_to(nyq, kn_ref.shape)


def _filter_spectra(s, d, L, dft):
    fb = min(L, 512)
    cmat, smat = dft
    w = 2 * HY_W
    return pl.pallas_call(
        functools.partial(_filter_dft_kernel, L=L, fb=fb),
        grid=(L // fb,),
        in_specs=[
            pl.BlockSpec((L, w), lambda f: (0, 0)),
            pl.BlockSpec((L, w), lambda f: (0, 0)),
            pl.BlockSpec((fb, L), lambda f: (f, 0)),
            pl.BlockSpec((fb, L), lambda f: (f, 0)),
        ],
        out_specs=[
            pl.BlockSpec((fb, w), lambda f: (f, 0)),
            pl.BlockSpec((fb, w), lambda f: (f, 0)),
            pl.BlockSpec((SUBLANES, w), lambda f: (0, 0)),
        ],
        out_shape=[
            jax.ShapeDtypeStruct((L, w), F32),
            jax.ShapeDtypeStruct((L, w), F32),
            jax.ShapeDtypeStruct((SUBLANES, w), F32),
        ],
        compiler_params=_params(("arbitrary",)),
        name="hyena_filter_dft",
    )(s.astype(BF16), d.astype(BF16), cmat, smat)


HY_TILE = 256


def _short_conv_tile(ref, r0, L, w_ref, b_ref):
    t = HY_TILE
    cur = ref[r0:r0 + t, :]
    rid = lax.broadcasted_iota(jnp.int32, cur.shape, 0)
    if r0 % L == 0:
        prev = jnp.where(rid == 0, 0.0, pltpu.roll(cur, 1, axis=0))
    else:
        prev = ref[r0 - 1:r0 - 1 + t, :]
    if (r0 + t) % L == 0:
        nxt = jnp.where(rid == t - 1, 0.0, pltpu.roll(cur, t - 1, axis=0))
    else:
        nxt = ref[r0 + 1:r0 + 1 + t, :]
    return prev * w_ref[0:1, :] + cur * w_ref[1:2, :] + nxt * w_ref[2:3, :] + b_ref[...]


def _hyena_conv_kernel(a_ref, x_ref, cr_ref, sr_ref, cc_ref, sc_ref, ka_ref, ki_ref, kn_ref,
                       wa_ref, ba_ref, wx_ref, bx_ref, hb_ref, o_ref, z_sc, acc_sc, *, L, nseq, conv_a):
    f = pl.program_id(1)
    nf = pl.num_programs(1)
    rows = nseq * L

    def a_tile(r0):
        if conv_a:
            return _short_conv_tile(a_ref, r0, L, wa_ref, ba_ref)
        return a_ref[r0:r0 + HY_TILE, :]

    @pl.when(f == 0)
    def _():
        for q in range(nseq):
            nyq = jnp.zeros((1, HY_W), F32)
            for r0 in range(q * L, (q + 1) * L, HY_TILE):
                zt = a_tile(r0)
                z_sc[r0:r0 + HY_TILE, :] = zt.astype(BF16)
                nyq = nyq + jnp.sum(zt * _alt_sign(zt.shape, r0), axis=0, keepdims=True)
            nyq = nyq * kn_ref[0:1, :]
            for r0 in range(q * L, (q + 1) * L, HY_TILE):
                acc_sc[r0:r0 + HY_TILE, :] = _alt_sign((HY_TILE, HY_W), r0) * nyq

    ka = ka_ref[...]
    ki = ki_ref[...]
    for q in range(nseq):
        z = z_sc[q * L:(q + 1) * L, :]
        a = jnp.dot(cr_ref[...], z, preferred_element_type=F32)
        b = jnp.dot(sr_ref[...], z, preferred_element_type=F32)
        p = (a * ka + b * ki).astype(BF16)
        qq = (b * ka - a * ki).astype(BF16)
        acc_sc[q * L:(q + 1) * L, :] += (jnp.dot(cc_ref[...], p, preferred_element_type=F32)
                                         + jnp.dot(sc_ref[...], qq, preferred_element_type=F32))

    @pl.when(f == nf - 1)
    def _():
        for r0 in range(0, rows, HY_TILE):
            y = acc_sc[r0:r0 + HY_TILE, :] + a_tile(r0) * hb_ref[...]
            o_ref[r0:r0 + HY_TILE, :] = y * _short_conv_tile(x_ref, r0, L, wx_ref, bx_ref)


def _hyena_conv(a, a_col, x, x_col, row0, n_rows, L, nseq, spectra, filt, wa, ba, wx, bx, hbias, conv_a, dft):
    fb = min(L, 256)
    cmat, smat = dft
    ka, ki, kn = spectra
    rb = nseq * L
    a_off = row0 // rb if a.shape[0] != n_rows else 0
    x_off = row0 // rb
    return pl.pallas_call(
        functools.partial(_hyena_conv_kernel, L=L, nseq=nseq, conv_a=conv_a),
        grid=(n_rows // rb, L // fb),
        in_specs=[
            pl.BlockSpec((rb, HY_W), lambda i, f: (i + a_off, a_col)),
            pl.BlockSpec((rb, HY_W), lambda i, f: (i + x_off, x_col)),
            pl.BlockSpec((fb, L), lambda i, f: (f, 0)),
            pl.BlockSpec((fb, L), lambda i, f: (f, 0)),
            pl.BlockSpec((L, fb), lambda i, f: (0, f)),
            pl.BlockSpec((L, fb), lambda i, f: (0, f)),
            pl.BlockSpec((fb, HY_W), lambda i, f: (f, filt)),
            pl.BlockSpec((fb, HY_W), lambda i, f: (f, filt)),
            pl.BlockSpec((SUBLANES, HY_W), lambda i, f: (0, filt)),
            pl.BlockSpec((3, HY_W), lambda i, f: (0, 0)),
            pl.BlockSpec((1, HY_W), lambda i, f: (0, 0)),
            pl.BlockSpec((3, HY_W), lambda i, f: (0, 0)),
            pl.BlockSpec((1, HY_W), lambda i, f: (0, 0)),
            pl.BlockSpec((1, HY_W), lambda i, f: (0, 0)),
        ],
        out_specs=pl.BlockSpec((rb, HY_W), lambda i, f: (i, 0)),
        out_shape=jax.ShapeDtypeStruct((n_rows, HY_W), F32),
        scratch_shapes=[pltpu.VMEM((rb, HY_W), BF16), pltpu.VMEM((rb, HY_W), F32)],
        compiler_params=_params(("parallel", "arbitrary")),
        name="hyena_conv",
    )(a, x, cmat, smat, cmat, smat, ka, ki, kn, wa, ba, wx, bx, hbias)


def _hyena_filter_taps(L, f1w, f1b, fr1, f2w, f2b, fr2, f3w):
    hp = lax.Precision.HIGHEST
    pos = jnp.arange(L, dtype=F32)
    t = pos / (L - 1)
    w = 2.0 * math.pi * pos / L
    f = jnp.linspace(1e-4, HY_BANDS - 1, HY_BANDS, dtype=F32)
    wf = w[:, None] * f[None, :]
    feat = jnp.concatenate([t[:, None], jnp.cos(wf), -jnp.sin(wf)], axis=-1)
    h = jnp.sin(fr1 * (jnp.dot(feat, f1w, precision=hp) + f1b))
    h = jnp.sin(fr2 * (jnp.dot(h, f2w, precision=hp) + f2b))
    h = jnp.dot(h, f3w, precision=hp).astype(F32).reshape(L, 2, 2, HY_W)
    deltas = jnp.linspace(math.log(HY_TARGET) / HY_SLOW, math.log(HY_TARGET) / HY_FAST, HY_W, dtype=F32)
    window = jnp.exp(-t[:, None] * jnp.abs(deltas)[None, :])
    h = h * window[:, None, None, :]
    h = h / (jnp.sum(jnp.abs(h), axis=(0, 2), keepdims=True) + EPS)
    fwd = h[:, :, 0].reshape(L, 2 * HY_W)
    bwd = h[:, :, 1].reshape(L, 2 * HY_W)
    bwd = bwd.at[0].set(0.0)
    return fwd + bwd, bwd - fwd


def _dot_t0(a, b):
    return lax.dot_general(a, b, (((0,), (0,)), ((), ())), preferred_element_type=F32)


def _dot_t1(a, b):
    return lax.dot_general(a, b, (((1,), (1,)), ((), ())), preferred_element_type=F32)


def _retention_kernel(*refs, L, has_s0, has_prev):
    refs = list(refs)
    q_ref, k_ref, v_ref, g_ref, dl_ref = refs[:5]
    pos = 5
    s0_ref = st_ref = None
    if has_s0:
        s0_ref = refs[pos]
        pos += 1
    if has_prev:
        pos += 1
    y_ref = refs[pos]
    pos += 1
    if not has_s0:
        st_ref = refs[pos]
        pos += 1
    sb_sc, sf_cur, sb_cur = refs[pos:]
    c = RET_CHUNK
    nc = L // c
    kscale = RET_HEAD_DIM ** -0.5
    ri = lax.broadcasted_iota(jnp.int32, (c, c), 0).astype(F32)
    ci = lax.broadcasted_iota(jnp.int32, (c, c), 1).astype(F32)
    diff = ri - ci
    dec = []
    for h in range(RET_HEADS):
        xf = dl_ref[0, h:h + 1, :]
        xb = dl_ref[1, h:h + 1, :]
        lgf = jnp.minimum(xf, 0.0) - jnp.log1p(jnp.exp(-jnp.abs(xf)))
        lgb = jnp.minimum(xb, 0.0) - jnp.log1p(jnp.exp(-jnp.abs(xb)))
        dec.append(dict(
            mask=(jnp.where(diff >= 0, jnp.exp(lgf * jnp.maximum(diff, 0.0)), 0.0)
                  + jnp.where(diff <= 0, jnp.exp(lgb * jnp.maximum(-diff, 0.0)), 0.0)),
            qdec_f=jnp.exp(lgf * (ri + 1.0)), kdec_f=jnp.exp(lgf * (c - 1.0 - ri)),
            qdec_b=jnp.exp(lgb * (c - ri)), kdec_b=jnp.exp(lgb * ri),
            cd_f=jnp.exp(lgf * c), cd_b=jnp.exp(lgb * c)))
        if has_s0:
            sf_cur[h] = s0_ref[0, 0, 0, h]
            sb_cur[h] = s0_ref[0, 0, 1, h]
        else:
            sf_cur[h] = jnp.zeros((RET_HEAD_DIM, RET_HEAD_DIM), F32)
            sb_cur[h] = jnp.zeros((RET_HEAD_DIM, RET_HEAD_DIM), F32)

    def bwd_body(i, carry):
        j = nc - 1 - i
        r0 = pl.multiple_of(j * c, c)
        for h in range(RET_HEADS):
            hs = slice(h * RET_HEAD_DIM, (h + 1) * RET_HEAD_DIM)
            sb = sb_cur[h]
            sb_sc[h * nc + j] = sb
            kc = k_ref[pl.ds(r0, c), hs] * kscale
            vc = v_ref[pl.ds(r0, c), hs]
            sb_cur[h] = sb * dec[h]["cd_b"] + _dot_t0((kc * dec[h]["kdec_b"]).astype(BF16), vc.astype(BF16))
        return carry

    lax.fori_loop(0, nc, bwd_body, 0)

    def fwd_body(j, carry):
        r0 = pl.multiple_of(j * c, c)
        for h in range(RET_HEADS):
            hs = slice(h * RET_HEAD_DIM, (h + 1) * RET_HEAD_DIM)
            dh = dec[h]
            sf = sf_cur[h]
            qc = q_ref[pl.ds(r0, c), hs]
            kc = k_ref[pl.ds(r0, c), hs] * kscale
            vc = v_ref[pl.ds(r0, c), hs].astype(BF16)
            scores = _dot_t1(qc.astype(BF16), kc.astype(BF16)) * dh["mask"]
            o = jnp.dot(scores.astype(BF16), vc, preferred_element_type=F32)
            o = o + jnp.dot((qc * dh["qdec_f"]).astype(BF16), sf.astype(BF16), preferred_element_type=F32)
            o = o + jnp.dot((qc * dh["qdec_b"]).astype(BF16), sb_sc[h * nc + j].astype(BF16),
                            preferred_element_type=F32)
            mu = jnp.mean(o, axis=-1, keepdims=True)
            var = jnp.mean(jnp.square(o - mu), axis=-1, keepdims=True)
            on = (o - mu) * lax.rsqrt(var + EPS)
            gc = g_ref[pl.ds(r0, c), hs]
            y_ref[pl.ds(r0, c), hs] = gc * (1.0 / (1.0 + jnp.exp(-gc))) * on
            sf_cur[h] = sf * dh["cd_f"] + _dot_t0((kc * dh["kdec_f"]).astype(BF16), vc)
        return carry

    lax.fori_loop(0, nc, fwd_body, 0)
    if st_ref is not None:
        for h in range(RET_HEADS):
            st_ref[0, 0, 0, h] = sf_cur[h]
            st_ref[0, 0, 1, h] = sb_cur[h]


def _retention(proj, dl, layer_i, row0, nseq, L, s0=None, st_prev=None):
    off = row0 // L
    has_s0 = s0 is not None
    has_prev = st_prev is not None
    nc = L // RET_CHUNK
    n_ret = (DEPTH + 1) // 2
    col = lambda j: pl.BlockSpec((L, RET_W), lambda b: (b + off, 3 + j))
    in_specs = [col(0), col(1), col(2), col(3),
                pl.BlockSpec((2, RET_HEADS, LANES), lambda b: (0, 0, 0))]
    args = [proj, proj, proj, proj, dl]
    y_spec = pl.BlockSpec((L, RET_W), lambda b: (b, 0))
    y_shape = jax.ShapeDtypeStruct((nseq * L, RET_W), F32)
    st_block = (1, 1, 2, RET_HEADS, RET_HEAD_DIM, RET_HEAD_DIM)
    st_spec = pl.BlockSpec(st_block, lambda b: (b, layer_i, 0, 0, 0, 0))
    aliases = {}
    if has_s0:
        in_specs.append(st_spec)
        args.append(s0)
        out_specs, out_shape = y_spec, y_shape
    else:
        if has_prev:
            aliases = {len(args): 1}
            in_specs.append(pl.BlockSpec(memory_space=pl.ANY))
            args.append(st_prev)
        out_specs = [y_spec, st_spec]
        out_shape = [y_shape, jax.ShapeDtypeStruct((nseq, n_ret) + st_block[2:], F32)]
    state = pltpu.VMEM((RET_HEADS, RET_HEAD_DIM, RET_HEAD_DIM), F32)
    return pl.pallas_call(
        functools.partial(_retention_kernel, L=L, has_s0=has_s0, has_prev=has_prev),
        grid=(nseq,),
        in_specs=in_specs,
        out_specs=out_specs,
        out_shape=out_shape,
        input_output_aliases=aliases,
        scratch_shapes=[pltpu.VMEM((RET_HEADS * nc, RET_HEAD_DIM, RET_HEAD_DIM), F32), state, state],
        compiler_params=_params(("parallel",)),
        name="retention",
    )(*args)


@functools.lru_cache(maxsize=None)
def _rope_tables():
    L = DEC_SEQ
    rows = L // GRID_W
    row = np.repeat(np.arange(rows, dtype=np.float64), GRID_W)
    col = np.tile(np.arange(GRID_W, dtype=np.float64), rows)
    quarter = DIFF_HEAD_DIM // 4
    freqs = ROPE_BASE ** (-np.arange(quarter, dtype=np.float64) / quarter)
    j = np.arange(LANES)
    pos = np.where(((j % DIFF_HEAD_DIM) < DIFF_HEAD_DIM // 2)[None, :], row[:, None], col[:, None])
    ang = pos * freqs[j % quarter][None, :]
    cos = np.cos(ang).astype(np.float32)
    sin = np.sin(ang).astype(np.float32)
    first = ((j % (2 * quarter)) < quarter)[None, :]
    sin_a = np.where(first, -sin, 0.0).astype(np.float32)
    sin_b = np.where(first, 0.0, sin).astype(np.float32)
    return jnp.asarray(cos), jnp.asarray(sin_a), jnp.asarray(sin_b)


def _rope_head(x, cos, sin_a, sin_b):
    quarter = DIFF_HEAD_DIM // 4
    up = pltpu.roll(x, LANES - quarter, axis=1)
    dn = pltpu.roll(x, quarter, axis=1)
    return x * cos + up * sin_a + dn * sin_b


def _kv_prep_kernel(q_ref, k_ref, v_ref, cos_ref, sa_ref, sb_ref, qo_ref, ko_ref, vo_ref):
    cos, sa, sb = cos_ref[...], sa_ref[...], sb_ref[...]
    for h in range(DIFF_HEADS):
        hs = slice(h * LANES, (h + 1) * LANES)
        qo_ref[:, hs] = _rope_head(q_ref[:, hs], cos, sa, sb).astype(BF16)
        ko_ref[0, :, hs] = _rope_head(k_ref[:, hs], cos, sa, sb).astype(BF16)
    vo_ref[0] = v_ref[...].astype(BF16)


def _cache_copy_kernel(ck_ref, cv_ref, k_in, v_in, ko_ref, vo_ref):
    del k_in, v_in
    ko_ref[0] = ck_ref[0, 0].astype(BF16)
    vo_ref[0] = cv_ref[0, 0].astype(BF16)


def _sample_qkv(proj, cache_k, cache_v, layer_i, tm=256):
    cos, sa, sb = _rope_tables()
    lk = PAST_LEN + DEC_SEQ
    pblk = N_PROMPT // tm
    nblk = DEC_SEQ // tm
    cblk = PAST_LEN // tm
    tab = pl.BlockSpec((tm, LANES), lambda b, i: (i, 0))
    q, k, v = pl.pallas_call(
        _kv_prep_kernel,
        grid=(DEC_BATCH, nblk),
        in_specs=[
            pl.BlockSpec((tm, D_MODEL), lambda b, i: (pblk + b * nblk + i, 0)),
            pl.BlockSpec((tm, D_MODEL), lambda b, i: (pblk + b * nblk + i, 1)),
            pl.BlockSpec((tm, D_MODEL), lambda b, i: (pblk + b * nblk + i, 2)),
            tab, tab, tab,
        ],
        out_specs=[
            pl.BlockSpec((tm, D_MODEL), lambda b, i: (b * nblk + i, 0)),
            pl.BlockSpec((1, tm, D_MODEL), lambda b, i: (b, cblk + i, 0)),
            pl.BlockSpec((1, tm, D_MODEL), lambda b, i: (b, cblk + i, 0)),
        ],
        out_shape=[
            jax.ShapeDtypeStruct((N_SAMPLE, D_MODEL), BF16),
            jax.ShapeDtypeStruct((DEC_BATCH, lk, D_MODEL), BF16),
            jax.ShapeDtypeStruct((DEC_BATCH, lk, D_MODEL), BF16),
        ],
        compiler_params=_params(("parallel", "parallel")),
        name="rope_qkv",
    )(proj, proj, proj, cos, sa, sb)
    n_att = DEPTH // 2
    ck = cache_k.reshape(DEC_BATCH, n_att, PAST_LEN, D_MODEL)
    cv = cache_v.reshape(DEC_BATCH, n_att, PAST_LEN, D_MODEL)
    k, v = pl.pallas_call(
        _cache_copy_kernel,
        grid=(DEC_BATCH,),
        in_specs=[
            pl.BlockSpec((1, 1, PAST_LEN, D_MODEL), lambda b: (b, layer_i, 0, 0)),
            pl.BlockSpec((1, 1, PAST_LEN, D_MODEL), lambda b: (b, layer_i, 0, 0)),
            pl.BlockSpec(memory_space=pl.ANY),
            pl.BlockSpec(memory_space=pl.ANY),
        ],
        out_specs=[
            pl.BlockSpec((1, PAST_LEN, D_MODEL), lambda b: (b, 0, 0)),
            pl.BlockSpec((1, PAST_LEN, D_MODEL), lambda b: (b, 0, 0)),
        ],
        out_shape=[
            jax.ShapeDtypeStruct((DEC_BATCH, lk, D_MODEL), BF16),
            jax.ShapeDtypeStruct((DEC_BATCH, lk, D_MODEL), BF16),
        ],
        input_output_aliases={2: 0, 3: 1},
        compiler_params=_params(("parallel",)),
        name="cache_prepend",
    )(ck, cv, k, v)
    return q, k, v


def _diff_attn_kernel(q_ref, k_ref, v_ref, lam_ref, sg_ref, o_ref, *, lam_init, batched_kv):
    lv = lam_ref[...]
    lam = (jnp.exp(jnp.sum(lv[0:1] * lv[1:2], axis=-1, keepdims=True))
           - jnp.exp(jnp.sum(lv[2:3] * lv[3:4], axis=-1, keepdims=True)) + lam_init)
    lane = lax.broadcasted_iota(jnp.int32, (1, LANES), 1)
    m1 = (lane < DIFF_HEAD_DIM).astype(F32)
    m2 = 1.0 - m1
    scale = DIFF_HEAD_DIM ** -0.5
    for h in range(DIFF_HEADS):
        hs = slice(h * LANES, (h + 1) * LANES)
        q = q_ref[:, hs].astype(F32) * scale
        if batched_kv:
            k = k_ref[0, :, hs].astype(BF16)
            v = v_ref[0, :, hs].astype(BF16)
        else:
            k = k_ref[:, hs].astype(BF16)
            v = v_ref[:, hs].astype(BF16)
        outs = []
        for m in (m1, m2):
            s = _dot_t1((q * m).astype(BF16), k)
            s = s - jnp.max(s, axis=-1, keepdims=True)
            p = jnp.exp(s)
            l = jnp.sum(p, axis=-1, keepdims=True)
            outs.append(jnp.dot(p.astype(BF16), v, preferred_element_type=F32) / l)
        o = outs[0] - lam * outs[1]
        ms = jnp.mean(o * o, axis=-1, keepdims=True)
        o_ref[:, hs] = o * lax.rsqrt(ms + EPS) * sg_ref[...] * (1.0 - lam_init)


def _diff_attention(q, k, v, lam_vec, subln, lam_init, *, nb, lq, lk, tq, q_row0, q_col, kv_cols, batched_kv):
    nq = lq // tq
    qoff = q_row0 // tq
    q_spec = pl.BlockSpec((tq, D_MODEL), lambda b, i: (qoff + b * nq + i, q_col))
    if batched_kv:
        k_spec = pl.BlockSpec((1, lk, D_MODEL), lambda b, i: (b, 0, 0))
        v_spec = k_spec
    else:
        k_spec = pl.BlockSpec((lk, D_MODEL), lambda b, i: (b, kv_cols[0]))
        v_spec = pl.BlockSpec((lk, D_MODEL), lambda b, i: (b, kv_cols[1]))
    return pl.pallas_call(
        functools.partial(_diff_attn_kernel, lam_init=lam_init, batched_kv=batched_kv),
        grid=(nb, nq),
        in_specs=[q_spec, k_spec, v_spec,
                  pl.BlockSpec((4, LANES), lambda b, i: (0, 0)),
                  pl.BlockSpec((1, LANES), lambda b, i: (0, 0))],
        out_specs=pl.BlockSpec((tq, D_MODEL), lambda b, i: (b * nq + i, 0)),
        out_shape=jax.ShapeDtypeStruct((nb * lq, D_MODEL), F32),
        compiler_params=_params(("parallel", "arbitrary")),
        name="diff_attention",
    )(q, k, v, lam_vec, subln)


def _router_kernel(x_ref, m_ref, g_ref, wr_ref, h_ref, aff_ref):
    h = _normed(x_ref[...], g_ref, m_ref, SH2, SC2)
    hb = h.astype(BF16)
    h_ref[...] = hb
    hl = (h - hb.astype(F32)).astype(BF16)
    wr = wr_ref[...]
    wh = wr.astype(BF16)
    wl = (wr - wh.astype(F32)).astype(BF16)
    logits = (jnp.dot(hb, wh, preferred_element_type=F32) + jnp.dot(hl, wh, preferred_element_type=F32)
              + jnp.dot(hb, wl, preferred_element_type=F32))
    lt = jnp.transpose(logits)[0:N_EXPERTS, :]
    lt = lt - jnp.max(lt, axis=0, keepdims=True)
    e = jnp.exp(lt)
    aff_ref[...] = e / jnp.sum(e, axis=0, keepdims=True)


def _router(x, mods_l, g, w_router, tm=512):
    wr = jnp.pad(w_router, ((0, 0), (0, LANES - N_EXPERTS)))
    return pl.pallas_call(
        _router_kernel,
        grid=(N_TOK // tm,),
        in_specs=[
            pl.BlockSpec((tm, D_MODEL), lambda i: (i, 0)),
            pl.BlockSpec((1, 6, D_MODEL), lambda i: (_group_of_block(i, tm), 0, 0)),
            pl.BlockSpec((1, D_MODEL), lambda i: (0, 0)),
            pl.BlockSpec((D_MODEL, LANES), lambda i: (0, 0)),
        ],
        out_specs=[
            pl.BlockSpec((tm, D_MODEL), lambda i: (i, 0)),
            pl.BlockSpec((N_EXPERTS, tm), lambda i: (0, i)),
        ],
        out_shape=[
            jax.ShapeDtypeStruct((N_TOK, D_MODEL), BF16),
            jax.ShapeDtypeStruct((N_EXPERTS, N_TOK), F32),
        ],
        compiler_params=_params(("parallel",)),
        name="norm_router",
    )(x, mods_l, g.reshape(1, D_MODEL), wr)


FFN_TF = 512
FFN_TR = 512


def _expert_ffn_kernel(xs_ref, wg_ref, wu_ref, wd_ref, gate_ref, o_ref):
    f = pl.program_id(1)
    wg = wg_ref[0].astype(BF16)
    wu = wu_ref[0].astype(BF16)
    wd = wd_ref[0].astype(BF16)
    for r0 in range(0, CAP_T, FFN_TR):
        rs = slice(r0, r0 + FFN_TR)
        xs = xs_ref[0, rs, :]
        a = jnp.dot(xs, wg, preferred_element_type=F32)
        u = jnp.dot(xs, wu, preferred_element_type=F32)
        hid = (a * (1.0 / (1.0 + jnp.exp(-a))) * u).astype(BF16)
        y = jnp.dot(hid, wd, preferred_element_type=F32)

        @pl.when(f == 0)
        def _():
            o_ref[0, rs, :] = y

        @pl.when(jnp.logical_and(f > 0, f < pl.num_programs(1) - 1))
        def _():
            o_ref[0, rs, :] += y

        @pl.when(f == pl.num_programs(1) - 1)
        def _():
            o_ref[0, rs, :] = (o_ref[0, rs, :] + y) * gate_ref[0, rs, :]


def _expert_ffn(xs, wg, wu, wd, gate, l):
    return pl.pallas_call(
        _expert_ffn_kernel,
        grid=(N_EXPERTS, EXPERT_FF // FFN_TF),
        in_specs=[
            pl.BlockSpec((1, CAP_T, D_MODEL), lambda e, f: (e, 0, 0)),
            pl.BlockSpec((None, 1, D_MODEL, FFN_TF), lambda e, f: (l, e, 0, f)),
            pl.BlockSpec((None, 1, D_MODEL, FFN_TF), lambda e, f: (l, e, 0, f)),
            pl.BlockSpec((None, 1, FFN_TF, D_MODEL), lambda e, f: (l, e, f, 0)),
            pl.BlockSpec((1, CAP_T, 1), lambda e, f: (e, 0, 0)),
        ],
        out_specs=pl.BlockSpec((1, CAP_T, D_MODEL), lambda e, f: (e, 0, 0)),
        out_shape=jax.ShapeDtypeStruct((N_EXPERTS, CAP_T, D_MODEL), F32),
        compiler_params=_params(("parallel", "arbitrary")),
        name="expert_ffn",
    )(xs, wg, wu, wd, gate)


def _final_norm_kernel(x_ref, g_ref, o_ref):
    x = x_ref[...]
    ms = jnp.mean(x * x, axis=-1, keepdims=True)
    o_ref[...] = x * lax.rsqrt(ms + EPS) * g_ref[...]


def _final_norm(x, g, row0, n_rows, tm=512):
    off = row0 // tm
    return pl.pallas_call(
        _final_norm_kernel,
        grid=(n_rows // tm,),
        in_specs=[pl.BlockSpec((tm, D_MODEL), lambda i: (i + off, 0)),
                  pl.BlockSpec((1, D_MODEL), lambda i: (0, 0))],
        out_specs=pl.BlockSpec((tm, D_MODEL), lambda i: (i, 0)),
        out_shape=jax.ShapeDtypeStruct((n_rows, D_MODEL), F32),
        compiler_params=_params(("parallel",)),
        name="final_norm",
    )(x, g.reshape(1, D_MODEL))


def _even_layer(x, mods_l, i, state_ret, st_prev, dfts, norm1_g, w_in_even, hy_short_w, hy_short_b, hy_f1_w, hy_f1_b,
                hy_freq1, hy_f2_w, hy_f2_b, hy_freq2, hy_f3_w, hy_bias, ret_decay, w_out_even):
    proj = _norm_project(x, mods_l, norm1_g, w_in_even[i].astype(BF16))
    sw, sbias = hy_short_w[i], hy_short_b[i].reshape(1, 3 * HY_W)
    hb = hy_bias[i]
    y_hy = []
    for (row0, n_rows, L, nseq) in ((0, N_PROMPT, SEQ, 8), (N_PROMPT, N_SAMPLE, DEC_SEQ, 1)):
        s, d = _hyena_filter_taps(L, hy_f1_w[i], hy_f1_b[i], hy_freq1[i], hy_f2_w[i], hy_f2_b[i], hy_freq2[i], hy_f3_w[i])
        spectra = _filter_spectra(s, d, L, dfts[L])
        wv, bv = sw[:, 0:HY_W], sbias[:, 0:HY_W]
        w1, b1 = sw[:, HY_W:2 * HY_W], sbias[:, HY_W:2 * HY_W]
        w2, b2 = sw[:, 2 * HY_W:], sbias[:, 2 * HY_W:]
        z1 = _hyena_conv(proj, 0, proj, 1, row0, n_rows, L, nseq, spectra, 0, wv, bv, w1, b1, hb[0:1], True, dfts[L])
        z2 = _hyena_conv(z1, 0, proj, 2, row0, n_rows, L, nseq, spectra, 1, wv, bv, w2, b2, hb[1:2], False, dfts[L])
        y_hy.append(z2)
    y_hy = jnp.concatenate(y_hy, axis=0)
    dl = jnp.broadcast_to(ret_decay[i].astype(F32)[:, :, None], (2, RET_HEADS, LANES))
    y_rp, st = _retention(proj, dl, i, 0, BATCH, SEQ, st_prev=st_prev)
    y_rs = _retention(proj, dl, i, N_PROMPT, DEC_BATCH, DEC_SEQ, s0=state_ret)
    y_ret = jnp.concatenate([y_rp, y_rs], axis=0)
    wo = w_out_even[i].astype(BF16)
    x = _project_residual([y_hy, y_ret], [wo[:HY_W], wo[HY_W:]], x, mods_l, G1)
    return x, st


def _cache_write_kernel(*refs):
    k_ref, v_ref = refs[:2]
    ko_ref, vo_ref = refs[-2:]
    ko_ref[0, 0] = k_ref[...]
    vo_ref[0, 0] = v_ref[...]


def _cache_write(proj, layer_i, prev):
    shape = jax.ShapeDtypeStruct((BATCH, DEPTH // 2, SEQ, D_MODEL), F32)
    in_specs = [pl.BlockSpec((SEQ, D_MODEL), lambda b: (b, 1)), pl.BlockSpec((SEQ, D_MODEL), lambda b: (b, 2))]
    args = [proj, proj]
    aliases = {}
    if prev is not None:
        in_specs += [pl.BlockSpec(memory_space=pl.ANY), pl.BlockSpec(memory_space=pl.ANY)]
        args += list(prev)
        aliases = {2: 0, 3: 1}
    out_spec = pl.BlockSpec((1, 1, SEQ, D_MODEL), lambda b: (b, layer_i, 0, 0))
    return pl.pallas_call(
        _cache_write_kernel,
        grid=(BATCH,),
        in_specs=in_specs,
        out_specs=[out_spec, out_spec],
        out_shape=[shape, shape],
        input_output_aliases=aliases,
        compiler_params=_params(("parallel",)),
        name="cache_write",
    )(*args)


def _odd_layer(x, mods_l, l, i, cache_k, cache_v, kv_prev, norm1_g, w_in_odd, lam_q1, lam_k1, lam_q2, lam_k2, subln_g,
               w_out_odd):
    lam_init = 0.8 - 0.6 * math.exp(-0.3 * l)
    proj = _norm_project(x, mods_l, norm1_g, w_in_odd[i].astype(BF16))
    lam_vec = jnp.pad(jnp.stack([lam_q1[i], lam_k1[i], lam_q2[i], lam_k2[i]]), ((0, 0), (0, LANES - DIFF_HEAD_DIM)))
    sg = subln_g[i].reshape(1, LANES)
    o_p = _diff_attention(proj, proj, proj, lam_vec, sg, lam_init, nb=BATCH, lq=SEQ, lk=SEQ, tq=SEQ,
                          q_row0=0, q_col=0, kv_cols=(1, 2), batched_kv=False)
    qs, ks, vs = _sample_qkv(proj, cache_k, cache_v, i)
    o_s = _diff_attention(qs, ks, vs, lam_vec, sg, lam_init, nb=DEC_BATCH, lq=DEC_SEQ, lk=PAST_LEN + DEC_SEQ, tq=256,
                          q_row0=0, q_col=0, kv_cols=None, batched_kv=True)
    o = jnp.concatenate([o_p, o_s], axis=0)
    x = _project_residual([o], [w_out_odd[i].astype(BF16)], x, mods_l, G1)
    return x, _cache_write(proj, i, kv_prev)


def _moe_layer(x, mods_l, l, norm2_g, w_router, wg, wu, wd):
    h, aff = _router(x, mods_l, norm2_g, w_router)
    gate_p, idx_p = lax.top_k(aff[:, :N_PROMPT], CAP_P)
    gate_s, idx_s = lax.top_k(aff[:, N_PROMPT:], CAP_S)
    idx = jnp.concatenate([idx_p, idx_s + N_PROMPT], axis=1)
    gate = jnp.concatenate([gate_p, gate_s], axis=1)
    xs = h[idx]
    out = _expert_ffn(xs, wg, wu, wd, gate[:, :, None], l)
    f = jnp.zeros((N_TOK, D_MODEL), F32).at[idx.reshape(-1)].add(out.reshape(-1, D_MODEL))
    gate2 = jnp.concatenate([jnp.broadcast_to(mods_l[0, G2][None], (N_PROMPT, D_MODEL))]
                            + [jnp.broadcast_to(mods_l[1 + b, G2][None], (DEC_SEQ, D_MODEL)) for b in range(DEC_BATCH)])
    return x + gate2 * f


def kernel(x_prompt, x_sample, state_ret, cache_k, cache_v, c, c_ctx, w_mod, b_mod, norm1_g, norm2_g, w_in_even, hy_short_w, hy_short_b, hy_f1_w, hy_f1_b, hy_freq1, hy_f2_w, hy_f2_b, hy_freq2, hy_f3_w, hy_bias, ret_decay, w_out_even, w_in_odd, lam_q1, lam_k1, lam_q2, lam_k2, subln_g, w_out_odd, moe_router, moe_wg, moe_wu, moe_wd, final_g):
    x = jnp.concatenate([x_prompt.reshape(N_PROMPT, D_MODEL), x_sample.reshape(N_SAMPLE, D_MODEL)], axis=0)
    mods = _modulation(c, c_ctx, w_mod, b_mod)
    dfts = {L: _dft_bf16(L) for L in (SEQ, DEC_SEQ)}
    state_ret = state_ret.astype(F32)
    st = kv = None
    for l in range(DEPTH):
        i = l // 2
        if l % 2 == 0:
            x, st = _even_layer(x, mods[l], i, state_ret, st, dfts, norm1_g[l], w_in_even, hy_short_w, hy_short_b,
                                hy_f1_w, hy_f1_b, hy_freq1, hy_f2_w, hy_f2_b, hy_freq2, hy_f3_w, hy_bias, ret_decay,
                                w_out_even)
        else:
            x, kv = _odd_layer(x, mods[l], l, i, cache_k, cache_v, kv, norm1_g[l], w_in_odd, lam_q1, lam_k1, lam_q2,
                               lam_k2, subln_g, w_out_odd)
        x = _moe_layer(x, mods[l], l, norm2_g[l], moe_router[l], moe_wg, moe_wu, moe_wd)
    y_prompt = _final_norm(x, final_g, 0, N_PROMPT).reshape(BATCH, SEQ, D_MODEL)
    y_sample = _final_norm(x, final_g, N_PROMPT, N_SAMPLE).reshape(DEC_BATCH, DEC_SEQ, D_MODEL)
    cache_shape = (BATCH, DEPTH // 2, SEQ, DIFF_HEADS, 2 * DIFF_HEAD_DIM)
    return (y_prompt, y_sample, st, kv[0].reshape(cache_shape), kv[1].reshape(cache_shape))
```

```python
import functools
import math

import numpy as np
import jax
import jax.numpy as jnp
from jax import lax
from jax.experimental import pallas as pl
from jax.experimental.pallas import tpu as pltpu

F32 = jnp.float32
BF16 = jnp.bfloat16

D_MODEL = 1024
BATCH = 32
SEQ = 256
DEPTH = 4
DEC_BATCH = 2
DEC_SEQ = 2048
PAST_LEN = 256
GRID_W = 64
HY_W = 512
HY_EMB = 33
HY_BANDS = 16
HY_FF = 64
HY_TARGET = 1e-2
HY_FAST = 0.3
HY_SLOW = 1.5
RET_W = 512
RET_HEADS = 4
RET_HEAD_DIM = 128
RET_CHUNK = 128
DIFF_HEADS = 8
DIFF_HEAD_DIM = 64
ROPE_BASE = 10000.0
N_EXPERTS = 16
EC_FACTOR = 2
EXPERT_FF = 1024
EVEN_IN = 3 * HY_W + 4 * RET_W
EPS = 1e-6

N_PROMPT = BATCH * SEQ
N_SAMPLE = DEC_BATCH * DEC_SEQ
N_TOK = N_PROMPT + N_SAMPLE
N_GROUPS = 1 + DEC_BATCH
CAP_P = EC_FACTOR * N_PROMPT // N_EXPERTS
CAP_S = EC_FACTOR * N_SAMPLE // N_EXPERTS
CAP_T = CAP_P + CAP_S

LANES = 128
SUBLANES = 8
VMEM_LIMIT = 56 * 1024 * 1024

SH1, SC1, G1, SH2, SC2, G2 = range(6)


def _params(sem, vmem=VMEM_LIMIT):
    return pltpu.CompilerParams(dimension_semantics=sem, vmem_limit_bytes=vmem)


def _group_of_block(i, tm):
    pb = N_PROMPT // tm
    return jnp.where(i < pb, 0, 1 + (i - pb) // (DEC_SEQ // tm))


MOD_TN = 1024


MOD_UNROLL = 4


def _mod_kernel(cb_ref, w_ref, b_ref, o_ref, a_sc):
    nchunk = MOD_TN // LANES
    cv = cb_ref[...]
    a_sc[...] = cv * (1.0 / (1.0 + jnp.exp(-cv)))

    def body(kb, accs):
        accs = list(accs)
        for u in range(MOD_UNROLL):
            k0 = pl.multiple_of((kb * MOD_UNROLL + u) * SUBLANES, SUBLANES)
            a = [a_sc[r, pl.ds(k0, SUBLANES), :] for r in range(N_GROUPS)]
            for ci in range(nchunk):
                wv = w_ref[0, pl.ds(k0, SUBLANES), ci * LANES:(ci + 1) * LANES]
                for r in range(N_GROUPS):
                    accs[ci * N_GROUPS + r] = accs[ci * N_GROUPS + r] + wv * a[r]
        return tuple(accs)

    init = tuple(jnp.zeros((SUBLANES, LANES), F32) for _ in range(N_GROUPS * nchunk))
    accs = lax.fori_loop(0, D_MODEL // (SUBLANES * MOD_UNROLL), body, init)
    o_ref[...] = jnp.zeros(o_ref.shape, F32)
    for r in range(N_GROUPS):
        for ci in range(nchunk):
            row = jnp.sum(accs[ci * N_GROUPS + r], axis=0, keepdims=True)
            o_ref[0, r:r + 1, ci * LANES:(ci + 1) * LANES] = row + b_ref[0, :, ci * LANES:(ci + 1) * LANES]


def _modulation(c, c_ctx, w_mod, b_mod):
    cond = jnp.concatenate([c_ctx[None, :], c], axis=0)
    cb = jnp.broadcast_to(cond[:, :, None], (N_GROUPS, D_MODEL, LANES))
    out = pl.pallas_call(
        _mod_kernel,
        grid=(DEPTH, 6 * D_MODEL // MOD_TN),
        in_specs=[
            pl.BlockSpec((N_GROUPS, D_MODEL, LANES), lambda l, j: (0, 0, 0)),
            pl.BlockSpec((1, D_MODEL, MOD_TN), lambda l, j: (l, 0, j)),
            pl.BlockSpec((1, 1, MOD_TN), lambda l, j: (l, 0, j)),
        ],
        out_specs=pl.BlockSpec((1, SUBLANES, MOD_TN), lambda l, j: (l, 0, j)),
        out_shape=jax.ShapeDtypeStruct((DEPTH, SUBLANES, 6 * D_MODEL), F32),
        scratch_shapes=[pltpu.VMEM((N_GROUPS, D_MODEL, LANES), F32)],
        compiler_params=_params(("parallel", "parallel")),
        name="ada_mod",
    )(cb, w_mod, b_mod.reshape(DEPTH, 1, 6 * D_MODEL))
    return out[:, :N_GROUPS].reshape(DEPTH, N_GROUPS, 6, D_MODEL)


def _normed(x, g_ref, m_ref, shift, scale):
    ms = jnp.mean(x * x, axis=-1, keepdims=True)
    y = x * lax.rsqrt(ms + EPS) * g_ref[...]
    return y * (1.0 + m_ref[0, scale:scale + 1, :]) + m_ref[0, shift:shift + 1, :]


def _norm_mm_kernel(x_ref, m_ref, g_ref, w_ref, o_ref, *, tn):
    h = _normed(x_ref[...], g_ref, m_ref, SH1, SC1).astype(BF16)
    for c0 in range(0, o_ref.shape[1], tn):
        o_ref[:, c0:c0 + tn] = jnp.dot(h, w_ref[:, c0:c0 + tn], preferred_element_type=F32)


def _norm_project(x, mods_l, g, w_bf16, tm=512, tn=512):
    nout = w_bf16.shape[1]
    return pl.pallas_call(
        functools.partial(_norm_mm_kernel, tn=tn),
        grid=(N_TOK // tm,),
        in_specs=[
            pl.BlockSpec((tm, D_MODEL), lambda i: (i, 0)),
            pl.BlockSpec((1, 6, D_MODEL), lambda i: (_group_of_block(i, tm), 0, 0)),
            pl.BlockSpec((1, D_MODEL), lambda i: (0, 0)),
            pl.BlockSpec((D_MODEL, nout), lambda i: (0, 0)),
        ],
        out_specs=pl.BlockSpec((tm, nout), lambda i: (i, 0)),
        out_shape=jax.ShapeDtypeStruct((N_TOK, nout), F32),
        compiler_params=_params(("parallel",)),
        name="norm_project",
    )(x, mods_l, g.reshape(1, D_MODEL), w_bf16)


def _proj_res_kernel(*refs, n_in, gate):
    a_refs = refs[:n_in]
    w_refs = refs[n_in:2 * n_in]
    x_ref, m_ref, o_ref = refs[2 * n_in:]
    acc = None
    for a_ref, w_ref in zip(a_refs, w_refs):
        t = jnp.dot(a_ref[...].astype(BF16), w_ref[...], preferred_element_type=F32)
        acc = t if acc is None else acc + t
    o_ref[...] = x_ref[...] + m_ref[0, gate:gate + 1, :] * acc


def _project_residual(acts, ws_bf16, x, mods_l, gate, tm=512):
    n_in = len(acts)
    in_specs = [pl.BlockSpec((tm, a.shape[1]), lambda i: (i, 0)) for a in acts]
    in_specs += [pl.BlockSpec(w.shape, lambda i: (0, 0)) for w in ws_bf16]
    in_specs += [
        pl.BlockSpec((tm, D_MODEL), lambda i: (i, 0)),
        pl.BlockSpec((1, 6, D_MODEL), lambda i: (_group_of_block(i, tm), 0, 0)),
    ]
    return pl.pallas_call(
        functools.partial(_proj_res_kernel, n_in=n_in, gate=gate),
        grid=(N_TOK // tm,),
        in_specs=in_specs,
        out_specs=pl.BlockSpec((tm, D_MODEL), lambda i: (i, 0)),
        out_shape=jax.ShapeDtypeStruct((N_TOK, D_MODEL), F32),
        input_output_aliases={2 * n_in: 0},
        compiler_params=_params(("parallel",)),
        name="project_residual",
    )(*acts, *ws_bf16, x, mods_l)


@functools.lru_cache(maxsize=None)
def _dft_mats(L):
    n = 2 * L
    ft = (np.arange(L, dtype=np.int64)[:, None] * np.arange(L, dtype=np.int64)[None, :]) % n
    ang = ft.astype(np.float64) * (2.0 * np.pi / n)
    return np.cos(ang).astype(np.float32), np.sin(ang).astype(np.float32)


def _dft_bf16(L):
    c, s = _dft_mats(L)
    return jnp.asarray(c).astype(BF16), jnp.asarray(s).astype(BF16)


def _alt_sign(shape, row0):
    t = lax.broadcasted_iota(jnp.int32, shape, 0) + row0
    return (1 - 2 * (t & 1)).astype(F32)


def _filter_dft_kernel(s_ref, d_ref, c_ref, sn_ref, ka_ref, ki_ref, kn_ref, *, L, fb):
    f0 = pl.program_id(0) * fb
    n = 2.0 * L
    s = s_ref[...]
    r = jnp.dot(c_ref[...], s, preferred_element_type=F32)
    im = jnp.dot(sn_ref[...], d_ref[...], preferred_element_type=F32)
    fidx = lax.broadcasted_iota(jnp.int32, r.shape, 0) + f0
    scale = jnp.where(fidx == 0, 1.0 / n, 2.0 / n)
    ka_ref[...] = r * scale
    ki_ref[...] = im * (2.0 / n)
    nyq = jnp.sum(s.astype(F32) * _alt_sign(s.shape, 0), axis=0, keepdims=True) * (1.0 / n)
    kn_ref[...] = jnp.broadcast_to(nyq, kn_ref.shape)


def _filter_spectra(s, d, L, dft):
    fb = min(L, 512)
    cmat, smat = dft
    w = 2 * HY_W
    return pl.pallas_call(
        functools.partial(_filter_dft_kernel, L=L, fb=fb),
        grid=(L // fb,),
        in_specs=[
            pl.BlockSpec((L, w), lambda f: (0, 0)),
            pl.BlockSpec((L, w), lambda f: (0, 0)),
            pl.BlockSpec((fb, L), lambda f: (f, 0)),
            pl.BlockSpec((fb, L), lambda f: (f, 0)),
        ],
        out_specs=[
            pl.BlockSpec((fb, w), lambda f: (f, 0)),
            pl.BlockSpec((fb, w), lambda f: (f, 0)),
            pl.BlockSpec((SUBLANES, w), lambda f: (0, 0)),
        ],
        out_shape=[
            jax.ShapeDtypeStruct((L, w), F32),
            jax.ShapeDtypeStruct((L, w), F32),
            jax.ShapeDtypeStruct((SUBLANES, w), F32),
        ],
        compiler_params=_params(("arbitrary",)),
        name="hyena_filter_dft",
    )(s.astype(BF16), d.astype(BF16), cmat, smat)


HY_TILE = 256


def _short_conv_tile(ref, r0, L, w_ref, b_ref):
    t = HY_TILE
    cur = ref[r0:r0 + t, :]
    rid = lax.broadcasted_iota(jnp.int32, cur.shape, 0)
    if r0 % L == 0:
        prev = jnp.where(rid == 0, 0.0, pltpu.roll(cur, 1, axis=0))
    else:
        prev = ref[r0 - 1:r0 - 1 + t, :]
    if (r0 + t) % L == 0:
        nxt = jnp.where(rid == t - 1, 0.0, pltpu.roll(cur, t - 1, axis=0))
    else:
        nxt = ref[r0 + 1:r0 + 1 + t, :]
    return prev * w_ref[0:1, :] + cur * w_ref[1:2, :] + nxt * w_ref[2:3, :] + b_ref[...]


def _hyena_conv_kernel(*refs, L, nseq, conv_a):
    (a_ref, x_ref, cr_ref, sr_ref, cc_ref, sc_ref, ka_ref, ki_ref, kn_ref,
     wa_ref, ba_ref, wx_ref, bx_ref, hb_ref) = refs[:14]
    o_ref, z_sc, acc_sc = refs[-3:]
    f = pl.program_id(1)
    nf = pl.num_programs(1)
    rows = nseq * L

    @pl.when(f == 0)
    def _():
        for q in range(nseq):
            nyq = jnp.zeros((1, HY_W), F32)
            for r0 in range(q * L, (q + 1) * L, HY_TILE):
                if conv_a:
                    zt = _short_conv_tile(a_ref, r0, L, wa_ref, ba_ref)
                else:
                    zt = a_ref[r0:r0 + HY_TILE, :]
                z_sc[r0:r0 + HY_TILE, :] = zt
                nyq = nyq + jnp.sum(zt * _alt_sign(zt.shape, r0), axis=0, keepdims=True)
            nyq = nyq * kn_ref[0:1, :]
            for r0 in range(q * L, (q + 1) * L, HY_TILE):
                acc_sc[r0:r0 + HY_TILE, :] = _alt_sign((HY_TILE, HY_W), r0) * nyq

    ka = ka_ref[...]
    ki = ki_ref[...]
    for q in range(nseq):
        z = z_sc[q * L:(q + 1) * L, :].astype(BF16)
        a = jnp.dot(cr_ref[...], z, preferred_element_type=F32)
        b = jnp.dot(sr_ref[...], z, preferred_element_type=F32)
        p = (a * ka + b * ki).astype(BF16)
        qq = (b * ka - a * ki).astype(BF16)
        acc_sc[q * L:(q + 1) * L, :] += (jnp.dot(cc_ref[...], p, preferred_element_type=F32)
                                         + jnp.dot(sc_ref[...], qq, preferred_element_type=F32))

    @pl.when(f == nf - 1)
    def _():
        for r0 in range(0, rows, HY_TILE):
            y = acc_sc[r0:r0 + HY_TILE, :] + z_sc[r0:r0 + HY_TILE, :] * hb_ref[...]
            o_ref[r0:r0 + HY_TILE, :] = y * _short_conv_tile(x_ref, r0, L, wx_ref, bx_ref)


def _hyena_conv(a, a_col, x, x_col, row0, n_rows, L, nseq, spectra, filt, wa, ba, wx, bx, hbias, conv_a, dft,
                full_out=False, out_prev=None):
    fb = min(L, 256)
    cmat, smat = dft
    ka, ki, kn = spectra
    rb = nseq * L
    a_off = row0 // rb if a.shape[0] != n_rows else 0
    x_off = row0 // rb
    o_off = row0 // rb if full_out else 0
    extra_specs, extra_args, aliases = [], [], {}
    if out_prev is not None:
        extra_specs, extra_args, aliases = [pl.BlockSpec(memory_space=pl.ANY)], [out_prev], {14: 0}
    return pl.pallas_call(
        functools.partial(_hyena_conv_kernel, L=L, nseq=nseq, conv_a=conv_a),
        grid=(n_rows // rb, L // fb),
        input_output_aliases=aliases,
        in_specs=extra_specs[:0] + [
            pl.BlockSpec((rb, HY_W), lambda i, f: (i + a_off, a_col)),
            pl.BlockSpec((rb, HY_W), lambda i, f: (i + x_off, x_col)),
            pl.BlockSpec((fb, L), lambda i, f: (f, 0)),
            pl.BlockSpec((fb, L), lambda i, f: (f, 0)),
            pl.BlockSpec((L, fb), lambda i, f: (0, f)),
            pl.BlockSpec((L, fb), lambda i, f: (0, f)),
            pl.BlockSpec((fb, HY_W), lambda i, f: (f, filt)),
            pl.BlockSpec((fb, HY_W), lambda i, f: (f, filt)),
            pl.BlockSpec((SUBLANES, HY_W), lambda i, f: (0, filt)),
            pl.BlockSpec((3, HY_W), lambda i, f: (0, 0)),
            pl.BlockSpec((1, HY_W), lambda i, f: (0, 0)),
            pl.BlockSpec((3, HY_W), lambda i, f: (0, 0)),
            pl.BlockSpec((1, HY_W), lambda i, f: (0, 0)),
            pl.BlockSpec((1, HY_W), lambda i, f: (0, 0)),
        ] + extra_specs,
        out_specs=pl.BlockSpec((rb, HY_W), lambda i, f: (i + o_off, 0)),
        out_shape=jax.ShapeDtypeStruct((N_TOK if full_out else n_rows, HY_W), F32),
        scratch_shapes=[pltpu.VMEM((rb, HY_W), F32), pltpu.VMEM((rb, HY_W), F32)],
        compiler_params=_params(("parallel", "arbitrary")),
        name="hyena_conv",
    )(a, x, cmat, smat, cmat, smat, ka, ki, kn, wa, ba, wx, bx, hbias, *extra_args)


def _hyena_filter_taps(L, f1w, f1b, fr1, f2w, f2b, fr2, f3w):
    hp = lax.Precision.HIGHEST
    pos = jnp.arange(L, dtype=F32)
    t = pos / (L - 1)
    w = 2.0 * math.pi * pos / L
    f = jnp.linspace(1e-4, HY_BANDS - 1, HY_BANDS, dtype=F32)
    wf = w[:, None] * f[None, :]
    feat = jnp.concatenate([t[:, None], jnp.cos(wf), -jnp.sin(wf)], axis=-1)
    h = jnp.sin(fr1 * (jnp.dot(feat, f1w, precision=hp) + f1b))
    h = jnp.sin(fr2 * (jnp.dot(h, f2w, precision=hp) + f2b))
    h = jnp.dot(h, f3w, precision=hp).astype(F32)
    deltas = jnp.linspace(math.log(HY_TARGET) / HY_SLOW, math.log(HY_TARGET) / HY_FAST, HY_W, dtype=F32)
    window = jnp.exp(-t[:, None] * jnp.abs(deltas)[None, :])
    w = HY_W
    parts = [h[:, k * w:(k + 1) * w] * window for k in range(4)]
    colsum = [jnp.sum(jnp.abs(p), axis=0, keepdims=True) for p in parts]
    den = [colsum[0] + colsum[1] + EPS, colsum[2] + colsum[3] + EPS]
    fwd = jnp.concatenate([parts[0] / den[0], parts[2] / den[1]], axis=1)
    bwd = jnp.concatenate([parts[1] / den[0], parts[3] / den[1]], axis=1)
    bwd = jnp.where(pos[:, None] == 0, 0.0, bwd)
    return fwd + bwd, bwd - fwd


def _dot_t0(a, b):
    return lax.dot_general(a, b, (((0,), (0,)), ((), ())), preferred_element_type=F32)


def _dot_t1(a, b):
    return lax.dot_general(a, b, (((1,), (1,)), ((), ())), preferred_element_type=F32)


def _retention_kernel(*refs, L, has_s0, has_prev):
    refs = list(refs)
    q_ref, k_ref, v_ref, g_ref, dl_ref = refs[:5]
    pos = 5
    s0_ref = st_ref = None
    if has_s0:
        s0_ref = refs[pos]
        pos += 1
    if has_prev:
        pos += 1
    y_ref = refs[pos]
    pos += 1
    if not has_s0:
        st_ref = refs[pos]
        pos += 1
    sb_sc, sf_cur, sb_cur = refs[pos:]
    c = RET_CHUNK
    nc = L // c
    kscale = RET_HEAD_DIM ** -0.5
    ri = lax.broadcasted_iota(jnp.int32, (c, c), 0).astype(F32)
    ci = lax.broadcasted_iota(jnp.int32, (c, c), 1).astype(F32)
    diff = ri - ci
    dec = []
    for h in range(RET_HEADS):
        xf = dl_ref[0, h:h + 1, :]
        xb = dl_ref[1, h:h + 1, :]
        lgf = jnp.minimum(xf, 0.0) - jnp.log1p(jnp.exp(-jnp.abs(xf)))
        lgb = jnp.minimum(xb, 0.0) - jnp.log1p(jnp.exp(-jnp.abs(xb)))
        dec.append(dict(
            mask=(jnp.where(diff >= 0, jnp.exp(lgf * jnp.maximum(diff, 0.0)), 0.0)
                  + jnp.where(diff <= 0, jnp.exp(lgb * jnp.maximum(-diff, 0.0)), 0.0)),
            qdec_f=jnp.exp(lgf * (ri + 1.0)), kdec_f=jnp.exp(lgf * (c - 1.0 - ri)),
            qdec_b=jnp.exp(lgb * (c - ri)), kdec_b=jnp.exp(lgb * ri),
            cd_f=jnp.exp(lgf * c), cd_b=jnp.exp(lgb * c)))
        if has_s0:
            sf_cur[h] = s0_ref[0, 0, 0, h]
            sb_cur[h] = s0_ref[0, 0, 1, h]
        else:
            sf_cur[h] = jnp.zeros((RET_HEAD_DIM, RET_HEAD_DIM), F32)
            sb_cur[h] = jnp.zeros((RET_HEAD_DIM, RET_HEAD_DIM), F32)

    def bwd_body(i, carry):
        j = nc - 1 - i
        r0 = pl.multiple_of(j * c, c)
        for h in range(RET_HEADS):
            hs = slice(h * RET_HEAD_DIM, (h + 1) * RET_HEAD_DIM)
            sb = sb_cur[h]
            sb_sc[h * nc + j] = sb
            kc = k_ref[pl.ds(r0, c), hs] * kscale
            vc = v_ref[pl.ds(r0, c), hs]
            sb_cur[h] = sb * dec[h]["cd_b"] + _dot_t0((kc * dec[h]["kdec_b"]).astype(BF16), vc.astype(BF16))
        return carry

    lax.fori_loop(0, nc, bwd_body, 0)

    def fwd_body(j, carry):
        r0 = pl.multiple_of(j * c, c)
        for h in range(RET_HEADS):
            hs = slice(h * RET_HEAD_DIM, (h + 1) * RET_HEAD_DIM)
            dh = dec[h]
            sf = sf_cur[h]
            qc = q_ref[pl.ds(r0, c), hs]
            kc = k_ref[pl.ds(r0, c), hs] * kscale
            vc = v_ref[pl.ds(r0, c), hs].astype(BF16)
            scores = _dot_t1(qc.astype(BF16), kc.astype(BF16)) * dh["mask"]
            o = jnp.dot(scores.astype(BF16), vc, preferred_element_type=F32)
            o = o + jnp.dot((qc * dh["qdec_f"]).astype(BF16), sf.astype(BF16), preferred_element_type=F32)
            o = o + jnp.dot((qc * dh["qdec_b"]).astype(BF16), sb_sc[h * nc + j].astype(BF16),
                            preferred_element_type=F32)
            mu = jnp.mean(o, axis=-1, keepdims=True)
            var = jnp.mean(jnp.square(o - mu), axis=-1, keepdims=True)
            on = (o - mu) * lax.rsqrt(var + EPS)
            gc = g_ref[pl.ds(r0, c), hs]
            y_ref[pl.ds(r0, c), hs] = gc * (1.0 / (1.0 + jnp.exp(-gc))) * on
            sf_cur[h] = sf * dh["cd_f"] + _dot_t0((kc * dh["kdec_f"]).astype(BF16), vc)
        return carry

    lax.fori_loop(0, nc, fwd_body, 0)
    if st_ref is not None:
        for h in range(RET_HEADS):
            st_ref[0, 0, 0, h] = sf_cur[h]
            st_ref[0, 0, 1, h] = sb_cur[h]


def _retention(proj, dl, layer_i, row0, nseq, L, s0=None, st_prev=None, y_prev=None):
    off = row0 // L
    has_s0 = s0 is not None
    has_prev = (st_prev is not None) or (y_prev is not None)
    assert not (st_prev is not None and y_prev is not None)
    nc = L // RET_CHUNK
    n_ret = (DEPTH + 1) // 2
    col = lambda j: pl.BlockSpec((L, RET_W), lambda b: (b + off, 3 + j))
    in_specs = [col(0), col(1), col(2), col(3),
                pl.BlockSpec((2, RET_HEADS, LANES), lambda b: (0, 0, 0))]
    args = [proj, proj, proj, proj, dl]
    y_spec = pl.BlockSpec((L, RET_W), lambda b: (b + off, 0))
    y_shape = jax.ShapeDtypeStruct((N_TOK, RET_W), F32)
    st_block = (1, 1, 2, RET_HEADS, RET_HEAD_DIM, RET_HEAD_DIM)
    st_spec = pl.BlockSpec(st_block, lambda b: (b, layer_i, 0, 0, 0, 0))
    aliases = {}
    if has_s0:
        in_specs.append(st_spec)
        args.append(s0)
        out_specs, out_shape = y_spec, y_shape
    else:
        out_specs = [y_spec, st_spec]
        out_shape = [y_shape, jax.ShapeDtypeStruct((nseq, n_ret) + st_block[2:], F32)]
    if has_prev:
        aliases = {len(args): 1 if st_prev is not None else 0}
        in_specs.append(pl.BlockSpec(memory_space=pl.ANY))
        args.append(st_prev if st_prev is not None else y_prev)
    state = pltpu.VMEM((RET_HEADS, RET_HEAD_DIM, RET_HEAD_DIM), F32)
    return pl.pallas_call(
        functools.partial(_retention_kernel, L=L, has_s0=has_s0, has_prev=has_prev),
        grid=(nseq,),
        in_specs=in_specs,
        out_specs=out_specs,
        out_shape=out_shape,
        input_output_aliases=aliases,
        scratch_shapes=[pltpu.VMEM((RET_HEADS * nc, RET_HEAD_DIM, RET_HEAD_DIM), F32), state, state],
        compiler_params=_params(("parallel",)),
        name="retention",
    )(*args)


@functools.lru_cache(maxsize=None)
def _rope_tables():
    L = DEC_SEQ
    rows = L // GRID_W
    row = np.repeat(np.arange(rows, dtype=np.float64), GRID_W)
    col = np.tile(np.arange(GRID_W, dtype=np.float64), rows)
    quarter = DIFF_HEAD_DIM // 4
    freqs = ROPE_BASE ** (-np.arange(quarter, dtype=np.float64) / quarter)
    j = np.arange(LANES)
    pos = np.where(((j % DIFF_HEAD_DIM) < DIFF_HEAD_DIM // 2)[None, :], row[:, None], col[:, None])
    ang = pos * freqs[j % quarter][None, :]
    cos = np.cos(ang).astype(np.float32)
    sin = np.sin(ang).astype(np.float32)
    first = ((j % (2 * quarter)) < quarter)[None, :]
    sin_a = np.where(first, -sin, 0.0).astype(np.float32)
    sin_b = np.where(first, 0.0, sin).astype(np.float32)
    return jnp.asarray(cos), jnp.asarray(sin_a), jnp.asarray(sin_b)


def _rope_head(x, cos, sin_a, sin_b):
    quarter = DIFF_HEAD_DIM // 4
    up = pltpu.roll(x, LANES - quarter, axis=1)
    dn = pltpu.roll(x, quarter, axis=1)
    return x * cos + up * sin_a + dn * sin_b


def _kv_prep_kernel(q_ref, k_ref, v_ref, cos_ref, sa_ref, sb_ref, qo_ref, ko_ref, vo_ref):
    cos, sa, sb = cos_ref[...], sa_ref[...], sb_ref[...]
    for h in range(DIFF_HEADS):
        hs = slice(h * LANES, (h + 1) * LANES)
        qo_ref[:, hs] = _rope_head(q_ref[:, hs], cos, sa, sb).astype(BF16)
        ko_ref[0, :, hs] = _rope_head(k_ref[:, hs], cos, sa, sb).astype(BF16)
    vo_ref[0] = v_ref[...].astype(BF16)


def _cache_copy_kernel(ck_ref, cv_ref, k_in, v_in, ko_ref, vo_ref):
    del k_in, v_in
    ko_ref[0] = ck_ref[0, 0].astype(BF16)
    vo_ref[0] = cv_ref[0, 0].astype(BF16)


def _sample_qkv(proj, cache_k, cache_v, layer_i, tm=256):
    cos, sa, sb = _rope_tables()
    lk = PAST_LEN + DEC_SEQ
    pblk = N_PROMPT // tm
    nblk = DEC_SEQ // tm
    cblk = PAST_LEN // tm
    tab = pl.BlockSpec((tm, LANES), lambda b, i: (i, 0))
    q, k, v = pl.pallas_call(
        _kv_prep_kernel,
        grid=(DEC_BATCH, nblk),
        in_specs=[
            pl.BlockSpec((tm, D_MODEL), lambda b, i: (pblk + b * nblk + i, 0)),
            pl.BlockSpec((tm, D_MODEL), lambda b, i: (pblk + b * nblk + i, 1)),
            pl.BlockSpec((tm, D_MODEL), lambda b, i: (pblk + b * nblk + i, 2)),
            tab, tab, tab,
        ],
        out_specs=[
            pl.BlockSpec((tm, D_MODEL), lambda b, i: (b * nblk + i, 0)),
            pl.BlockSpec((1, tm, D_MODEL), lambda b, i: (b, cblk + i, 0)),
            pl.BlockSpec((1, tm, D_MODEL), lambda b, i: (b, cblk + i, 0)),
        ],
        out_shape=[
            jax.ShapeDtypeStruct((N_SAMPLE, D_MODEL), BF16),
            jax.ShapeDtypeStruct((DEC_BATCH, lk, D_MODEL), BF16),
            jax.ShapeDtypeStruct((DEC_BATCH, lk, D_MODEL), BF16),
        ],
        compiler_params=_params(("parallel", "parallel")),
        name="rope_qkv",
    )(proj, proj, proj, cos, sa, sb)
    n_att = DEPTH // 2
    ck = cache_k.reshape(DEC_BATCH, n_att, PAST_LEN, D_MODEL)
    cv = cache_v.reshape(DEC_BATCH, n_att, PAST_LEN, D_MODEL)
    k, v = pl.pallas_call(
        _cache_copy_kernel,
        grid=(DEC_BATCH,),
        in_specs=[
            pl.BlockSpec((1, 1, PAST_LEN, D_MODEL), lambda b: (b, layer_i, 0, 0)),
            pl.BlockSpec((1, 1, PAST_LEN, D_MODEL), lambda b: (b, layer_i, 0, 0)),
            pl.BlockSpec(memory_space=pl.ANY),
            pl.BlockSpec(memory_space=pl.ANY),
        ],
        out_specs=[
            pl.BlockSpec((1, PAST_LEN, D_MODEL), lambda b: (b, 0, 0)),
            pl.BlockSpec((1, PAST_LEN, D_MODEL), lambda b: (b, 0, 0)),
        ],
        out_shape=[
            jax.ShapeDtypeStruct((DEC_BATCH, lk, D_MODEL), BF16),
            jax.ShapeDtypeStruct((DEC_BATCH, lk, D_MODEL), BF16),
        ],
        input_output_aliases={2: 0, 3: 1},
        compiler_params=_params(("parallel",)),
        name="cache_prepend",
    )(ck, cv, k, v)
    return q, k, v


def _diff_attn_kernel(*refs, lam_init, batched_kv):
    q_ref, k_ref, v_ref, lam_ref, sg_ref = refs[:5]
    o_ref = refs[-1]
    lv = lam_ref[...]
    lam = (jnp.exp(jnp.sum(lv[0:1] * lv[1:2], axis=-1, keepdims=True))
           - jnp.exp(jnp.sum(lv[2:3] * lv[3:4], axis=-1, keepdims=True)) + lam_init)
    lane = lax.broadcasted_iota(jnp.int32, (1, LANES), 1)
    m1 = (lane < DIFF_HEAD_DIM).astype(F32)
    m2 = 1.0 - m1
    scale = DIFF_HEAD_DIM ** -0.5
    for h in range(DIFF_HEADS):
        hs = slice(h * LANES, (h + 1) * LANES)
        q = q_ref[:, hs].astype(F32) * scale
        if batched_kv:
            k = k_ref[0, :, hs].astype(BF16)
            v = v_ref[0, :, hs].astype(BF16)
        else:
            k = k_ref[:, hs].astype(BF16)
            v = v_ref[:, hs].astype(BF16)
        outs = []
        for m in (m1, m2):
            s = _dot_t1((q * m).astype(BF16), k)
            s = s - jnp.max(s, axis=-1, keepdims=True)
            p = jnp.exp(s)
            l = jnp.sum(p, axis=-1, keepdims=True)
            outs.append(jnp.dot(p.astype(BF16), v, preferred_element_type=F32) / l)
        o = outs[0] - lam * outs[1]
        ms = jnp.mean(o * o, axis=-1, keepdims=True)
        o_ref[:, hs] = o * lax.rsqrt(ms + EPS) * sg_ref[...] * (1.0 - lam_init)


def _diff_attention(q, k, v, lam_vec, subln, lam_init, *, nb, lq, lk, tq, q_row0, q_col, kv_cols, batched_kv,
                    out_row0=0, out_prev=None):
    nq = lq // tq
    qoff = q_row0 // tq
    ooff = out_row0 // tq
    extra_specs, extra_args, aliases = [], [], {}
    if out_prev is not None:
        extra_specs, extra_args, aliases = [pl.BlockSpec(memory_space=pl.ANY)], [out_prev], {5: 0}
    q_spec = pl.BlockSpec((tq, D_MODEL), lambda b, i: (qoff + b * nq + i, q_col))
    if batched_kv:
        k_spec = pl.BlockSpec((1, lk, D_MODEL), lambda b, i: (b, 0, 0))
        v_spec = k_spec
    else:
        k_spec = pl.BlockSpec((lk, D_MODEL), lambda b, i: (b, kv_cols[0]))
        v_spec = pl.BlockSpec((lk, D_MODEL), lambda b, i: (b, kv_cols[1]))
    return pl.pallas_call(
        functools.partial(_diff_attn_kernel, lam_init=lam_init, batched_kv=batched_kv),
        grid=(nb, nq),
        in_specs=[q_spec, k_spec, v_spec,
                  pl.BlockSpec((4, LANES), lambda b, i: (0, 0)),
                  pl.BlockSpec((1, LANES), lambda b, i: (0, 0))] + extra_specs,
        out_specs=pl.BlockSpec((tq, D_MODEL), lambda b, i: (ooff + b * nq + i, 0)),
        out_shape=jax.ShapeDtypeStruct((N_TOK, D_MODEL), F32),
        input_output_aliases=aliases,
        compiler_params=_params(("parallel", "arbitrary")),
        name="diff_attention",
    )(q, k, v, lam_vec, subln, *extra_args)


def _router_kernel(x_ref, m_ref, g_ref, wr_ref, h_ref, aff_ref):
    h = _normed(x_ref[...], g_ref, m_ref, SH2, SC2)
    hb = h.astype(BF16)
    h_ref[...] = hb
    hl = (h - hb.astype(F32)).astype(BF16)
    wr = wr_ref[...]
    wh = wr.astype(BF16)
    wl = (wr - wh.astype(F32)).astype(BF16)
    logits = (jnp.dot(hb, wh, preferred_element_type=F32) + jnp.dot(hl, wh, preferred_element_type=F32)
              + jnp.dot(hb, wl, preferred_element_type=F32))
    lt = jnp.transpose(logits)[0:N_EXPERTS, :]
    lt = lt - jnp.max(lt, axis=0, keepdims=True)
    e = jnp.exp(lt)
    aff_ref[...] = e / jnp.sum(e, axis=0, keepdims=True)


def _router(x, mods_l, g, w_router, tm=512):
    wr = jnp.pad(w_router, ((0, 0), (0, LANES - N_EXPERTS)))
    return pl.pallas_call(
        _router_kernel,
        grid=(N_TOK // tm,),
        in_specs=[
            pl.BlockSpec((tm, D_MODEL), lambda i: (i, 0)),
            pl.BlockSpec((1, 6, D_MODEL), lambda i: (_group_of_block(i, tm), 0, 0)),
            pl.BlockSpec((1, D_MODEL), lambda i: (0, 0)),
            pl.BlockSpec((D_MODEL, LANES), lambda i: (0, 0)),
        ],
        out_specs=[
            pl.BlockSpec((tm, D_MODEL), lambda i: (i, 0)),
            pl.BlockSpec((N_EXPERTS, tm), lambda i: (0, i)),
        ],
        out_shape=[
            jax.ShapeDtypeStruct((N_TOK, D_MODEL), BF16),
            jax.ShapeDtypeStruct((N_EXPERTS, N_TOK), F32),
        ],
        compiler_params=_params(("parallel",)),
        name="norm_router",
    )(x, mods_l, g.reshape(1, D_MODEL), wr)


FFN_TF = 512
FFN_TR = 512


def _expert_ffn_kernel(xs_ref, wg_ref, wu_ref, wd_ref, gate_ref, o_ref):
    f = pl.program_id(1)
    wg = wg_ref[0].astype(BF16)
    wu = wu_ref[0].astype(BF16)
    wd = wd_ref[0].astype(BF16)
    for r0 in range(0, CAP_T, FFN_TR):
        rs = slice(r0, r0 + FFN_TR)
        xs = xs_ref[0, rs, :]
        a = jnp.dot(xs, wg, preferred_element_type=F32)
        u = jnp.dot(xs, wu, preferred_element_type=F32)
        hid = (a * (1.0 / (1.0 + jnp.exp(-a))) * u).astype(BF16)
        y = jnp.dot(hid, wd, preferred_element_type=F32)

        @pl.when(f == 0)
        def _():
            o_ref[0, rs, :] = y

        @pl.when(jnp.logical_and(f > 0, f < pl.num_programs(1) - 1))
        def _():
            o_ref[0, rs, :] += y

        @pl.when(f == pl.num_programs(1) - 1)
        def _():
            o_ref[0, rs, :] = (o_ref[0, rs, :] + y) * gate_ref[0, rs, :]


def _expert_ffn(xs, wg, wu, wd, gate, l):
    return pl.pallas_call(
        _expert_ffn_kernel,
        grid=(N_EXPERTS, EXPERT_FF // FFN_TF),
        in_specs=[
            pl.BlockSpec((1, CAP_T, D_MODEL), lambda e, f: (e, 0, 0)),
            pl.BlockSpec((None, 1, D_MODEL, FFN_TF), lambda e, f: (l, e, 0, f)),
            pl.BlockSpec((None, 1, D_MODEL, FFN_TF), lambda e, f: (l, e, 0, f)),
            pl.BlockSpec((None, 1, FFN_TF, D_MODEL), lambda e, f: (l, e, f, 0)),
            pl.BlockSpec((1, CAP_T, 1), lambda e, f: (e, 0, 0)),
        ],
        out_specs=pl.BlockSpec((1, CAP_T, D_MODEL), lambda e, f: (e, 0, 0)),
        out_shape=jax.ShapeDtypeStruct((N_EXPERTS, CAP_T, D_MODEL), F32),
        compiler_params=_params(("parallel", "arbitrary")),
        name="expert_ffn",
    )(xs, wg, wu, wd, gate)


def _final_norm_kernel(x_ref, g_ref, o_ref):
    x = x_ref[...]
    ms = jnp.mean(x * x, axis=-1, keepdims=True)
    o_ref[...] = x * lax.rsqrt(ms + EPS) * g_ref[...]


def _final_norm(x, g, row0, n_rows, tm=512):
    off = row0 // tm
    return pl.pallas_call(
        _final_norm_kernel,
        grid=(n_rows // tm,),
        in_specs=[pl.BlockSpec((tm, D_MODEL), lambda i: (i + off, 0)),
                  pl.BlockSpec((1, D_MODEL), lambda i: (0, 0))],
        out_specs=pl.BlockSpec((tm, D_MODEL), lambda i: (i, 0)),
        out_shape=jax.ShapeDtypeStruct((n_rows, D_MODEL), F32),
        compiler_params=_params(("parallel",)),
        name="final_norm",
    )(x, g.reshape(1, D_MODEL))


def _even_layer(x, mods_l, i, state_ret, st_prev, dfts, norm1_g, w_in_even, hy_short_w, hy_short_b, hy_f1_w, hy_f1_b,
                hy_freq1, hy_f2_w, hy_f2_b, hy_freq2, hy_f3_w, hy_bias, ret_decay, w_out_even):
    proj = _norm_project(x, mods_l, norm1_g, w_in_even[i].astype(BF16))
    sw, sbias = hy_short_w[i], hy_short_b[i].reshape(1, 3 * HY_W)
    hb = hy_bias[i]
    y_hy = None
    for (row0, n_rows, L, nseq) in ((0, N_PROMPT, SEQ, 8), (N_PROMPT, N_SAMPLE, DEC_SEQ, 1)):
        s, d = _hyena_filter_taps(L, hy_f1_w[i], hy_f1_b[i], hy_freq1[i], hy_f2_w[i], hy_f2_b[i], hy_freq2[i], hy_f3_w[i])
        spectra = _filter_spectra(s, d, L, dfts[L])
        wv, bv = sw[:, 0:HY_W], sbias[:, 0:HY_W]
        w1, b1 = sw[:, HY_W:2 * HY_W], sbias[:, HY_W:2 * HY_W]
        w2, b2 = sw[:, 2 * HY_W:], sbias[:, 2 * HY_W:]
        z1 = _hyena_conv(proj, 0, proj, 1, row0, n_rows, L, nseq, spectra, 0, wv, bv, w1, b1, hb[0:1], True, dfts[L])
        y_hy = _hyena_conv(z1, 0, proj, 2, row0, n_rows, L, nseq, spectra, 1, wv, bv, w2, b2, hb[1:2], False, dfts[L],
                           full_out=True, out_prev=y_hy)
    dl = jnp.broadcast_to(ret_decay[i].astype(F32)[:, :, None], (2, RET_HEADS, LANES))
    y_ret, st = _retention(proj, dl, i, 0, BATCH, SEQ, st_prev=st_prev)
    y_ret = _retention(proj, dl, i, N_PROMPT, DEC_BATCH, DEC_SEQ, s0=state_ret, y_prev=y_ret)
    wo = w_out_even[i].astype(BF16)
    x = _project_residual([y_hy, y_ret], [wo[:HY_W], wo[HY_W:]], x, mods_l, G1)
    return x, st


def _cache_write_kernel(*refs):
    k_ref, v_ref = refs[:2]
    ko_ref, vo_ref = refs[-2:]
    ko_ref[0, 0] = k_ref[...]
    vo_ref[0, 0] = v_ref[...]


def _cache_write(proj, layer_i, prev):
    shape = jax.ShapeDtypeStruct((BATCH, DEPTH // 2, SEQ, D_MODEL), F32)
    in_specs = [pl.BlockSpec((SEQ, D_MODEL), lambda b: (b, 1)), pl.BlockSpec((SEQ, D_MODEL), lambda b: (b, 2))]
    args = [proj, proj]
    aliases = {}
    if prev is not None:
        in_specs += [pl.BlockSpec(memory_space=pl.ANY), pl.BlockSpec(memory_space=pl.ANY)]
        args += list(prev)
        aliases = {2: 0, 3: 1}
    out_spec = pl.BlockSpec((1, 1, SEQ, D_MODEL), lambda b: (b, layer_i, 0, 0))
    return pl.pallas_call(
        _cache_write_kernel,
        grid=(BATCH,),
        in_specs=in_specs,
        out_specs=[out_spec, out_spec],
        out_shape=[shape, shape],
        input_output_aliases=aliases,
        compiler_params=_params(("parallel",)),
        name="cache_write",
    )(*args)


def _odd_layer(x, mods_l, l, i, cache_k, cache_v, kv_prev, norm1_g, w_in_odd, lam_q1, lam_k1, lam_q2, lam_k2, subln_g,
               w_out_odd):
    lam_init = 0.8 - 0.6 * math.exp(-0.3 * l)
    proj = _norm_project(x, mods_l, norm1_g, w_in_odd[i].astype(BF16))
    lam_vec = jnp.pad(jnp.stack([lam_q1[i], lam_k1[i], lam_q2[i], lam_k2[i]]), ((0, 0), (0, LANES - DIFF_HEAD_DIM)))
    sg = subln_g[i].reshape(1, LANES)
    o_p = _diff_attention(proj, proj, proj, lam_vec, sg, lam_init, nb=BATCH, lq=SEQ, lk=SEQ, tq=SEQ,
                          q_row0=0, q_col=0, kv_cols=(1, 2), batched_kv=False)
    qs, ks, vs = _sample_qkv(proj, cache_k, cache_v, i)
    o = _diff_attention(qs, ks, vs, lam_vec, sg, lam_init, nb=DEC_BATCH, lq=DEC_SEQ, lk=PAST_LEN + DEC_SEQ, tq=256,
                        q_row0=0, q_col=0, kv_cols=None, batched_kv=True, out_row0=N_PROMPT, out_prev=o_p)
    x = _project_residual([o], [w_out_odd[i].astype(BF16)], x, mods_l, G1)
    return x, _cache_write(proj, i, kv_prev)


def _moe_layer(x, mods_l, l, norm2_g, w_router, wg, wu, wd):
    h, aff = _router(x, mods_l, norm2_g, w_router)
    gate_p, idx_p = lax.top_k(aff[:, :N_PROMPT], CAP_P)
    gate_s, idx_s = lax.top_k(aff[:, N_PROMPT:], CAP_S)
    idx = jnp.concatenate([idx_p, idx_s + N_PROMPT], axis=1)
    gate = jnp.concatenate([gate_p, gate_s], axis=1)
    xs = h[idx]
    out = _expert_ffn(xs, wg, wu, wd, gate[:, :, None], l)
    f = jnp.zeros((N_TOK, D_MODEL), F32).at[idx.reshape(-1)].add(out.reshape(-1, D_MODEL))
    gate2 = jnp.concatenate([jnp.broadcast_to(mods_l[0, G2][None], (N_PROMPT, D_MODEL))]
                            + [jnp.broadcast_to(mods_l[1 + b, G2][None], (DEC_SEQ, D_MODEL)) for b in range(DEC_BATCH)])
    return x + gate2 * f


def kernel(x_prompt, x_sample, state_ret, cache_k, cache_v, c, c_ctx, w_mod, b_mod, norm1_g, norm2_g, w_in_even, hy_short_w, hy_short_b, hy_f1_w, hy_f1_b, hy_freq1, hy_f2_w, hy_f2_b, hy_freq2, hy_f3_w, hy_bias, ret_decay, w_out_even, w_in_odd, lam_q1, lam_k1, lam_q2, lam_k2, subln_g, w_out_odd, moe_router, moe_wg, moe_wu, moe_wd, final_g):
    x = jnp.concatenate([x_prompt.reshape(N_PROMPT, D_MODEL), x_sample.reshape(N_SAMPLE, D_MODEL)], axis=0)
    mods = _modulation(c, c_ctx, w_mod, b_mod)
    dfts = {L: _dft_bf16(L) for L in (SEQ, DEC_SEQ)}
    state_ret = state_ret.astype(F32)
    st = kv = None
    for l in range(DEPTH):
        i = l // 2
        if l % 2 == 0:
            x, st = _even_layer(x, mods[l], i, state_ret, st, dfts, norm1_g[l], w_in_even, hy_short_w, hy_short_b,
                                hy_f1_w, hy_f1_b, hy_freq1, hy_f2_w, hy_f2_b, hy_freq2, hy_f3_w, hy_bias, ret_decay,
                                w_out_even)
        else:
            x, kv = _odd_layer(x, mods[l], l, i, cache_k, cache_v, kv, norm1_g[l], w_in_odd, lam_q1, lam_k1, lam_q2,
                               lam_k2, subln_g, w_out_odd)
        x = _moe_layer(x, mods[l], l, norm2_g[l], moe_router[l], moe_wg, moe_wu, moe_wd)
    y_prompt = _final_norm(x, final_g, 0, N_PROMPT).reshape(BATCH, SEQ, D_MODEL)
    y_sample = _final_norm(x, final_g, N_PROMPT, N_SAMPLE).reshape(DEC_BATCH, DEC_SEQ, D_MODEL)
    cache_shape = (BATCH, DEPTH // 2, SEQ, DIFF_HEADS, 2 * DIFF_HEAD_DIM)
    return (y_prompt, y_sample, st, kv[0].reshape(cache_shape), kv[1].reshape(cache_shape))
```

```python
import functools
import math

import numpy as np
import jax
import jax.numpy as jnp
from jax import lax
from jax.experimental import pallas as pl
from jax.experimental.pallas import tpu as pltpu

F32 = jnp.float32
BF16 = jnp.bfloat16

D_MODEL = 1024
BATCH = 32
SEQ = 256
DEPTH = 4
DEC_BATCH = 2
DEC_SEQ = 2048
PAST_LEN = 256
GRID_W = 64
HY_W = 512
HY_EMB = 33
HY_BANDS = 16
HY_FF = 64
HY_TARGET = 1e-2
HY_FAST = 0.3
HY_SLOW = 1.5
RET_W = 512
RET_HEADS = 4
RET_HEAD_DIM = 128
RET_CHUNK = 128
DIFF_HEADS = 8
DIFF_HEAD_DIM = 64
ROPE_BASE = 10000.0
N_EXPERTS = 16
EC_FACTOR = 2
EXPERT_FF = 1024
EVEN_IN = 3 * HY_W + 4 * RET_W
EPS = 1e-6

N_PROMPT = BATCH * SEQ
N_SAMPLE = DEC_BATCH * DEC_SEQ
N_TOK = N_PROMPT + N_SAMPLE
N_GROUPS = 1 + DEC_BATCH
CAP_P = EC_FACTOR * N_PROMPT // N_EXPERTS
CAP_S = EC_FACTOR * N_SAMPLE // N_EXPERTS
CAP_T = CAP_P + CAP_S

LANES = 128
SUBLANES = 8
VMEM_LIMIT = 56 * 1024 * 1024

SH1, SC1, G1, SH2, SC2, G2 = range(6)


def _params(sem, vmem=VMEM_LIMIT):
    return pltpu.CompilerParams(dimension_semantics=sem, vmem_limit_bytes=vmem)


def _group_of_block(i, tm):
    pb = N_PROMPT // tm
    return jnp.where(i < pb, 0, 1 + (i - pb) // (DEC_SEQ // tm))


MOD_TN = 1024


MOD_UNROLL = 4


def _mod_kernel(cb_ref, w_ref, b_ref, o_ref, a_sc):
    nchunk = MOD_TN // LANES
    cv = cb_ref[...]
    a_sc[...] = cv * (1.0 / (1.0 + jnp.exp(-cv)))

    def body(kb, accs):
        accs = list(accs)
        for u in range(MOD_UNROLL):
            k0 = pl.multiple_of((kb * MOD_UNROLL + u) * SUBLANES, SUBLANES)
            a = [a_sc[r, pl.ds(k0, SUBLANES), :] for r in range(N_GROUPS)]
            for ci in range(nchunk):
                wv = w_ref[0, pl.ds(k0, SUBLANES), ci * LANES:(ci + 1) * LANES]
                for r in range(N_GROUPS):
                    accs[ci * N_GROUPS + r] = accs[ci * N_GROUPS + r] + wv * a[r]
        return tuple(accs)

    init = tuple(jnp.zeros((SUBLANES, LANES), F32) for _ in range(N_GROUPS * nchunk))
    accs = lax.fori_loop(0, D_MODEL // (SUBLANES * MOD_UNROLL), body, init)
    o_ref[...] = jnp.zeros(o_ref.shape, F32)
    for r in range(N_GROUPS):
        for ci in range(nchunk):
            row = jnp.sum(accs[ci * N_GROUPS + r], axis=0, keepdims=True)
            o_ref[0, r:r + 1, ci * LANES:(ci + 1) * LANES] = row + b_ref[0, :, ci * LANES:(ci + 1) * LANES]


def _modulation(c, c_ctx, w_mod, b_mod):
    cond = jnp.concatenate([c_ctx[None, :], c], axis=0)
    cb = jnp.broadcast_to(cond[:, :, None], (N_GROUPS, D_MODEL, LANES))
    out = pl.pallas_call(
        _mod_kernel,
        grid=(DEPTH, 6 * D_MODEL // MOD_TN),
        in_specs=[
            pl.BlockSpec((N_GROUPS, D_MODEL, LANES), lambda l, j: (0, 0, 0)),
            pl.BlockSpec((1, D_MODEL, MOD_TN), lambda l, j: (l, 0, j)),
            pl.BlockSpec((1, 1, MOD_TN), lambda l, j: (l, 0, j)),
        ],
        out_specs=pl.BlockSpec((1, SUBLANES, MOD_TN), lambda l, j: (l, 0, j)),
        out_shape=jax.ShapeDtypeStruct((DEPTH, SUBLANES, 6 * D_MODEL), F32),
        scratch_shapes=[pltpu.VMEM((N_GROUPS, D_MODEL, LANES), F32)],
        compiler_params=_params(("parallel", "parallel")),
        name="ada_mod",
    )(cb, w_mod, b_mod.reshape(DEPTH, 1, 6 * D_MODEL))
    return out[:, :N_GROUPS].reshape(DEPTH, N_GROUPS, 6, D_MODEL)


def _normed(x, g_ref, m_ref, shift, scale):
    ms = jnp.mean(x * x, axis=-1, keepdims=True)
    y = x * lax.rsqrt(ms + EPS) * g_ref[...]
    return y * (1.0 + m_ref[0, scale:scale + 1, :]) + m_ref[0, shift:shift + 1, :]


def _norm_mm_kernel(x_ref, m_ref, g_ref, w_ref, o_ref, *, tn):
    h = _normed(x_ref[...], g_ref, m_ref, SH1, SC1).astype(BF16)
    for c0 in range(0, o_ref.shape[1], tn):
        o_ref[:, c0:c0 + tn] = jnp.dot(h, w_ref[:, c0:c0 + tn], preferred_element_type=F32)


def _norm_project(x, mods_l, g, w_bf16, tm=512, tn=512):
    nout = w_bf16.shape[1]
    return pl.pallas_call(
        functools.partial(_norm_mm_kernel, tn=tn),
        grid=(N_TOK // tm,),
        in_specs=[
            pl.BlockSpec((tm, D_MODEL), lambda i: (i, 0)),
            pl.BlockSpec((1, 6, D_MODEL), lambda i: (_group_of_block(i, tm), 0, 0)),
            pl.BlockSpec((1, D_MODEL), lambda i: (0, 0)),
            pl.BlockSpec((D_MODEL, nout), lambda i: (0, 0)),
        ],
        out_specs=pl.BlockSpec((tm, nout), lambda i: (i, 0)),
        out_shape=jax.ShapeDtypeStruct((N_TOK, nout), F32),
        compiler_params=_params(("parallel",)),
        name="norm_project",
    )(x, mods_l, g.reshape(1, D_MODEL), w_bf16)


def _proj_res_kernel(*refs, n_in, gate):
    a_refs = refs[:n_in]
    w_refs = refs[n_in:2 * n_in]
    x_ref, m_ref, o_ref = refs[2 * n_in:]
    acc = None
    for a_ref, w_ref in zip(a_refs, w_refs):
        t = jnp.dot(a_ref[...].astype(BF16), w_ref[...], preferred_element_type=F32)
        acc = t if acc is None else acc + t
    o_ref[...] = x_ref[...] + m_ref[0, gate:gate + 1, :] * acc


def _project_residual(acts, ws_bf16, x, mods_l, gate, tm=512):
    n_in = len(acts)
    in_specs = [pl.BlockSpec((tm, a.shape[1]), lambda i: (i, 0)) for a in acts]
    in_specs += [pl.BlockSpec(w.shape, lambda i: (0, 0)) for w in ws_bf16]
    in_specs += [
        pl.BlockSpec((tm, D_MODEL), lambda i: (i, 0)),
        pl.BlockSpec((1, 6, D_MODEL), lambda i: (_group_of_block(i, tm), 0, 0)),
    ]
    return pl.pallas_call(
        functools.partial(_proj_res_kernel, n_in=n_in, gate=gate),
        grid=(N_TOK // tm,),
        in_specs=in_specs,
        out_specs=pl.BlockSpec((tm, D_MODEL), lambda i: (i, 0)),
        out_shape=jax.ShapeDtypeStruct((N_TOK, D_MODEL), F32),
        input_output_aliases={2 * n_in: 0},
        compiler_params=_params(("parallel",)),
        name="project_residual",
    )(*acts, *ws_bf16, x, mods_l)


@functools.lru_cache(maxsize=None)
def _dft_mats(L):
    n = 2 * L
    ft = (np.arange(L, dtype=np.int64)[:, None] * np.arange(L, dtype=np.int64)[None, :]) % n
    ang = ft.astype(np.float64) * (2.0 * np.pi / n)
    return np.cos(ang).astype(np.float32), np.sin(ang).astype(np.float32)


def _dft_bf16(L):
    c, s = _dft_mats(L)
    return jnp.asarray(c).astype(BF16), jnp.asarray(s).astype(BF16)


def _alt_sign(shape, row0):
    t = lax.broadcasted_iota(jnp.int32, shape, 0) + row0
    return (1 - 2 * (t & 1)).astype(F32)


def _filter_dft_kernel(s_ref, d_ref, c_ref, sn_ref, ka_ref, ki_ref, kn_ref, *, L, fb):
    f0 = pl.program_id(0) * fb
    n = 2.0 * L
    s = s_ref[...]
    r = jnp.dot(c_ref[...], s, preferred_element_type=F32)
    im = jnp.dot(sn_ref[...], d_ref[...], preferred_element_type=F32)
    fidx = lax.broadcasted_iota(jnp.int32, r.shape, 0) + f0
    scale = jnp.where(fidx == 0, 1.0 / n, 2.0 / n)
    ka_ref[...] = r * scale
    ki_ref[...] = im * (2.0 / n)
    nyq = jnp.sum(s.astype(F32) * _alt_sign(s.shape, 0), axis=0, keepdims=True) * (1.0 / n)
    kn_ref[...] = jnp.broadcast_to(nyq, kn_ref.shape)


def _filter_spectra(s, d, L, dft):
    fb = min(L, 512)
    cmat, smat = dft
    w = 2 * HY_W
    return pl.pallas_call(
        functools.partial(_filter_dft_kernel, L=L, fb=fb),
        grid=(L // fb,),
        in_specs=[
            pl.BlockSpec((L, w), lambda f: (0, 0)),
            pl.BlockSpec((L, w), lambda f: (0, 0)),
            pl.BlockSpec((fb, L), lambda f: (f, 0)),
            pl.BlockSpec((fb, L), lambda f: (f, 0)),
        ],
        out_specs=[
            pl.BlockSpec((fb, w), lambda f: (f, 0)),
            pl.BlockSpec((fb, w), lambda f: (f, 0)),
            pl.BlockSpec((SUBLANES, w), lambda f: (0, 0)),
        ],
        out_shape=[
            jax.ShapeDtypeStruct((L, w), F32),
            jax.ShapeDtypeStruct((L, w), F32),
            jax.ShapeDtypeStruct((SUBLANES, w), F32),
        ],
        compiler_params=_params(("arbitrary",)),
        name="hyena_filter_dft",
    )(s.astype(BF16), d.astype(BF16), cmat, smat)


HY_TILE = 256


def _short_conv_tile(ref, r0, L, w_ref, b_ref):
    t = HY_TILE
    cur = ref[r0:r0 + t, :]
    rid = lax.broadcasted_iota(jnp.int32, cur.shape, 0)
    if r0 % L == 0:
        prev = jnp.where(rid == 0, 0.0, pltpu.roll(cur, 1, axis=0))
    else:
        prev = ref[r0 - 1:r0 - 1 + t, :]
    if (r0 + t) % L == 0:
        nxt = jnp.where(rid == t - 1, 0.0, pltpu.roll(cur, t - 1, axis=0))
    else:
        nxt = ref[r0 + 1:r0 + 1 + t, :]
    return prev * w_ref[0:1, :] + cur * w_ref[1:2, :] + nxt * w_ref[2:3, :] + b_ref[...]


def _hyena_conv_kernel(*refs, L, nseq, conv_a):
    (a_ref, x_ref, cr_ref, sr_ref, cc_ref, sc_ref, ka_ref, ki_ref, kn_ref,
     wa_ref, ba_ref, wx_ref, bx_ref, hb_ref) = refs[:14]
    o_ref, z_sc, acc_sc = refs[-3:]
    f = pl.program_id(1)
    nf = pl.num_programs(1)
    rows = nseq * L

    @pl.when(f == 0)
    def _():
        for q in range(nseq):
            nyq = jnp.zeros((1, HY_W), F32)
            for r0 in range(q * L, (q + 1) * L, HY_TILE):
                if conv_a:
                    zt = _short_conv_tile(a_ref, r0, L, wa_ref, ba_ref)
                else:
                    zt = a_ref[r0:r0 + HY_TILE, :]
                z_sc[r0:r0 + HY_TILE, :] = zt
                nyq = nyq + jnp.sum(zt * _alt_sign(zt.shape, r0), axis=0, keepdims=True)
            nyq = nyq * kn_ref[0:1, :]
            for r0 in range(q * L, (q + 1) * L, HY_TILE):
                acc_sc[r0:r0 + HY_TILE, :] = _alt_sign((HY_TILE, HY_W), r0) * nyq

    ka = ka_ref[...]
    ki = ki_ref[...]
    for q in range(nseq):
        z = z_sc[q * L:(q + 1) * L, :].astype(BF16)
        a = jnp.dot(cr_ref[...], z, preferred_element_type=F32)
        b = jnp.dot(sr_ref[...], z, preferred_element_type=F32)
        p = (a * ka + b * ki).astype(BF16)
        qq = (b * ka - a * ki).astype(BF16)
        acc_sc[q * L:(q + 1) * L, :] += (jnp.dot(cc_ref[...], p, preferred_element_type=F32)
                                         + jnp.dot(sc_ref[...], qq, preferred_element_type=F32))

    @pl.when(f == nf - 1)
    def _():
        for r0 in range(0, rows, HY_TILE):
            y = acc_sc[r0:r0 + HY_TILE, :] + z_sc[r0:r0 + HY_TILE, :] * hb_ref[...]
            o_ref[r0:r0 + HY_TILE, :] = y * _short_conv_tile(x_ref, r0, L, wx_ref, bx_ref)


def _hyena_conv(a, a_col, x, x_col, row0, n_rows, L, nseq, spectra, filt, wa, ba, wx, bx, hbias, conv_a, dft,
                full_out=False, out_prev=None):
    fb = min(L, 256)
    cmat, smat = dft
    ka, ki, kn = spectra
    rb = nseq * L
    a_off = row0 // rb if a.shape[0] != n_rows else 0
    x_off = row0 // rb
    o_off = row0 // rb if full_out else 0
    extra_specs, extra_args, aliases = [], [], {}
    if out_prev is not None:
        extra_specs, extra_args, aliases = [pl.BlockSpec(memory_space=pl.ANY)], [out_prev], {14: 0}
    return pl.pallas_call(
        functools.partial(_hyena_conv_kernel, L=L, nseq=nseq, conv_a=conv_a),
        grid=(n_rows // rb, L // fb),
        input_output_aliases=aliases,
        in_specs=extra_specs[:0] + [
            pl.BlockSpec((rb, HY_W), lambda i, f: (i + a_off, a_col)),
            pl.BlockSpec((rb, HY_W), lambda i, f: (i + x_off, x_col)),
            pl.BlockSpec((fb, L), lambda i, f: (f, 0)),
            pl.BlockSpec((fb, L), lambda i, f: (f, 0)),
            pl.BlockSpec((L, fb), lambda i, f: (0, f)),
            pl.BlockSpec((L, fb), lambda i, f: (0, f)),
            pl.BlockSpec((fb, HY_W), lambda i, f: (f, filt)),
            pl.BlockSpec((fb, HY_W), lambda i, f: (f, filt)),
            pl.BlockSpec((SUBLANES, HY_W), lambda i, f: (0, filt)),
            pl.BlockSpec((3, HY_W), lambda i, f: (0, 0)),
            pl.BlockSpec((1, HY_W), lambda i, f: (0, 0)),
            pl.BlockSpec((3, HY_W), lambda i, f: (0, 0)),
            pl.BlockSpec((1, HY_W), lambda i, f: (0, 0)),
            pl.BlockSpec((1, HY_W), lambda i, f: (0, 0)),
        ] + extra_specs,
        out_specs=pl.BlockSpec((rb, HY_W), lambda i, f: (i + o_off, 0)),
        out_shape=jax.ShapeDtypeStruct((N_TOK if full_out else n_rows, HY_W), F32),
        scratch_shapes=[pltpu.VMEM((rb, HY_W), F32), pltpu.VMEM((rb, HY_W), F32)],
        compiler_params=_params(("parallel", "arbitrary")),
        name="hyena_conv",
    )(a, x, cmat, smat, cmat, smat, ka, ki, kn, wa, ba, wx, bx, hbias, *extra_args)


def _hyena_filter_taps(L, f1w, f1b, fr1, f2w, f2b, fr2, f3w):
    hp = lax.Precision.HIGHEST
    pos = jnp.arange(L, dtype=F32)
    t = pos / (L - 1)
    w = 2.0 * math.pi * pos / L
    f = jnp.linspace(1e-4, HY_BANDS - 1, HY_BANDS, dtype=F32)
    wf = w[:, None] * f[None, :]
    feat = jnp.concatenate([t[:, None], jnp.cos(wf), -jnp.sin(wf)], axis=-1)
    h = jnp.sin(fr1 * (jnp.dot(feat, f1w, precision=hp) + f1b))
    h = jnp.sin(fr2 * (jnp.dot(h, f2w, precision=hp) + f2b))
    h = jnp.dot(h, f3w, precision=hp).astype(F32)
    deltas = jnp.linspace(math.log(HY_TARGET) / HY_SLOW, math.log(HY_TARGET) / HY_FAST, HY_W, dtype=F32)
    window = jnp.exp(-t[:, None] * jnp.abs(deltas)[None, :])
    w = HY_W
    parts = [h[:, k * w:(k + 1) * w] * window for k in range(4)]
    colsum = [jnp.sum(jnp.abs(p), axis=0, keepdims=True) for p in parts]
    den = [colsum[0] + colsum[1] + EPS, colsum[2] + colsum[3] + EPS]
    fwd = jnp.concatenate([parts[0] / den[0], parts[2] / den[1]], axis=1)
    bwd = jnp.concatenate([parts[1] / den[0], parts[3] / den[1]], axis=1)
    bwd = jnp.where(pos[:, None] == 0, 0.0, bwd)
    return fwd + bwd, bwd - fwd


def _dot_t0(a, b):
    return lax.dot_general(a, b, (((0,), (0,)), ((), ())), preferred_element_type=F32)


def _dot_t1(a, b):
    return lax.dot_general(a, b, (((1,), (1,)), ((), ())), preferred_element_type=F32)


def _retention_kernel(*refs, L, has_s0, has_prev):
    refs = list(refs)
    q_ref, k_ref, v_ref, g_ref, dl_ref = refs[:5]
    pos = 5
    s0_ref = st_ref = None
    if has_s0:
        s0_ref = refs[pos]
        pos += 1
    if has_prev:
        pos += 1
    y_ref = refs[pos]
    pos += 1
    if not has_s0:
        st_ref = refs[pos]
        pos += 1
    sb_sc, sf_cur, sb_cur = refs[pos:]
    c = RET_CHUNK
    nc = L // c
    kscale = RET_HEAD_DIM ** -0.5
    ri = lax.broadcasted_iota(jnp.int32, (c, c), 0).astype(F32)
    ci = lax.broadcasted_iota(jnp.int32, (c, c), 1).astype(F32)
    diff = ri - ci
    dec = []
    for h in range(RET_HEADS):
        xf = dl_ref[0, h:h + 1, :]
        xb = dl_ref[1, h:h + 1, :]
        lgf = jnp.minimum(xf, 0.0) - jnp.log1p(jnp.exp(-jnp.abs(xf)))
        lgb = jnp.minimum(xb, 0.0) - jnp.log1p(jnp.exp(-jnp.abs(xb)))
        dec.append(dict(
            mask=(jnp.where(diff >= 0, jnp.exp(lgf * jnp.maximum(diff, 0.0)), 0.0)
                  + jnp.where(diff <= 0, jnp.exp(lgb * jnp.maximum(-diff, 0.0)), 0.0)),
            qdec_f=jnp.exp(lgf * (ri + 1.0)), kdec_f=jnp.exp(lgf * (c - 1.0 - ri)),
            qdec_b=jnp.exp(lgb * (c - ri)), kdec_b=jnp.exp(lgb * ri),
            cd_f=jnp.exp(lgf * c), cd_b=jnp.exp(lgb * c)))
        if has_s0:
            sf_cur[h] = s0_ref[0, 0, 0, h]
            sb_cur[h] = s0_ref[0, 0, 1, h]
        else:
            sf_cur[h] = jnp.zeros((RET_HEAD_DIM, RET_HEAD_DIM), F32)
            sb_cur[h] = jnp.zeros((RET_HEAD_DIM, RET_HEAD_DIM), F32)

    def bwd_body(i, carry):
        j = nc - 1 - i
        r0 = pl.multiple_of(j * c, c)
        for h in range(RET_HEADS):
            hs = slice(h * RET_HEAD_DIM, (h + 1) * RET_HEAD_DIM)
            sb = sb_cur[h]
            sb_sc[h * nc + j] = sb
            kc = k_ref[pl.ds(r0, c), hs] * kscale
            vc = v_ref[pl.ds(r0, c), hs]
            sb_cur[h] = sb * dec[h]["cd_b"] + _dot_t0((kc * dec[h]["kdec_b"]).astype(BF16), vc.astype(BF16))
        return carry

    lax.fori_loop(0, nc, bwd_body, 0)

    def fwd_body(j, carry):
        r0 = pl.multiple_of(j * c, c)
        for h in range(RET_HEADS):
            hs = slice(h * RET_HEAD_DIM, (h + 1) * RET_HEAD_DIM)
            dh = dec[h]
            sf = sf_cur[h]
            qc = q_ref[pl.ds(r0, c), hs]
            kc = k_ref[pl.ds(r0, c), hs] * kscale
            vc = v_ref[pl.ds(r0, c), hs].astype(BF16)
            scores = _dot_t1(qc.astype(BF16), kc.astype(BF16)) * dh["mask"]
            o = jnp.dot(scores.astype(BF16), vc, preferred_element_type=F32)
            o = o + jnp.dot((qc * dh["qdec_f"]).astype(BF16), sf.astype(BF16), preferred_element_type=F32)
            o = o + jnp.dot((qc * dh["qdec_b"]).astype(BF16), sb_sc[h * nc + j].astype(BF16),
                            preferred_element_type=F32)
            mu = jnp.mean(o, axis=-1, keepdims=True)
            var = jnp.mean(jnp.square(o - mu), axis=-1, keepdims=True)
            on = (o - mu) * lax.rsqrt(var + EPS)
            gc = g_ref[pl.ds(r0, c), hs]
            y_ref[pl.ds(r0, c), hs] = gc * (1.0 / (1.0 + jnp.exp(-gc))) * on
            sf_cur[h] = sf * dh["cd_f"] + _dot_t0((kc * dh["kdec_f"]).astype(BF16), vc)
        return carry

    lax.fori_loop(0, nc, fwd_body, 0)
    if st_ref is not None:
        for h in range(RET_HEADS):
            st_ref[0, 0, 0, h] = sf_cur[h]
            st_ref[0, 0, 1, h] = sb_cur[h]


def _retention(proj, dl, layer_i, row0, nseq, L, s0=None, st_prev=None, y_prev=None):
    off = row0 // L
    has_s0 = s0 is not None
    has_prev = (st_prev is not None) or (y_prev is not None)
    assert not (st_prev is not None and y_prev is not None)
    nc = L // RET_CHUNK
    n_ret = (DEPTH + 1) // 2
    col = lambda j: pl.BlockSpec((L, RET_W), lambda b: (b + off, 3 + j))
    in_specs = [col(0), col(1), col(2), col(3),
                pl.BlockSpec((2, RET_HEADS, LANES), lambda b: (0, 0, 0))]
    args = [proj, proj, proj, proj, dl]
    y_spec = pl.BlockSpec((L, RET_W), lambda b: (b + off, 0))
    y_shape = jax.ShapeDtypeStruct((N_TOK, RET_W), F32)
    st_block = (1, 1, 2, RET_HEADS, RET_HEAD_DIM, RET_HEAD_DIM)
    st_spec = pl.BlockSpec(st_block, lambda b: (b, layer_i, 0, 0, 0, 0))
    aliases = {}
    if has_s0:
        in_specs.append(st_spec)
        args.append(s0)
        out_specs, out_shape = y_spec, y_shape
    else:
        out_specs = [y_spec, st_spec]
        out_shape = [y_shape, jax.ShapeDtypeStruct((nseq, n_ret) + st_block[2:], F32)]
    if has_prev:
        aliases = {len(args): 1 if st_prev is not None else 0}
        in_specs.append(pl.BlockSpec(memory_space=pl.ANY))
        args.append(st_prev if st_prev is not None else y_prev)
    state = pltpu.VMEM((RET_HEADS, RET_HEAD_DIM, RET_HEAD_DIM), F32)
    return pl.pallas_call(
        functools.partial(_retention_kernel, L=L, has_s0=has_s0, has_prev=has_prev),
        grid=(nseq,),
        in_specs=in_specs,
        out_specs=out_specs,
        out_shape=out_shape,
        input_output_aliases=aliases,
        scratch_shapes=[pltpu.VMEM((RET_HEADS * nc, RET_HEAD_DIM, RET_HEAD_DIM), F32), state, state],
        compiler_params=_params(("parallel",)),
        name="retention",
    )(*args)


@functools.lru_cache(maxsize=None)
def _rope_tables():
    L = DEC_SEQ
    rows = L // GRID_W
    row = np.repeat(np.arange(rows, dtype=np.float64), GRID_W)
    col = np.tile(np.arange(GRID_W, dtype=np.float64), rows)
    quarter = DIFF_HEAD_DIM // 4
    freqs = ROPE_BASE ** (-np.arange(quarter, dtype=np.float64) / quarter)
    j = np.arange(LANES)
    pos = np.where(((j % DIFF_HEAD_DIM) < DIFF_HEAD_DIM // 2)[None, :], row[:, None], col[:, None])
    ang = pos * freqs[j % quarter][None, :]
    cos = np.cos(ang).astype(np.float32)
    sin = np.sin(ang).astype(np.float32)
    first = ((j % (2 * quarter)) < quarter)[None, :]
    sin_a = np.where(first, -sin, 0.0).astype(np.float32)
    sin_b = np.where(first, 0.0, sin).astype(np.float32)
    return jnp.asarray(cos), jnp.asarray(sin_a), jnp.asarray(sin_b)


def _rope_head(x, cos, sin_a, sin_b):
    quarter = DIFF_HEAD_DIM // 4
    up = pltpu.roll(x, LANES - quarter, axis=1)
    dn = pltpu.roll(x, quarter, axis=1)
    return x * cos + up * sin_a + dn * sin_b


def _kv_prep_kernel(q_ref, k_ref, v_ref, cos_ref, sa_ref, sb_ref, qo_ref, ko_ref, vo_ref):
    cos, sa, sb = cos_ref[...], sa_ref[...], sb_ref[...]
    for h in range(DIFF_HEADS):
        hs = slice(h * LANES, (h + 1) * LANES)
        qo_ref[:, hs] = _rope_head(q_ref[:, hs], cos, sa, sb).astype(BF16)
        ko_ref[0, :, hs] = _rope_head(k_ref[:, hs], cos, sa, sb).astype(BF16)
    vo_ref[0] = v_ref[...].astype(BF16)


def _cache_copy_kernel(ck_ref, cv_ref, k_in, v_in, ko_ref, vo_ref):
    del k_in, v_in
    ko_ref[0] = ck_ref[0, 0].astype(BF16)
    vo_ref[0] = cv_ref[0, 0].astype(BF16)


def _sample_qkv(proj, cache_k, cache_v, layer_i, tm=256):
    cos, sa, sb = _rope_tables()
    lk = PAST_LEN + DEC_SEQ
    pblk = N_PROMPT // tm
    nblk = DEC_SEQ // tm
    cblk = PAST_LEN // tm
    tab = pl.BlockSpec((tm, LANES), lambda b, i: (i, 0))
    q, k, v = pl.pallas_call(
        _kv_prep_kernel,
        grid=(DEC_BATCH, nblk),
        in_specs=[
            pl.BlockSpec((tm, D_MODEL), lambda b, i: (pblk + b * nblk + i, 0)),
            pl.BlockSpec((tm, D_MODEL), lambda b, i: (pblk + b * nblk + i, 1)),
            pl.BlockSpec((tm, D_MODEL), lambda b, i: (pblk + b * nblk + i, 2)),
            tab, tab, tab,
        ],
        out_specs=[
            pl.BlockSpec((tm, D_MODEL), lambda b, i: (b * nblk + i, 0)),
            pl.BlockSpec((1, tm, D_MODEL), lambda b, i: (b, cblk + i, 0)),
            pl.BlockSpec((1, tm, D_MODEL), lambda b, i: (b, cblk + i, 0)),
        ],
        out_shape=[
            jax.ShapeDtypeStruct((N_SAMPLE, D_MODEL), BF16),
            jax.ShapeDtypeStruct((DEC_BATCH, lk, D_MODEL), BF16),
            jax.ShapeDtypeStruct((DEC_BATCH, lk, D_MODEL), BF16),
        ],
        compiler_params=_params(("parallel", "parallel")),
        name="rope_qkv",
    )(proj, proj, proj, cos, sa, sb)
    n_att = DEPTH // 2
    ck = cache_k.reshape(DEC_BATCH, n_att, PAST_LEN, D_MODEL)
    cv = cache_v.reshape(DEC_BATCH, n_att, PAST_LEN, D_MODEL)
    k, v = pl.pallas_call(
        _cache_copy_kernel,
        grid=(DEC_BATCH,),
        in_specs=[
            pl.BlockSpec((1, 1, PAST_LEN, D_MODEL), lambda b: (b, layer_i, 0, 0)),
            pl.BlockSpec((1, 1, PAST_LEN, D_MODEL), lambda b: (b, layer_i, 0, 0)),
            pl.BlockSpec(memory_space=pl.ANY),
            pl.BlockSpec(memory_space=pl.ANY),
        ],
        out_specs=[
            pl.BlockSpec((1, PAST_LEN, D_MODEL), lambda b: (b, 0, 0)),
            pl.BlockSpec((1, PAST_LEN, D_MODEL), lambda b: (b, 0, 0)),
        ],
        out_shape=[
            jax.ShapeDtypeStruct((DEC_BATCH, lk, D_MODEL), BF16),
            jax.ShapeDtypeStruct((DEC_BATCH, lk, D_MODEL), BF16),
        ],
        input_output_aliases={2: 0, 3: 1},
        compiler_params=_params(("parallel",)),
        name="cache_prepend",
    )(ck, cv, k, v)
    return q, k, v


def _diff_attn_kernel(*refs, lam_init, batched_kv, cache_out):
    q_ref, k_ref, v_ref, lam_ref, sg_ref = refs[:5]
    if cache_out:
        o_ref, kc_ref, vc_ref = refs[-3:]
        kc_ref[0, 0] = k_ref[...]
        vc_ref[0, 0] = v_ref[...]
    else:
        o_ref = refs[-1]
    lv = lam_ref[...]
    lam = (jnp.exp(jnp.sum(lv[0:1] * lv[1:2], axis=-1, keepdims=True))
           - jnp.exp(jnp.sum(lv[2:3] * lv[3:4], axis=-1, keepdims=True)) + lam_init)
    lane = lax.broadcasted_iota(jnp.int32, (1, LANES), 1)
    m1 = (lane < DIFF_HEAD_DIM).astype(F32)
    m2 = 1.0 - m1
    scale = DIFF_HEAD_DIM ** -0.5
    for h in range(DIFF_HEADS):
        hs = slice(h * LANES, (h + 1) * LANES)
        q = q_ref[:, hs].astype(F32) * scale
        if batched_kv:
            k = k_ref[0, :, hs].astype(BF16)
            v = v_ref[0, :, hs].astype(BF16)
        else:
            k = k_ref[:, hs].astype(BF16)
            v = v_ref[:, hs].astype(BF16)
        outs = []
        for m in (m1, m2):
            s = _dot_t1((q * m).astype(BF16), k)
            s = s - jnp.max(s, axis=-1, keepdims=True)
            p = jnp.exp(s)
            l = jnp.sum(p, axis=-1, keepdims=True)
            outs.append(jnp.dot(p.astype(BF16), v, preferred_element_type=F32) / l)
        o = outs[0] - lam * outs[1]
        ms = jnp.mean(o * o, axis=-1, keepdims=True)
        o_ref[:, hs] = o * lax.rsqrt(ms + EPS) * sg_ref[...] * (1.0 - lam_init)


def _diff_attention(q, k, v, lam_vec, subln, lam_init, *, nb, lq, lk, tq, q_row0, q_col, kv_cols, batched_kv,
                    out_row0=0, out_prev=None, cache_layer=None, cache_prev=None):
    nq = lq // tq
    qoff = q_row0 // tq
    ooff = out_row0 // tq
    cache_out = cache_layer is not None
    extra_specs, extra_args, aliases = [], [], {}
    if out_prev is not None:
        extra_specs, extra_args, aliases = [pl.BlockSpec(memory_space=pl.ANY)], [out_prev], {5: 0}
    o_spec = pl.BlockSpec((tq, D_MODEL), lambda b, i: (ooff + b * nq + i, 0))
    o_shape = jax.ShapeDtypeStruct((N_TOK, D_MODEL), F32)
    if cache_out:
        assert not batched_kv and nq == 1 and out_prev is None
        if cache_prev is not None:
            extra_specs = [pl.BlockSpec(memory_space=pl.ANY)] * 2
            extra_args = list(cache_prev)
            aliases = {5: 1, 6: 2}
        c_spec = pl.BlockSpec((1, 1, lk, D_MODEL), lambda b, i: (b, cache_layer, 0, 0))
        c_shape = jax.ShapeDtypeStruct((nb, DEPTH // 2, lk, D_MODEL), F32)
        o_spec, o_shape = [o_spec, c_spec, c_spec], [o_shape, c_shape, c_shape]
    q_spec = pl.BlockSpec((tq, D_MODEL), lambda b, i: (qoff + b * nq + i, q_col))
    if batched_kv:
        k_spec = pl.BlockSpec((1, lk, D_MODEL), lambda b, i: (b, 0, 0))
        v_spec = k_spec
    else:
        k_spec = pl.BlockSpec((lk, D_MODEL), lambda b, i: (b, kv_cols[0]))
        v_spec = pl.BlockSpec((lk, D_MODEL), lambda b, i: (b, kv_cols[1]))
    return pl.pallas_call(
        functools.partial(_diff_attn_kernel, lam_init=lam_init, batched_kv=batched_kv, cache_out=cache_out),
        grid=(nb, nq),
        in_specs=[q_spec, k_spec, v_spec,
                  pl.BlockSpec((4, LANES), lambda b, i: (0, 0)),
                  pl.BlockSpec((1, LANES), lambda b, i: (0, 0))] + extra_specs,
        out_specs=o_spec,
        out_shape=o_shape,
        input_output_aliases=aliases,
        compiler_params=_params(("parallel", "arbitrary")),
        name="diff_attention",
    )(q, k, v, lam_vec, subln, *extra_args)


def _router_kernel(x_ref, m_ref, g_ref, wr_ref, h_ref, aff_ref):
    h = _normed(x_ref[...], g_ref, m_ref, SH2, SC2)
    hb = h.astype(BF16)
    h_ref[...] = hb
    hl = (h - hb.astype(F32)).astype(BF16)
    wr = wr_ref[...]
    wh = wr.astype(BF16)
    wl = (wr - wh.astype(F32)).astype(BF16)
    logits = (jnp.dot(hb, wh, preferred_element_type=F32) + jnp.dot(hl, wh, preferred_element_type=F32)
              + jnp.dot(hb, wl, preferred_element_type=F32))
    lt = jnp.transpose(logits)[0:N_EXPERTS, :]
    lt = lt - jnp.max(lt, axis=0, keepdims=True)
    e = jnp.exp(lt)
    aff_ref[...] = e / jnp.sum(e, axis=0, keepdims=True)


def _router(x, mods_l, g, w_router, tm=512):
    wr = jnp.pad(w_router, ((0, 0), (0, LANES - N_EXPERTS)))
    return pl.pallas_call(
        _router_kernel,
        grid=(N_TOK // tm,),
        in_specs=[
            pl.BlockSpec((tm, D_MODEL), lambda i: (i, 0)),
            pl.BlockSpec((1, 6, D_MODEL), lambda i: (_group_of_block(i, tm), 0, 0)),
            pl.BlockSpec((1, D_MODEL), lambda i: (0, 0)),
            pl.BlockSpec((D_MODEL, LANES), lambda i: (0, 0)),
        ],
        out_specs=[
            pl.BlockSpec((tm, D_MODEL), lambda i: (i, 0)),
            pl.BlockSpec((N_EXPERTS, tm), lambda i: (0, i)),
        ],
        out_shape=[
            jax.ShapeDtypeStruct((N_TOK, D_MODEL), BF16),
            jax.ShapeDtypeStruct((N_EXPERTS, N_TOK), F32),
        ],
        compiler_params=_params(("parallel",)),
        name="norm_router",
    )(x, mods_l, g.reshape(1, D_MODEL), wr)


FFN_TF = 512
FFN_TR = 512


def _expert_ffn_kernel(xs_ref, wg_ref, wu_ref, wd_ref, gate_ref, m_ref, o_ref):
    f = pl.program_id(1)
    wg = wg_ref[0].astype(BF16)
    wu = wu_ref[0].astype(BF16)
    wd = wd_ref[0].astype(BF16)
    for r0 in range(0, CAP_T, FFN_TR):
        rs = slice(r0, r0 + FFN_TR)
        xs = xs_ref[0, rs, :]
        a = jnp.dot(xs, wg, preferred_element_type=F32)
        u = jnp.dot(xs, wu, preferred_element_type=F32)
        hid = (a * (1.0 / (1.0 + jnp.exp(-a))) * u).astype(BF16)
        y = jnp.dot(hid, wd, preferred_element_type=F32)

        @pl.when(f == 0)
        def _():
            o_ref[0, rs, :] = y

        @pl.when(jnp.logical_and(f > 0, f < pl.num_programs(1) - 1))
        def _():
            o_ref[0, rs, :] += y

        @pl.when(f == pl.num_programs(1) - 1)
        def _():
            gg = gate_ref[0, rs, :]
            scale = gg[:, 0:1] * m_ref[0, G2:G2 + 1, :]
            for g in range(1, N_GROUPS):
                scale = scale + gg[:, g:g + 1] * m_ref[g, G2:G2 + 1, :]
            o_ref[0, rs, :] = (o_ref[0, rs, :] + y) * scale


def _expert_ffn(xs, wg, wu, wd, gate, mods_l, l):
    return pl.pallas_call(
        _expert_ffn_kernel,
        grid=(N_EXPERTS, EXPERT_FF // FFN_TF),
        in_specs=[
            pl.BlockSpec((1, CAP_T, D_MODEL), lambda e, f: (e, 0, 0)),
            pl.BlockSpec((None, 1, D_MODEL, FFN_TF), lambda e, f: (l, e, 0, f)),
            pl.BlockSpec((None, 1, D_MODEL, FFN_TF), lambda e, f: (l, e, 0, f)),
            pl.BlockSpec((None, 1, FFN_TF, D_MODEL), lambda e, f: (l, e, f, 0)),
            pl.BlockSpec((1, CAP_T, N_GROUPS), lambda e, f: (e, 0, 0)),
            pl.BlockSpec((N_GROUPS, 6, D_MODEL), lambda e, f: (0, 0, 0)),
        ],
        out_specs=pl.BlockSpec((1, CAP_T, D_MODEL), lambda e, f: (e, 0, 0)),
        out_shape=jax.ShapeDtypeStruct((N_EXPERTS, CAP_T, D_MODEL), F32),
        compiler_params=_params(("parallel", "arbitrary")),
        name="expert_ffn",
    )(xs, wg, wu, wd, gate, mods_l)


def _final_norm_kernel(x_ref, g_ref, o_ref):
    x = x_ref[...]
    ms = jnp.mean(x * x, axis=-1, keepdims=True)
    o_ref[...] = x * lax.rsqrt(ms + EPS) * g_ref[...]


def _final_norm(x, g, row0, n_rows, tm=512):
    off = row0 // tm
    return pl.pallas_call(
        _final_norm_kernel,
        grid=(n_rows // tm,),
        in_specs=[pl.BlockSpec((tm, D_MODEL), lambda i: (i + off, 0)),
                  pl.BlockSpec((1, D_MODEL), lambda i: (0, 0))],
        out_specs=pl.BlockSpec((tm, D_MODEL), lambda i: (i, 0)),
        out_shape=jax.ShapeDtypeStruct((n_rows, D_MODEL), F32),
        compiler_params=_params(("parallel",)),
        name="final_norm",
    )(x, g.reshape(1, D_MODEL))


def _even_layer(x, mods_l, i, state_ret, st_prev, dfts, norm1_g, w_in_even, hy_short_w, hy_short_b, hy_f1_w, hy_f1_b,
                hy_freq1, hy_f2_w, hy_f2_b, hy_freq2, hy_f3_w, hy_bias, ret_decay, w_out_even):
    proj = _norm_project(x, mods_l, norm1_g, w_in_even[i].astype(BF16))
    sw, sbias = hy_short_w[i], hy_short_b[i].reshape(1, 3 * HY_W)
    hb = hy_bias[i]
    y_hy = None
    for (row0, n_rows, L, nseq) in ((0, N_PROMPT, SEQ, 8), (N_PROMPT, N_SAMPLE, DEC_SEQ, 1)):
        s, d = _hyena_filter_taps(L, hy_f1_w[i], hy_f1_b[i], hy_freq1[i], hy_f2_w[i], hy_f2_b[i], hy_freq2[i], hy_f3_w[i])
        spectra = _filter_spectra(s, d, L, dfts[L])
        wv, bv = sw[:, 0:HY_W], sbias[:, 0:HY_W]
        w1, b1 = sw[:, HY_W:2 * HY_W], sbias[:, HY_W:2 * HY_W]
        w2, b2 = sw[:, 2 * HY_W:], sbias[:, 2 * HY_W:]
        z1 = _hyena_conv(proj, 0, proj, 1, row0, n_rows, L, nseq, spectra, 0, wv, bv, w1, b1, hb[0:1], True, dfts[L])
        y_hy = _hyena_conv(z1, 0, proj, 2, row0, n_rows, L, nseq, spectra, 1, wv, bv, w2, b2, hb[1:2], False, dfts[L],
                           full_out=True, out_prev=y_hy)
    dl = jnp.broadcast_to(ret_decay[i].astype(F32)[:, :, None], (2, RET_HEADS, LANES))
    y_ret, st = _retention(proj, dl, i, 0, BATCH, SEQ, st_prev=st_prev)
    y_ret = _retention(proj, dl, i, N_PROMPT, DEC_BATCH, DEC_SEQ, s0=state_ret, y_prev=y_ret)
    wo = w_out_even[i].astype(BF16)
    x = _project_residual([y_hy, y_ret], [wo[:HY_W], wo[HY_W:]], x, mods_l, G1)
    return x, st


def _odd_layer(x, mods_l, l, i, cache_k, cache_v, kv_prev, norm1_g, w_in_odd, lam_q1, lam_k1, lam_q2, lam_k2, subln_g,
               w_out_odd):
    lam_init = 0.8 - 0.6 * math.exp(-0.3 * l)
    proj = _norm_project(x, mods_l, norm1_g, w_in_odd[i].astype(BF16))
    lam_vec = jnp.pad(jnp.stack([lam_q1[i], lam_k1[i], lam_q2[i], lam_k2[i]]), ((0, 0), (0, LANES - DIFF_HEAD_DIM)))
    sg = subln_g[i].reshape(1, LANES)
    o_p, kc, vc = _diff_attention(proj, proj, proj, lam_vec, sg, lam_init, nb=BATCH, lq=SEQ, lk=SEQ, tq=SEQ,
                                  q_row0=0, q_col=0, kv_cols=(1, 2), batched_kv=False,
                                  cache_layer=i, cache_prev=kv_prev)
    qs, ks, vs = _sample_qkv(proj, cache_k, cache_v, i)
    o = _diff_attention(qs, ks, vs, lam_vec, sg, lam_init, nb=DEC_BATCH, lq=DEC_SEQ, lk=PAST_LEN + DEC_SEQ, tq=256,
                        q_row0=0, q_col=0, kv_cols=None, batched_kv=True, out_row0=N_PROMPT, out_prev=o_p)
    x = _project_residual([o], [w_out_odd[i].astype(BF16)], x, mods_l, G1)
    return x, (kc, vc)


def _moe_layer(x, mods_l, l, norm2_g, w_router, wg, wu, wd):
    h, aff = _router(x, mods_l, norm2_g, w_router)
    gate_p, idx_p = lax.top_k(aff[:, :N_PROMPT], CAP_P)
    gate_s, idx_s = lax.top_k(aff[:, N_PROMPT:], CAP_S)
    idx = jnp.concatenate([idx_p, idx_s + N_PROMPT], axis=1)
    gate = jnp.concatenate([gate_p, gate_s], axis=1)
    xs = h[idx]
    grp = jnp.where(idx < N_PROMPT, 0, 1 + (idx - N_PROMPT) // DEC_SEQ)
    gate_grp = jnp.where(grp[:, :, None] == jnp.arange(N_GROUPS)[None, None, :], gate[:, :, None], 0.0)
    out = _expert_ffn(xs, wg, wu, wd, gate_grp, mods_l, l)
    return x.at[idx.reshape(-1)].add(out.reshape(-1, D_MODEL))


def kernel(x_prompt, x_sample, state_ret, cache_k, cache_v, c, c_ctx, w_mod, b_mod, norm1_g, norm2_g, w_in_even, hy_short_w, hy_short_b, hy_f1_w, hy_f1_b, hy_freq1, hy_f2_w, hy_f2_b, hy_freq2, hy_f3_w, hy_bias, ret_decay, w_out_even, w_in_odd, lam_q1, lam_k1, lam_q2, lam_k2, subln_g, w_out_odd, moe_router, moe_wg, moe_wu, moe_wd, final_g):
    x = jnp.concatenate([x_prompt.reshape(N_PROMPT, D_MODEL), x_sample.reshape(N_SAMPLE, D_MODEL)], axis=0)
    mods = _modulation(c, c_ctx, w_mod, b_mod)
    dfts = {L: _dft_bf16(L) for L in (SEQ, DEC_SEQ)}
    state_ret = state_ret.astype(F32)
    st = kv = None
    for l in range(DEPTH):
        i = l // 2
        if l % 2 == 0:
            x, st = _even_layer(x, mods[l], i, state_ret, st, dfts, norm1_g[l], w_in_even, hy_short_w, hy_short_b,
                                hy_f1_w, hy_f1_b, hy_freq1, hy_f2_w, hy_f2_b, hy_freq2, hy_f3_w, hy_bias, ret_decay,
                                w_out_even)
        else:
            x, kv = _odd_layer(x, mods[l], l, i, cache_k, cache_v, kv, norm1_g[l], w_in_odd, lam_q1, lam_k1, lam_q2,
                               lam_k2, subln_g, w_out_odd)
        x = _moe_layer(x, mods[l], l, norm2_g[l], moe_router[l], moe_wg, moe_wu, moe_wd)
    y_prompt = _final_norm(x, final_g, 0, N_PROMPT).reshape(BATCH, SEQ, D_MODEL)
    y_sample = _final_norm(x, final_g, N_PROMPT, N_SAMPLE).reshape(DEC_BATCH, DEC_SEQ, D_MODEL)
    cache_shape = (BATCH, DEPTH // 2, SEQ, DIFF_HEADS, 2 * DIFF_HEAD_DIM)
    return (y_prompt, y_sample, st, kv[0].reshape(cache_shape), kv[1].reshape(cache_shape))
```

```python
import functools
import math

import numpy as np
import jax
import jax.numpy as jnp
from jax import lax
from jax.experimental import pallas as pl
from jax.experimental.pallas import tpu as pltpu
from jax.experimental.pallas import tpu_sc as plsc

F32 = jnp.float32
BF16 = jnp.bfloat16

D_MODEL = 1024
BATCH = 32
SEQ = 256
DEPTH = 4
DEC_BATCH = 2
DEC_SEQ = 2048
PAST_LEN = 256
GRID_W = 64
HY_W = 512
HY_EMB = 33
HY_BANDS = 16
HY_FF = 64
HY_TARGET = 1e-2
HY_FAST = 0.3
HY_SLOW = 1.5
RET_W = 512
RET_HEADS = 4
RET_HEAD_DIM = 128
RET_CHUNK = 128
DIFF_HEADS = 8
DIFF_HEAD_DIM = 64
ROPE_BASE = 10000.0
N_EXPERTS = 16
EC_FACTOR = 2
EXPERT_FF = 1024
EVEN_IN = 3 * HY_W + 4 * RET_W
EPS = 1e-6

N_PROMPT = BATCH * SEQ
N_SAMPLE = DEC_BATCH * DEC_SEQ
N_TOK = N_PROMPT + N_SAMPLE
N_GROUPS = 1 + DEC_BATCH
CAP_P = EC_FACTOR * N_PROMPT // N_EXPERTS
CAP_S = EC_FACTOR * N_SAMPLE // N_EXPERTS
CAP_T = CAP_P + CAP_S

LANES = 128
SUBLANES = 8
VMEM_LIMIT = 56 * 1024 * 1024

SH1, SC1, G1, SH2, SC2, G2 = range(6)


def _params(sem, vmem=VMEM_LIMIT):
    return pltpu.CompilerParams(dimension_semantics=sem, vmem_limit_bytes=vmem)


def _group_of_block(i, tm):
    pb = N_PROMPT // tm
    return jnp.where(i < pb, 0, 1 + (i - pb) // (DEC_SEQ // tm))


MOD_TN = 1024


MOD_UNROLL = 4


def _mod_kernel(cb_ref, w_ref, b_ref, o_ref, a_sc):
    nchunk = MOD_TN // LANES
    cv = cb_ref[...]
    a_sc[...] = cv * (1.0 / (1.0 + jnp.exp(-cv)))

    def body(kb, accs):
        accs = list(accs)
        for u in range(MOD_UNROLL):
            k0 = pl.multiple_of((kb * MOD_UNROLL + u) * SUBLANES, SUBLANES)
            a = [a_sc[r, pl.ds(k0, SUBLANES), :] for r in range(N_GROUPS)]
            for ci in range(nchunk):
                wv = w_ref[0, pl.ds(k0, SUBLANES), ci * LANES:(ci + 1) * LANES]
                for r in range(N_GROUPS):
                    accs[ci * N_GROUPS + r] = accs[ci * N_GROUPS + r] + wv * a[r]
        return tuple(accs)

    init = tuple(jnp.zeros((SUBLANES, LANES), F32) for _ in range(N_GROUPS * nchunk))
    accs = lax.fori_loop(0, D_MODEL // (SUBLANES * MOD_UNROLL), body, init)
    o_ref[...] = jnp.zeros(o_ref.shape, F32)
    for r in range(N_GROUPS):
        for ci in range(nchunk):
            row = jnp.sum(accs[ci * N_GROUPS + r], axis=0, keepdims=True)
            o_ref[0, r:r + 1, ci * LANES:(ci + 1) * LANES] = row + b_ref[0, :, ci * LANES:(ci + 1) * LANES]


def _modulation(c, c_ctx, w_mod, b_mod):
    cond = jnp.concatenate([c_ctx[None, :], c], axis=0)
    cb = jnp.broadcast_to(cond[:, :, None], (N_GROUPS, D_MODEL, LANES))
    out = pl.pallas_call(
        _mod_kernel,
        grid=(DEPTH, 6 * D_MODEL // MOD_TN),
        in_specs=[
            pl.BlockSpec((N_GROUPS, D_MODEL, LANES), lambda l, j: (0, 0, 0)),
            pl.BlockSpec((1, D_MODEL, MOD_TN), lambda l, j: (l, 0, j)),
            pl.BlockSpec((1, 1, MOD_TN), lambda l, j: (l, 0, j)),
        ],
        out_specs=pl.BlockSpec((1, SUBLANES, MOD_TN), lambda l, j: (l, 0, j)),
        out_shape=jax.ShapeDtypeStruct((DEPTH, SUBLANES, 6 * D_MODEL), F32),
        scratch_shapes=[pltpu.VMEM((N_GROUPS, D_MODEL, LANES), F32)],
        compiler_params=_params(("parallel", "parallel")),
        name="ada_mod",
    )(cb, w_mod, b_mod.reshape(DEPTH, 1, 6 * D_MODEL))
    return out[:, :N_GROUPS].reshape(DEPTH, N_GROUPS, 6, D_MODEL)


def _normed(x, g_ref, m_ref, shift, scale):
    ms = jnp.mean(x * x, axis=-1, keepdims=True)
    y = x * lax.rsqrt(ms + EPS) * g_ref[...]
    return y * (1.0 + m_ref[0, scale:scale + 1, :]) + m_ref[0, shift:shift + 1, :]


def _norm_mm_kernel(x_ref, m_ref, g_ref, w_ref, o_ref, *, tn):
    h = _normed(x_ref[...], g_ref, m_ref, SH1, SC1).astype(BF16)
    for c0 in range(0, o_ref.shape[1], tn):
        o_ref[:, c0:c0 + tn] = jnp.dot(h, w_ref[:, c0:c0 + tn], preferred_element_type=F32)


def _norm_project(x, mods_l, g, w_bf16, tm=512, tn=512):
    nout = w_bf16.shape[1]
    return pl.pallas_call(
        functools.partial(_norm_mm_kernel, tn=tn),
        grid=(N_TOK // tm,),
        in_specs=[
            pl.BlockSpec((tm, D_MODEL), lambda i: (i, 0)),
            pl.BlockSpec((1, 6, D_MODEL), lambda i: (_group_of_block(i, tm), 0, 0)),
            pl.BlockSpec((1, D_MODEL), lambda i: (0, 0)),
            pl.BlockSpec((D_MODEL, nout), lambda i: (0, 0)),
        ],
        out_specs=pl.BlockSpec((tm, nout), lambda i: (i, 0)),
        out_shape=jax.ShapeDtypeStruct((N_TOK, nout), F32),
        compiler_params=_params(("parallel",)),
        name="norm_project",
    )(x, mods_l, g.reshape(1, D_MODEL), w_bf16)


def _proj_res_kernel(*refs, n_in, gate):
    a_refs = refs[:n_in]
    w_refs = refs[n_in:2 * n_in]
    x_ref, m_ref, o_ref = refs[2 * n_in:]
    acc = None
    for a_ref, w_ref in zip(a_refs, w_refs):
        t = jnp.dot(a_ref[...].astype(BF16), w_ref[...], preferred_element_type=F32)
        acc = t if acc is None else acc + t
    o_ref[...] = x_ref[...] + m_ref[0, gate:gate + 1, :] * acc


def _project_residual(acts, ws_bf16, x, mods_l, gate, tm=512):
    n_in = len(acts)
    in_specs = [pl.BlockSpec((tm, a.shape[1]), lambda i: (i, 0)) for a in acts]
    in_specs += [pl.BlockSpec(w.shape, lambda i: (0, 0)) for w in ws_bf16]
    in_specs += [
        pl.BlockSpec((tm, D_MODEL), lambda i: (i, 0)),
        pl.BlockSpec((1, 6, D_MODEL), lambda i: (_group_of_block(i, tm), 0, 0)),
    ]
    return pl.pallas_call(
        functools.partial(_proj_res_kernel, n_in=n_in, gate=gate),
        grid=(N_TOK // tm,),
        in_specs=in_specs,
        out_specs=pl.BlockSpec((tm, D_MODEL), lambda i: (i, 0)),
        out_shape=jax.ShapeDtypeStruct((N_TOK, D_MODEL), F32),
        input_output_aliases={2 * n_in: 0},
        compiler_params=_params(("parallel",)),
        name="project_residual",
    )(*acts, *ws_bf16, x, mods_l)


@functools.lru_cache(maxsize=None)
def _dft_mats(L):
    n = 2 * L
    ft = (np.arange(L, dtype=np.int64)[:, None] * np.arange(L, dtype=np.int64)[None, :]) % n
    ang = ft.astype(np.float64) * (2.0 * np.pi / n)
    return np.cos(ang).astype(np.float32), np.sin(ang).astype(np.float32)


def _dft_bf16(L):
    c, s = _dft_mats(L)
    return jnp.asarray(c).astype(BF16), jnp.asarray(s).astype(BF16)


def _alt_sign(shape, row0):
    t = lax.broadcasted_iota(jnp.int32, shape, 0) + row0
    return (1 - 2 * (t & 1)).astype(F32)


def _filter_dft_kernel(s_ref, d_ref, c_ref, sn_ref, ka_ref, ki_ref, kn_ref, *, L, fb):
    f0 = pl.program_id(0) * fb
    n = 2.0 * L
    s = s_ref[...]
    r = jnp.dot(c_ref[...], s, preferred_element_type=F32)
    im = jnp.dot(sn_ref[...], d_ref[...], preferred_element_type=F32)
    fidx = lax.broadcasted_iota(jnp.int32, r.shape, 0) + f0
    scale = jnp.where(fidx == 0, 1.0 / n, 2.0 / n)
    ka_ref[...] = r * scale
    ki_ref[...] = im * (2.0 / n)
    nyq = jnp.sum(s.astype(F32) * _alt_sign(s.shape, 0), axis=0, keepdims=True) * (1.0 / n)
    kn_ref[...] = jnp.broadcast_to(nyq, kn_ref.shape)


def _filter_spectra(s, d, L, dft):
    fb = min(L, 512)
    cmat, smat = dft
    w = 2 * HY_W
    return pl.pallas_call(
        functools.partial(_filter_dft_kernel, L=L, fb=fb),
        grid=(L // fb,),
        in_specs=[
            pl.BlockSpec((L, w), lambda f: (0, 0)),
            pl.BlockSpec((L, w), lambda f: (0, 0)),
            pl.BlockSpec((fb, L), lambda f: (f, 0)),
            pl.BlockSpec((fb, L), lambda f: (f, 0)),
        ],
        out_specs=[
            pl.BlockSpec((fb, w), lambda f: (f, 0)),
            pl.BlockSpec((fb, w), lambda f: (f, 0)),
            pl.BlockSpec((SUBLANES, w), lambda f: (0, 0)),
        ],
        out_shape=[
            jax.ShapeDtypeStruct((L, w), F32),
            jax.ShapeDtypeStruct((L, w), F32),
            jax.ShapeDtypeStruct((SUBLANES, w), F32),
        ],
        compiler_params=_params(("arbitrary",)),
        name="hyena_filter_dft",
    )(s.astype(BF16), d.astype(BF16), cmat, smat)


HY_TILE = 256


def _short_conv_tile(ref, r0, L, w_ref, b_ref):
    t = HY_TILE
    cur = ref[r0:r0 + t, :]
    rid = lax.broadcasted_iota(jnp.int32, cur.shape, 0)
    if r0 % L == 0:
        prev = jnp.where(rid == 0, 0.0, pltpu.roll(cur, 1, axis=0))
    else:
        prev = ref[r0 - 1:r0 - 1 + t, :]
    if (r0 + t) % L == 0:
        nxt = jnp.where(rid == t - 1, 0.0, pltpu.roll(cur, t - 1, axis=0))
    else:
        nxt = ref[r0 + 1:r0 + 1 + t, :]
    return prev * w_ref[0:1, :] + cur * w_ref[1:2, :] + nxt * w_ref[2:3, :] + b_ref[...]


def _hyena_conv_kernel(*refs, L, nseq, conv_a):
    (a_ref, x_ref, cr_ref, sr_ref, cc_ref, sc_ref, ka_ref, ki_ref, kn_ref,
     wa_ref, ba_ref, wx_ref, bx_ref, hb_ref) = refs[:14]
    o_ref, z_sc, acc_sc = refs[-3:]
    f = pl.program_id(1)
    nf = pl.num_programs(1)
    rows = nseq * L

    @pl.when(f == 0)
    def _():
        for q in range(nseq):
            nyq = jnp.zeros((1, HY_W), F32)
            for r0 in range(q * L, (q + 1) * L, HY_TILE):
                if conv_a:
                    zt = _short_conv_tile(a_ref, r0, L, wa_ref, ba_ref)
                else:
                    zt = a_ref[r0:r0 + HY_TILE, :]
                z_sc[r0:r0 + HY_TILE, :] = zt
                nyq = nyq + jnp.sum(zt * _alt_sign(zt.shape, r0), axis=0, keepdims=True)
            nyq = nyq * kn_ref[0:1, :]
            for r0 in range(q * L, (q + 1) * L, HY_TILE):
                acc_sc[r0:r0 + HY_TILE, :] = _alt_sign((HY_TILE, HY_W), r0) * nyq

    ka = ka_ref[...]
    ki = ki_ref[...]
    for q in range(nseq):
        z = z_sc[q * L:(q + 1) * L, :].astype(BF16)
        a = jnp.dot(cr_ref[...], z, preferred_element_type=F32)
        b = jnp.dot(sr_ref[...], z, preferred_element_type=F32)
        p = (a * ka + b * ki).astype(BF16)
        qq = (b * ka - a * ki).astype(BF16)
        acc_sc[q * L:(q + 1) * L, :] += (jnp.dot(cc_ref[...], p, preferred_element_type=F32)
                                         + jnp.dot(sc_ref[...], qq, preferred_element_type=F32))

    @pl.when(f == nf - 1)
    def _():
        for r0 in range(0, rows, HY_TILE):
            y = acc_sc[r0:r0 + HY_TILE, :] + z_sc[r0:r0 + HY_TILE, :] * hb_ref[...]
            o_ref[r0:r0 + HY_TILE, :] = y * _short_conv_tile(x_ref, r0, L, wx_ref, bx_ref)


def _hyena_conv(a, a_col, x, x_col, row0, n_rows, L, nseq, spectra, filt, wa, ba, wx, bx, hbias, conv_a, dft,
                full_out=False, out_prev=None):
    fb = min(L, 256)
    cmat, smat = dft
    ka, ki, kn = spectra
    rb = nseq * L
    a_off = row0 // rb if a.shape[0] != n_rows else 0
    x_off = row0 // rb
    o_off = row0 // rb if full_out else 0
    extra_specs, extra_args, aliases = [], [], {}
    if out_prev is not None:
        extra_specs, extra_args, aliases = [pl.BlockSpec(memory_space=pl.ANY)], [out_prev], {14: 0}
    return pl.pallas_call(
        functools.partial(_hyena_conv_kernel, L=L, nseq=nseq, conv_a=conv_a),
        grid=(n_rows // rb, L // fb),
        input_output_aliases=aliases,
        in_specs=extra_specs[:0] + [
            pl.BlockSpec((rb, HY_W), lambda i, f: (i + a_off, a_col)),
            pl.BlockSpec((rb, HY_W), lambda i, f: (i + x_off, x_col)),
            pl.BlockSpec((fb, L), lambda i, f: (f, 0)),
            pl.BlockSpec((fb, L), lambda i, f: (f, 0)),
            pl.BlockSpec((L, fb), lambda i, f: (0, f)),
            pl.BlockSpec((L, fb), lambda i, f: (0, f)),
            pl.BlockSpec((fb, HY_W), lambda i, f: (f, filt)),
            pl.BlockSpec((fb, HY_W), lambda i, f: (f, filt)),
            pl.BlockSpec((SUBLANES, HY_W), lambda i, f: (0, filt)),
            pl.BlockSpec((3, HY_W), lambda i, f: (0, 0)),
            pl.BlockSpec((1, HY_W), lambda i, f: (0, 0)),
            pl.BlockSpec((3, HY_W), lambda i, f: (0, 0)),
            pl.BlockSpec((1, HY_W), lambda i, f: (0, 0)),
            pl.BlockSpec((1, HY_W), lambda i, f: (0, 0)),
        ] + extra_specs,
        out_specs=pl.BlockSpec((rb, HY_W), lambda i, f: (i + o_off, 0)),
        out_shape=jax.ShapeDtypeStruct((N_TOK if full_out else n_rows, HY_W), F32),
        scratch_shapes=[pltpu.VMEM((rb, HY_W), F32), pltpu.VMEM((rb, HY_W), F32)],
        compiler_params=_params(("parallel", "arbitrary")),
        name="hyena_conv",
    )(a, x, cmat, smat, cmat, smat, ka, ki, kn, wa, ba, wx, bx, hbias, *extra_args)


def _hyena_filter_taps(L, f1w, f1b, fr1, f2w, f2b, fr2, f3w):
    hp = lax.Precision.HIGHEST
    pos = jnp.arange(L, dtype=F32)
    t = pos / (L - 1)
    w = 2.0 * math.pi * pos / L
    f = jnp.linspace(1e-4, HY_BANDS - 1, HY_BANDS, dtype=F32)
    wf = w[:, None] * f[None, :]
    feat = jnp.concatenate([t[:, None], jnp.cos(wf), -jnp.sin(wf)], axis=-1)
    h = jnp.sin(fr1 * (jnp.dot(feat, f1w, precision=hp) + f1b))
    h = jnp.sin(fr2 * (jnp.dot(h, f2w, precision=hp) + f2b))
    h = jnp.dot(h, f3w, precision=hp).astype(F32)
    deltas = jnp.linspace(math.log(HY_TARGET) / HY_SLOW, math.log(HY_TARGET) / HY_FAST, HY_W, dtype=F32)
    window = jnp.exp(-t[:, None] * jnp.abs(deltas)[None, :])
    w = HY_W
    parts = [h[:, k * w:(k + 1) * w] * window for k in range(4)]
    colsum = [jnp.sum(jnp.abs(p), axis=0, keepdims=True) for p in parts]
    den = [colsum[0] + colsum[1] + EPS, colsum[2] + colsum[3] + EPS]
    fwd = jnp.concatenate([parts[0] / den[0], parts[2] / den[1]], axis=1)
    bwd = jnp.concatenate([parts[1] / den[0], parts[3] / den[1]], axis=1)
    bwd = jnp.where(pos[:, None] == 0, 0.0, bwd)
    return fwd + bwd, bwd - fwd


def _dot_t0(a, b):
    return lax.dot_general(a, b, (((0,), (0,)), ((), ())), preferred_element_type=F32)


def _dot_t1(a, b):
    return lax.dot_general(a, b, (((1,), (1,)), ((), ())), preferred_element_type=F32)


def _retention_kernel(*refs, L, has_s0, has_prev):
    refs = list(refs)
    q_ref, k_ref, v_ref, g_ref, dl_ref = refs[:5]
    pos = 5
    s0_ref = st_ref = None
    if has_s0:
        s0_ref = refs[pos]
        pos += 1
    if has_prev:
        pos += 1
    y_ref = refs[pos]
    pos += 1
    if not has_s0:
        st_ref = refs[pos]
        pos += 1
    sb_sc, sf_cur, sb_cur = refs[pos:]
    c = RET_CHUNK
    nc = L // c
    kscale = RET_HEAD_DIM ** -0.5
    ri = lax.broadcasted_iota(jnp.int32, (c, c), 0).astype(F32)
    ci = lax.broadcasted_iota(jnp.int32, (c, c), 1).astype(F32)
    diff = ri - ci
    dec = []
    for h in range(RET_HEADS):
        xf = dl_ref[0, h:h + 1, :]
        xb = dl_ref[1, h:h + 1, :]
        lgf = jnp.minimum(xf, 0.0) - jnp.log1p(jnp.exp(-jnp.abs(xf)))
        lgb = jnp.minimum(xb, 0.0) - jnp.log1p(jnp.exp(-jnp.abs(xb)))
        dec.append(dict(
            mask=(jnp.where(diff >= 0, jnp.exp(lgf * jnp.maximum(diff, 0.0)), 0.0)
                  + jnp.where(diff <= 0, jnp.exp(lgb * jnp.maximum(-diff, 0.0)), 0.0)),
            qdec_f=jnp.exp(lgf * (ri + 1.0)), kdec_f=jnp.exp(lgf * (c - 1.0 - ri)),
            qdec_b=jnp.exp(lgb * (c - ri)), kdec_b=jnp.exp(lgb * ri),
            cd_f=jnp.exp(lgf * c), cd_b=jnp.exp(lgb * c)))
        if has_s0:
            sf_cur[h] = s0_ref[0, 0, 0, h]
            sb_cur[h] = s0_ref[0, 0, 1, h]
        else:
            sf_cur[h] = jnp.zeros((RET_HEAD_DIM, RET_HEAD_DIM), F32)
            sb_cur[h] = jnp.zeros((RET_HEAD_DIM, RET_HEAD_DIM), F32)

    def bwd_body(i, carry):
        j = nc - 1 - i
        r0 = pl.multiple_of(j * c, c)
        for h in range(RET_HEADS):
            hs = slice(h * RET_HEAD_DIM, (h + 1) * RET_HEAD_DIM)
            sb = sb_cur[h]
            sb_sc[h * nc + j] = sb
            kc = k_ref[pl.ds(r0, c), hs] * kscale
            vc = v_ref[pl.ds(r0, c), hs]
            sb_cur[h] = sb * dec[h]["cd_b"] + _dot_t0((kc * dec[h]["kdec_b"]).astype(BF16), vc.astype(BF16))
        return carry

    lax.fori_loop(0, nc, bwd_body, 0)

    def fwd_body(j, carry):
        r0 = pl.multiple_of(j * c, c)
        for h in range(RET_HEADS):
            hs = slice(h * RET_HEAD_DIM, (h + 1) * RET_HEAD_DIM)
            dh = dec[h]
            sf = sf_cur[h]
            qc = q_ref[pl.ds(r0, c), hs]
            kc = k_ref[pl.ds(r0, c), hs] * kscale
            vc = v_ref[pl.ds(r0, c), hs].astype(BF16)
            scores = _dot_t1(qc.astype(BF16), kc.astype(BF16)) * dh["mask"]
            o = jnp.dot(scores.astype(BF16), vc, preferred_element_type=F32)
            o = o + jnp.dot((qc * dh["qdec_f"]).astype(BF16), sf.astype(BF16), preferred_element_type=F32)
            o = o + jnp.dot((qc * dh["qdec_b"]).astype(BF16), sb_sc[h * nc + j].astype(BF16),
                            preferred_element_type=F32)
            mu = jnp.mean(o, axis=-1, keepdims=True)
            var = jnp.mean(jnp.square(o - mu), axis=-1, keepdims=True)
            on = (o - mu) * lax.rsqrt(var + EPS)
            gc = g_ref[pl.ds(r0, c), hs]
            y_ref[pl.ds(r0, c), hs] = gc * (1.0 / (1.0 + jnp.exp(-gc))) * on
            sf_cur[h] = sf * dh["cd_f"] + _dot_t0((kc * dh["kdec_f"]).astype(BF16), vc)
        return carry

    lax.fori_loop(0, nc, fwd_body, 0)
    if st_ref is not None:
        for h in range(RET_HEADS):
            st_ref[0, 0, 0, h] = sf_cur[h]
            st_ref[0, 0, 1, h] = sb_cur[h]


def _retention(proj, dl, layer_i, row0, nseq, L, s0=None, st_prev=None, y_prev=None):
    off = row0 // L
    has_s0 = s0 is not None
    has_prev = (st_prev is not None) or (y_prev is not None)
    assert not (st_prev is not None and y_prev is not None)
    nc = L // RET_CHUNK
    n_ret = (DEPTH + 1) // 2
    col = lambda j: pl.BlockSpec((L, RET_W), lambda b: (b + off, 3 + j))
    in_specs = [col(0), col(1), col(2), col(3),
                pl.BlockSpec((2, RET_HEADS, LANES), lambda b: (0, 0, 0))]
    args = [proj, proj, proj, proj, dl]
    y_spec = pl.BlockSpec((L, RET_W), lambda b: (b + off, 0))
    y_shape = jax.ShapeDtypeStruct((N_TOK, RET_W), F32)
    st_block = (1, 1, 2, RET_HEADS, RET_HEAD_DIM, RET_HEAD_DIM)
    st_spec = pl.BlockSpec(st_block, lambda b: (b, layer_i, 0, 0, 0, 0))
    aliases = {}
    if has_s0:
        in_specs.append(st_spec)
        args.append(s0)
        out_specs, out_shape = y_spec, y_shape
    else:
        out_specs = [y_spec, st_spec]
        out_shape = [y_shape, jax.ShapeDtypeStruct((nseq, n_ret) + st_block[2:], F32)]
    if has_prev:
        aliases = {len(args): 1 if st_prev is not None else 0}
        in_specs.append(pl.BlockSpec(memory_space=pl.ANY))
        args.append(st_prev if st_prev is not None else y_prev)
    state = pltpu.VMEM((RET_HEADS, RET_HEAD_DIM, RET_HEAD_DIM), F32)
    return pl.pallas_call(
        functools.partial(_retention_kernel, L=L, has_s0=has_s0, has_prev=has_prev),
        grid=(nseq,),
        in_specs=in_specs,
        out_specs=out_specs,
        out_shape=out_shape,
        input_output_aliases=aliases,
        scratch_shapes=[pltpu.VMEM((RET_HEADS * nc, RET_HEAD_DIM, RET_HEAD_DIM), F32), state, state],
        compiler_params=_params(("parallel",)),
        name="retention",
    )(*args)


@functools.lru_cache(maxsize=None)
def _rope_tables():
    L = DEC_SEQ
    rows = L // GRID_W
    row = np.repeat(np.arange(rows, dtype=np.float64), GRID_W)
    col = np.tile(np.arange(GRID_W, dtype=np.float64), rows)
    quarter = DIFF_HEAD_DIM // 4
    freqs = ROPE_BASE ** (-np.arange(quarter, dtype=np.float64) / quarter)
    j = np.arange(LANES)
    pos = np.where(((j % DIFF_HEAD_DIM) < DIFF_HEAD_DIM // 2)[None, :], row[:, None], col[:, None])
    ang = pos * freqs[j % quarter][None, :]
    cos = np.cos(ang).astype(np.float32)
    sin = np.sin(ang).astype(np.float32)
    first = ((j % (2 * quarter)) < quarter)[None, :]
    sin_a = np.where(first, -sin, 0.0).astype(np.float32)
    sin_b = np.where(first, 0.0, sin).astype(np.float32)
    return jnp.asarray(cos), jnp.asarray(sin_a), jnp.asarray(sin_b)


def _rope_head(x, cos, sin_a, sin_b):
    quarter = DIFF_HEAD_DIM // 4
    up = pltpu.roll(x, LANES - quarter, axis=1)
    dn = pltpu.roll(x, quarter, axis=1)
    return x * cos + up * sin_a + dn * sin_b


def _kv_prep_kernel(q_ref, k_ref, v_ref, cos_ref, sa_ref, sb_ref, qo_ref, ko_ref, vo_ref):
    cos, sa, sb = cos_ref[...], sa_ref[...], sb_ref[...]
    for h in range(DIFF_HEADS):
        hs = slice(h * LANES, (h + 1) * LANES)
        qo_ref[:, hs] = _rope_head(q_ref[:, hs], cos, sa, sb).astype(BF16)
        ko_ref[0, :, hs] = _rope_head(k_ref[:, hs], cos, sa, sb).astype(BF16)
    vo_ref[0] = v_ref[...].astype(BF16)


def _cache_copy_kernel(ck_ref, cv_ref, k_in, v_in, ko_ref, vo_ref):
    del k_in, v_in
    ko_ref[0] = ck_ref[0, 0].astype(BF16)
    vo_ref[0] = cv_ref[0, 0].astype(BF16)


def _sample_qkv(proj, cache_k, cache_v, layer_i, tm=256):
    cos, sa, sb = _rope_tables()
    lk = PAST_LEN + DEC_SEQ
    pblk = N_PROMPT // tm
    nblk = DEC_SEQ // tm
    cblk = PAST_LEN // tm
    tab = pl.BlockSpec((tm, LANES), lambda b, i: (i, 0))
    q, k, v = pl.pallas_call(
        _kv_prep_kernel,
        grid=(DEC_BATCH, nblk),
        in_specs=[
            pl.BlockSpec((tm, D_MODEL), lambda b, i: (pblk + b * nblk + i, 0)),
            pl.BlockSpec((tm, D_MODEL), lambda b, i: (pblk + b * nblk + i, 1)),
            pl.BlockSpec((tm, D_MODEL), lambda b, i: (pblk + b * nblk + i, 2)),
            tab, tab, tab,
        ],
        out_specs=[
            pl.BlockSpec((tm, D_MODEL), lambda b, i: (b * nblk + i, 0)),
            pl.BlockSpec((1, tm, D_MODEL), lambda b, i: (b, cblk + i, 0)),
            pl.BlockSpec((1, tm, D_MODEL), lambda b, i: (b, cblk + i, 0)),
        ],
        out_shape=[
            jax.ShapeDtypeStruct((N_SAMPLE, D_MODEL), BF16),
            jax.ShapeDtypeStruct((DEC_BATCH, lk, D_MODEL), BF16),
            jax.ShapeDtypeStruct((DEC_BATCH, lk, D_MODEL), BF16),
        ],
        compiler_params=_params(("parallel", "parallel")),
        name="rope_qkv",
    )(proj, proj, proj, cos, sa, sb)
    n_att = DEPTH // 2
    ck = cache_k.reshape(DEC_BATCH, n_att, PAST_LEN, D_MODEL)
    cv = cache_v.reshape(DEC_BATCH, n_att, PAST_LEN, D_MODEL)
    k, v = pl.pallas_call(
        _cache_copy_kernel,
        grid=(DEC_BATCH,),
        in_specs=[
            pl.BlockSpec((1, 1, PAST_LEN, D_MODEL), lambda b: (b, layer_i, 0, 0)),
            pl.BlockSpec((1, 1, PAST_LEN, D_MODEL), lambda b: (b, layer_i, 0, 0)),
            pl.BlockSpec(memory_space=pl.ANY),
            pl.BlockSpec(memory_space=pl.ANY),
        ],
        out_specs=[
            pl.BlockSpec((1, PAST_LEN, D_MODEL), lambda b: (b, 0, 0)),
            pl.BlockSpec((1, PAST_LEN, D_MODEL), lambda b: (b, 0, 0)),
        ],
        out_shape=[
            jax.ShapeDtypeStruct((DEC_BATCH, lk, D_MODEL), BF16),
            jax.ShapeDtypeStruct((DEC_BATCH, lk, D_MODEL), BF16),
        ],
        input_output_aliases={2: 0, 3: 1},
        compiler_params=_params(("parallel",)),
        name="cache_prepend",
    )(ck, cv, k, v)
    return q, k, v


def _diff_attn_kernel(*refs, lam_init, batched_kv, cache_out):
    q_ref, k_ref, v_ref, lam_ref, sg_ref = refs[:5]
    if cache_out:
        o_ref, kc_ref, vc_ref = refs[-3:]
        kc_ref[0, 0] = k_ref[...]
        vc_ref[0, 0] = v_ref[...]
    else:
        o_ref = refs[-1]
    lv = lam_ref[...]
    lam = (jnp.exp(jnp.sum(lv[0:1] * lv[1:2], axis=-1, keepdims=True))
           - jnp.exp(jnp.sum(lv[2:3] * lv[3:4], axis=-1, keepdims=True)) + lam_init)
    lane = lax.broadcasted_iota(jnp.int32, (1, LANES), 1)
    m1 = (lane < DIFF_HEAD_DIM).astype(F32)
    m2 = 1.0 - m1
    scale = DIFF_HEAD_DIM ** -0.5
    for h in range(DIFF_HEADS):
        hs = slice(h * LANES, (h + 1) * LANES)
        q = q_ref[:, hs].astype(F32) * scale
        if batched_kv:
            k = k_ref[0, :, hs].astype(BF16)
            v = v_ref[0, :, hs].astype(BF16)
        else:
            k = k_ref[:, hs].astype(BF16)
            v = v_ref[:, hs].astype(BF16)
        outs = []
        for m in (m1, m2):
            s = _dot_t1((q * m).astype(BF16), k)
            s = s - jnp.max(s, axis=-1, keepdims=True)
            p = jnp.exp(s)
            l = jnp.sum(p, axis=-1, keepdims=True)
            outs.append(jnp.dot(p.astype(BF16), v, preferred_element_type=F32) / l)
        o = outs[0] - lam * outs[1]
        ms = jnp.mean(o * o, axis=-1, keepdims=True)
        o_ref[:, hs] = o * lax.rsqrt(ms + EPS) * sg_ref[...] * (1.0 - lam_init)


def _diff_attention(q, k, v, lam_vec, subln, lam_init, *, nb, lq, lk, tq, q_row0, q_col, kv_cols, batched_kv,
                    out_row0=0, out_prev=None, cache_layer=None, cache_prev=None):
    nq = lq // tq
    qoff = q_row0 // tq
    ooff = out_row0 // tq
    cache_out = cache_layer is not None
    extra_specs, extra_args, aliases = [], [], {}
    if out_prev is not None:
        extra_specs, extra_args, aliases = [pl.BlockSpec(memory_space=pl.ANY)], [out_prev], {5: 0}
    o_spec = pl.BlockSpec((tq, D_MODEL), lambda b, i: (ooff + b * nq + i, 0))
    o_shape = jax.ShapeDtypeStruct((N_TOK, D_MODEL), F32)
    if cache_out:
        assert not batched_kv and nq == 1 and out_prev is None
        if cache_prev is not None:
            extra_specs = [pl.BlockSpec(memory_space=pl.ANY)] * 2
            extra_args = list(cache_prev)
            aliases = {5: 1, 6: 2}
        c_spec = pl.BlockSpec((1, 1, lk, D_MODEL), lambda b, i: (b, cache_layer, 0, 0))
        c_shape = jax.ShapeDtypeStruct((nb, DEPTH // 2, lk, D_MODEL), F32)
        o_spec, o_shape = [o_spec, c_spec, c_spec], [o_shape, c_shape, c_shape]
    q_spec = pl.BlockSpec((tq, D_MODEL), lambda b, i: (qoff + b * nq + i, q_col))
    if batched_kv:
        k_spec = pl.BlockSpec((1, lk, D_MODEL), lambda b, i: (b, 0, 0))
        v_spec = k_spec
    else:
        k_spec = pl.BlockSpec((lk, D_MODEL), lambda b, i: (b, kv_cols[0]))
        v_spec = pl.BlockSpec((lk, D_MODEL), lambda b, i: (b, kv_cols[1]))
    return pl.pallas_call(
        functools.partial(_diff_attn_kernel, lam_init=lam_init, batched_kv=batched_kv, cache_out=cache_out),
        grid=(nb, nq),
        in_specs=[q_spec, k_spec, v_spec,
                  pl.BlockSpec((4, LANES), lambda b, i: (0, 0)),
                  pl.BlockSpec((1, LANES), lambda b, i: (0, 0))] + extra_specs,
        out_specs=o_spec,
        out_shape=o_shape,
        input_output_aliases=aliases,
        compiler_params=_params(("parallel", "arbitrary")),
        name="diff_attention",
    )(q, k, v, lam_vec, subln, *extra_args)


PACK_W = D_MODEL // 4


def _router_kernel(x_ref, m_ref, g_ref, wr_ref, h_ref, aff_ref):
    h = _normed(x_ref[...], g_ref, m_ref, SH2, SC2)
    hb = h.astype(BF16)
    hf = hb.astype(F32)
    bits = lax.bitcast_convert_type(hf, jnp.int32)
    for p in range(2):
        lo = lax.shift_right_logical(bits[:, p * PACK_W:(p + 1) * PACK_W], 16)
        hi = bits[:, (2 + p) * PACK_W:(3 + p) * PACK_W] & jnp.int32(-65536)
        h_ref[p] = hi | lo
    hl = (h - hf).astype(BF16)
    wr = wr_ref[...]
    wh = wr.astype(BF16)
    wl = (wr - wh.astype(F32)).astype(BF16)
    logits = (jnp.dot(hb, wh, preferred_element_type=F32) + jnp.dot(hl, wh, preferred_element_type=F32)
              + jnp.dot(hb, wl, preferred_element_type=F32))
    lt = jnp.transpose(logits)[0:N_EXPERTS, :]
    lt = lt - jnp.max(lt, axis=0, keepdims=True)
    e = jnp.exp(lt)
    aff_ref[...] = e / jnp.sum(e, axis=0, keepdims=True)


def _router(x, mods_l, g, w_router, tm=512):
    wr = jnp.pad(w_router, ((0, 0), (0, LANES - N_EXPERTS)))
    return pl.pallas_call(
        _router_kernel,
        grid=(N_TOK // tm,),
        in_specs=[
            pl.BlockSpec((tm, D_MODEL), lambda i: (i, 0)),
            pl.BlockSpec((1, 6, D_MODEL), lambda i: (_group_of_block(i, tm), 0, 0)),
            pl.BlockSpec((1, D_MODEL), lambda i: (0, 0)),
            pl.BlockSpec((D_MODEL, LANES), lambda i: (0, 0)),
        ],
        out_specs=[
            pl.BlockSpec((2, tm, PACK_W), lambda i: (0, i, 0)),
            pl.BlockSpec((N_EXPERTS, tm), lambda i: (0, i)),
        ],
        out_shape=[
            jax.ShapeDtypeStruct((2, N_TOK, PACK_W), jnp.int32),
            jax.ShapeDtypeStruct((N_EXPERTS, N_TOK), F32),
        ],
        compiler_params=_params(("parallel",)),
        name="norm_router",
    )(x, mods_l, g.reshape(1, D_MODEL), wr)


SC_WINDOW = 128


def _gather_rows(table, idx):
    n = idx.shape[0]
    mesh = plsc.VectorSubcoreMesh(core_axis_name="core", subcore_axis_name="subcore")

    @pl.kernel(out_type=jax.ShapeDtypeStruct((n, PACK_W), table.dtype), mesh=mesh, scratch_types=[])
    def gather_kernel(t_hbm, i_hbm, o_hbm):
        def body(i_vmem, o_vmem):
            pltpu.sync_copy(t_hbm.at[i_vmem.at[0]], o_vmem)

        pltpu.emit_pipeline(
            body,
            grid=(n // SC_WINDOW,),
            in_specs=[pl.BlockSpec((1, SC_WINDOW), index_map=lambda i: (0, i))],
            out_specs=[pl.BlockSpec((SC_WINDOW, PACK_W), index_map=lambda i: (i, 0))],
            core_axis_name=("core", "subcore"),
            dimension_semantics=(pltpu.PARALLEL,),
        )(i_hbm, o_hbm)

    return gather_kernel(table, idx.reshape(1, n))


FFN_TF = 512
FFN_TR = 512


def _unpack_rows(pa, pb):
    def lo(w):
        return lax.bitcast_convert_type(lax.shift_left(w, 16), F32).astype(BF16)

    def hi(w):
        return lax.bitcast_convert_type(w & jnp.int32(-65536), F32).astype(BF16)

    return jnp.concatenate([lo(pa), lo(pb), hi(pa), hi(pb)], axis=1)


def _expert_ffn_kernel(xs_ref, wg_ref, wu_ref, wd_ref, gate_ref, m_ref, o_ref):
    f = pl.program_id(1)
    wg = wg_ref[0].astype(BF16)
    wu = wu_ref[0].astype(BF16)
    wd = wd_ref[0].astype(BF16)
    for r0 in range(0, CAP_T, FFN_TR):
        rs = slice(r0, r0 + FFN_TR)
        xs = _unpack_rows(xs_ref[0, 0, rs, :], xs_ref[1, 0, rs, :])
        a = jnp.dot(xs, wg, preferred_element_type=F32)
        u = jnp.dot(xs, wu, preferred_element_type=F32)
        hid = (a * (1.0 / (1.0 + jnp.exp(-a))) * u).astype(BF16)
        y = jnp.dot(hid, wd, preferred_element_type=F32)

        @pl.when(f == 0)
        def _():
            o_ref[0, rs, :] = y

        @pl.when(jnp.logical_and(f > 0, f < pl.num_programs(1) - 1))
        def _():
            o_ref[0, rs, :] += y

        @pl.when(f == pl.num_programs(1) - 1)
        def _():
            gg = gate_ref[0, rs, :]
            scale = gg[:, 0:1] * m_ref[0, G2:G2 + 1, :]
            for g in range(1, N_GROUPS):
                scale = scale + gg[:, g:g + 1] * m_ref[g, G2:G2 + 1, :]
            o_ref[0, rs, :] = (o_ref[0, rs, :] + y) * scale


def _expert_ffn(xs, wg, wu, wd, gate, mods_l, l):
    return pl.pallas_call(
        _expert_ffn_kernel,
        grid=(N_EXPERTS, EXPERT_FF // FFN_TF),
        in_specs=[
            pl.BlockSpec((2, 1, CAP_T, PACK_W), lambda e, f: (0, e, 0, 0)),
            pl.BlockSpec((None, 1, D_MODEL, FFN_TF), lambda e, f: (l, e, 0, f)),
            pl.BlockSpec((None, 1, D_MODEL, FFN_TF), lambda e, f: (l, e, 0, f)),
            pl.BlockSpec((None, 1, FFN_TF, D_MODEL), lambda e, f: (l, e, f, 0)),
            pl.BlockSpec((1, CAP_T, N_GROUPS), lambda e, f: (e, 0, 0)),
            pl.BlockSpec((N_GROUPS, 6, D_MODEL), lambda e, f: (0, 0, 0)),
        ],
        out_specs=pl.BlockSpec((1, CAP_T, D_MODEL), lambda e, f: (e, 0, 0)),
        out_shape=jax.ShapeDtypeStruct((N_EXPERTS, CAP_T, D_MODEL), F32),
        compiler_params=_params(("parallel", "arbitrary")),
        name="expert_ffn",
    )(xs, wg, wu, wd, gate, mods_l)


def _final_norm_kernel(x_ref, g_ref, o_ref):
    x = x_ref[...]
    ms = jnp.mean(x * x, axis=-1, keepdims=True)
    o_ref[...] = x * lax.rsqrt(ms + EPS) * g_ref[...]


def _final_norm(x, g, row0, n_rows, tm=512):
    off = row0 // tm
    return pl.pallas_call(
        _final_norm_kernel,
        grid=(n_rows // tm,),
        in_specs=[pl.BlockSpec((tm, D_MODEL), lambda i: (i + off, 0)),
                  pl.BlockSpec((1, D_MODEL), lambda i: (0, 0))],
        out_specs=pl.BlockSpec((tm, D_MODEL), lambda i: (i, 0)),
        out_shape=jax.ShapeDtypeStruct((n_rows, D_MODEL), F32),
        compiler_params=_params(("parallel",)),
        name="final_norm",
    )(x, g.reshape(1, D_MODEL))


def _even_layer(x, mods_l, i, state_ret, st_prev, dfts, norm1_g, w_in_even, hy_short_w, hy_short_b, hy_f1_w, hy_f1_b,
                hy_freq1, hy_f2_w, hy_f2_b, hy_freq2, hy_f3_w, hy_bias, ret_decay, w_out_even):
    proj = _norm_project(x, mods_l, norm1_g, w_in_even[i].astype(BF16))
    sw, sbias = hy_short_w[i], hy_short_b[i].reshape(1, 3 * HY_W)
    hb = hy_bias[i]
    y_hy = None
    for (row0, n_rows, L, nseq) in ((0, N_PROMPT, SEQ, 8), (N_PROMPT, N_SAMPLE, DEC_SEQ, 1)):
        s, d = _hyena_filter_taps(L, hy_f1_w[i], hy_f1_b[i], hy_freq1[i], hy_f2_w[i], hy_f2_b[i], hy_freq2[i], hy_f3_w[i])
        spectra = _filter_spectra(s, d, L, dfts[L])
        wv, bv = sw[:, 0:HY_W], sbias[:, 0:HY_W]
        w1, b1 = sw[:, HY_W:2 * HY_W], sbias[:, HY_W:2 * HY_W]
        w2, b2 = sw[:, 2 * HY_W:], sbias[:, 2 * HY_W:]
        z1 = _hyena_conv(proj, 0, proj, 1, row0, n_rows, L, nseq, spectra, 0, wv, bv, w1, b1, hb[0:1], True, dfts[L])
        y_hy = _hyena_conv(z1, 0, proj, 2, row0, n_rows, L, nseq, spectra, 1, wv, bv, w2, b2, hb[1:2], False, dfts[L],
                           full_out=True, out_prev=y_hy)
    dl = jnp.broadcast_to(ret_decay[i].astype(F32)[:, :, None], (2, RET_HEADS, LANES))
    y_ret, st = _retention(proj, dl, i, 0, BATCH, SEQ, st_prev=st_prev)
    y_ret = _retention(proj, dl, i, N_PROMPT, DEC_BATCH, DEC_SEQ, s0=state_ret, y_prev=y_ret)
    wo = w_out_even[i].astype(BF16)
    x = _project_residual([y_hy, y_ret], [wo[:HY_W], wo[HY_W:]], x, mods_l, G1)
    return x, st


def _odd_layer(x, mods_l, l, i, cache_k, cache_v, kv_prev, norm1_g, w_in_odd, lam_q1, lam_k1, lam_q2, lam_k2, subln_g,
               w_out_odd):
    lam_init = 0.8 - 0.6 * math.exp(-0.3 * l)
    proj = _norm_project(x, mods_l, norm1_g, w_in_odd[i].astype(BF16))
    lam_vec = jnp.pad(jnp.stack([lam_q1[i], lam_k1[i], lam_q2[i], lam_k2[i]]), ((0, 0), (0, LANES - DIFF_HEAD_DIM)))
    sg = subln_g[i].reshape(1, LANES)
    o_p, kc, vc = _diff_attention(proj, proj, proj, lam_vec, sg, lam_init, nb=BATCH, lq=SEQ, lk=SEQ, tq=SEQ,
                                  q_row0=0, q_col=0, kv_cols=(1, 2), batched_kv=False,
                                  cache_layer=i, cache_prev=kv_prev)
    qs, ks, vs = _sample_qkv(proj, cache_k, cache_v, i)
    o = _diff_attention(qs, ks, vs, lam_vec, sg, lam_init, nb=DEC_BATCH, lq=DEC_SEQ, lk=PAST_LEN + DEC_SEQ, tq=256,
                        q_row0=0, q_col=0, kv_cols=None, batched_kv=True, out_row0=N_PROMPT, out_prev=o_p)
    x = _project_residual([o], [w_out_odd[i].astype(BF16)], x, mods_l, G1)
    return x, (kc, vc)


def _moe_layer(x, mods_l, l, norm2_g, w_router, wg, wu, wd):
    h, aff = _router(x, mods_l, norm2_g, w_router)
    gate_p, idx_p = lax.top_k(aff[:, :N_PROMPT], CAP_P)
    gate_s, idx_s = lax.top_k(aff[:, N_PROMPT:], CAP_S)
    idx = jnp.concatenate([idx_p, idx_s + N_PROMPT], axis=1)
    gate = jnp.concatenate([gate_p, gate_s], axis=1)
    flat = idx.reshape(-1)
    xs = _gather_rows(h.reshape(2 * N_TOK, PACK_W), jnp.concatenate([flat, flat + N_TOK]))
    xs = xs.reshape(2, N_EXPERTS, CAP_T, PACK_W)
    grp = jnp.where(idx < N_PROMPT, 0, 1 + (idx - N_PROMPT) // DEC_SEQ)
    gate_grp = jnp.where(grp[:, :, None] == jnp.arange(N_GROUPS)[None, None, :], gate[:, :, None], 0.0)
    out = _expert_ffn(xs, wg, wu, wd, gate_grp, mods_l, l)
    return x.at[idx.reshape(-1)].add(out.reshape(-1, D_MODEL))


def kernel(x_prompt, x_sample, state_ret, cache_k, cache_v, c, c_ctx, w_mod, b_mod, norm1_g, norm2_g, w_in_even, hy_short_w, hy_short_b, hy_f1_w, hy_f1_b, hy_freq1, hy_f2_w, hy_f2_b, hy_freq2, hy_f3_w, hy_bias, ret_decay, w_out_even, w_in_odd, lam_q1, lam_k1, lam_q2, lam_k2, subln_g, w_out_odd, moe_router, moe_wg, moe_wu, moe_wd, final_g):
    x = jnp.concatenate([x_prompt.reshape(N_PROMPT, D_MODEL), x_sample.reshape(N_SAMPLE, D_MODEL)], axis=0)
    mods = _modulation(c, c_ctx, w_mod, b_mod)
    dfts = {L: _dft_bf16(L) for L in (SEQ, DEC_SEQ)}
    state_ret = state_ret.astype(F32)
    st = kv = None
    for l in range(DEPTH):
        i = l // 2
        if l % 2 == 0:
            x, st = _even_layer(x, mods[l], i, state_ret, st, dfts, norm1_g[l], w_in_even, hy_short_w, hy_short_b,
                                hy_f1_w, hy_f1_b, hy_freq1, hy_f2_w, hy_f2_b, hy_freq2, hy_f3_w, hy_bias, ret_decay,
                                w_out_even)
        else:
            x, kv = _odd_layer(x, mods[l], l, i, cache_k, cache_v, kv, norm1_g[l], w_in_odd, lam_q1, lam_k1, lam_q2,
                               lam_k2, subln_g, w_out_odd)
        x = _moe_layer(x, mods[l], l, norm2_g[l], moe_router[l], moe_wg, moe_wu, moe_wd)
    y_prompt = _final_norm(x, final_g, 0, N_PROMPT).reshape(BATCH, SEQ, D_MODEL)
    y_sample = _final_norm(x, final_g, N_PROMPT, N_SAMPLE).reshape(DEC_BATCH, DEC_SEQ, D_MODEL)
    cache_shape = (BATCH, DEPTH // 2, SEQ, DIFF_HEADS, 2 * DIFF_HEAD_DIM)
    return (y_prompt, y_sample, st, kv[0].reshape(cache_shape), kv[1].reshape(cache_shape))
```

```python
import functools
import math

import numpy as np
import jax
import jax.numpy as jnp
from jax import lax
from jax.experimental import pallas as pl
from jax.experimental.pallas import tpu as pltpu
from jax.experimental.pallas import tpu_sc as plsc

F32 = jnp.float32
BF16 = jnp.bfloat16

D_MODEL = 1024
BATCH = 32
SEQ = 256
DEPTH = 4
DEC_BATCH = 2
DEC_SEQ = 2048
PAST_LEN = 256
GRID_W = 64
HY_W = 512
HY_EMB = 33
HY_BANDS = 16
HY_FF = 64
HY_TARGET = 1e-2
HY_FAST = 0.3
HY_SLOW = 1.5
RET_W = 512
RET_HEADS = 4
RET_HEAD_DIM = 128
RET_CHUNK = 128
DIFF_HEADS = 8
DIFF_HEAD_DIM = 64
ROPE_BASE = 10000.0
N_EXPERTS = 16
EC_FACTOR = 2
EXPERT_FF = 1024
EVEN_IN = 3 * HY_W + 4 * RET_W
EPS = 1e-6

N_PROMPT = BATCH * SEQ
N_SAMPLE = DEC_BATCH * DEC_SEQ
N_TOK = N_PROMPT + N_SAMPLE
N_GROUPS = 1 + DEC_BATCH
CAP_P = EC_FACTOR * N_PROMPT // N_EXPERTS
CAP_S = EC_FACTOR * N_SAMPLE // N_EXPERTS
CAP_T = CAP_P + CAP_S

LANES = 128
SUBLANES = 8
VMEM_LIMIT = 56 * 1024 * 1024

SH1, SC1, G1, SH2, SC2, G2 = range(6)


def _params(sem, vmem=VMEM_LIMIT):
    return pltpu.CompilerParams(dimension_semantics=sem, vmem_limit_bytes=vmem)


def _group_of_block(i, tm):
    pb = N_PROMPT // tm
    return jnp.where(i < pb, 0, 1 + (i - pb) // (DEC_SEQ // tm))


MOD_TN = 1024


MOD_UNROLL = 4


def _mod_kernel(cb_ref, w_ref, b_ref, o_ref, a_sc):
    nchunk = MOD_TN // LANES
    cv = cb_ref[...]
    a_sc[...] = cv * (1.0 / (1.0 + jnp.exp(-cv)))

    def body(kb, accs):
        accs = list(accs)
        for u in range(MOD_UNROLL):
            k0 = pl.multiple_of((kb * MOD_UNROLL + u) * SUBLANES, SUBLANES)
            a = [a_sc[r, pl.ds(k0, SUBLANES), :] for r in range(N_GROUPS)]
            for ci in range(nchunk):
                wv = w_ref[0, pl.ds(k0, SUBLANES), ci * LANES:(ci + 1) * LANES]
                for r in range(N_GROUPS):
                    accs[ci * N_GROUPS + r] = accs[ci * N_GROUPS + r] + wv * a[r]
        return tuple(accs)

    init = tuple(jnp.zeros((SUBLANES, LANES), F32) for _ in range(N_GROUPS * nchunk))
    accs = lax.fori_loop(0, D_MODEL // (SUBLANES * MOD_UNROLL), body, init)
    o_ref[...] = jnp.zeros(o_ref.shape, F32)
    for r in range(N_GROUPS):
        for ci in range(nchunk):
            row = jnp.sum(accs[ci * N_GROUPS + r], axis=0, keepdims=True)
            o_ref[0, r:r + 1, ci * LANES:(ci + 1) * LANES] = row + b_ref[0, :, ci * LANES:(ci + 1) * LANES]


def _modulation(c, c_ctx, w_mod, b_mod):
    cond = jnp.concatenate([c_ctx[None, :], c], axis=0)
    cb = jnp.broadcast_to(cond[:, :, None], (N_GROUPS, D_MODEL, LANES))
    out = pl.pallas_call(
        _mod_kernel,
        grid=(DEPTH, 6 * D_MODEL // MOD_TN),
        in_specs=[
            pl.BlockSpec((N_GROUPS, D_MODEL, LANES), lambda l, j: (0, 0, 0)),
            pl.BlockSpec((1, D_MODEL, MOD_TN), lambda l, j: (l, 0, j)),
            pl.BlockSpec((1, 1, MOD_TN), lambda l, j: (l, 0, j)),
        ],
        out_specs=pl.BlockSpec((1, SUBLANES, MOD_TN), lambda l, j: (l, 0, j)),
        out_shape=jax.ShapeDtypeStruct((DEPTH, SUBLANES, 6 * D_MODEL), F32),
        scratch_shapes=[pltpu.VMEM((N_GROUPS, D_MODEL, LANES), F32)],
        compiler_params=_params(("parallel", "parallel")),
        name="ada_mod",
    )(cb, w_mod, b_mod.reshape(DEPTH, 1, 6 * D_MODEL))
    return out[:, :N_GROUPS].reshape(DEPTH, N_GROUPS, 6, D_MODEL)


def _normed(x, g_ref, m_ref, shift, scale):
    ms = jnp.mean(x * x, axis=-1, keepdims=True)
    y = x * lax.rsqrt(ms + EPS) * g_ref[...]
    return y * (1.0 + m_ref[0, scale:scale + 1, :]) + m_ref[0, shift:shift + 1, :]


def _norm_mm_kernel(x_ref, m_ref, g_ref, w_ref, o_ref, *, tn):
    h = _normed(x_ref[...], g_ref, m_ref, SH1, SC1).astype(BF16)
    for c0 in range(0, o_ref.shape[1], tn):
        o_ref[:, c0:c0 + tn] = jnp.dot(h, w_ref[:, c0:c0 + tn], preferred_element_type=F32)


def _norm_project(x, mods_l, g, w_bf16, tm=512, tn=512):
    nout = w_bf16.shape[1]
    return pl.pallas_call(
        functools.partial(_norm_mm_kernel, tn=tn),
        grid=(N_TOK // tm,),
        in_specs=[
            pl.BlockSpec((tm, D_MODEL), lambda i: (i, 0)),
            pl.BlockSpec((1, 6, D_MODEL), lambda i: (_group_of_block(i, tm), 0, 0)),
            pl.BlockSpec((1, D_MODEL), lambda i: (0, 0)),
            pl.BlockSpec((D_MODEL, nout), lambda i: (0, 0)),
        ],
        out_specs=pl.BlockSpec((tm, nout), lambda i: (i, 0)),
        out_shape=jax.ShapeDtypeStruct((N_TOK, nout), F32),
        compiler_params=_params(("parallel",)),
        name="norm_project",
    )(x, mods_l, g.reshape(1, D_MODEL), w_bf16)


def _proj_res_kernel(*refs, n_in, gate):
    a_refs = refs[:n_in]
    w_refs = refs[n_in:2 * n_in]
    x_ref, m_ref, o_ref = refs[2 * n_in:]
    acc = None
    for a_ref, w_ref in zip(a_refs, w_refs):
        t = jnp.dot(a_ref[...].astype(BF16), w_ref[...], preferred_element_type=F32)
        acc = t if acc is None else acc + t
    o_ref[...] = x_ref[...] + m_ref[0, gate:gate + 1, :] * acc


def _project_residual(acts, ws_bf16, x, mods_l, gate, tm=512):
    n_in = len(acts)
    in_specs = [pl.BlockSpec((tm, a.shape[1]), lambda i: (i, 0)) for a in acts]
    in_specs += [pl.BlockSpec(w.shape, lambda i: (0, 0)) for w in ws_bf16]
    in_specs += [
        pl.BlockSpec((tm, D_MODEL), lambda i: (i, 0)),
        pl.BlockSpec((1, 6, D_MODEL), lambda i: (_group_of_block(i, tm), 0, 0)),
    ]
    return pl.pallas_call(
        functools.partial(_proj_res_kernel, n_in=n_in, gate=gate),
        grid=(N_TOK // tm,),
        in_specs=in_specs,
        out_specs=pl.BlockSpec((tm, D_MODEL), lambda i: (i, 0)),
        out_shape=jax.ShapeDtypeStruct((N_TOK, D_MODEL), F32),
        input_output_aliases={2 * n_in: 0},
        compiler_params=_params(("parallel",)),
        name="project_residual",
    )(*acts, *ws_bf16, x, mods_l)


@functools.lru_cache(maxsize=None)
def _dft_mats(L):
    n = 2 * L
    ft = (np.arange(L, dtype=np.int64)[:, None] * np.arange(L, dtype=np.int64)[None, :]) % n
    ang = ft.astype(np.float64) * (2.0 * np.pi / n)
    return np.cos(ang).astype(np.float32), np.sin(ang).astype(np.float32)


def _dft_bf16(L):
    c, s = _dft_mats(L)
    return jnp.asarray(c).astype(BF16), jnp.asarray(s).astype(BF16)


def _alt_sign(shape, row0):
    t = lax.broadcasted_iota(jnp.int32, shape, 0) + row0
    return (1 - 2 * (t & 1)).astype(F32)


def _filter_dft_kernel(s_ref, d_ref, c_ref, sn_ref, ka_ref, ki_ref, kn_ref, *, L, fb):
    f0 = pl.program_id(0) * fb
    n = 2.0 * L
    s = s_ref[...]
    r = jnp.dot(c_ref[...], s, preferred_element_type=F32)
    im = jnp.dot(sn_ref[...], d_ref[...], preferred_element_type=F32)
    fidx = lax.broadcasted_iota(jnp.int32, r.shape, 0) + f0
    scale = jnp.where(fidx == 0, 1.0 / n, 2.0 / n)
    ka_ref[...] = r * scale
    ki_ref[...] = im * (2.0 / n)
    nyq = jnp.sum(s.astype(F32) * _alt_sign(s.shape, 0), axis=0, keepdims=True) * (1.0 / n)
    kn_ref[...] = jnp.broadcast_to(nyq, kn_ref.shape)


def _filter_spectra(s, d, L, dft):
    fb = min(L, 512)
    cmat, smat = dft
    w = 2 * HY_W
    return pl.pallas_call(
        functools.partial(_filter_dft_kernel, L=L, fb=fb),
        grid=(L // fb,),
        in_specs=[
            pl.BlockSpec((L, w), lambda f: (0, 0)),
            pl.BlockSpec((L, w), lambda f: (0, 0)),
            pl.BlockSpec((fb, L), lambda f: (f, 0)),
            pl.BlockSpec((fb, L), lambda f: (f, 0)),
        ],
        out_specs=[
            pl.BlockSpec((fb, w), lambda f: (f, 0)),
            pl.BlockSpec((fb, w), lambda f: (f, 0)),
            pl.BlockSpec((SUBLANES, w), lambda f: (0, 0)),
        ],
        out_shape=[
            jax.ShapeDtypeStruct((L, w), F32),
            jax.ShapeDtypeStruct((L, w), F32),
            jax.ShapeDtypeStruct((SUBLANES, w), F32),
        ],
        compiler_params=_params(("arbitrary",)),
        name="hyena_filter_dft",
    )(s.astype(BF16), d.astype(BF16), cmat, smat)


HY_TILE = 256


def _short_conv_tile(ref, r0, L, w_ref, b_ref):
    t = HY_TILE
    cur = ref[r0:r0 + t, :]
    rid = lax.broadcasted_iota(jnp.int32, cur.shape, 0)
    if r0 % L == 0:
        prev = jnp.where(rid == 0, 0.0, pltpu.roll(cur, 1, axis=0))
    else:
        prev = ref[r0 - 1:r0 - 1 + t, :]
    if (r0 + t) % L == 0:
        nxt = jnp.where(rid == t - 1, 0.0, pltpu.roll(cur, t - 1, axis=0))
    else:
        nxt = ref[r0 + 1:r0 + 1 + t, :]
    return prev * w_ref[0:1, :] + cur * w_ref[1:2, :] + nxt * w_ref[2:3, :] + b_ref[...]


def _hyena_conv_kernel(*refs, L, nseq, conv_a):
    (a_ref, x_ref, cr_ref, sr_ref, cc_ref, sc_ref, ka_ref, ki_ref, kn_ref,
     wa_ref, ba_ref, wx_ref, bx_ref, hb_ref) = refs[:14]
    o_ref, z_sc, acc_sc = refs[-3:]
    f = pl.program_id(1)
    nf = pl.num_programs(1)
    rows = nseq * L

    @pl.when(f == 0)
    def _():
        for q in range(nseq):
            nyq = jnp.zeros((1, HY_W), F32)
            for r0 in range(q * L, (q + 1) * L, HY_TILE):
                if conv_a:
                    zt = _short_conv_tile(a_ref, r0, L, wa_ref, ba_ref)
                else:
                    zt = a_ref[r0:r0 + HY_TILE, :]
                z_sc[r0:r0 + HY_TILE, :] = zt
                nyq = nyq + jnp.sum(zt * _alt_sign(zt.shape, r0), axis=0, keepdims=True)
            nyq = nyq * kn_ref[0:1, :]
            for r0 in range(q * L, (q + 1) * L, HY_TILE):
                acc_sc[r0:r0 + HY_TILE, :] = _alt_sign((HY_TILE, HY_W), r0) * nyq

    ka = ka_ref[...]
    ki = ki_ref[...]
    for q in range(nseq):
        z = z_sc[q * L:(q + 1) * L, :].astype(BF16)
        a = jnp.dot(cr_ref[...], z, preferred_element_type=F32)
        b = jnp.dot(sr_ref[...], z, preferred_element_type=F32)
        p = (a * ka + b * ki).astype(BF16)
        qq = (b * ka - a * ki).astype(BF16)
        acc_sc[q * L:(q + 1) * L, :] += (jnp.dot(cc_ref[...], p, preferred_element_type=F32)
                                         + jnp.dot(sc_ref[...], qq, preferred_element_type=F32))

    @pl.when(f == nf - 1)
    def _():
        for r0 in range(0, rows, HY_TILE):
            y = acc_sc[r0:r0 + HY_TILE, :] + z_sc[r0:r0 + HY_TILE, :] * hb_ref[...]
            o_ref[r0:r0 + HY_TILE, :] = (y * _short_conv_tile(x_ref, r0, L, wx_ref, bx_ref)).astype(o_ref.dtype)


def _hyena_conv(a, a_col, x, x_col, row0, n_rows, L, nseq, spectra, filt, wa, ba, wx, bx, hbias, conv_a, dft,
                full_out=False, out_prev=None):
    fb = min(L, 256)
    cmat, smat = dft
    ka, ki, kn = spectra
    rb = nseq * L
    a_off = row0 // rb if a.shape[0] != n_rows else 0
    x_off = row0 // rb
    o_off = row0 // rb if full_out else 0
    extra_specs, extra_args, aliases = [], [], {}
    if out_prev is not None:
        extra_specs, extra_args, aliases = [pl.BlockSpec(memory_space=pl.ANY)], [out_prev], {14: 0}
    return pl.pallas_call(
        functools.partial(_hyena_conv_kernel, L=L, nseq=nseq, conv_a=conv_a),
        grid=(n_rows // rb, L // fb),
        input_output_aliases=aliases,
        in_specs=extra_specs[:0] + [
            pl.BlockSpec((rb, HY_W), lambda i, f: (i + a_off, a_col)),
            pl.BlockSpec((rb, HY_W), lambda i, f: (i + x_off, x_col)),
            pl.BlockSpec((fb, L), lambda i, f: (f, 0)),
            pl.BlockSpec((fb, L), lambda i, f: (f, 0)),
            pl.BlockSpec((L, fb), lambda i, f: (0, f)),
            pl.BlockSpec((L, fb), lambda i, f: (0, f)),
            pl.BlockSpec((fb, HY_W), lambda i, f: (f, filt)),
            pl.BlockSpec((fb, HY_W), lambda i, f: (f, filt)),
            pl.BlockSpec((SUBLANES, HY_W), lambda i, f: (0, filt)),
            pl.BlockSpec((3, HY_W), lambda i, f: (0, 0)),
            pl.BlockSpec((1, HY_W), lambda i, f: (0, 0)),
            pl.BlockSpec((3, HY_W), lambda i, f: (0, 0)),
            pl.BlockSpec((1, HY_W), lambda i, f: (0, 0)),
            pl.BlockSpec((1, HY_W), lambda i, f: (0, 0)),
        ] + extra_specs,
        out_specs=pl.BlockSpec((rb, HY_W), lambda i, f: (i + o_off, 0)),
        out_shape=jax.ShapeDtypeStruct((N_TOK, HY_W), BF16) if full_out else jax.ShapeDtypeStruct((n_rows, HY_W), F32),
        scratch_shapes=[pltpu.VMEM((rb, HY_W), F32), pltpu.VMEM((rb, HY_W), F32)],
        compiler_params=_params(("parallel", "arbitrary")),
        name="hyena_conv",
    )(a, x, cmat, smat, cmat, smat, ka, ki, kn, wa, ba, wx, bx, hbias, *extra_args)


def _hyena_filter_taps(L, f1w, f1b, fr1, f2w, f2b, fr2, f3w):
    hp = lax.Precision.HIGHEST
    pos = jnp.arange(L, dtype=F32)
    t = pos / (L - 1)
    w = 2.0 * math.pi * pos / L
    f = jnp.linspace(1e-4, HY_BANDS - 1, HY_BANDS, dtype=F32)
    wf = w[:, None] * f[None, :]
    feat = jnp.concatenate([t[:, None], jnp.cos(wf), -jnp.sin(wf)], axis=-1)
    h = jnp.sin(fr1 * (jnp.dot(feat, f1w, precision=hp) + f1b))
    h = jnp.sin(fr2 * (jnp.dot(h, f2w, precision=hp) + f2b))
    h = jnp.dot(h, f3w, precision=hp).astype(F32)
    deltas = jnp.linspace(math.log(HY_TARGET) / HY_SLOW, math.log(HY_TARGET) / HY_FAST, HY_W, dtype=F32)
    window = jnp.exp(-t[:, None] * jnp.abs(deltas)[None, :])
    w = HY_W
    parts = [h[:, k * w:(k + 1) * w] * window for k in range(4)]
    colsum = [jnp.sum(jnp.abs(p), axis=0, keepdims=True) for p in parts]
    den = [colsum[0] + colsum[1] + EPS, colsum[2] + colsum[3] + EPS]
    fwd = jnp.concatenate([parts[0] / den[0], parts[2] / den[1]], axis=1)
    bwd = jnp.concatenate([parts[1] / den[0], parts[3] / den[1]], axis=1)
    bwd = jnp.where(pos[:, None] == 0, 0.0, bwd)
    return fwd + bwd, bwd - fwd


def _dot_t0(a, b):
    return lax.dot_general(a, b, (((0,), (0,)), ((), ())), preferred_element_type=F32)


def _dot_t1(a, b):
    return lax.dot_general(a, b, (((1,), (1,)), ((), ())), preferred_element_type=F32)


def _retention_kernel(*refs, L, has_s0, has_prev):
    refs = list(refs)
    q_ref, k_ref, v_ref, g_ref, dl_ref = refs[:5]
    pos = 5
    s0_ref = st_ref = None
    if has_s0:
        s0_ref = refs[pos]
        pos += 1
    if has_prev:
        pos += 1
    y_ref = refs[pos]
    pos += 1
    if not has_s0:
        st_ref = refs[pos]
        pos += 1
    sb_sc, sf_cur, sb_cur = refs[pos:]
    c = RET_CHUNK
    nc = L // c
    kscale = RET_HEAD_DIM ** -0.5
    ri = lax.broadcasted_iota(jnp.int32, (c, c), 0).astype(F32)
    ci = lax.broadcasted_iota(jnp.int32, (c, c), 1).astype(F32)
    diff = ri - ci
    dec = []
    for h in range(RET_HEADS):
        xf = dl_ref[0, h:h + 1, :]
        xb = dl_ref[1, h:h + 1, :]
        lgf = jnp.minimum(xf, 0.0) - jnp.log1p(jnp.exp(-jnp.abs(xf)))
        lgb = jnp.minimum(xb, 0.0) - jnp.log1p(jnp.exp(-jnp.abs(xb)))
        dec.append(dict(
            mask=(jnp.where(diff >= 0, jnp.exp(lgf * jnp.maximum(diff, 0.0)), 0.0)
                  + jnp.where(diff <= 0, jnp.exp(lgb * jnp.maximum(-diff, 0.0)), 0.0)),
            qdec_f=jnp.exp(lgf * (ri + 1.0)), kdec_f=jnp.exp(lgf * (c - 1.0 - ri)),
            qdec_b=jnp.exp(lgb * (c - ri)), kdec_b=jnp.exp(lgb * ri),
            cd_f=jnp.exp(lgf * c), cd_b=jnp.exp(lgb * c)))
        if has_s0:
            sf_cur[h] = s0_ref[0, 0, 0, h]
            sb_cur[h] = s0_ref[0, 0, 1, h]
        else:
            sf_cur[h] = jnp.zeros((RET_HEAD_DIM, RET_HEAD_DIM), F32)
            sb_cur[h] = jnp.zeros((RET_HEAD_DIM, RET_HEAD_DIM), F32)

    def bwd_body(i, carry):
        j = nc - 1 - i
        r0 = pl.multiple_of(j * c, c)
        for h in range(RET_HEADS):
            hs = slice(h * RET_HEAD_DIM, (h + 1) * RET_HEAD_DIM)
            sb = sb_cur[h]
            sb_sc[h * nc + j] = sb
            kc = k_ref[pl.ds(r0, c), hs] * kscale
            vc = v_ref[pl.ds(r0, c), hs]
            sb_cur[h] = sb * dec[h]["cd_b"] + _dot_t0((kc * dec[h]["kdec_b"]).astype(BF16), vc.astype(BF16))
        return carry

    lax.fori_loop(0, nc, bwd_body, 0)

    def fwd_body(j, carry):
        r0 = pl.multiple_of(j * c, c)
        for h in range(RET_HEADS):
            hs = slice(h * RET_HEAD_DIM, (h + 1) * RET_HEAD_DIM)
            dh = dec[h]
            sf = sf_cur[h]
            qc = q_ref[pl.ds(r0, c), hs]
            kc = k_ref[pl.ds(r0, c), hs] * kscale
            vc = v_ref[pl.ds(r0, c), hs].astype(BF16)
            scores = _dot_t1(qc.astype(BF16), kc.astype(BF16)) * dh["mask"]
            o = jnp.dot(scores.astype(BF16), vc, preferred_element_type=F32)
            o = o + jnp.dot((qc * dh["qdec_f"]).astype(BF16), sf.astype(BF16), preferred_element_type=F32)
            o = o + jnp.dot((qc * dh["qdec_b"]).astype(BF16), sb_sc[h * nc + j].astype(BF16),
                            preferred_element_type=F32)
            mu = jnp.mean(o, axis=-1, keepdims=True)
            var = jnp.mean(jnp.square(o - mu), axis=-1, keepdims=True)
            on = (o - mu) * lax.rsqrt(var + EPS)
            gc = g_ref[pl.ds(r0, c), hs]
            y_ref[pl.ds(r0, c), hs] = (gc * (1.0 / (1.0 + jnp.exp(-gc))) * on).astype(y_ref.dtype)
            sf_cur[h] = sf * dh["cd_f"] + _dot_t0((kc * dh["kdec_f"]).astype(BF16), vc)
        return carry

    lax.fori_loop(0, nc, fwd_body, 0)
    if st_ref is not None:
        for h in range(RET_HEADS):
            st_ref[0, 0, 0, h] = sf_cur[h]
            st_ref[0, 0, 1, h] = sb_cur[h]


def _retention(proj, dl, layer_i, row0, nseq, L, s0=None, st_prev=None, y_prev=None):
    off = row0 // L
    has_s0 = s0 is not None
    has_prev = (st_prev is not None) or (y_prev is not None)
    assert not (st_prev is not None and y_prev is not None)
    nc = L // RET_CHUNK
    n_ret = (DEPTH + 1) // 2
    col = lambda j: pl.BlockSpec((L, RET_W), lambda b: (b + off, 3 + j))
    in_specs = [col(0), col(1), col(2), col(3),
                pl.BlockSpec((2, RET_HEADS, LANES), lambda b: (0, 0, 0))]
    args = [proj, proj, proj, proj, dl]
    y_spec = pl.BlockSpec((L, RET_W), lambda b: (b + off, 0))
    y_shape = jax.ShapeDtypeStruct((N_TOK, RET_W), BF16)
    st_block = (1, 1, 2, RET_HEADS, RET_HEAD_DIM, RET_HEAD_DIM)
    st_spec = pl.BlockSpec(st_block, lambda b: (b, layer_i, 0, 0, 0, 0))
    aliases = {}
    if has_s0:
        in_specs.append(st_spec)
        args.append(s0)
        out_specs, out_shape = y_spec, y_shape
    else:
        out_specs = [y_spec, st_spec]
        out_shape = [y_shape, jax.ShapeDtypeStruct((nseq, n_ret) + st_block[2:], F32)]
    if has_prev:
        aliases = {len(args): 1 if st_prev is not None else 0}
        in_specs.append(pl.BlockSpec(memory_space=pl.ANY))
        args.append(st_prev if st_prev is not None else y_prev)
    state = pltpu.VMEM((RET_HEADS, RET_HEAD_DIM, RET_HEAD_DIM), F32)
    return pl.pallas_call(
        functools.partial(_retention_kernel, L=L, has_s0=has_s0, has_prev=has_prev),
        grid=(nseq,),
        in_specs=in_specs,
        out_specs=out_specs,
        out_shape=out_shape,
        input_output_aliases=aliases,
        scratch_shapes=[pltpu.VMEM((RET_HEADS * nc, RET_HEAD_DIM, RET_HEAD_DIM), F32), state, state],
        compiler_params=_params(("parallel",)),
        name="retention",
    )(*args)


@functools.lru_cache(maxsize=None)
def _rope_tables():
    L = DEC_SEQ
    rows = L // GRID_W
    row = np.repeat(np.arange(rows, dtype=np.float64), GRID_W)
    col = np.tile(np.arange(GRID_W, dtype=np.float64), rows)
    quarter = DIFF_HEAD_DIM // 4
    freqs = ROPE_BASE ** (-np.arange(quarter, dtype=np.float64) / quarter)
    j = np.arange(LANES)
    pos = np.where(((j % DIFF_HEAD_DIM) < DIFF_HEAD_DIM // 2)[None, :], row[:, None], col[:, None])
    ang = pos * freqs[j % quarter][None, :]
    cos = np.cos(ang).astype(np.float32)
    sin = np.sin(ang).astype(np.float32)
    first = ((j % (2 * quarter)) < quarter)[None, :]
    sin_a = np.where(first, -sin, 0.0).astype(np.float32)
    sin_b = np.where(first, 0.0, sin).astype(np.float32)
    return jnp.asarray(cos), jnp.asarray(sin_a), jnp.asarray(sin_b)


def _rope_head(x, cos, sin_a, sin_b):
    quarter = DIFF_HEAD_DIM // 4
    up = pltpu.roll(x, LANES - quarter, axis=1)
    dn = pltpu.roll(x, quarter, axis=1)
    return x * cos + up * sin_a + dn * sin_b


def _kv_prep_kernel(q_ref, k_ref, v_ref, cos_ref, sa_ref, sb_ref, qo_ref, ko_ref, vo_ref):
    cos, sa, sb = cos_ref[...], sa_ref[...], sb_ref[...]
    for h in range(DIFF_HEADS):
        hs = slice(h * LANES, (h + 1) * LANES)
        qo_ref[:, hs] = _rope_head(q_ref[:, hs], cos, sa, sb).astype(BF16)
        ko_ref[0, :, hs] = _rope_head(k_ref[:, hs], cos, sa, sb).astype(BF16)
    vo_ref[0] = v_ref[...].astype(BF16)


def _cache_copy_kernel(ck_ref, cv_ref, k_in, v_in, ko_ref, vo_ref):
    del k_in, v_in
    ko_ref[0] = ck_ref[0, 0].astype(BF16)
    vo_ref[0] = cv_ref[0, 0].astype(BF16)


def _sample_qkv(proj, cache_k, cache_v, layer_i, tm=256):
    cos, sa, sb = _rope_tables()
    lk = PAST_LEN + DEC_SEQ
    pblk = N_PROMPT // tm
    nblk = DEC_SEQ // tm
    cblk = PAST_LEN // tm
    tab = pl.BlockSpec((tm, LANES), lambda b, i: (i, 0))
    q, k, v = pl.pallas_call(
        _kv_prep_kernel,
        grid=(DEC_BATCH, nblk),
        in_specs=[
            pl.BlockSpec((tm, D_MODEL), lambda b, i: (pblk + b * nblk + i, 0)),
            pl.BlockSpec((tm, D_MODEL), lambda b, i: (pblk + b * nblk + i, 1)),
            pl.BlockSpec((tm, D_MODEL), lambda b, i: (pblk + b * nblk + i, 2)),
            tab, tab, tab,
        ],
        out_specs=[
            pl.BlockSpec((tm, D_MODEL), lambda b, i: (b * nblk + i, 0)),
            pl.BlockSpec((1, tm, D_MODEL), lambda b, i: (b, cblk + i, 0)),
            pl.BlockSpec((1, tm, D_MODEL), lambda b, i: (b, cblk + i, 0)),
        ],
        out_shape=[
            jax.ShapeDtypeStruct((N_SAMPLE, D_MODEL), BF16),
            jax.ShapeDtypeStruct((DEC_BATCH, lk, D_MODEL), BF16),
            jax.ShapeDtypeStruct((DEC_BATCH, lk, D_MODEL), BF16),
        ],
        compiler_params=_params(("parallel", "parallel")),
        name="rope_qkv",
    )(proj, proj, proj, cos, sa, sb)
    n_att = DEPTH // 2
    ck = cache_k.reshape(DEC_BATCH, n_att, PAST_LEN, D_MODEL)
    cv = cache_v.reshape(DEC_BATCH, n_att, PAST_LEN, D_MODEL)
    k, v = pl.pallas_call(
        _cache_copy_kernel,
        grid=(DEC_BATCH,),
        in_specs=[
            pl.BlockSpec((1, 1, PAST_LEN, D_MODEL), lambda b: (b, layer_i, 0, 0)),
            pl.BlockSpec((1, 1, PAST_LEN, D_MODEL), lambda b: (b, layer_i, 0, 0)),
            pl.BlockSpec(memory_space=pl.ANY),
            pl.BlockSpec(memory_space=pl.ANY),
        ],
        out_specs=[
            pl.BlockSpec((1, PAST_LEN, D_MODEL), lambda b: (b, 0, 0)),
            pl.BlockSpec((1, PAST_LEN, D_MODEL), lambda b: (b, 0, 0)),
        ],
        out_shape=[
            jax.ShapeDtypeStruct((DEC_BATCH, lk, D_MODEL), BF16),
            jax.ShapeDtypeStruct((DEC_BATCH, lk, D_MODEL), BF16),
        ],
        input_output_aliases={2: 0, 3: 1},
        compiler_params=_params(("parallel",)),
        name="cache_prepend",
    )(ck, cv, k, v)
    return q, k, v


def _diff_attn_kernel(*refs, lam_init, batched_kv, cache_out):
    q_ref, k_ref, v_ref, lam_ref, sg_ref = refs[:5]
    if cache_out:
        o_ref, kc_ref, vc_ref = refs[-3:]
        kc_ref[0, 0] = k_ref[...]
        vc_ref[0, 0] = v_ref[...]
    else:
        o_ref = refs[-1]
    lv = lam_ref[...]
    lam = (jnp.exp(jnp.sum(lv[0:1] * lv[1:2], axis=-1, keepdims=True))
           - jnp.exp(jnp.sum(lv[2:3] * lv[3:4], axis=-1, keepdims=True)) + lam_init)
    lane = lax.broadcasted_iota(jnp.int32, (1, LANES), 1)
    m1 = (lane < DIFF_HEAD_DIM).astype(F32)
    m2 = 1.0 - m1
    scale = DIFF_HEAD_DIM ** -0.5
    for h in range(DIFF_HEADS):
        hs = slice(h * LANES, (h + 1) * LANES)
        q = q_ref[:, hs].astype(F32) * scale
        if batched_kv:
            k = k_ref[0, :, hs].astype(BF16)
            v = v_ref[0, :, hs].astype(BF16)
        else:
            k = k_ref[:, hs].astype(BF16)
            v = v_ref[:, hs].astype(BF16)
        v_ext = jnp.concatenate([v, jnp.ones_like(v)], axis=1)
        outs = []
        for m in (m1, m2):
            s = _dot_t1((q * m).astype(BF16), k)
            s = s - jnp.max(s, axis=-1, keepdims=True)
            pv = jnp.dot(jnp.exp(s).astype(BF16), v_ext, preferred_element_type=F32)
            outs.append(pv[:, :LANES] / pv[:, LANES:])
        o = outs[0] - lam * outs[1]
        ms = jnp.mean(o * o, axis=-1, keepdims=True)
        o_ref[:, hs] = (o * lax.rsqrt(ms + EPS) * sg_ref[...] * (1.0 - lam_init)).astype(o_ref.dtype)


def _diff_attention(q, k, v, lam_vec, subln, lam_init, *, nb, lq, lk, tq, q_row0, q_col, kv_cols, batched_kv,
                    out_row0=0, out_prev=None, cache_layer=None, cache_prev=None):
    nq = lq // tq
    qoff = q_row0 // tq
    ooff = out_row0 // tq
    cache_out = cache_layer is not None
    extra_specs, extra_args, aliases = [], [], {}
    if out_prev is not None:
        extra_specs, extra_args, aliases = [pl.BlockSpec(memory_space=pl.ANY)], [out_prev], {5: 0}
    o_spec = pl.BlockSpec((tq, D_MODEL), lambda b, i: (ooff + b * nq + i, 0))
    o_shape = jax.ShapeDtypeStruct((N_TOK, D_MODEL), BF16)
    if cache_out:
        assert not batched_kv and nq == 1 and out_prev is None
        if cache_prev is not None:
            extra_specs = [pl.BlockSpec(memory_space=pl.ANY)] * 2
            extra_args = list(cache_prev)
            aliases = {5: 1, 6: 2}
        c_spec = pl.BlockSpec((1, 1, lk, D_MODEL), lambda b, i: (b, cache_layer, 0, 0))
        c_shape = jax.ShapeDtypeStruct((nb, DEPTH // 2, lk, D_MODEL), F32)
        o_spec, o_shape = [o_spec, c_spec, c_spec], [o_shape, c_shape, c_shape]
    q_spec = pl.BlockSpec((tq, D_MODEL), lambda b, i: (qoff + b * nq + i, q_col))
    if batched_kv:
        k_spec = pl.BlockSpec((1, lk, D_MODEL), lambda b, i: (b, 0, 0))
        v_spec = k_spec
    else:
        k_spec = pl.BlockSpec((lk, D_MODEL), lambda b, i: (b, kv_cols[0]))
        v_spec = pl.BlockSpec((lk, D_MODEL), lambda b, i: (b, kv_cols[1]))
    return pl.pallas_call(
        functools.partial(_diff_attn_kernel, lam_init=lam_init, batched_kv=batched_kv, cache_out=cache_out),
        grid=(nb, nq),
        in_specs=[q_spec, k_spec, v_spec,
                  pl.BlockSpec((4, LANES), lambda b, i: (0, 0)),
                  pl.BlockSpec((1, LANES), lambda b, i: (0, 0))] + extra_specs,
        out_specs=o_spec,
        out_shape=o_shape,
        input_output_aliases=aliases,
        compiler_params=_params(("parallel", "arbitrary")),
        name="diff_attention",
    )(q, k, v, lam_vec, subln, *extra_args)


PACK_W = D_MODEL // 4


def _router_kernel(x_ref, m_ref, g_ref, wr_ref, h_ref, aff_ref):
    h = _normed(x_ref[...], g_ref, m_ref, SH2, SC2)
    hb = h.astype(BF16)
    hf = hb.astype(F32)
    bits = lax.bitcast_convert_type(hf, jnp.int32)
    for p in range(2):
        lo = lax.shift_right_logical(bits[:, p * PACK_W:(p + 1) * PACK_W], 16)
        hi = bits[:, (2 + p) * PACK_W:(3 + p) * PACK_W] & jnp.int32(-65536)
        h_ref[p] = hi | lo
    hl = (h - hf).astype(BF16)
    wr = wr_ref[...]
    wh = wr.astype(BF16)
    wl = (wr - wh.astype(F32)).astype(BF16)
    logits = (jnp.dot(hb, wh, preferred_element_type=F32) + jnp.dot(hl, wh, preferred_element_type=F32)
              + jnp.dot(hb, wl, preferred_element_type=F32))
    lt = jnp.transpose(logits)[0:N_EXPERTS, :]
    lt = lt - jnp.max(lt, axis=0, keepdims=True)
    e = jnp.exp(lt)
    aff_ref[...] = e / jnp.sum(e, axis=0, keepdims=True)


def _router(x, mods_l, g, w_router, tm=512):
    wr = jnp.pad(w_router, ((0, 0), (0, LANES - N_EXPERTS)))
    return pl.pallas_call(
        _router_kernel,
        grid=(N_TOK // tm,),
        in_specs=[
            pl.BlockSpec((tm, D_MODEL), lambda i: (i, 0)),
            pl.BlockSpec((1, 6, D_MODEL), lambda i: (_group_of_block(i, tm), 0, 0)),
            pl.BlockSpec((1, D_MODEL), lambda i: (0, 0)),
            pl.BlockSpec((D_MODEL, LANES), lambda i: (0, 0)),
        ],
        out_specs=[
            pl.BlockSpec((2, tm, PACK_W), lambda i: (0, i, 0)),
            pl.BlockSpec((N_EXPERTS, tm), lambda i: (0, i)),
        ],
        out_shape=[
            jax.ShapeDtypeStruct((2, N_TOK, PACK_W), jnp.int32),
            jax.ShapeDtypeStruct((N_EXPERTS, N_TOK), F32),
        ],
        compiler_params=_params(("parallel",)),
        name="norm_router",
    )(x, mods_l, g.reshape(1, D_MODEL), wr)


SC_WINDOW = 128


def _gather_rows(table, idx):
    n = idx.shape[0]
    mesh = plsc.VectorSubcoreMesh(core_axis_name="core", subcore_axis_name="subcore")

    @pl.kernel(out_type=jax.ShapeDtypeStruct((n, PACK_W), table.dtype), mesh=mesh, scratch_types=[])
    def gather_kernel(t_hbm, i_hbm, o_hbm):
        def body(i_vmem, o_vmem):
            pltpu.sync_copy(t_hbm.at[i_vmem.at[0]], o_vmem)

        pltpu.emit_pipeline(
            body,
            grid=(n // SC_WINDOW,),
            in_specs=[pl.BlockSpec((1, SC_WINDOW), index_map=lambda i: (0, i))],
            out_specs=[pl.BlockSpec((SC_WINDOW, PACK_W), index_map=lambda i: (i, 0))],
            core_axis_name=("core", "subcore"),
            dimension_semantics=(pltpu.PARALLEL,),
        )(i_hbm, o_hbm)

    return gather_kernel(table, idx.reshape(1, n))


FFN_TF = 512
FFN_TR = 512


def _unpack_rows(pa, pb):
    def lo(w):
        return lax.bitcast_convert_type(lax.shift_left(w, 16), F32).astype(BF16)

    def hi(w):
        return lax.bitcast_convert_type(w & jnp.int32(-65536), F32).astype(BF16)

    return jnp.concatenate([lo(pa), lo(pb), hi(pa), hi(pb)], axis=1)


def _expert_ffn_kernel(xs_ref, wg_ref, wu_ref, wd_ref, gate_ref, m_ref, o_ref):
    f = pl.program_id(1)
    wg = wg_ref[0].astype(BF16)
    wu = wu_ref[0].astype(BF16)
    wd = wd_ref[0].astype(BF16)
    for r0 in range(0, CAP_T, FFN_TR):
        rs = slice(r0, r0 + FFN_TR)
        xs = _unpack_rows(xs_ref[0, 0, rs, :], xs_ref[1, 0, rs, :])
        a = jnp.dot(xs, wg, preferred_element_type=F32)
        u = jnp.dot(xs, wu, preferred_element_type=F32)
        hid = (a * (1.0 / (1.0 + jnp.exp(-a))) * u).astype(BF16)
        y = jnp.dot(hid, wd, preferred_element_type=F32)

        @pl.when(f == 0)
        def _():
            o_ref[0, rs, :] = y

        @pl.when(jnp.logical_and(f > 0, f < pl.num_programs(1) - 1))
        def _():
            o_ref[0, rs, :] += y

        @pl.when(f == pl.num_programs(1) - 1)
        def _():
            gg = gate_ref[0, rs, :]
            scale = gg[:, 0:1] * m_ref[0, G2:G2 + 1, :]
            for g in range(1, N_GROUPS):
                scale = scale + gg[:, g:g + 1] * m_ref[g, G2:G2 + 1, :]
            o_ref[0, rs, :] = (o_ref[0, rs, :] + y) * scale


def _expert_ffn(xs, wg, wu, wd, gate, mods_l, l):
    return pl.pallas_call(
        _expert_ffn_kernel,
        grid=(N_EXPERTS, EXPERT_FF // FFN_TF),
        in_specs=[
            pl.BlockSpec((2, 1, CAP_T, PACK_W), lambda e, f: (0, e, 0, 0)),
            pl.BlockSpec((None, 1, D_MODEL, FFN_TF), lambda e, f: (l, e, 0, f)),
            pl.BlockSpec((None, 1, D_MODEL, FFN_TF), lambda e, f: (l, e, 0, f)),
            pl.BlockSpec((None, 1, FFN_TF, D_MODEL), lambda e, f: (l, e, f, 0)),
            pl.BlockSpec((1, CAP_T, N_GROUPS), lambda e, f: (e, 0, 0)),
            pl.BlockSpec((N_GROUPS, 6, D_MODEL), lambda e, f: (0, 0, 0)),
        ],
        out_specs=pl.BlockSpec((1, CAP_T, D_MODEL), lambda e, f: (e, 0, 0)),
        out_shape=jax.ShapeDtypeStruct((N_EXPERTS, CAP_T, D_MODEL), F32),
        compiler_params=_params(("parallel", "arbitrary")),
        name="expert_ffn",
    )(xs, wg, wu, wd, gate, mods_l)


def _final_norm_kernel(x_ref, g_ref, o_ref):
    x = x_ref[...]
    ms = jnp.mean(x * x, axis=-1, keepdims=True)
    o_ref[...] = x * lax.rsqrt(ms + EPS) * g_ref[...]


def _final_norm(x, g, row0, n_rows, tm=512):
    off = row0 // tm
    return pl.pallas_call(
        _final_norm_kernel,
        grid=(n_rows // tm,),
        in_specs=[pl.BlockSpec((tm, D_MODEL), lambda i: (i + off, 0)),
                  pl.BlockSpec((1, D_MODEL), lambda i: (0, 0))],
        out_specs=pl.BlockSpec((tm, D_MODEL), lambda i: (i, 0)),
        out_shape=jax.ShapeDtypeStruct((n_rows, D_MODEL), F32),
        compiler_params=_params(("parallel",)),
        name="final_norm",
    )(x, g.reshape(1, D_MODEL))


def _even_layer(x, mods_l, i, state_ret, st_prev, dfts, norm1_g, w_in_even, hy_short_w, hy_short_b, hy_f1_w, hy_f1_b,
                hy_freq1, hy_f2_w, hy_f2_b, hy_freq2, hy_f3_w, hy_bias, ret_decay, w_out_even):
    proj = _norm_project(x, mods_l, norm1_g, w_in_even[i].astype(BF16))
    sw, sbias = hy_short_w[i], hy_short_b[i].reshape(1, 3 * HY_W)
    hb = hy_bias[i]
    y_hy = None
    for (row0, n_rows, L, nseq) in ((0, N_PROMPT, SEQ, 8), (N_PROMPT, N_SAMPLE, DEC_SEQ, 1)):
        s, d = _hyena_filter_taps(L, hy_f1_w[i], hy_f1_b[i], hy_freq1[i], hy_f2_w[i], hy_f2_b[i], hy_freq2[i], hy_f3_w[i])
        spectra = _filter_spectra(s, d, L, dfts[L])
        wv, bv = sw[:, 0:HY_W], sbias[:, 0:HY_W]
        w1, b1 = sw[:, HY_W:2 * HY_W], sbias[:, HY_W:2 * HY_W]
        w2, b2 = sw[:, 2 * HY_W:], sbias[:, 2 * HY_W:]
        z1 = _hyena_conv(proj, 0, proj, 1, row0, n_rows, L, nseq, spectra, 0, wv, bv, w1, b1, hb[0:1], True, dfts[L])
        y_hy = _hyena_conv(z1, 0, proj, 2, row0, n_rows, L, nseq, spectra, 1, wv, bv, w2, b2, hb[1:2], False, dfts[L],
                           full_out=True, out_prev=y_hy)
    dl = jnp.broadcast_to(ret_decay[i].astype(F32)[:, :, None], (2, RET_HEADS, LANES))
    y_ret, st = _retention(proj, dl, i, 0, BATCH, SEQ, st_prev=st_prev)
    y_ret = _retention(proj, dl, i, N_PROMPT, DEC_BATCH, DEC_SEQ, s0=state_ret, y_prev=y_ret)
    wo = w_out_even[i].astype(BF16)
    x = _project_residual([y_hy, y_ret], [wo[:HY_W], wo[HY_W:]], x, mods_l, G1)
    return x, st


def _odd_layer(x, mods_l, l, i, cache_k, cache_v, kv_prev, norm1_g, w_in_odd, lam_q1, lam_k1, lam_q2, lam_k2, subln_g,
               w_out_odd):
    lam_init = 0.8 - 0.6 * math.exp(-0.3 * l)
    proj = _norm_project(x, mods_l, norm1_g, w_in_odd[i].astype(BF16))
    lam_vec = jnp.pad(jnp.stack([lam_q1[i], lam_k1[i], lam_q2[i], lam_k2[i]]), ((0, 0), (0, LANES - DIFF_HEAD_DIM)))
    sg = subln_g[i].reshape(1, LANES)
    o_p, kc, vc = _diff_attention(proj, proj, proj, lam_vec, sg, lam_init, nb=BATCH, lq=SEQ, lk=SEQ, tq=SEQ,
                                  q_row0=0, q_col=0, kv_cols=(1, 2), batched_kv=False,
                                  cache_layer=i, cache_prev=kv_prev)
    qs, ks, vs = _sample_qkv(proj, cache_k, cache_v, i)
    o = _diff_attention(qs, ks, vs, lam_vec, sg, lam_init, nb=DEC_BATCH, lq=DEC_SEQ, lk=PAST_LEN + DEC_SEQ, tq=256,
                        q_row0=0, q_col=0, kv_cols=None, batched_kv=True, out_row0=N_PROMPT, out_prev=o_p)
    x = _project_residual([o], [w_out_odd[i].astype(BF16)], x, mods_l, G1)
    return x, (kc, vc)


def _moe_layer(x, mods_l, l, norm2_g, w_router, wg, wu, wd):
    h, aff = _router(x, mods_l, norm2_g, w_router)
    gate_p, idx_p = lax.top_k(aff[:, :N_PROMPT], CAP_P)
    gate_s, idx_s = lax.top_k(aff[:, N_PROMPT:], CAP_S)
    idx = jnp.concatenate([idx_p, idx_s + N_PROMPT], axis=1)
    gate = jnp.concatenate([gate_p, gate_s], axis=1)
    flat = idx.reshape(-1)
    xs = _gather_rows(h.reshape(2 * N_TOK, PACK_W), jnp.concatenate([flat, flat + N_TOK]))
    xs = xs.reshape(2, N_EXPERTS, CAP_T, PACK_W)
    grp = jnp.where(idx < N_PROMPT, 0, 1 + (idx - N_PROMPT) // DEC_SEQ)
    gate_grp = jnp.where(grp[:, :, None] == jnp.arange(N_GROUPS)[None, None, :], gate[:, :, None], 0.0)
    out = _expert_ffn(xs, wg, wu, wd, gate_grp, mods_l, l)
    return x.at[idx.reshape(-1)].add(out.reshape(-1, D_MODEL))


def kernel(x_prompt, x_sample, state_ret, cache_k, cache_v, c, c_ctx, w_mod, b_mod, norm1_g, norm2_g, w_in_even, hy_short_w, hy_short_b, hy_f1_w, hy_f1_b, hy_freq1, hy_f2_w, hy_f2_b, hy_freq2, hy_f3_w, hy_bias, ret_decay, w_out_even, w_in_odd, lam_q1, lam_k1, lam_q2, lam_k2, subln_g, w_out_odd, moe_router, moe_wg, moe_wu, moe_wd, final_g):
    x = jnp.concatenate([x_prompt.reshape(N_PROMPT, D_MODEL), x_sample.reshape(N_SAMPLE, D_MODEL)], axis=0)
    mods = _modulation(c, c_ctx, w_mod, b_mod)
    dfts = {L: _dft_bf16(L) for L in (SEQ, DEC_SEQ)}
    state_ret = state_ret.astype(F32)
    st = kv = None
    for l in range(DEPTH):
        i = l // 2
        if l % 2 == 0:
            x, st = _even_layer(x, mods[l], i, state_ret, st, dfts, norm1_g[l], w_in_even, hy_short_w, hy_short_b,
                                hy_f1_w, hy_f1_b, hy_freq1, hy_f2_w, hy_f2_b, hy_freq2, hy_f3_w, hy_bias, ret_decay,
                                w_out_even)
        else:
            x, kv = _odd_layer(x, mods[l], l, i, cache_k, cache_v, kv, norm1_g[l], w_in_odd, lam_q1, lam_k1, lam_q2,
                               lam_k2, subln_g, w_out_odd)
        x = _moe_layer(x, mods[l], l, norm2_g[l], moe_router[l], moe_wg, moe_wu, moe_wd)
    y_prompt = _final_norm(x, final_g, 0, N_PROMPT).reshape(BATCH, SEQ, D_MODEL)
    y_sample = _final_norm(x, final_g, N_PROMPT, N_SAMPLE).reshape(DEC_BATCH, DEC_SEQ, D_MODEL)
    cache_shape = (BATCH, DEPTH // 2, SEQ, DIFF_HEADS, 2 * DIFF_HEAD_DIM)
    return (y_prompt, y_sample, st, kv[0].reshape(cache_shape), kv[1].reshape(cache_shape))
```

```python
import functools
import math

import numpy as np
import jax
import jax.numpy as jnp
from jax import lax
from jax.experimental import pallas as pl
from jax.experimental.pallas import tpu as pltpu
from jax.experimental.pallas import tpu_sc as plsc

F32 = jnp.float32
BF16 = jnp.bfloat16

D_MODEL = 1024
BATCH = 32
SEQ = 256
DEPTH = 4
DEC_BATCH = 2
DEC_SEQ = 2048
PAST_LEN = 256
GRID_W = 64
HY_W = 512
HY_EMB = 33
HY_BANDS = 16
HY_FF = 64
HY_TARGET = 1e-2
HY_FAST = 0.3
HY_SLOW = 1.5
RET_W = 512
RET_HEADS = 4
RET_HEAD_DIM = 128
RET_CHUNK = 128
DIFF_HEADS = 8
DIFF_HEAD_DIM = 64
ROPE_BASE = 10000.0
N_EXPERTS = 16
EC_FACTOR = 2
EXPERT_FF = 1024
EVEN_IN = 3 * HY_W + 4 * RET_W
EPS = 1e-6

N_PROMPT = BATCH * SEQ
N_SAMPLE = DEC_BATCH * DEC_SEQ
N_TOK = N_PROMPT + N_SAMPLE
N_GROUPS = 1 + DEC_BATCH
CAP_P = EC_FACTOR * N_PROMPT // N_EXPERTS
CAP_S = EC_FACTOR * N_SAMPLE // N_EXPERTS
CAP_T = CAP_P + CAP_S

LANES = 128
SUBLANES = 8
VMEM_LIMIT = 56 * 1024 * 1024

SH1, SC1, G1, SH2, SC2, G2 = range(6)


def _params(sem, vmem=VMEM_LIMIT):
    return pltpu.CompilerParams(dimension_semantics=sem, vmem_limit_bytes=vmem)


def _group_of_block(i, tm):
    pb = N_PROMPT // tm
    return jnp.where(i < pb, 0, 1 + (i - pb) // (DEC_SEQ // tm))


MOD_TN = 1024


MOD_UNROLL = 4


def _mod_kernel(cb_ref, w_ref, b_ref, o_ref, a_sc):
    nchunk = MOD_TN // LANES
    cv = cb_ref[...]
    a_sc[...] = cv * (1.0 / (1.0 + jnp.exp(-cv)))

    def body(kb, accs):
        accs = list(accs)
        for u in range(MOD_UNROLL):
            k0 = pl.multiple_of((kb * MOD_UNROLL + u) * SUBLANES, SUBLANES)
            a = [a_sc[r, pl.ds(k0, SUBLANES), :] for r in range(N_GROUPS)]
            for ci in range(nchunk):
                wv = w_ref[0, pl.ds(k0, SUBLANES), ci * LANES:(ci + 1) * LANES]
                for r in range(N_GROUPS):
                    accs[ci * N_GROUPS + r] = accs[ci * N_GROUPS + r] + wv * a[r]
        return tuple(accs)

    init = tuple(jnp.zeros((SUBLANES, LANES), F32) for _ in range(N_GROUPS * nchunk))
    accs = lax.fori_loop(0, D_MODEL // (SUBLANES * MOD_UNROLL), body, init)
    o_ref[...] = jnp.zeros(o_ref.shape, F32)
    for r in range(N_GROUPS):
        for ci in range(nchunk):
            row = jnp.sum(accs[ci * N_GROUPS + r], axis=0, keepdims=True)
            o_ref[0, r:r + 1, ci * LANES:(ci + 1) * LANES] = row + b_ref[0, :, ci * LANES:(ci + 1) * LANES]


def _modulation(c, c_ctx, w_mod, b_mod):
    cond = jnp.concatenate([c_ctx[None, :], c], axis=0)
    cb = jnp.broadcast_to(cond[:, :, None], (N_GROUPS, D_MODEL, LANES))
    out = pl.pallas_call(
        _mod_kernel,
        grid=(DEPTH, 6 * D_MODEL // MOD_TN),
        in_specs=[
            pl.BlockSpec((N_GROUPS, D_MODEL, LANES), lambda l, j: (0, 0, 0)),
            pl.BlockSpec((1, D_MODEL, MOD_TN), lambda l, j: (l, 0, j)),
            pl.BlockSpec((1, 1, MOD_TN), lambda l, j: (l, 0, j)),
        ],
        out_specs=pl.BlockSpec((1, SUBLANES, MOD_TN), lambda l, j: (l, 0, j)),
        out_shape=jax.ShapeDtypeStruct((DEPTH, SUBLANES, 6 * D_MODEL), F32),
        scratch_shapes=[pltpu.VMEM((N_GROUPS, D_MODEL, LANES), F32)],
        compiler_params=_params(("parallel", "parallel")),
        name="ada_mod",
    )(cb, w_mod, b_mod.reshape(DEPTH, 1, 6 * D_MODEL))
    return out[:, :N_GROUPS].reshape(DEPTH, N_GROUPS, 6, D_MODEL)


def _normed(x, g_ref, m_ref, shift, scale):
    ms = jnp.mean(x * x, axis=-1, keepdims=True)
    y = x * lax.rsqrt(ms + EPS) * g_ref[...]
    return y * (1.0 + m_ref[0, scale:scale + 1, :]) + m_ref[0, shift:shift + 1, :]


def _x_specs(x, tm):
    if not isinstance(x, tuple):
        return [x], [pl.BlockSpec((tm, D_MODEL), lambda i: (i, 0))]
    pb = N_PROMPT // tm
    return list(x), [pl.BlockSpec((tm, D_MODEL), lambda i: (jnp.minimum(i, pb - 1), 0)),
                     pl.BlockSpec((tm, D_MODEL), lambda i: (jnp.maximum(i - pb, 0), 0))]


def _x_block(x_refs, tm):
    if len(x_refs) == 1:
        return x_refs[0][...]
    return jnp.where(pl.program_id(0) < N_PROMPT // tm, x_refs[0][...], x_refs[1][...])


def _norm_mm_kernel(*refs, tn, tm):
    m_ref, g_ref, w_ref, o_ref = refs[-4:]
    h = _normed(_x_block(refs[:-4], tm), g_ref, m_ref, SH1, SC1).astype(BF16)
    for c0 in range(0, o_ref.shape[1], tn):
        o_ref[:, c0:c0 + tn] = jnp.dot(h, w_ref[:, c0:c0 + tn], preferred_element_type=F32)


def _norm_project(x, mods_l, g, w_bf16, tm=512, tn=512):
    nout = w_bf16.shape[1]
    x_args, x_specs = _x_specs(x, tm)
    return pl.pallas_call(
        functools.partial(_norm_mm_kernel, tn=tn, tm=tm),
        grid=(N_TOK // tm,),
        in_specs=x_specs + [
            pl.BlockSpec((1, 6, D_MODEL), lambda i: (_group_of_block(i, tm), 0, 0)),
            pl.BlockSpec((1, D_MODEL), lambda i: (0, 0)),
            pl.BlockSpec((D_MODEL, nout), lambda i: (0, 0)),
        ],
        out_specs=pl.BlockSpec((tm, nout), lambda i: (i, 0)),
        out_shape=jax.ShapeDtypeStruct((N_TOK, nout), F32),
        compiler_params=_params(("parallel",)),
        name="norm_project",
    )(*x_args, mods_l, g.reshape(1, D_MODEL), w_bf16)


def _proj_res_kernel(*refs, n_in, gate, tm):
    a_refs = refs[:n_in]
    w_refs = refs[n_in:2 * n_in]
    x_refs = refs[2 * n_in:-2]
    m_ref, o_ref = refs[-2:]
    acc = None
    for a_ref, w_ref in zip(a_refs, w_refs):
        t = jnp.dot(a_ref[...].astype(BF16), w_ref[...], preferred_element_type=F32)
        acc = t if acc is None else acc + t
    o_ref[...] = _x_block(x_refs, tm) + m_ref[0, gate:gate + 1, :] * acc


def _project_residual(acts, ws_bf16, x, mods_l, gate, tm=512):
    n_in = len(acts)
    x_args, x_specs = _x_specs(x, tm)
    in_specs = [pl.BlockSpec((tm, a.shape[1]), lambda i: (i, 0)) for a in acts]
    in_specs += [pl.BlockSpec(w.shape, lambda i: (0, 0)) for w in ws_bf16]
    in_specs += x_specs + [pl.BlockSpec((1, 6, D_MODEL), lambda i: (_group_of_block(i, tm), 0, 0))]
    return pl.pallas_call(
        functools.partial(_proj_res_kernel, n_in=n_in, gate=gate, tm=tm),
        grid=(N_TOK // tm,),
        in_specs=in_specs,
        out_specs=pl.BlockSpec((tm, D_MODEL), lambda i: (i, 0)),
        out_shape=jax.ShapeDtypeStruct((N_TOK, D_MODEL), F32),
        input_output_aliases={2 * n_in: 0} if len(x_args) == 1 else {},
        compiler_params=_params(("parallel",)),
        name="project_residual",
    )(*acts, *ws_bf16, *x_args, mods_l)


@functools.lru_cache(maxsize=None)
def _dft_mats(L):
    n = 2 * L
    ft = (np.arange(L, dtype=np.int64)[:, None] * np.arange(L, dtype=np.int64)[None, :]) % n
    ang = ft.astype(np.float64) * (2.0 * np.pi / n)
    return np.cos(ang).astype(np.float32), np.sin(ang).astype(np.float32)


def _dft_bf16(L):
    c, s = _dft_mats(L)
    return jnp.asarray(c).astype(BF16), jnp.asarray(s).astype(BF16)


def _alt_sign(shape, row0):
    t = lax.broadcasted_iota(jnp.int32, shape, 0) + row0
    return (1 - 2 * (t & 1)).astype(F32)


def _filter_dft_kernel(s_ref, d_ref, c_ref, sn_ref, ka_ref, ki_ref, kn_ref, *, L, fb):
    f0 = pl.program_id(0) * fb
    n = 2.0 * L
    s = s_ref[...]
    r = jnp.dot(c_ref[...], s, preferred_element_type=F32)
    im = jnp.dot(sn_ref[...], d_ref[...], preferred_element_type=F32)
    fidx = lax.broadcasted_iota(jnp.int32, r.shape, 0) + f0
    scale = jnp.where(fidx == 0, 1.0 / n, 2.0 / n)
    ka_ref[...] = r * scale
    ki_ref[...] = im * (2.0 / n)
    nyq = jnp.sum(s.astype(F32) * _alt_sign(s.shape, 0), axis=0, keepdims=True) * (1.0 / n)
    kn_ref[...] = jnp.broadcast_to(nyq, kn_ref.shape)


def _filter_spectra(s, d, L, dft):
    fb = min(L, 512)
    cmat, smat = dft
    w = 2 * HY_W
    return pl.pallas_call(
        functools.partial(_filter_dft_kernel, L=L, fb=fb),
        grid=(L // fb,),
        in_specs=[
            pl.BlockSpec((L, w), lambda f: (0, 0)),
            pl.BlockSpec((L, w), lambda f: (0, 0)),
            pl.BlockSpec((fb, L), lambda f: (f, 0)),
            pl.BlockSpec((fb, L), lambda f: (f, 0)),
        ],
        out_specs=[
            pl.BlockSpec((fb, w), lambda f: (f, 0)),
            pl.BlockSpec((fb, w), lambda f: (f, 0)),
            pl.BlockSpec((SUBLANES, w), lambda f: (0, 0)),
        ],
        out_shape=[
            jax.ShapeDtypeStruct((L, w), F32),
            jax.ShapeDtypeStruct((L, w), F32),
            jax.ShapeDtypeStruct((SUBLANES, w), F32),
        ],
        compiler_params=_params(("arbitrary",)),
        name="hyena_filter_dft",
    )(s.astype(BF16), d.astype(BF16), cmat, smat)


HY_TILE = 256


def _short_conv_tile(ref, r0, L, w_ref, b_ref):
    t = HY_TILE
    cur = ref[r0:r0 + t, :]
    rid = lax.broadcasted_iota(jnp.int32, cur.shape, 0)
    if r0 % L == 0:
        prev = jnp.where(rid == 0, 0.0, pltpu.roll(cur, 1, axis=0))
    else:
        prev = ref[r0 - 1:r0 - 1 + t, :]
    if (r0 + t) % L == 0:
        nxt = jnp.where(rid == t - 1, 0.0, pltpu.roll(cur, t - 1, axis=0))
    else:
        nxt = ref[r0 + 1:r0 + 1 + t, :]
    return prev * w_ref[0:1, :] + cur * w_ref[1:2, :] + nxt * w_ref[2:3, :] + b_ref[...]


def _hyena_conv_kernel(*refs, L, nseq, conv_a):
    (a_ref, x_ref, cr_ref, sr_ref, cc_ref, sc_ref, ka_ref, ki_ref, kn_ref,
     wa_ref, ba_ref, wx_ref, bx_ref, hb_ref) = refs[:14]
    o_ref, z_sc, acc_sc = refs[-3:]
    f = pl.program_id(1)
    nf = pl.num_programs(1)
    rows = nseq * L

    @pl.when(f == 0)
    def _():
        for q in range(nseq):
            nyq = jnp.zeros((1, HY_W), F32)
            for r0 in range(q * L, (q + 1) * L, HY_TILE):
                if conv_a:
                    zt = _short_conv_tile(a_ref, r0, L, wa_ref, ba_ref)
                else:
                    zt = a_ref[r0:r0 + HY_TILE, :]
                z_sc[r0:r0 + HY_TILE, :] = zt
                nyq = nyq + jnp.sum(zt * _alt_sign(zt.shape, r0), axis=0, keepdims=True)
            nyq = nyq * kn_ref[0:1, :]
            for r0 in range(q * L, (q + 1) * L, HY_TILE):
                acc_sc[r0:r0 + HY_TILE, :] = _alt_sign((HY_TILE, HY_W), r0) * nyq

    ka = ka_ref[...]
    ki = ki_ref[...]
    for q in range(nseq):
        z = z_sc[q * L:(q + 1) * L, :].astype(BF16)
        a = jnp.dot(cr_ref[...], z, preferred_element_type=F32)
        b = jnp.dot(sr_ref[...], z, preferred_element_type=F32)
        p = (a * ka + b * ki).astype(BF16)
        qq = (b * ka - a * ki).astype(BF16)
        acc_sc[q * L:(q + 1) * L, :] += (jnp.dot(cc_ref[...], p, preferred_element_type=F32)
                                         + jnp.dot(sc_ref[...], qq, preferred_element_type=F32))

    @pl.when(f == nf - 1)
    def _():
        for r0 in range(0, rows, HY_TILE):
            y = acc_sc[r0:r0 + HY_TILE, :] + z_sc[r0:r0 + HY_TILE, :] * hb_ref[...]
            o_ref[r0:r0 + HY_TILE, :] = (y * _short_conv_tile(x_ref, r0, L, wx_ref, bx_ref)).astype(o_ref.dtype)


def _hyena_conv(a, a_col, x, x_col, row0, n_rows, L, nseq, spectra, filt, wa, ba, wx, bx, hbias, conv_a, dft,
                full_out=False, out_prev=None):
    fb = min(L, 256)
    cmat, smat = dft
    ka, ki, kn = spectra
    rb = nseq * L
    a_off = row0 // rb if a.shape[0] != n_rows else 0
    x_off = row0 // rb
    o_off = row0 // rb if full_out else 0
    extra_specs, extra_args, aliases = [], [], {}
    if out_prev is not None:
        extra_specs, extra_args, aliases = [pl.BlockSpec(memory_space=pl.ANY)], [out_prev], {14: 0}
    return pl.pallas_call(
        functools.partial(_hyena_conv_kernel, L=L, nseq=nseq, conv_a=conv_a),
        grid=(n_rows // rb, L // fb),
        input_output_aliases=aliases,
        in_specs=extra_specs[:0] + [
            pl.BlockSpec((rb, HY_W), lambda i, f: (i + a_off, a_col)),
            pl.BlockSpec((rb, HY_W), lambda i, f: (i + x_off, x_col)),
            pl.BlockSpec((fb, L), lambda i, f: (f, 0)),
            pl.BlockSpec((fb, L), lambda i, f: (f, 0)),
            pl.BlockSpec((L, fb), lambda i, f: (0, f)),
            pl.BlockSpec((L, fb), lambda i, f: (0, f)),
            pl.BlockSpec((fb, HY_W), lambda i, f: (f, filt)),
            pl.BlockSpec((fb, HY_W), lambda i, f: (f, filt)),
            pl.BlockSpec((SUBLANES, HY_W), lambda i, f: (0, filt)),
            pl.BlockSpec((3, HY_W), lambda i, f: (0, 0)),
            pl.BlockSpec((1, HY_W), lambda i, f: (0, 0)),
            pl.BlockSpec((3, HY_W), lambda i, f: (0, 0)),
            pl.BlockSpec((1, HY_W), lambda i, f: (0, 0)),
            pl.BlockSpec((1, HY_W), lambda i, f: (0, 0)),
        ] + extra_specs,
        out_specs=pl.BlockSpec((rb, HY_W), lambda i, f: (i + o_off, 0)),
        out_shape=jax.ShapeDtypeStruct((N_TOK, HY_W), BF16) if full_out else jax.ShapeDtypeStruct((n_rows, HY_W), F32),
        scratch_shapes=[pltpu.VMEM((rb, HY_W), F32), pltpu.VMEM((rb, HY_W), F32)],
        compiler_params=_params(("parallel", "arbitrary")),
        name="hyena_conv",
    )(a, x, cmat, smat, cmat, smat, ka, ki, kn, wa, ba, wx, bx, hbias, *extra_args)


def _hyena_filter_taps(L, f1w, f1b, fr1, f2w, f2b, fr2, f3w):
    hp = lax.Precision.HIGHEST
    pos = jnp.arange(L, dtype=F32)
    t = pos / (L - 1)
    w = 2.0 * math.pi * pos / L
    f = jnp.linspace(1e-4, HY_BANDS - 1, HY_BANDS, dtype=F32)
    wf = w[:, None] * f[None, :]
    feat = jnp.concatenate([t[:, None], jnp.cos(wf), -jnp.sin(wf)], axis=-1)
    h = jnp.sin(fr1 * (jnp.dot(feat, f1w, precision=hp) + f1b))
    h = jnp.sin(fr2 * (jnp.dot(h, f2w, precision=hp) + f2b))
    h = jnp.dot(h, f3w, precision=hp).astype(F32)
    deltas = jnp.linspace(math.log(HY_TARGET) / HY_SLOW, math.log(HY_TARGET) / HY_FAST, HY_W, dtype=F32)
    window = jnp.exp(-t[:, None] * jnp.abs(deltas)[None, :])
    w = HY_W
    parts = [h[:, k * w:(k + 1) * w] * window for k in range(4)]
    colsum = [jnp.sum(jnp.abs(p), axis=0, keepdims=True) for p in parts]
    den = [colsum[0] + colsum[1] + EPS, colsum[2] + colsum[3] + EPS]
    fwd = jnp.concatenate([parts[0] / den[0], parts[2] / den[1]], axis=1)
    bwd = jnp.concatenate([parts[1] / den[0], parts[3] / den[1]], axis=1)
    bwd = jnp.where(pos[:, None] == 0, 0.0, bwd)
    return fwd + bwd, bwd - fwd


def _dot_t0(a, b):
    return lax.dot_general(a, b, (((0,), (0,)), ((), ())), preferred_element_type=F32)


def _dot_t1(a, b):
    return lax.dot_general(a, b, (((1,), (1,)), ((), ())), preferred_element_type=F32)


def _retention_kernel(*refs, L, has_s0, has_prev):
    refs = list(refs)
    q_ref, k_ref, v_ref, g_ref, dl_ref = refs[:5]
    pos = 5
    s0_ref = st_ref = None
    if has_s0:
        s0_ref = refs[pos]
        pos += 1
    if has_prev:
        pos += 1
    y_ref = refs[pos]
    pos += 1
    if not has_s0:
        st_ref = refs[pos]
        pos += 1
    sb_sc, sf_cur, sb_cur = refs[pos:]
    c = RET_CHUNK
    nc = L // c
    kscale = RET_HEAD_DIM ** -0.5
    ri = lax.broadcasted_iota(jnp.int32, (c, c), 0).astype(F32)
    ci = lax.broadcasted_iota(jnp.int32, (c, c), 1).astype(F32)
    diff = ri - ci
    dec = []
    for h in range(RET_HEADS):
        xf = dl_ref[0, h:h + 1, :]
        xb = dl_ref[1, h:h + 1, :]
        lgf = jnp.minimum(xf, 0.0) - jnp.log1p(jnp.exp(-jnp.abs(xf)))
        lgb = jnp.minimum(xb, 0.0) - jnp.log1p(jnp.exp(-jnp.abs(xb)))
        dec.append(dict(
            mask=(jnp.where(diff >= 0, jnp.exp(lgf * jnp.maximum(diff, 0.0)), 0.0)
                  + jnp.where(diff <= 0, jnp.exp(lgb * jnp.maximum(-diff, 0.0)), 0.0)),
            qdec_f=jnp.exp(lgf * (ri + 1.0)), kdec_f=jnp.exp(lgf * (c - 1.0 - ri)),
            qdec_b=jnp.exp(lgb * (c - ri)), kdec_b=jnp.exp(lgb * ri),
            cd_f=jnp.exp(lgf * c), cd_b=jnp.exp(lgb * c)))
        if has_s0:
            sf_cur[h] = s0_ref[0, 0, 0, h]
            sb_cur[h] = s0_ref[0, 0, 1, h]
        else:
            sf_cur[h] = jnp.zeros((RET_HEAD_DIM, RET_HEAD_DIM), F32)
            sb_cur[h] = jnp.zeros((RET_HEAD_DIM, RET_HEAD_DIM), F32)

    def bwd_body(i, carry):
        j = nc - 1 - i
        r0 = pl.multiple_of(j * c, c)
        for h in range(RET_HEADS):
            hs = slice(h * RET_HEAD_DIM, (h + 1) * RET_HEAD_DIM)
            sb = sb_cur[h]
            sb_sc[h * nc + j] = sb
            kc = k_ref[pl.ds(r0, c), hs] * kscale
            vc = v_ref[pl.ds(r0, c), hs]
            sb_cur[h] = sb * dec[h]["cd_b"] + _dot_t0((kc * dec[h]["kdec_b"]).astype(BF16), vc.astype(BF16))
        return carry

    lax.fori_loop(0, nc, bwd_body, 0)

    def fwd_body(j, carry):
        r0 = pl.multiple_of(j * c, c)
        for h in range(RET_HEADS):
            hs = slice(h * RET_HEAD_DIM, (h + 1) * RET_HEAD_DIM)
            dh = dec[h]
            sf = sf_cur[h]
            qc = q_ref[pl.ds(r0, c), hs]
            kc = k_ref[pl.ds(r0, c), hs] * kscale
            vc = v_ref[pl.ds(r0, c), hs].astype(BF16)
            scores = _dot_t1(qc.astype(BF16), kc.astype(BF16)) * dh["mask"]
            o = jnp.dot(scores.astype(BF16), vc, preferred_element_type=F32)
            o = o + jnp.dot((qc * dh["qdec_f"]).astype(BF16), sf.astype(BF16), preferred_element_type=F32)
            o = o + jnp.dot((qc * dh["qdec_b"]).astype(BF16), sb_sc[h * nc + j].astype(BF16),
                            preferred_element_type=F32)
            mu = jnp.mean(o, axis=-1, keepdims=True)
            var = jnp.mean(jnp.square(o - mu), axis=-1, keepdims=True)
            on = (o - mu) * lax.rsqrt(var + EPS)
            gc = g_ref[pl.ds(r0, c), hs]
            y_ref[pl.ds(r0, c), hs] = (gc * (1.0 / (1.0 + jnp.exp(-gc))) * on).astype(y_ref.dtype)
            sf_cur[h] = sf * dh["cd_f"] + _dot_t0((kc * dh["kdec_f"]).astype(BF16), vc)
        return carry

    lax.fori_loop(0, nc, fwd_body, 0)
    if st_ref is not None:
        for h in range(RET_HEADS):
            st_ref[0, 0, 0, h] = sf_cur[h]
            st_ref[0, 0, 1, h] = sb_cur[h]


def _retention(proj, dl, layer_i, row0, nseq, L, s0=None, st_prev=None, y_prev=None):
    off = row0 // L
    has_s0 = s0 is not None
    has_prev = (st_prev is not None) or (y_prev is not None)
    assert not (st_prev is not None and y_prev is not None)
    nc = L // RET_CHUNK
    n_ret = (DEPTH + 1) // 2
    col = lambda j: pl.BlockSpec((L, RET_W), lambda b: (b + off, 3 + j))
    in_specs = [col(0), col(1), col(2), col(3),
                pl.BlockSpec((2, RET_HEADS, LANES), lambda b: (0, 0, 0))]
    args = [proj, proj, proj, proj, dl]
    y_spec = pl.BlockSpec((L, RET_W), lambda b: (b + off, 0))
    y_shape = jax.ShapeDtypeStruct((N_TOK, RET_W), BF16)
    st_block = (1, 1, 2, RET_HEADS, RET_HEAD_DIM, RET_HEAD_DIM)
    st_spec = pl.BlockSpec(st_block, lambda b: (b, layer_i, 0, 0, 0, 0))
    aliases = {}
    if has_s0:
        in_specs.append(st_spec)
        args.append(s0)
        out_specs, out_shape = y_spec, y_shape
    else:
        out_specs = [y_spec, st_spec]
        out_shape = [y_shape, jax.ShapeDtypeStruct((nseq, n_ret) + st_block[2:], F32)]
    if has_prev:
        aliases = {len(args): 1 if st_prev is not None else 0}
        in_specs.append(pl.BlockSpec(memory_space=pl.ANY))
        args.append(st_prev if st_prev is not None else y_prev)
    state = pltpu.VMEM((RET_HEADS, RET_HEAD_DIM, RET_HEAD_DIM), F32)
    return pl.pallas_call(
        functools.partial(_retention_kernel, L=L, has_s0=has_s0, has_prev=has_prev),
        grid=(nseq,),
        in_specs=in_specs,
        out_specs=out_specs,
        out_shape=out_shape,
        input_output_aliases=aliases,
        scratch_shapes=[pltpu.VMEM((RET_HEADS * nc, RET_HEAD_DIM, RET_HEAD_DIM), F32), state, state],
        compiler_params=_params(("parallel",)),
        name="retention",
    )(*args)


@functools.lru_cache(maxsize=None)
def _rope_tables():
    L = DEC_SEQ
    rows = L // GRID_W
    row = np.repeat(np.arange(rows, dtype=np.float64), GRID_W)
    col = np.tile(np.arange(GRID_W, dtype=np.float64), rows)
    quarter = DIFF_HEAD_DIM // 4
    freqs = ROPE_BASE ** (-np.arange(quarter, dtype=np.float64) / quarter)
    j = np.arange(LANES)
    pos = np.where(((j % DIFF_HEAD_DIM) < DIFF_HEAD_DIM // 2)[None, :], row[:, None], col[:, None])
    ang = pos * freqs[j % quarter][None, :]
    cos = np.cos(ang).astype(np.float32)
    sin = np.sin(ang).astype(np.float32)
    first = ((j % (2 * quarter)) < quarter)[None, :]
    sin_a = np.where(first, -sin, 0.0).astype(np.float32)
    sin_b = np.where(first, 0.0, sin).astype(np.float32)
    return jnp.asarray(cos), jnp.asarray(sin_a), jnp.asarray(sin_b)


def _rope_head(x, cos, sin_a, sin_b):
    quarter = DIFF_HEAD_DIM // 4
    up = pltpu.roll(x, LANES - quarter, axis=1)
    dn = pltpu.roll(x, quarter, axis=1)
    return x * cos + up * sin_a + dn * sin_b


def _kv_prep_kernel(q_ref, k_ref, v_ref, cos_ref, sa_ref, sb_ref, qo_ref, ko_ref, vo_ref):
    cos, sa, sb = cos_ref[...], sa_ref[...], sb_ref[...]
    for h in range(DIFF_HEADS):
        hs = slice(h * LANES, (h + 1) * LANES)
        qo_ref[:, hs] = _rope_head(q_ref[:, hs], cos, sa, sb).astype(BF16)
        ko_ref[0, :, hs] = _rope_head(k_ref[:, hs], cos, sa, sb).astype(BF16)
    vo_ref[0] = v_ref[...].astype(BF16)


def _cache_copy_kernel(ck_ref, cv_ref, k_in, v_in, ko_ref, vo_ref):
    del k_in, v_in
    ko_ref[0] = ck_ref[0, 0].astype(BF16)
    vo_ref[0] = cv_ref[0, 0].astype(BF16)


def _sample_qkv(proj, cache_k, cache_v, layer_i, tm=256):
    cos, sa, sb = _rope_tables()
    lk = PAST_LEN + DEC_SEQ
    pblk = N_PROMPT // tm
    nblk = DEC_SEQ // tm
    cblk = PAST_LEN // tm
    tab = pl.BlockSpec((tm, LANES), lambda b, i: (i, 0))
    q, k, v = pl.pallas_call(
        _kv_prep_kernel,
        grid=(DEC_BATCH, nblk),
        in_specs=[
            pl.BlockSpec((tm, D_MODEL), lambda b, i: (pblk + b * nblk + i, 0)),
            pl.BlockSpec((tm, D_MODEL), lambda b, i: (pblk + b * nblk + i, 1)),
            pl.BlockSpec((tm, D_MODEL), lambda b, i: (pblk + b * nblk + i, 2)),
            tab, tab, tab,
        ],
        out_specs=[
            pl.BlockSpec((tm, D_MODEL), lambda b, i: (b * nblk + i, 0)),
            pl.BlockSpec((1, tm, D_MODEL), lambda b, i: (b, cblk + i, 0)),
            pl.BlockSpec((1, tm, D_MODEL), lambda b, i: (b, cblk + i, 0)),
        ],
        out_shape=[
            jax.ShapeDtypeStruct((N_SAMPLE, D_MODEL), BF16),
            jax.ShapeDtypeStruct((DEC_BATCH, lk, D_MODEL), BF16),
            jax.ShapeDtypeStruct((DEC_BATCH, lk, D_MODEL), BF16),
        ],
        compiler_params=_params(("parallel", "parallel")),
        name="rope_qkv",
    )(proj, proj, proj, cos, sa, sb)
    n_att = DEPTH // 2
    ck = cache_k.reshape(DEC_BATCH, n_att, PAST_LEN, D_MODEL)
    cv = cache_v.reshape(DEC_BATCH, n_att, PAST_LEN, D_MODEL)
    k, v = pl.pallas_call(
        _cache_copy_kernel,
        grid=(DEC_BATCH,),
        in_specs=[
            pl.BlockSpec((1, 1, PAST_LEN, D_MODEL), lambda b: (b, layer_i, 0, 0)),
            pl.BlockSpec((1, 1, PAST_LEN, D_MODEL), lambda b: (b, layer_i, 0, 0)),
            pl.BlockSpec(memory_space=pl.ANY),
            pl.BlockSpec(memory_space=pl.ANY),
        ],
        out_specs=[
            pl.BlockSpec((1, PAST_LEN, D_MODEL), lambda b: (b, 0, 0)),
            pl.BlockSpec((1, PAST_LEN, D_MODEL), lambda b: (b, 0, 0)),
        ],
        out_shape=[
            jax.ShapeDtypeStruct((DEC_BATCH, lk, D_MODEL), BF16),
            jax.ShapeDtypeStruct((DEC_BATCH, lk, D_MODEL), BF16),
        ],
        input_output_aliases={2: 0, 3: 1},
        compiler_params=_params(("parallel",)),
        name="cache_prepend",
    )(ck, cv, k, v)
    return q, k, v


def _diff_attn_kernel(*refs, lam_init, batched_kv, cache_out):
    q_ref, k_ref, v_ref, lam_ref, sg_ref = refs[:5]
    if cache_out:
        o_ref, kc_ref, vc_ref = refs[-3:]
        kc_ref[0, 0] = k_ref[...]
        vc_ref[0, 0] = v_ref[...]
    else:
        o_ref = refs[-1]
    lv = lam_ref[...]
    lam = (jnp.exp(jnp.sum(lv[0:1] * lv[1:2], axis=-1, keepdims=True))
           - jnp.exp(jnp.sum(lv[2:3] * lv[3:4], axis=-1, keepdims=True)) + lam_init)
    lane = lax.broadcasted_iota(jnp.int32, (1, LANES), 1)
    m1 = (lane < DIFF_HEAD_DIM).astype(F32)
    m2 = 1.0 - m1
    scale = DIFF_HEAD_DIM ** -0.5
    for h in range(DIFF_HEADS):
        hs = slice(h * LANES, (h + 1) * LANES)
        q = q_ref[:, hs].astype(F32) * scale
        if batched_kv:
            k = k_ref[0, :, hs].astype(BF16)
            v = v_ref[0, :, hs].astype(BF16)
        else:
            k = k_ref[:, hs].astype(BF16)
            v = v_ref[:, hs].astype(BF16)
        v_ext = jnp.concatenate([v, jnp.ones_like(v)], axis=1)
        outs = []
        for m in (m1, m2):
            s = _dot_t1((q * m).astype(BF16), k)
            s = s - jnp.max(s, axis=-1, keepdims=True)
            pv = jnp.dot(jnp.exp(s).astype(BF16), v_ext, preferred_element_type=F32)
            outs.append(pv[:, :LANES] / pv[:, LANES:])
        o = outs[0] - lam * outs[1]
        ms = jnp.mean(o * o, axis=-1, keepdims=True)
        o_ref[:, hs] = (o * lax.rsqrt(ms + EPS) * sg_ref[...] * (1.0 - lam_init)).astype(o_ref.dtype)


def _diff_attention(q, k, v, lam_vec, subln, lam_init, *, nb, lq, lk, tq, q_row0, q_col, kv_cols, batched_kv,
                    out_row0=0, out_prev=None, cache_layer=None, cache_prev=None):
    nq = lq // tq
    qoff = q_row0 // tq
    ooff = out_row0 // tq
    cache_out = cache_layer is not None
    extra_specs, extra_args, aliases = [], [], {}
    if out_prev is not None:
        extra_specs, extra_args, aliases = [pl.BlockSpec(memory_space=pl.ANY)], [out_prev], {5: 0}
    o_spec = pl.BlockSpec((tq, D_MODEL), lambda b, i: (ooff + b * nq + i, 0))
    o_shape = jax.ShapeDtypeStruct((N_TOK, D_MODEL), BF16)
    if cache_out:
        assert not batched_kv and nq == 1 and out_prev is None
        if cache_prev is not None:
            extra_specs = [pl.BlockSpec(memory_space=pl.ANY)] * 2
            extra_args = list(cache_prev)
            aliases = {5: 1, 6: 2}
        c_spec = pl.BlockSpec((1, 1, lk, D_MODEL), lambda b, i: (b, cache_layer, 0, 0))
        c_shape = jax.ShapeDtypeStruct((nb, DEPTH // 2, lk, D_MODEL), F32)
        o_spec, o_shape = [o_spec, c_spec, c_spec], [o_shape, c_shape, c_shape]
    q_spec = pl.BlockSpec((tq, D_MODEL), lambda b, i: (qoff + b * nq + i, q_col))
    if batched_kv:
        k_spec = pl.BlockSpec((1, lk, D_MODEL), lambda b, i: (b, 0, 0))
        v_spec = k_spec
    else:
        k_spec = pl.BlockSpec((lk, D_MODEL), lambda b, i: (b, kv_cols[0]))
        v_spec = pl.BlockSpec((lk, D_MODEL), lambda b, i: (b, kv_cols[1]))
    return pl.pallas_call(
        functools.partial(_diff_attn_kernel, lam_init=lam_init, batched_kv=batched_kv, cache_out=cache_out),
        grid=(nb, nq),
        in_specs=[q_spec, k_spec, v_spec,
                  pl.BlockSpec((4, LANES), lambda b, i: (0, 0)),
                  pl.BlockSpec((1, LANES), lambda b, i: (0, 0))] + extra_specs,
        out_specs=o_spec,
        out_shape=o_shape,
        input_output_aliases=aliases,
        compiler_params=_params(("parallel", "arbitrary")),
        name="diff_attention",
    )(q, k, v, lam_vec, subln, *extra_args)


PACK_W = D_MODEL // 4


def _router_kernel(x_ref, m_ref, g_ref, wr_ref, h_ref, aff_ref):
    h = _normed(x_ref[...], g_ref, m_ref, SH2, SC2)
    hb = h.astype(BF16)
    hf = hb.astype(F32)
    bits = lax.bitcast_convert_type(hf, jnp.int32)
    for p in range(2):
        lo = lax.shift_right_logical(bits[:, p * PACK_W:(p + 1) * PACK_W], 16)
        hi = bits[:, (2 + p) * PACK_W:(3 + p) * PACK_W] & jnp.int32(-65536)
        h_ref[p] = hi | lo
    hl = (h - hf).astype(BF16)
    wr = wr_ref[...]
    wh = wr.astype(BF16)
    wl = (wr - wh.astype(F32)).astype(BF16)
    logits = (jnp.dot(hb, wh, preferred_element_type=F32) + jnp.dot(hl, wh, preferred_element_type=F32)
              + jnp.dot(hb, wl, preferred_element_type=F32))
    lt = jnp.transpose(logits)[0:N_EXPERTS, :]
    lt = lt - jnp.max(lt, axis=0, keepdims=True)
    e = jnp.exp(lt)
    aff_ref[...] = e / jnp.sum(e, axis=0, keepdims=True)


def _router(x, mods_l, g, w_router, tm=512):
    wr = jnp.pad(w_router, ((0, 0), (0, LANES - N_EXPERTS)))
    return pl.pallas_call(
        _router_kernel,
        grid=(N_TOK // tm,),
        in_specs=[
            pl.BlockSpec((tm, D_MODEL), lambda i: (i, 0)),
            pl.BlockSpec((1, 6, D_MODEL), lambda i: (_group_of_block(i, tm), 0, 0)),
            pl.BlockSpec((1, D_MODEL), lambda i: (0, 0)),
            pl.BlockSpec((D_MODEL, LANES), lambda i: (0, 0)),
        ],
        out_specs=[
            pl.BlockSpec((2, tm, PACK_W), lambda i: (0, i, 0)),
            pl.BlockSpec((N_EXPERTS, tm), lambda i: (0, i)),
        ],
        out_shape=[
            jax.ShapeDtypeStruct((2, N_TOK, PACK_W), jnp.int32),
            jax.ShapeDtypeStruct((N_EXPERTS, N_TOK), F32),
        ],
        compiler_params=_params(("parallel",)),
        name="norm_router",
    )(x, mods_l, g.reshape(1, D_MODEL), wr)


SC_WINDOW = 128


def _gather_rows(table, idx):
    n = idx.shape[0]
    mesh = plsc.VectorSubcoreMesh(core_axis_name="core", subcore_axis_name="subcore")

    @pl.kernel(out_type=jax.ShapeDtypeStruct((n, PACK_W), table.dtype), mesh=mesh, scratch_types=[])
    def gather_kernel(t_hbm, i_hbm, o_hbm):
        def body(i_vmem, o_vmem):
            pltpu.sync_copy(t_hbm.at[i_vmem.at[0]], o_vmem)

        pltpu.emit_pipeline(
            body,
            grid=(n // SC_WINDOW,),
            in_specs=[pl.BlockSpec((1, SC_WINDOW), index_map=lambda i: (0, i))],
            out_specs=[pl.BlockSpec((SC_WINDOW, PACK_W), index_map=lambda i: (i, 0))],
            core_axis_name=("core", "subcore"),
            dimension_semantics=(pltpu.PARALLEL,),
        )(i_hbm, o_hbm)

    return gather_kernel(table, idx.reshape(1, n))


FFN_TF = 512
FFN_TR = 512


def _unpack_rows(pa, pb):
    def lo(w):
        return lax.bitcast_convert_type(lax.shift_left(w, 16), F32).astype(BF16)

    def hi(w):
        return lax.bitcast_convert_type(w & jnp.int32(-65536), F32).astype(BF16)

    return jnp.concatenate([lo(pa), lo(pb), hi(pa), hi(pb)], axis=1)


def _expert_ffn_kernel(xs_ref, wg_ref, wu_ref, wd_ref, gate_ref, m_ref, o_ref, xs_sc, acc_sc):
    f = pl.program_id(1)

    @pl.when(f == 0)
    def _():
        for r0 in range(0, CAP_T, FFN_TR):
            xs_sc[r0:r0 + FFN_TR, :] = _unpack_rows(xs_ref[0, 0, r0:r0 + FFN_TR, :], xs_ref[1, 0, r0:r0 + FFN_TR, :])

    wg = wg_ref[0].astype(BF16)
    wu = wu_ref[0].astype(BF16)
    wd = wd_ref[0].astype(BF16)
    for r0 in range(0, CAP_T, FFN_TR):
        rs = slice(r0, r0 + FFN_TR)
        xs = xs_sc[rs, :]
        a = jnp.dot(xs, wg, preferred_element_type=F32)
        u = jnp.dot(xs, wu, preferred_element_type=F32)
        hid = (a * (1.0 / (1.0 + jnp.exp(-a))) * u).astype(BF16)
        y = jnp.dot(hid, wd, preferred_element_type=F32)

        @pl.when(f == 0)
        def _():
            acc_sc[rs, :] = y

        @pl.when(jnp.logical_and(f > 0, f < pl.num_programs(1) - 1))
        def _():
            acc_sc[rs, :] += y

        @pl.when(f == pl.num_programs(1) - 1)
        def _():
            gg = gate_ref[0, rs, :]
            scale = gg[:, 0:1] * m_ref[0, G2:G2 + 1, :]
            for g in range(1, N_GROUPS):
                scale = scale + gg[:, g:g + 1] * m_ref[g, G2:G2 + 1, :]
            o_ref[0, rs, :] = ((acc_sc[rs, :] + y) * scale).astype(o_ref.dtype)


def _expert_ffn(xs, wg, wu, wd, gate, mods_l, l):
    return pl.pallas_call(
        _expert_ffn_kernel,
        grid=(N_EXPERTS, EXPERT_FF // FFN_TF),
        in_specs=[
            pl.BlockSpec((2, 1, CAP_T, PACK_W), lambda e, f: (0, e, 0, 0)),
            pl.BlockSpec((None, 1, D_MODEL, FFN_TF), lambda e, f: (l, e, 0, f)),
            pl.BlockSpec((None, 1, D_MODEL, FFN_TF), lambda e, f: (l, e, 0, f)),
            pl.BlockSpec((None, 1, FFN_TF, D_MODEL), lambda e, f: (l, e, f, 0)),
            pl.BlockSpec((1, CAP_T, N_GROUPS), lambda e, f: (e, 0, 0)),
            pl.BlockSpec((N_GROUPS, 6, D_MODEL), lambda e, f: (0, 0, 0)),
        ],
        out_specs=pl.BlockSpec((1, CAP_T, D_MODEL), lambda e, f: (e, 0, 0)),
        out_shape=jax.ShapeDtypeStruct((N_EXPERTS, CAP_T, D_MODEL), BF16),
        scratch_shapes=[pltpu.VMEM((CAP_T, D_MODEL), BF16), pltpu.VMEM((CAP_T, D_MODEL), F32)],
        compiler_params=_params(("parallel", "arbitrary")),
        name="expert_ffn",
    )(xs, wg, wu, wd, gate, mods_l)


def _final_norm_kernel(x_ref, g_ref, o_ref):
    x = x_ref[...]
    ms = jnp.mean(x * x, axis=-1, keepdims=True)
    o_ref[...] = x * lax.rsqrt(ms + EPS) * g_ref[...]


def _final_norm(x, g, row0, n_rows, tm=512):
    off = row0 // tm
    return pl.pallas_call(
        _final_norm_kernel,
        grid=(n_rows // tm,),
        in_specs=[pl.BlockSpec((tm, D_MODEL), lambda i: (i + off, 0)),
                  pl.BlockSpec((1, D_MODEL), lambda i: (0, 0))],
        out_specs=pl.BlockSpec((tm, D_MODEL), lambda i: (i, 0)),
        out_shape=jax.ShapeDtypeStruct((n_rows, D_MODEL), F32),
        compiler_params=_params(("parallel",)),
        name="final_norm",
    )(x, g.reshape(1, D_MODEL))


def _even_layer(x, mods_l, i, state_ret, st_prev, dfts, norm1_g, w_in_even, hy_short_w, hy_short_b, hy_f1_w, hy_f1_b,
                hy_freq1, hy_f2_w, hy_f2_b, hy_freq2, hy_f3_w, hy_bias, ret_decay, w_out_even):
    proj = _norm_project(x, mods_l, norm1_g, w_in_even[i].astype(BF16))
    sw, sbias = hy_short_w[i], hy_short_b[i].reshape(1, 3 * HY_W)
    hb = hy_bias[i]
    y_hy = None
    for (row0, n_rows, L, nseq) in ((0, N_PROMPT, SEQ, 8), (N_PROMPT, N_SAMPLE, DEC_SEQ, 1)):
        s, d = _hyena_filter_taps(L, hy_f1_w[i], hy_f1_b[i], hy_freq1[i], hy_f2_w[i], hy_f2_b[i], hy_freq2[i], hy_f3_w[i])
        spectra = _filter_spectra(s, d, L, dfts[L])
        wv, bv = sw[:, 0:HY_W], sbias[:, 0:HY_W]
        w1, b1 = sw[:, HY_W:2 * HY_W], sbias[:, HY_W:2 * HY_W]
        w2, b2 = sw[:, 2 * HY_W:], sbias[:, 2 * HY_W:]
        z1 = _hyena_conv(proj, 0, proj, 1, row0, n_rows, L, nseq, spectra, 0, wv, bv, w1, b1, hb[0:1], True, dfts[L])
        y_hy = _hyena_conv(z1, 0, proj, 2, row0, n_rows, L, nseq, spectra, 1, wv, bv, w2, b2, hb[1:2], False, dfts[L],
                           full_out=True, out_prev=y_hy)
    dl = jnp.broadcast_to(ret_decay[i].astype(F32)[:, :, None], (2, RET_HEADS, LANES))
    y_ret, st = _retention(proj, dl, i, 0, BATCH, SEQ, st_prev=st_prev)
    y_ret = _retention(proj, dl, i, N_PROMPT, DEC_BATCH, DEC_SEQ, s0=state_ret, y_prev=y_ret)
    wo = w_out_even[i].astype(BF16)
    x = _project_residual([y_hy, y_ret], [wo[:HY_W], wo[HY_W:]], x, mods_l, G1)
    return x, st


def _odd_layer(x, mods_l, l, i, cache_k, cache_v, kv_prev, norm1_g, w_in_odd, lam_q1, lam_k1, lam_q2, lam_k2, subln_g,
               w_out_odd):
    lam_init = 0.8 - 0.6 * math.exp(-0.3 * l)
    proj = _norm_project(x, mods_l, norm1_g, w_in_odd[i].astype(BF16))
    lam_vec = jnp.pad(jnp.stack([lam_q1[i], lam_k1[i], lam_q2[i], lam_k2[i]]), ((0, 0), (0, LANES - DIFF_HEAD_DIM)))
    sg = subln_g[i].reshape(1, LANES)
    o_p, kc, vc = _diff_attention(proj, proj, proj, lam_vec, sg, lam_init, nb=BATCH, lq=SEQ, lk=SEQ, tq=SEQ,
                                  q_row0=0, q_col=0, kv_cols=(1, 2), batched_kv=False,
                                  cache_layer=i, cache_prev=kv_prev)
    qs, ks, vs = _sample_qkv(proj, cache_k, cache_v, i)
    o = _diff_attention(qs, ks, vs, lam_vec, sg, lam_init, nb=DEC_BATCH, lq=DEC_SEQ, lk=PAST_LEN + DEC_SEQ, tq=256,
                        q_row0=0, q_col=0, kv_cols=None, batched_kv=True, out_row0=N_PROMPT, out_prev=o_p)
    x = _project_residual([o], [w_out_odd[i].astype(BF16)], x, mods_l, G1)
    return x, (kc, vc)


def _moe_layer(x, mods_l, l, norm2_g, w_router, wg, wu, wd):
    h, aff = _router(x, mods_l, norm2_g, w_router)
    gate_p, idx_p = lax.top_k(aff[:, :N_PROMPT], CAP_P)
    gate_s, idx_s = lax.top_k(aff[:, N_PROMPT:], CAP_S)
    idx = jnp.concatenate([idx_p, idx_s + N_PROMPT], axis=1)
    gate = jnp.concatenate([gate_p, gate_s], axis=1)
    flat = idx.reshape(-1)
    xs = _gather_rows(h.reshape(2 * N_TOK, PACK_W), jnp.concatenate([flat, flat + N_TOK]))
    xs = xs.reshape(2, N_EXPERTS, CAP_T, PACK_W)
    grp = jnp.where(idx < N_PROMPT, 0, 1 + (idx - N_PROMPT) // DEC_SEQ)
    gate_grp = jnp.where(grp[:, :, None] == jnp.arange(N_GROUPS)[None, None, :], gate[:, :, None], 0.0)
    out = _expert_ffn(xs, wg, wu, wd, gate_grp, mods_l, l)
    return x.at[idx.reshape(-1)].add(out.reshape(-1, D_MODEL).astype(F32))


def kernel(x_prompt, x_sample, state_ret, cache_k, cache_v, c, c_ctx, w_mod, b_mod, norm1_g, norm2_g, w_in_even, hy_short_w, hy_short_b, hy_f1_w, hy_f1_b, hy_freq1, hy_f2_w, hy_f2_b, hy_freq2, hy_f3_w, hy_bias, ret_decay, w_out_even, w_in_odd, lam_q1, lam_k1, lam_q2, lam_k2, subln_g, w_out_odd, moe_router, moe_wg, moe_wu, moe_wd, final_g):
    x = (x_prompt.reshape(N_PROMPT, D_MODEL), x_sample.reshape(N_SAMPLE, D_MODEL))
    mods = _modulation(c, c_ctx, w_mod, b_mod)
    dfts = {L: _dft_bf16(L) for L in (SEQ, DEC_SEQ)}
    state_ret = state_ret.astype(F32)
    st = kv = None
    for l in range(DEPTH):
        i = l // 2
        if l % 2 == 0:
            x, st = _even_layer(x, mods[l], i, state_ret, st, dfts, norm1_g[l], w_in_even, hy_short_w, hy_short_b,
                                hy_f1_w, hy_f1_b, hy_freq1, hy_f2_w, hy_f2_b, hy_freq2, hy_f3_w, hy_bias, ret_decay,
                                w_out_even)
        else:
            x, kv = _odd_layer(x, mods[l], l, i, cache_k, cache_v, kv, norm1_g[l], w_in_odd, lam_q1, lam_k1, lam_q2,
                               lam_k2, subln_g, w_out_odd)
        x = _moe_layer(x, mods[l], l, norm2_g[l], moe_router[l], moe_wg, moe_wu, moe_wd)
    y_prompt = _final_norm(x, final_g, 0, N_PROMPT).reshape(BATCH, SEQ, D_MODEL)
    y_sample = _final_norm(x, final_g, N_PROMPT, N_SAMPLE).reshape(DEC_BATCH, DEC_SEQ, D_MODEL)
    cache_shape = (BATCH, DEPTH // 2, SEQ, DIFF_HEADS, 2 * DIFF_HEAD_DIM)
    return (y_prompt, y_sample, st, kv[0].reshape(cache_shape), kv[1].reshape(cache_shape))
```

```python
import functools
import math

import numpy as np
import jax
import jax.numpy as jnp
from jax import lax
from jax.experimental import pallas as pl
from jax.experimental.pallas import tpu as pltpu
from jax.experimental.pallas import tpu_sc as plsc

F32 = jnp.float32
BF16 = jnp.bfloat16

D_MODEL = 1024
BATCH = 32
SEQ = 256
DEPTH = 4
DEC_BATCH = 2
DEC_SEQ = 2048
PAST_LEN = 256
GRID_W = 64
HY_W = 512
HY_EMB = 33
HY_BANDS = 16
HY_FF = 64
HY_TARGET = 1e-2
HY_FAST = 0.3
HY_SLOW = 1.5
RET_W = 512
RET_HEADS = 4
RET_HEAD_DIM = 128
RET_CHUNK = 128
DIFF_HEADS = 8
DIFF_HEAD_DIM = 64
ROPE_BASE = 10000.0
N_EXPERTS = 16
EC_FACTOR = 2
EXPERT_FF = 1024
EVEN_IN = 3 * HY_W + 4 * RET_W
EPS = 1e-6

N_PROMPT = BATCH * SEQ
N_SAMPLE = DEC_BATCH * DEC_SEQ
N_TOK = N_PROMPT + N_SAMPLE
N_GROUPS = 1 + DEC_BATCH
CAP_P = EC_FACTOR * N_PROMPT // N_EXPERTS
CAP_S = EC_FACTOR * N_SAMPLE // N_EXPERTS
CAP_T = CAP_P + CAP_S

LANES = 128
SUBLANES = 8
VMEM_LIMIT = 56 * 1024 * 1024

SH1, SC1, G1, SH2, SC2, G2 = range(6)


def _params(sem, vmem=VMEM_LIMIT):
    return pltpu.CompilerParams(dimension_semantics=sem, vmem_limit_bytes=vmem)


def _group_of_block(i, tm):
    pb = N_PROMPT // tm
    return jnp.where(i < pb, 0, 1 + (i - pb) // (DEC_SEQ // tm))


MOD_TN = 1024


MOD_UNROLL = 4


def _mod_kernel(cb_ref, w_ref, b_ref, o_ref, a_sc):
    nchunk = MOD_TN // LANES
    cv = cb_ref[...]
    a_sc[...] = cv * (1.0 / (1.0 + jnp.exp(-cv)))

    def body(kb, accs):
        accs = list(accs)
        for u in range(MOD_UNROLL):
            k0 = pl.multiple_of((kb * MOD_UNROLL + u) * SUBLANES, SUBLANES)
            a = [a_sc[r, pl.ds(k0, SUBLANES), :] for r in range(N_GROUPS)]
            for ci in range(nchunk):
                wv = w_ref[0, pl.ds(k0, SUBLANES), ci * LANES:(ci + 1) * LANES]
                for r in range(N_GROUPS):
                    accs[ci * N_GROUPS + r] = accs[ci * N_GROUPS + r] + wv * a[r]
        return tuple(accs)

    init = tuple(jnp.zeros((SUBLANES, LANES), F32) for _ in range(N_GROUPS * nchunk))
    accs = lax.fori_loop(0, D_MODEL // (SUBLANES * MOD_UNROLL), body, init)
    o_ref[...] = jnp.zeros(o_ref.shape, F32)
    for r in range(N_GROUPS):
        for ci in range(nchunk):
            row = jnp.sum(accs[ci * N_GROUPS + r], axis=0, keepdims=True)
            o_ref[0, r:r + 1, ci * LANES:(ci + 1) * LANES] = row + b_ref[0, :, ci * LANES:(ci + 1) * LANES]


def _modulation(c, c_ctx, w_mod, b_mod):
    cond = jnp.concatenate([c_ctx[None, :], c], axis=0)
    cb = jnp.broadcast_to(cond[:, :, None], (N_GROUPS, D_MODEL, LANES))
    out = pl.pallas_call(
        _mod_kernel,
        grid=(DEPTH, 6 * D_MODEL // MOD_TN),
        in_specs=[
            pl.BlockSpec((N_GROUPS, D_MODEL, LANES), lambda l, j: (0, 0, 0)),
            pl.BlockSpec((1, D_MODEL, MOD_TN), lambda l, j: (l, 0, j)),
            pl.BlockSpec((1, 1, MOD_TN), lambda l, j: (l, 0, j)),
        ],
        out_specs=pl.BlockSpec((1, SUBLANES, MOD_TN), lambda l, j: (l, 0, j)),
        out_shape=jax.ShapeDtypeStruct((DEPTH, SUBLANES, 6 * D_MODEL), F32),
        scratch_shapes=[pltpu.VMEM((N_GROUPS, D_MODEL, LANES), F32)],
        compiler_params=_params(("parallel", "parallel")),
        name="ada_mod",
    )(cb, w_mod, b_mod.reshape(DEPTH, 1, 6 * D_MODEL))
    return out[:, :N_GROUPS].reshape(DEPTH, N_GROUPS, 6, D_MODEL)


def _normed(x, g_ref, m_ref, shift, scale):
    ms = jnp.mean(x * x, axis=-1, keepdims=True)
    y = x * lax.rsqrt(ms + EPS) * g_ref[...]
    return y * (1.0 + m_ref[0, scale:scale + 1, :]) + m_ref[0, shift:shift + 1, :]


def _cast_kernel(w_ref, o_ref):
    o_ref[...] = w_ref[0].astype(BF16)


def _weight_bf16(w, i, tr=256):
    _, rows, cols = w.shape
    return pl.pallas_call(
        _cast_kernel,
        grid=(rows // tr,),
        in_specs=[pl.BlockSpec((1, tr, cols), lambda r: (i, r, 0))],
        out_specs=pl.BlockSpec((tr, cols), lambda r: (r, 0)),
        out_shape=jax.ShapeDtypeStruct((rows, cols), BF16),
        compiler_params=_params(("parallel",)),
        name="weight_bf16",
    )(w)


def _x_specs(x, tm):
    if not isinstance(x, tuple):
        return [x], [pl.BlockSpec((tm, D_MODEL), lambda i: (i, 0))]
    pb = N_PROMPT // tm
    return list(x), [pl.BlockSpec((tm, D_MODEL), lambda i: (jnp.minimum(i, pb - 1), 0)),
                     pl.BlockSpec((tm, D_MODEL), lambda i: (jnp.maximum(i - pb, 0), 0))]


def _x_block(x_refs, tm):
    if len(x_refs) == 1:
        return x_refs[0][...]
    return jnp.where(pl.program_id(0) < N_PROMPT // tm, x_refs[0][...], x_refs[1][...])


def _norm_mm_kernel(*refs, tn, tm):
    m_ref, g_ref, w_ref, o_ref = refs[-4:]
    h = _normed(_x_block(refs[:-4], tm), g_ref, m_ref, SH1, SC1).astype(BF16)
    for c0 in range(0, o_ref.shape[1], tn):
        o_ref[:, c0:c0 + tn] = jnp.dot(h, w_ref[:, c0:c0 + tn], preferred_element_type=F32)


def _norm_project(x, mods_l, g, w_bf16, tm=512, tn=512):
    nout = w_bf16.shape[1]
    x_args, x_specs = _x_specs(x, tm)
    return pl.pallas_call(
        functools.partial(_norm_mm_kernel, tn=tn, tm=tm),
        grid=(N_TOK // tm,),
        in_specs=x_specs + [
            pl.BlockSpec((1, 6, D_MODEL), lambda i: (_group_of_block(i, tm), 0, 0)),
            pl.BlockSpec((1, D_MODEL), lambda i: (0, 0)),
            pl.BlockSpec((D_MODEL, nout), lambda i: (0, 0)),
        ],
        out_specs=pl.BlockSpec((tm, nout), lambda i: (i, 0)),
        out_shape=jax.ShapeDtypeStruct((N_TOK, nout), F32),
        compiler_params=_params(("parallel",)),
        name="norm_project",
    )(*x_args, mods_l, g.reshape(1, D_MODEL), w_bf16)


def _proj_res_kernel(*refs, n_in, gate, tm):
    a_refs = refs[:n_in]
    w_refs = refs[n_in:2 * n_in]
    x_refs = refs[2 * n_in:-2]
    m_ref, o_ref = refs[-2:]
    acc = None
    for a_ref, w_ref in zip(a_refs, w_refs):
        t = jnp.dot(a_ref[...].astype(BF16), w_ref[...], preferred_element_type=F32)
        acc = t if acc is None else acc + t
    o_ref[...] = _x_block(x_refs, tm) + m_ref[0, gate:gate + 1, :] * acc


def _project_residual(acts, ws_bf16, x, mods_l, gate, tm=512):
    n_in = len(acts)
    x_args, x_specs = _x_specs(x, tm)
    in_specs = [pl.BlockSpec((tm, a.shape[1]), lambda i: (i, 0)) for a in acts]
    in_specs += [pl.BlockSpec(w.shape, lambda i: (0, 0)) for w in ws_bf16]
    in_specs += x_specs + [pl.BlockSpec((1, 6, D_MODEL), lambda i: (_group_of_block(i, tm), 0, 0))]
    return pl.pallas_call(
        functools.partial(_proj_res_kernel, n_in=n_in, gate=gate, tm=tm),
        grid=(N_TOK // tm,),
        in_specs=in_specs,
        out_specs=pl.BlockSpec((tm, D_MODEL), lambda i: (i, 0)),
        out_shape=jax.ShapeDtypeStruct((N_TOK, D_MODEL), F32),
        input_output_aliases={2 * n_in: 0} if len(x_args) == 1 else {},
        compiler_params=_params(("parallel",)),
        name="project_residual",
    )(*acts, *ws_bf16, *x_args, mods_l)


@functools.lru_cache(maxsize=None)
def _dft_mats(L):
    n = 2 * L
    ft = (np.arange(L, dtype=np.int64)[:, None] * np.arange(L, dtype=np.int64)[None, :]) % n
    ang = ft.astype(np.float64) * (2.0 * np.pi / n)
    return np.cos(ang).astype(np.float32), np.sin(ang).astype(np.float32)


def _dft_bf16(L):
    c, s = _dft_mats(L)
    return jnp.asarray(c).astype(BF16), jnp.asarray(s).astype(BF16)


def _alt_sign(shape, row0):
    t = lax.broadcasted_iota(jnp.int32, shape, 0) + row0
    return (1 - 2 * (t & 1)).astype(F32)


def _filter_dft_kernel(s_ref, d_ref, c_ref, sn_ref, ka_ref, ki_ref, kn_ref, *, L, fb):
    f0 = pl.program_id(0) * fb
    n = 2.0 * L
    s = s_ref[...]
    r = jnp.dot(c_ref[...], s, preferred_element_type=F32)
    im = jnp.dot(sn_ref[...], d_ref[...], preferred_element_type=F32)
    fidx = lax.broadcasted_iota(jnp.int32, r.shape, 0) + f0
    scale = jnp.where(fidx == 0, 1.0 / n, 2.0 / n)
    ka_ref[...] = r * scale
    ki_ref[...] = im * (2.0 / n)
    nyq = jnp.sum(s.astype(F32) * _alt_sign(s.shape, 0), axis=0, keepdims=True) * (1.0 / n)
    kn_ref[...] = jnp.broadcast_to(nyq, kn_ref.shape)


def _filter_spectra(s, d, L, dft):
    fb = min(L, 512)
    cmat, smat = dft
    w = 2 * HY_W
    return pl.pallas_call(
        functools.partial(_filter_dft_kernel, L=L, fb=fb),
        grid=(L // fb,),
        in_specs=[
            pl.BlockSpec((L, w), lambda f: (0, 0)),
            pl.BlockSpec((L, w), lambda f: (0, 0)),
            pl.BlockSpec((fb, L), lambda f: (f, 0)),
            pl.BlockSpec((fb, L), lambda f: (f, 0)),
        ],
        out_specs=[
            pl.BlockSpec((fb, w), lambda f: (f, 0)),
            pl.BlockSpec((fb, w), lambda f: (f, 0)),
            pl.BlockSpec((SUBLANES, w), lambda f: (0, 0)),
        ],
        out_shape=[
            jax.ShapeDtypeStruct((L, w), F32),
            jax.ShapeDtypeStruct((L, w), F32),
            jax.ShapeDtypeStruct((SUBLANES, w), F32),
        ],
        compiler_params=_params(("arbitrary",)),
        name="hyena_filter_dft",
    )(s.astype(BF16), d.astype(BF16), cmat, smat)


HY_TILE = 256


def _short_conv_tile(ref, r0, L, w_ref, b_ref):
    t = HY_TILE
    cur = ref[r0:r0 + t, :]
    rid = lax.broadcasted_iota(jnp.int32, cur.shape, 0)
    if r0 % L == 0:
        prev = jnp.where(rid == 0, 0.0, pltpu.roll(cur, 1, axis=0))
    else:
        prev = ref[r0 - 1:r0 - 1 + t, :]
    if (r0 + t) % L == 0:
        nxt = jnp.where(rid == t - 1, 0.0, pltpu.roll(cur, t - 1, axis=0))
    else:
        nxt = ref[r0 + 1:r0 + 1 + t, :]
    return prev * w_ref[0:1, :] + cur * w_ref[1:2, :] + nxt * w_ref[2:3, :] + b_ref[...]


def _hyena_conv_kernel(*refs, L, nseq, conv_a):
    (a_ref, x_ref, cr_ref, sr_ref, cc_ref, sc_ref, ka_ref, ki_ref, kn_ref,
     wa_ref, ba_ref, wx_ref, bx_ref, hb_ref) = refs[:14]
    o_ref, z_sc, acc_sc = refs[-3:]
    f = pl.program_id(1)
    nf = pl.num_programs(1)
    rows = nseq * L

    @pl.when(f == 0)
    def _():
        for q in range(nseq):
            nyq = jnp.zeros((1, HY_W), F32)
            for r0 in range(q * L, (q + 1) * L, HY_TILE):
                if conv_a:
                    zt = _short_conv_tile(a_ref, r0, L, wa_ref, ba_ref)
                else:
                    zt = a_ref[r0:r0 + HY_TILE, :]
                z_sc[r0:r0 + HY_TILE, :] = zt
                nyq = nyq + jnp.sum(zt * _alt_sign(zt.shape, r0), axis=0, keepdims=True)
            nyq = nyq * kn_ref[0:1, :]
            for r0 in range(q * L, (q + 1) * L, HY_TILE):
                acc_sc[r0:r0 + HY_TILE, :] = _alt_sign((HY_TILE, HY_W), r0) * nyq

    ka = ka_ref[...]
    ki = ki_ref[...]
    for q in range(nseq):
        z = z_sc[q * L:(q + 1) * L, :].astype(BF16)
        a = jnp.dot(cr_ref[...], z, preferred_element_type=F32)
        b = jnp.dot(sr_ref[...], z, preferred_element_type=F32)
        p = (a * ka + b * ki).astype(BF16)
        qq = (b * ka - a * ki).astype(BF16)
        acc_sc[q * L:(q + 1) * L, :] += (jnp.dot(cc_ref[...], p, preferred_element_type=F32)
                                         + jnp.dot(sc_ref[...], qq, preferred_element_type=F32))

    @pl.when(f == nf - 1)
    def _():
        for r0 in range(0, rows, HY_TILE):
            y = acc_sc[r0:r0 + HY_TILE, :] + z_sc[r0:r0 + HY_TILE, :] * hb_ref[...]
            o_ref[r0:r0 + HY_TILE, :] = (y * _short_conv_tile(x_ref, r0, L, wx_ref, bx_ref)).astype(o_ref.dtype)


def _hyena_conv(a, a_col, x, x_col, row0, n_rows, L, nseq, spectra, filt, wa, ba, wx, bx, hbias, conv_a, dft,
                full_out=False, out_prev=None):
    fb = min(L, 256)
    cmat, smat = dft
    ka, ki, kn = spectra
    rb = nseq * L
    a_off = row0 // rb if a.shape[0] != n_rows else 0
    x_off = row0 // rb
    o_off = row0 // rb if full_out else 0
    extra_specs, extra_args, aliases = [], [], {}
    if out_prev is not None:
        extra_specs, extra_args, aliases = [pl.BlockSpec(memory_space=pl.ANY)], [out_prev], {14: 0}
    return pl.pallas_call(
        functools.partial(_hyena_conv_kernel, L=L, nseq=nseq, conv_a=conv_a),
        grid=(n_rows // rb, L // fb),
        input_output_aliases=aliases,
        in_specs=extra_specs[:0] + [
            pl.BlockSpec((rb, HY_W), lambda i, f: (i + a_off, a_col)),
            pl.BlockSpec((rb, HY_W), lambda i, f: (i + x_off, x_col)),
            pl.BlockSpec((fb, L), lambda i, f: (f, 0)),
            pl.BlockSpec((fb, L), lambda i, f: (f, 0)),
            pl.BlockSpec((L, fb), lambda i, f: (0, f)),
            pl.BlockSpec((L, fb), lambda i, f: (0, f)),
            pl.BlockSpec((fb, HY_W), lambda i, f: (f, filt)),
            pl.BlockSpec((fb, HY_W), lambda i, f: (f, filt)),
            pl.BlockSpec((SUBLANES, HY_W), lambda i, f: (0, filt)),
            pl.BlockSpec((3, HY_W), lambda i, f: (0, 0)),
            pl.BlockSpec((1, HY_W), lambda i, f: (0, 0)),
            pl.BlockSpec((3, HY_W), lambda i, f: (0, 0)),
            pl.BlockSpec((1, HY_W), lambda i, f: (0, 0)),
            pl.BlockSpec((1, HY_W), lambda i, f: (0, 0)),
        ] + extra_specs,
        out_specs=pl.BlockSpec((rb, HY_W), lambda i, f: (i + o_off, 0)),
        out_shape=jax.ShapeDtypeStruct((N_TOK, HY_W), BF16) if full_out else jax.ShapeDtypeStruct((n_rows, HY_W), F32),
        scratch_shapes=[pltpu.VMEM((rb, HY_W), F32), pltpu.VMEM((rb, HY_W), F32)],
        compiler_params=_params(("parallel", "arbitrary")),
        name="hyena_conv",
    )(a, x, cmat, smat, cmat, smat, ka, ki, kn, wa, ba, wx, bx, hbias, *extra_args)


def _hyena_filter_taps(L, f1w, f1b, fr1, f2w, f2b, fr2, f3w):
    hp = lax.Precision.HIGHEST
    pos = jnp.arange(L, dtype=F32)
    t = pos / (L - 1)
    w = 2.0 * math.pi * pos / L
    f = jnp.linspace(1e-4, HY_BANDS - 1, HY_BANDS, dtype=F32)
    wf = w[:, None] * f[None, :]
    feat = jnp.concatenate([t[:, None], jnp.cos(wf), -jnp.sin(wf)], axis=-1)
    h = jnp.sin(fr1 * (jnp.dot(feat, f1w, precision=hp) + f1b))
    h = jnp.sin(fr2 * (jnp.dot(h, f2w, precision=hp) + f2b))
    h = jnp.dot(h, f3w, precision=hp).astype(F32)
    deltas = jnp.linspace(math.log(HY_TARGET) / HY_SLOW, math.log(HY_TARGET) / HY_FAST, HY_W, dtype=F32)
    window = jnp.exp(-t[:, None] * jnp.abs(deltas)[None, :])
    w = HY_W
    parts = [h[:, k * w:(k + 1) * w] * window for k in range(4)]
    colsum = [jnp.sum(jnp.abs(p), axis=0, keepdims=True) for p in parts]
    den = [colsum[0] + colsum[1] + EPS, colsum[2] + colsum[3] + EPS]
    fwd = jnp.concatenate([parts[0] / den[0], parts[2] / den[1]], axis=1)
    bwd = jnp.concatenate([parts[1] / den[0], parts[3] / den[1]], axis=1)
    bwd = jnp.where(pos[:, None] == 0, 0.0, bwd)
    return fwd + bwd, bwd - fwd


def _dot_t0(a, b):
    return lax.dot_general(a, b, (((0,), (0,)), ((), ())), preferred_element_type=F32)


def _dot_t1(a, b):
    return lax.dot_general(a, b, (((1,), (1,)), ((), ())), preferred_element_type=F32)


def _retention_kernel(*refs, L, has_s0, has_prev):
    refs = list(refs)
    q_ref, k_ref, v_ref, g_ref, dl_ref = refs[:5]
    pos = 5
    s0_ref = st_ref = None
    if has_s0:
        s0_ref = refs[pos]
        pos += 1
    if has_prev:
        pos += 1
    y_ref = refs[pos]
    pos += 1
    if not has_s0:
        st_ref = refs[pos]
        pos += 1
    sb_sc, sf_cur, sb_cur = refs[pos:]
    c = RET_CHUNK
    nc = L // c
    kscale = RET_HEAD_DIM ** -0.5
    ri = lax.broadcasted_iota(jnp.int32, (c, c), 0).astype(F32)
    ci = lax.broadcasted_iota(jnp.int32, (c, c), 1).astype(F32)
    diff = ri - ci
    dec = []
    for h in range(RET_HEADS):
        xf = dl_ref[0, h:h + 1, :]
        xb = dl_ref[1, h:h + 1, :]
        lgf = jnp.minimum(xf, 0.0) - jnp.log1p(jnp.exp(-jnp.abs(xf)))
        lgb = jnp.minimum(xb, 0.0) - jnp.log1p(jnp.exp(-jnp.abs(xb)))
        dec.append(dict(
            mask=(jnp.where(diff >= 0, jnp.exp(lgf * jnp.maximum(diff, 0.0)), 0.0)
                  + jnp.where(diff <= 0, jnp.exp(lgb * jnp.maximum(-diff, 0.0)), 0.0)),
            qdec_f=jnp.exp(lgf * (ri + 1.0)), kdec_f=jnp.exp(lgf * (c - 1.0 - ri)),
            qdec_b=jnp.exp(lgb * (c - ri)), kdec_b=jnp.exp(lgb * ri),
            cd_f=jnp.exp(lgf * c), cd_b=jnp.exp(lgb * c)))
        if has_s0:
            sf_cur[h] = s0_ref[0, 0, 0, h]
            sb_cur[h] = s0_ref[0, 0, 1, h]
        else:
            sf_cur[h] = jnp.zeros((RET_HEAD_DIM, RET_HEAD_DIM), F32)
            sb_cur[h] = jnp.zeros((RET_HEAD_DIM, RET_HEAD_DIM), F32)

    def bwd_body(i, carry):
        j = nc - 1 - i
        r0 = pl.multiple_of(j * c, c)
        for h in range(RET_HEADS):
            hs = slice(h * RET_HEAD_DIM, (h + 1) * RET_HEAD_DIM)
            sb = sb_cur[h]
            sb_sc[h * nc + j] = sb
            kc = k_ref[pl.ds(r0, c), hs] * kscale
            vc = v_ref[pl.ds(r0, c), hs]
            sb_cur[h] = sb * dec[h]["cd_b"] + _dot_t0((kc * dec[h]["kdec_b"]).astype(BF16), vc.astype(BF16))
        return carry

    lax.fori_loop(0, nc, bwd_body, 0)

    def fwd_body(j, carry):
        r0 = pl.multiple_of(j * c, c)
        for h in range(RET_HEADS):
            hs = slice(h * RET_HEAD_DIM, (h + 1) * RET_HEAD_DIM)
            dh = dec[h]
            sf = sf_cur[h]
            qc = q_ref[pl.ds(r0, c), hs]
            kc = k_ref[pl.ds(r0, c), hs] * kscale
            vc = v_ref[pl.ds(r0, c), hs].astype(BF16)
            scores = _dot_t1(qc.astype(BF16), kc.astype(BF16)) * dh["mask"]
            o = jnp.dot(scores.astype(BF16), vc, preferred_element_type=F32)
            o = o + jnp.dot((qc * dh["qdec_f"]).astype(BF16), sf.astype(BF16), preferred_element_type=F32)
            o = o + jnp.dot((qc * dh["qdec_b"]).astype(BF16), sb_sc[h * nc + j].astype(BF16),
                            preferred_element_type=F32)
            mu = jnp.mean(o, axis=-1, keepdims=True)
            var = jnp.mean(jnp.square(o - mu), axis=-1, keepdims=True)
            on = (o - mu) * lax.rsqrt(var + EPS)
            gc = g_ref[pl.ds(r0, c), hs]
            y_ref[pl.ds(r0, c), hs] = (gc * (1.0 / (1.0 + jnp.exp(-gc))) * on).astype(y_ref.dtype)
            sf_cur[h] = sf * dh["cd_f"] + _dot_t0((kc * dh["kdec_f"]).astype(BF16), vc)
        return carry

    lax.fori_loop(0, nc, fwd_body, 0)
    if st_ref is not None:
        for h in range(RET_HEADS):
            st_ref[0, 0, 0, h] = sf_cur[h]
            st_ref[0, 0, 1, h] = sb_cur[h]


def _retention(proj, dl, layer_i, row0, nseq, L, s0=None, st_prev=None, y_prev=None):
    off = row0 // L
    has_s0 = s0 is not None
    has_prev = (st_prev is not None) or (y_prev is not None)
    assert not (st_prev is not None and y_prev is not None)
    nc = L // RET_CHUNK
    n_ret = (DEPTH + 1) // 2
    col = lambda j: pl.BlockSpec((L, RET_W), lambda b: (b + off, 3 + j))
    in_specs = [col(0), col(1), col(2), col(3),
                pl.BlockSpec((2, RET_HEADS, LANES), lambda b: (0, 0, 0))]
    args = [proj, proj, proj, proj, dl]
    y_spec = pl.BlockSpec((L, RET_W), lambda b: (b + off, 0))
    y_shape = jax.ShapeDtypeStruct((N_TOK, RET_W), BF16)
    st_block = (1, 1, 2, RET_HEADS, RET_HEAD_DIM, RET_HEAD_DIM)
    st_spec = pl.BlockSpec(st_block, lambda b: (b, layer_i, 0, 0, 0, 0))
    aliases = {}
    if has_s0:
        in_specs.append(st_spec)
        args.append(s0)
        out_specs, out_shape = y_spec, y_shape
    else:
        out_specs = [y_spec, st_spec]
        out_shape = [y_shape, jax.ShapeDtypeStruct((nseq, n_ret) + st_block[2:], F32)]
    if has_prev:
        aliases = {len(args): 1 if st_prev is not None else 0}
        in_specs.append(pl.BlockSpec(memory_space=pl.ANY))
        args.append(st_prev if st_prev is not None else y_prev)
    state = pltpu.VMEM((RET_HEADS, RET_HEAD_DIM, RET_HEAD_DIM), F32)
    return pl.pallas_call(
        functools.partial(_retention_kernel, L=L, has_s0=has_s0, has_prev=has_prev),
        grid=(nseq,),
        in_specs=in_specs,
        out_specs=out_specs,
        out_shape=out_shape,
        input_output_aliases=aliases,
        scratch_shapes=[pltpu.VMEM((RET_HEADS * nc, RET_HEAD_DIM, RET_HEAD_DIM), F32), state, state],
        compiler_params=_params(("parallel",)),
        name="retention",
    )(*args)


@functools.lru_cache(maxsize=None)
def _rope_tables():
    L = DEC_SEQ
    rows = L // GRID_W
    row = np.repeat(np.arange(rows, dtype=np.float64), GRID_W)
    col = np.tile(np.arange(GRID_W, dtype=np.float64), rows)
    quarter = DIFF_HEAD_DIM // 4
    freqs = ROPE_BASE ** (-np.arange(quarter, dtype=np.float64) / quarter)
    j = np.arange(LANES)
    pos = np.where(((j % DIFF_HEAD_DIM) < DIFF_HEAD_DIM // 2)[None, :], row[:, None], col[:, None])
    ang = pos * freqs[j % quarter][None, :]
    cos = np.cos(ang).astype(np.float32)
    sin = np.sin(ang).astype(np.float32)
    first = ((j % (2 * quarter)) < quarter)[None, :]
    sin_a = np.where(first, -sin, 0.0).astype(np.float32)
    sin_b = np.where(first, 0.0, sin).astype(np.float32)
    return jnp.asarray(cos), jnp.asarray(sin_a), jnp.asarray(sin_b)


def _rope_head(x, cos, sin_a, sin_b):
    quarter = DIFF_HEAD_DIM // 4
    up = pltpu.roll(x, LANES - quarter, axis=1)
    dn = pltpu.roll(x, quarter, axis=1)
    return x * cos + up * sin_a + dn * sin_b


def _kv_prep_kernel(q_ref, k_ref, v_ref, cos_ref, sa_ref, sb_ref, qo_ref, ko_ref, vo_ref):
    cos, sa, sb = cos_ref[...], sa_ref[...], sb_ref[...]
    for h in range(DIFF_HEADS):
        hs = slice(h * LANES, (h + 1) * LANES)
        qo_ref[:, hs] = _rope_head(q_ref[:, hs], cos, sa, sb).astype(BF16)
        ko_ref[0, :, hs] = _rope_head(k_ref[:, hs], cos, sa, sb).astype(BF16)
    vo_ref[0] = v_ref[...].astype(BF16)


def _cache_copy_kernel(ck_ref, cv_ref, k_in, v_in, ko_ref, vo_ref):
    del k_in, v_in
    ko_ref[0] = ck_ref[0, 0].astype(BF16)
    vo_ref[0] = cv_ref[0, 0].astype(BF16)


def _sample_qkv(proj, cache_k, cache_v, layer_i, tm=256):
    cos, sa, sb = _rope_tables()
    lk = PAST_LEN + DEC_SEQ
    pblk = N_PROMPT // tm
    nblk = DEC_SEQ // tm
    cblk = PAST_LEN // tm
    tab = pl.BlockSpec((tm, LANES), lambda b, i: (i, 0))
    q, k, v = pl.pallas_call(
        _kv_prep_kernel,
        grid=(DEC_BATCH, nblk),
        in_specs=[
            pl.BlockSpec((tm, D_MODEL), lambda b, i: (pblk + b * nblk + i, 0)),
            pl.BlockSpec((tm, D_MODEL), lambda b, i: (pblk + b * nblk + i, 1)),
            pl.BlockSpec((tm, D_MODEL), lambda b, i: (pblk + b * nblk + i, 2)),
            tab, tab, tab,
        ],
        out_specs=[
            pl.BlockSpec((tm, D_MODEL), lambda b, i: (b * nblk + i, 0)),
            pl.BlockSpec((1, tm, D_MODEL), lambda b, i: (b, cblk + i, 0)),
            pl.BlockSpec((1, tm, D_MODEL), lambda b, i: (b, cblk + i, 0)),
        ],
        out_shape=[
            jax.ShapeDtypeStruct((N_SAMPLE, D_MODEL), BF16),
            jax.ShapeDtypeStruct((DEC_BATCH, lk, D_MODEL), BF16),
            jax.ShapeDtypeStruct((DEC_BATCH, lk, D_MODEL), BF16),
        ],
        compiler_params=_params(("parallel", "parallel")),
        name="rope_qkv",
    )(proj, proj, proj, cos, sa, sb)
    n_att = DEPTH // 2
    ck = cache_k.reshape(DEC_BATCH, n_att, PAST_LEN, D_MODEL)
    cv = cache_v.reshape(DEC_BATCH, n_att, PAST_LEN, D_MODEL)
    k, v = pl.pallas_call(
        _cache_copy_kernel,
        grid=(DEC_BATCH,),
        in_specs=[
            pl.BlockSpec((1, 1, PAST_LEN, D_MODEL), lambda b: (b, layer_i, 0, 0)),
            pl.BlockSpec((1, 1, PAST_LEN, D_MODEL), lambda b: (b, layer_i, 0, 0)),
            pl.BlockSpec(memory_space=pl.ANY),
            pl.BlockSpec(memory_space=pl.ANY),
        ],
        out_specs=[
            pl.BlockSpec((1, PAST_LEN, D_MODEL), lambda b: (b, 0, 0)),
            pl.BlockSpec((1, PAST_LEN, D_MODEL), lambda b: (b, 0, 0)),
        ],
        out_shape=[
            jax.ShapeDtypeStruct((DEC_BATCH, lk, D_MODEL), BF16),
            jax.ShapeDtypeStruct((DEC_BATCH, lk, D_MODEL), BF16),
        ],
        input_output_aliases={2: 0, 3: 1},
        compiler_params=_params(("parallel",)),
        name="cache_prepend",
    )(ck, cv, k, v)
    return q, k, v


def _diff_attn_kernel(*refs, lam_init, batched_kv, cache_out):
    q_ref, k_ref, v_ref, lam_ref, sg_ref = refs[:5]
    if cache_out:
        o_ref, kc_ref, vc_ref = refs[-3:]
        kc_ref[0, 0] = k_ref[...]
        vc_ref[0, 0] = v_ref[...]
    else:
        o_ref = refs[-1]
    lv = lam_ref[...]
    lam = (jnp.exp(jnp.sum(lv[0:1] * lv[1:2], axis=-1, keepdims=True))
           - jnp.exp(jnp.sum(lv[2:3] * lv[3:4], axis=-1, keepdims=True)) + lam_init)
    lane = lax.broadcasted_iota(jnp.int32, (1, LANES), 1)
    m1 = (lane < DIFF_HEAD_DIM).astype(F32)
    m2 = 1.0 - m1
    scale = DIFF_HEAD_DIM ** -0.5
    for h in range(DIFF_HEADS):
        hs = slice(h * LANES, (h + 1) * LANES)
        q = q_ref[:, hs].astype(F32) * scale
        if batched_kv:
            k = k_ref[0, :, hs].astype(BF16)
            v = v_ref[0, :, hs].astype(BF16)
        else:
            k = k_ref[:, hs].astype(BF16)
            v = v_ref[:, hs].astype(BF16)
        v_ext = jnp.concatenate([v, jnp.ones_like(v)], axis=1)
        outs = []
        for m in (m1, m2):
            s = _dot_t1((q * m).astype(BF16), k)
            s = s - jnp.max(s, axis=-1, keepdims=True)
            pv = jnp.dot(jnp.exp(s).astype(BF16), v_ext, preferred_element_type=F32)
            outs.append(pv[:, :LANES] / pv[:, LANES:])
        o = outs[0] - lam * outs[1]
        ms = jnp.mean(o * o, axis=-1, keepdims=True)
        o_ref[:, hs] = (o * lax.rsqrt(ms + EPS) * sg_ref[...] * (1.0 - lam_init)).astype(o_ref.dtype)


def _diff_attention(q, k, v, lam_vec, subln, lam_init, *, nb, lq, lk, tq, q_row0, q_col, kv_cols, batched_kv,
                    out_row0=0, out_prev=None, cache_layer=None, cache_prev=None):
    nq = lq // tq
    qoff = q_row0 // tq
    ooff = out_row0 // tq
    cache_out = cache_layer is not None
    extra_specs, extra_args, aliases = [], [], {}
    if out_prev is not None:
        extra_specs, extra_args, aliases = [pl.BlockSpec(memory_space=pl.ANY)], [out_prev], {5: 0}
    o_spec = pl.BlockSpec((tq, D_MODEL), lambda b, i: (ooff + b * nq + i, 0))
    o_shape = jax.ShapeDtypeStruct((N_TOK, D_MODEL), BF16)
    if cache_out:
        assert not batched_kv and nq == 1 and out_prev is None
        if cache_prev is not None:
            extra_specs = [pl.BlockSpec(memory_space=pl.ANY)] * 2
            extra_args = list(cache_prev)
            aliases = {5: 1, 6: 2}
        c_spec = pl.BlockSpec((1, 1, lk, D_MODEL), lambda b, i: (b, cache_layer, 0, 0))
        c_shape = jax.ShapeDtypeStruct((nb, DEPTH // 2, lk, D_MODEL), F32)
        o_spec, o_shape = [o_spec, c_spec, c_spec], [o_shape, c_shape, c_shape]
    q_spec = pl.BlockSpec((tq, D_MODEL), lambda b, i: (qoff + b * nq + i, q_col))
    if batched_kv:
        k_spec = pl.BlockSpec((1, lk, D_MODEL), lambda b, i: (b, 0, 0))
        v_spec = k_spec
    else:
        k_spec = pl.BlockSpec((lk, D_MODEL), lambda b, i: (b, kv_cols[0]))
        v_spec = pl.BlockSpec((lk, D_MODEL), lambda b, i: (b, kv_cols[1]))
    return pl.pallas_call(
        functools.partial(_diff_attn_kernel, lam_init=lam_init, batched_kv=batched_kv, cache_out=cache_out),
        grid=(nb, nq),
        in_specs=[q_spec, k_spec, v_spec,
                  pl.BlockSpec((4, LANES), lambda b, i: (0, 0)),
                  pl.BlockSpec((1, LANES), lambda b, i: (0, 0))] + extra_specs,
        out_specs=o_spec,
        out_shape=o_shape,
        input_output_aliases=aliases,
        compiler_params=_params(("parallel", "arbitrary")),
        name="diff_attention",
    )(q, k, v, lam_vec, subln, *extra_args)


PACK_W = D_MODEL // 4


def _router_kernel(x_ref, m_ref, g_ref, wr_ref, h_ref, aff_ref):
    h = _normed(x_ref[...], g_ref, m_ref, SH2, SC2)
    hb = h.astype(BF16)
    hf = hb.astype(F32)
    bits = lax.bitcast_convert_type(hf, jnp.int32)
    for p in range(2):
        lo = lax.shift_right_logical(bits[:, p * PACK_W:(p + 1) * PACK_W], 16)
        hi = bits[:, (2 + p) * PACK_W:(3 + p) * PACK_W] & jnp.int32(-65536)
        h_ref[p] = hi | lo
    hl = (h - hf).astype(BF16)
    wr = wr_ref[...]
    wh = wr.astype(BF16)
    wl = (wr - wh.astype(F32)).astype(BF16)
    logits = (jnp.dot(hb, wh, preferred_element_type=F32) + jnp.dot(hl, wh, preferred_element_type=F32)
              + jnp.dot(hb, wl, preferred_element_type=F32))
    lt = jnp.transpose(logits)[0:N_EXPERTS, :]
    lt = lt - jnp.max(lt, axis=0, keepdims=True)
    e = jnp.exp(lt)
    aff_ref[...] = e / jnp.sum(e, axis=0, keepdims=True)


def _router(x, mods_l, g, w_router, tm=512):
    wr = jnp.pad(w_router, ((0, 0), (0, LANES - N_EXPERTS)))
    return pl.pallas_call(
        _router_kernel,
        grid=(N_TOK // tm,),
        in_specs=[
            pl.BlockSpec((tm, D_MODEL), lambda i: (i, 0)),
            pl.BlockSpec((1, 6, D_MODEL), lambda i: (_group_of_block(i, tm), 0, 0)),
            pl.BlockSpec((1, D_MODEL), lambda i: (0, 0)),
            pl.BlockSpec((D_MODEL, LANES), lambda i: (0, 0)),
        ],
        out_specs=[
            pl.BlockSpec((2, tm, PACK_W), lambda i: (0, i, 0)),
            pl.BlockSpec((N_EXPERTS, tm), lambda i: (0, i)),
        ],
        out_shape=[
            jax.ShapeDtypeStruct((2, N_TOK, PACK_W), jnp.int32),
            jax.ShapeDtypeStruct((N_EXPERTS, N_TOK), F32),
        ],
        compiler_params=_params(("parallel",)),
        name="norm_router",
    )(x, mods_l, g.reshape(1, D_MODEL), wr)


SC_WINDOW = 128


def _gather_rows(table, idx):
    n = idx.shape[0]
    mesh = plsc.VectorSubcoreMesh(core_axis_name="core", subcore_axis_name="subcore")

    @pl.kernel(out_type=jax.ShapeDtypeStruct((n, PACK_W), table.dtype), mesh=mesh, scratch_types=[])
    def gather_kernel(t_hbm, i_hbm, o_hbm):
        def body(i_vmem, o_vmem):
            pltpu.sync_copy(t_hbm.at[i_vmem.at[0]], o_vmem)

        pltpu.emit_pipeline(
            body,
            grid=(n // SC_WINDOW,),
            in_specs=[pl.BlockSpec((1, SC_WINDOW), index_map=lambda i: (0, i))],
            out_specs=[pl.BlockSpec((SC_WINDOW, PACK_W), index_map=lambda i: (i, 0))],
            core_axis_name=("core", "subcore"),
            dimension_semantics=(pltpu.PARALLEL,),
        )(i_hbm, o_hbm)

    return gather_kernel(table, idx.reshape(1, n))


FFN_TF = 512
FFN_TR = 512


def _unpack_rows(pa, pb):
    def lo(w):
        return lax.bitcast_convert_type(lax.shift_left(w, 16), F32).astype(BF16)

    def hi(w):
        return lax.bitcast_convert_type(w & jnp.int32(-65536), F32).astype(BF16)

    return jnp.concatenate([lo(pa), lo(pb), hi(pa), hi(pb)], axis=1)


def _expert_ffn_kernel(xs_ref, wg_ref, wu_ref, wd_ref, gate_ref, m_ref, o_ref, xs_sc):
    f = pl.program_id(1)

    @pl.when(f == 0)
    def _():
        for r0 in range(0, CAP_T, FFN_TR):
            xs_sc[r0:r0 + FFN_TR, :] = _unpack_rows(xs_ref[0, 0, r0:r0 + FFN_TR, :], xs_ref[1, 0, r0:r0 + FFN_TR, :])

    wg = wg_ref[0].astype(BF16)
    wu = wu_ref[0].astype(BF16)
    wd = wd_ref[0].astype(BF16)
    for r0 in range(0, CAP_T, FFN_TR):
        rs = slice(r0, r0 + FFN_TR)
        xs = xs_sc[rs, :]
        a = jnp.dot(xs, wg, preferred_element_type=F32)
        u = jnp.dot(xs, wu, preferred_element_type=F32)
        hid = (a * (1.0 / (1.0 + jnp.exp(-a))) * u).astype(BF16)
        y = jnp.dot(hid, wd, preferred_element_type=F32)

        @pl.when(f == 0)
        def _():
            o_ref[0, rs, :] = y

        @pl.when(jnp.logical_and(f > 0, f < pl.num_programs(1) - 1))
        def _():
            o_ref[0, rs, :] += y

        @pl.when(f == pl.num_programs(1) - 1)
        def _():
            gg = gate_ref[0, rs, :]
            scale = gg[:, 0:1] * m_ref[0, G2:G2 + 1, :]
            for g in range(1, N_GROUPS):
                scale = scale + gg[:, g:g + 1] * m_ref[g, G2:G2 + 1, :]
            o_ref[0, rs, :] = (o_ref[0, rs, :] + y) * scale


def _expert_ffn(xs, wg, wu, wd, gate, mods_l, l):
    return pl.pallas_call(
        _expert_ffn_kernel,
        grid=(N_EXPERTS, EXPERT_FF // FFN_TF),
        in_specs=[
            pl.BlockSpec((2, 1, CAP_T, PACK_W), lambda e, f: (0, e, 0, 0)),
            pl.BlockSpec((None, 1, D_MODEL, FFN_TF), lambda e, f: (l, e, 0, f)),
            pl.BlockSpec((None, 1, D_MODEL, FFN_TF), lambda e, f: (l, e, 0, f)),
            pl.BlockSpec((None, 1, FFN_TF, D_MODEL), lambda e, f: (l, e, f, 0)),
            pl.BlockSpec((1, CAP_T, N_GROUPS), lambda e, f: (e, 0, 0)),
            pl.BlockSpec((N_GROUPS, 6, D_MODEL), lambda e, f: (0, 0, 0)),
        ],
        out_specs=pl.BlockSpec((1, CAP_T, D_MODEL), lambda e, f: (e, 0, 0)),
        out_shape=jax.ShapeDtypeStruct((N_EXPERTS, CAP_T, D_MODEL), F32),
        scratch_shapes=[pltpu.VMEM((CAP_T, D_MODEL), BF16)],
        compiler_params=_params(("parallel", "arbitrary")),
        name="expert_ffn",
    )(xs, wg, wu, wd, gate, mods_l)


def _final_norm_kernel(x_ref, g_ref, o_ref):
    x = x_ref[...]
    ms = jnp.mean(x * x, axis=-1, keepdims=True)
    o_ref[...] = x * lax.rsqrt(ms + EPS) * g_ref[...]


def _final_norm(x, g, row0, n_rows, tm=512):
    off = row0 // tm
    return pl.pallas_call(
        _final_norm_kernel,
        grid=(n_rows // tm,),
        in_specs=[pl.BlockSpec((tm, D_MODEL), lambda i: (i + off, 0)),
                  pl.BlockSpec((1, D_MODEL), lambda i: (0, 0))],
        out_specs=pl.BlockSpec((tm, D_MODEL), lambda i: (i, 0)),
        out_shape=jax.ShapeDtypeStruct((n_rows, D_MODEL), F32),
        compiler_params=_params(("parallel",)),
        name="final_norm",
    )(x, g.reshape(1, D_MODEL))


def _even_layer(x, mods_l, i, state_ret, st_prev, dfts, norm1_g, w_in_even, hy_short_w, hy_short_b, hy_f1_w, hy_f1_b,
                hy_freq1, hy_f2_w, hy_f2_b, hy_freq2, hy_f3_w, hy_bias, ret_decay, w_out_even):
    proj = _norm_project(x, mods_l, norm1_g, _weight_bf16(w_in_even, i))
    sw, sbias = hy_short_w[i], hy_short_b[i].reshape(1, 3 * HY_W)
    hb = hy_bias[i]
    y_hy = None
    for (row0, n_rows, L, nseq) in ((0, N_PROMPT, SEQ, 8), (N_PROMPT, N_SAMPLE, DEC_SEQ, 1)):
        s, d = _hyena_filter_taps(L, hy_f1_w[i], hy_f1_b[i], hy_freq1[i], hy_f2_w[i], hy_f2_b[i], hy_freq2[i], hy_f3_w[i])
        spectra = _filter_spectra(s, d, L, dfts[L])
        wv, bv = sw[:, 0:HY_W], sbias[:, 0:HY_W]
        w1, b1 = sw[:, HY_W:2 * HY_W], sbias[:, HY_W:2 * HY_W]
        w2, b2 = sw[:, 2 * HY_W:], sbias[:, 2 * HY_W:]
        z1 = _hyena_conv(proj, 0, proj, 1, row0, n_rows, L, nseq, spectra, 0, wv, bv, w1, b1, hb[0:1], True, dfts[L])
        y_hy = _hyena_conv(z1, 0, proj, 2, row0, n_rows, L, nseq, spectra, 1, wv, bv, w2, b2, hb[1:2], False, dfts[L],
                           full_out=True, out_prev=y_hy)
    dl = jnp.broadcast_to(ret_decay[i].astype(F32)[:, :, None], (2, RET_HEADS, LANES))
    y_ret, st = _retention(proj, dl, i, 0, BATCH, SEQ, st_prev=st_prev)
    y_ret = _retention(proj, dl, i, N_PROMPT, DEC_BATCH, DEC_SEQ, s0=state_ret, y_prev=y_ret)
    wo = _weight_bf16(w_out_even, i)
    x = _project_residual([y_hy, y_ret], [wo[:HY_W], wo[HY_W:]], x, mods_l, G1)
    return x, st


def _odd_layer(x, mods_l, l, i, cache_k, cache_v, kv_prev, norm1_g, w_in_odd, lam_q1, lam_k1, lam_q2, lam_k2, subln_g,
               w_out_odd):
    lam_init = 0.8 - 0.6 * math.exp(-0.3 * l)
    proj = _norm_project(x, mods_l, norm1_g, _weight_bf16(w_in_odd, i))
    lam_vec = jnp.pad(jnp.stack([lam_q1[i], lam_k1[i], lam_q2[i], lam_k2[i]]), ((0, 0), (0, LANES - DIFF_HEAD_DIM)))
    sg = subln_g[i].reshape(1, LANES)
    o_p, kc, vc = _diff_attention(proj, proj, proj, lam_vec, sg, lam_init, nb=BATCH, lq=SEQ, lk=SEQ, tq=SEQ,
                                  q_row0=0, q_col=0, kv_cols=(1, 2), batched_kv=False,
                                  cache_layer=i, cache_prev=kv_prev)
    qs, ks, vs = _sample_qkv(proj, cache_k, cache_v, i)
    o = _diff_attention(qs, ks, vs, lam_vec, sg, lam_init, nb=DEC_BATCH, lq=DEC_SEQ, lk=PAST_LEN + DEC_SEQ, tq=256,
                        q_row0=0, q_col=0, kv_cols=None, batched_kv=True, out_row0=N_PROMPT, out_prev=o_p)
    x = _project_residual([o], [_weight_bf16(w_out_odd, i)], x, mods_l, G1)
    return x, (kc, vc)


def _moe_layer(x, mods_l, l, norm2_g, w_router, wg, wu, wd):
    h, aff = _router(x, mods_l, norm2_g, w_router)
    gate_p, idx_p = lax.top_k(aff[:, :N_PROMPT], CAP_P)
    gate_s, idx_s = lax.top_k(aff[:, N_PROMPT:], CAP_S)
    idx = jnp.concatenate([idx_p, idx_s + N_PROMPT], axis=1)
    gate = jnp.concatenate([gate_p, gate_s], axis=1)
    flat = idx.reshape(-1)
    xs = _gather_rows(h.reshape(2 * N_TOK, PACK_W), jnp.concatenate([flat, flat + N_TOK]))
    xs = xs.reshape(2, N_EXPERTS, CAP_T, PACK_W)
    grp = jnp.where(idx < N_PROMPT, 0, 1 + (idx - N_PROMPT) // DEC_SEQ)
    gate_grp = jnp.where(grp[:, :, None] == jnp.arange(N_GROUPS)[None, None, :], gate[:, :, None], 0.0)
    out = _expert_ffn(xs, wg, wu, wd, gate_grp, mods_l, l)
    return x.at[idx.reshape(-1)].add(out.reshape(-1, D_MODEL))


def kernel(x_prompt, x_sample, state_ret, cache_k, cache_v, c, c_ctx, w_mod, b_mod, norm1_g, norm2_g, w_in_even, hy_short_w, hy_short_b, hy_f1_w, hy_f1_b, hy_freq1, hy_f2_w, hy_f2_b, hy_freq2, hy_f3_w, hy_bias, ret_decay, w_out_even, w_in_odd, lam_q1, lam_k1, lam_q2, lam_k2, subln_g, w_out_odd, moe_router, moe_wg, moe_wu, moe_wd, final_g):
    x = (x_prompt.reshape(N_PROMPT, D_MODEL), x_sample.reshape(N_SAMPLE, D_MODEL))
    mods = _modulation(c, c_ctx, w_mod, b_mod)
    dfts = {L: _dft_bf16(L) for L in (SEQ, DEC_SEQ)}
    state_ret = state_ret.astype(F32)
    st = kv = None
    for l in range(DEPTH):
        i = l // 2
        if l % 2 == 0:
            x, st = _even_layer(x, mods[l], i, state_ret, st, dfts, norm1_g[l], w_in_even, hy_short_w, hy_short_b,
                                hy_f1_w, hy_f1_b, hy_freq1, hy_f2_w, hy_f2_b, hy_freq2, hy_f3_w, hy_bias, ret_decay,
                                w_out_even)
        else:
            x, kv = _odd_layer(x, mods[l], l, i, cache_k, cache_v, kv, norm1_g[l], w_in_odd, lam_q1, lam_k1, lam_q2,
                               lam_k2, subln_g, w_out_odd)
        x = _moe_layer(x, mods[l], l, norm2_g[l], moe_router[l], moe_wg, moe_wu, moe_wd)
    y_prompt = _final_norm(x, final_g, 0, N_PROMPT).reshape(BATCH, SEQ, D_MODEL)
    y_sample = _final_norm(x, final_g, N_PROMPT, N_SAMPLE).reshape(DEC_BATCH, DEC_SEQ, D_MODEL)
    cache_shape = (BATCH, DEPTH // 2, SEQ, DIFF_HEADS, 2 * DIFF_HEAD_DIM)
    return (y_prompt, y_sample, st, kv[0].reshape(cache_shape), kv[1].reshape(cache_shape))
```

```python
import functools
import math

import numpy as np
import jax
import jax.numpy as jnp
from jax import lax
from jax.experimental import pallas as pl
from jax.experimental.pallas import tpu as pltpu
from jax.experimental.pallas import tpu_sc as plsc

F32 = jnp.float32
BF16 = jnp.bfloat16

D_MODEL = 1024
BATCH = 32
SEQ = 256
DEPTH = 4
DEC_BATCH = 2
DEC_SEQ = 2048
PAST_LEN = 256
GRID_W = 64
HY_W = 512
HY_EMB = 33
HY_BANDS = 16
HY_FF = 64
HY_TARGET = 1e-2
HY_FAST = 0.3
HY_SLOW = 1.5
RET_W = 512
RET_HEADS = 4
RET_HEAD_DIM = 128
RET_CHUNK = 128
DIFF_HEADS = 8
DIFF_HEAD_DIM = 64
ROPE_BASE = 10000.0
N_EXPERTS = 16
EC_FACTOR = 2
EXPERT_FF = 1024
EVEN_IN = 3 * HY_W + 4 * RET_W
EPS = 1e-6

N_PROMPT = BATCH * SEQ
N_SAMPLE = DEC_BATCH * DEC_SEQ
N_TOK = N_PROMPT + N_SAMPLE
N_GROUPS = 1 + DEC_BATCH
CAP_P = EC_FACTOR * N_PROMPT // N_EXPERTS
CAP_S = EC_FACTOR * N_SAMPLE // N_EXPERTS
CAP_T = CAP_P + CAP_S

LANES = 128
SUBLANES = 8
VMEM_LIMIT = 56 * 1024 * 1024

SH1, SC1, G1, SH2, SC2, G2 = range(6)


def _params(sem, vmem=VMEM_LIMIT):
    return pltpu.CompilerParams(dimension_semantics=sem, vmem_limit_bytes=vmem)


def _group_of_block(i, tm):
    pb = N_PROMPT // tm
    return jnp.where(i < pb, 0, 1 + (i - pb) // (DEC_SEQ // tm))


MOD_TN = 1024


MOD_UNROLL = 4


def _mod_kernel(cb_ref, w_ref, b_ref, o_ref, a_sc):
    nchunk = MOD_TN // LANES
    cv = cb_ref[...]
    a_sc[...] = cv * (1.0 / (1.0 + jnp.exp(-cv)))

    def body(kb, accs):
        accs = list(accs)
        for u in range(MOD_UNROLL):
            k0 = pl.multiple_of((kb * MOD_UNROLL + u) * SUBLANES, SUBLANES)
            a = [a_sc[r, pl.ds(k0, SUBLANES), :] for r in range(N_GROUPS)]
            for ci in range(nchunk):
                wv = w_ref[0, pl.ds(k0, SUBLANES), ci * LANES:(ci + 1) * LANES]
                for r in range(N_GROUPS):
                    accs[ci * N_GROUPS + r] = accs[ci * N_GROUPS + r] + wv * a[r]
        return tuple(accs)

    init = tuple(jnp.zeros((SUBLANES, LANES), F32) for _ in range(N_GROUPS * nchunk))
    accs = lax.fori_loop(0, D_MODEL // (SUBLANES * MOD_UNROLL), body, init)
    o_ref[...] = jnp.zeros(o_ref.shape, F32)
    for r in range(N_GROUPS):
        for ci in range(nchunk):
            row = jnp.sum(accs[ci * N_GROUPS + r], axis=0, keepdims=True)
            o_ref[0, r:r + 1, ci * LANES:(ci + 1) * LANES] = row + b_ref[0, :, ci * LANES:(ci + 1) * LANES]


def _modulation(c, c_ctx, w_mod, b_mod):
    cond = jnp.concatenate([c_ctx[None, :], c], axis=0)
    cb = jnp.broadcast_to(cond[:, :, None], (N_GROUPS, D_MODEL, LANES))
    out = pl.pallas_call(
        _mod_kernel,
        grid=(DEPTH, 6 * D_MODEL // MOD_TN),
        in_specs=[
            pl.BlockSpec((N_GROUPS, D_MODEL, LANES), lambda l, j: (0, 0, 0)),
            pl.BlockSpec((1, D_MODEL, MOD_TN), lambda l, j: (l, 0, j)),
            pl.BlockSpec((1, 1, MOD_TN), lambda l, j: (l, 0, j)),
        ],
        out_specs=pl.BlockSpec((1, SUBLANES, MOD_TN), lambda l, j: (l, 0, j)),
        out_shape=jax.ShapeDtypeStruct((DEPTH, SUBLANES, 6 * D_MODEL), F32),
        scratch_shapes=[pltpu.VMEM((N_GROUPS, D_MODEL, LANES), F32)],
        compiler_params=_params(("parallel", "parallel")),
        name="ada_mod",
    )(cb, w_mod, b_mod.reshape(DEPTH, 1, 6 * D_MODEL))
    return out[:, :N_GROUPS].reshape(DEPTH, N_GROUPS, 6, D_MODEL)


def _normed(x, g_ref, m_ref, shift, scale):
    ms = jnp.mean(x * x, axis=-1, keepdims=True)
    y = x * lax.rsqrt(ms + EPS) * g_ref[...]
    return y * (1.0 + m_ref[0, scale:scale + 1, :]) + m_ref[0, shift:shift + 1, :]


def _weight_bf16(w, i):
    return w[i].astype(BF16)


def _x_specs(x, tm):
    if not isinstance(x, tuple):
        return [x], [pl.BlockSpec((tm, D_MODEL), lambda i: (i, 0))]
    pb = N_PROMPT // tm
    return list(x), [pl.BlockSpec((tm, D_MODEL), lambda i: (jnp.minimum(i, pb - 1), 0)),
                     pl.BlockSpec((tm, D_MODEL), lambda i: (jnp.maximum(i - pb, 0), 0))]


def _x_block(x_refs, tm):
    if len(x_refs) == 1:
        return x_refs[0][...]
    return jnp.where(pl.program_id(0) < N_PROMPT // tm, x_refs[0][...], x_refs[1][...])


def _norm_mm_kernel(*refs, tn, tm):
    m_ref, g_ref, w_ref, o_ref = refs[-4:]
    h = _normed(_x_block(refs[:-4], tm), g_ref, m_ref, SH1, SC1).astype(BF16)
    for c0 in range(0, o_ref.shape[1], tn):
        o_ref[:, c0:c0 + tn] = jnp.dot(h, w_ref[:, c0:c0 + tn], preferred_element_type=F32)


def _norm_project(x, mods_l, g, w_bf16, tm=512, tn=512):
    nout = w_bf16.shape[1]
    x_args, x_specs = _x_specs(x, tm)
    return pl.pallas_call(
        functools.partial(_norm_mm_kernel, tn=tn, tm=tm),
        grid=(N_TOK // tm,),
        in_specs=x_specs + [
            pl.BlockSpec((1, 6, D_MODEL), lambda i: (_group_of_block(i, tm), 0, 0)),
            pl.BlockSpec((1, D_MODEL), lambda i: (0, 0)),
            pl.BlockSpec((D_MODEL, nout), lambda i: (0, 0)),
        ],
        out_specs=pl.BlockSpec((tm, nout), lambda i: (i, 0)),
        out_shape=jax.ShapeDtypeStruct((N_TOK, nout), F32),
        compiler_params=_params(("parallel",)),
        name="norm_project",
    )(*x_args, mods_l, g.reshape(1, D_MODEL), w_bf16)


def _proj_res_kernel(*refs, n_in, gate, tm):
    a_refs = refs[:n_in]
    w_refs = refs[n_in:2 * n_in]
    x_refs = refs[2 * n_in:-2]
    m_ref, o_ref = refs[-2:]
    acc = None
    for a_ref, w_ref in zip(a_refs, w_refs):
        t = jnp.dot(a_ref[...].astype(BF16), w_ref[...], preferred_element_type=F32)
        acc = t if acc is None else acc + t
    o_ref[...] = _x_block(x_refs, tm) + m_ref[0, gate:gate + 1, :] * acc


def _project_residual(acts, ws_bf16, x, mods_l, gate, tm=512):
    n_in = len(acts)
    x_args, x_specs = _x_specs(x, tm)
    in_specs = [pl.BlockSpec((tm, a.shape[1]), lambda i: (i, 0)) for a in acts]
    in_specs += [pl.BlockSpec(w.shape, lambda i: (0, 0)) for w in ws_bf16]
    in_specs += x_specs + [pl.BlockSpec((1, 6, D_MODEL), lambda i: (_group_of_block(i, tm), 0, 0))]
    return pl.pallas_call(
        functools.partial(_proj_res_kernel, n_in=n_in, gate=gate, tm=tm),
        grid=(N_TOK // tm,),
        in_specs=in_specs,
        out_specs=pl.BlockSpec((tm, D_MODEL), lambda i: (i, 0)),
        out_shape=jax.ShapeDtypeStruct((N_TOK, D_MODEL), F32),
        input_output_aliases={2 * n_in: 0} if len(x_args) == 1 else {},
        compiler_params=_params(("parallel",)),
        name="project_residual",
    )(*acts, *ws_bf16, *x_args, mods_l)


@functools.lru_cache(maxsize=None)
def _dft_mats(L):
    n = 2 * L
    ft = (np.arange(L, dtype=np.int64)[:, None] * np.arange(L, dtype=np.int64)[None, :]) % n
    ang = ft.astype(np.float64) * (2.0 * np.pi / n)
    return np.cos(ang).astype(np.float32), np.sin(ang).astype(np.float32)


def _dft_bf16(L):
    c, s = _dft_mats(L)
    return jnp.asarray(c).astype(BF16), jnp.asarray(s).astype(BF16)


def _alt_sign(shape, row0):
    t = lax.broadcasted_iota(jnp.int32, shape, 0) + row0
    return (1 - 2 * (t & 1)).astype(F32)


def _filter_dft_kernel(s_ref, d_ref, c_ref, sn_ref, ka_ref, ki_ref, kn_ref, *, L, fb):
    f0 = pl.program_id(0) * fb
    n = 2.0 * L
    s = s_ref[...]
    r = jnp.dot(c_ref[...], s, preferred_element_type=F32)
    im = jnp.dot(sn_ref[...], d_ref[...], preferred_element_type=F32)
    fidx = lax.broadcasted_iota(jnp.int32, r.shape, 0) + f0
    scale = jnp.where(fidx == 0, 1.0 / n, 2.0 / n)
    ka_ref[...] = r * scale
    ki_ref[...] = im * (2.0 / n)
    nyq = jnp.sum(s.astype(F32) * _alt_sign(s.shape, 0), axis=0, keepdims=True) * (1.0 / n)
    kn_ref[...] = jnp.broadcast_to(nyq, kn_ref.shape)


def _filter_spectra(s, d, L, dft):
    fb = min(L, 512)
    cmat, smat = dft
    w = 2 * HY_W
    return pl.pallas_call(
        functools.partial(_filter_dft_kernel, L=L, fb=fb),
        grid=(L // fb,),
        in_specs=[
            pl.BlockSpec((L, w), lambda f: (0, 0)),
            pl.BlockSpec((L, w), lambda f: (0, 0)),
            pl.BlockSpec((fb, L), lambda f: (f, 0)),
            pl.BlockSpec((fb, L), lambda f: (f, 0)),
        ],
        out_specs=[
            pl.BlockSpec((fb, w), lambda f: (f, 0)),
            pl.BlockSpec((fb, w), lambda f: (f, 0)),
            pl.BlockSpec((SUBLANES, w), lambda f: (0, 0)),
        ],
        out_shape=[
            jax.ShapeDtypeStruct((L, w), F32),
            jax.ShapeDtypeStruct((L, w), F32),
            jax.ShapeDtypeStruct((SUBLANES, w), F32),
        ],
        compiler_params=_params(("arbitrary",)),
        name="hyena_filter_dft",
    )(s.astype(BF16), d.astype(BF16), cmat, smat)


HY_TILE = 256


def _short_conv_tile(ref, r0, L, w_ref, b_ref):
    t = HY_TILE
    cur = ref[r0:r0 + t, :]
    rid = lax.broadcasted_iota(jnp.int32, cur.shape, 0)
    if r0 % L == 0:
        prev = jnp.where(rid == 0, 0.0, pltpu.roll(cur, 1, axis=0))
    else:
        prev = ref[r0 - 1:r0 - 1 + t, :]
    if (r0 + t) % L == 0:
        nxt = jnp.where(rid == t - 1, 0.0, pltpu.roll(cur, t - 1, axis=0))
    else:
        nxt = ref[r0 + 1:r0 + 1 + t, :]
    return prev * w_ref[0:1, :] + cur * w_ref[1:2, :] + nxt * w_ref[2:3, :] + b_ref[...]


def _hyena_conv_kernel(*refs, L, nseq, conv_a):
    (a_ref, x_ref, cr_ref, sr_ref, cc_ref, sc_ref, ka_ref, ki_ref, kn_ref,
     wa_ref, ba_ref, wx_ref, bx_ref, hb_ref) = refs[:14]
    o_ref, z_sc, acc_sc = refs[-3:]
    f = pl.program_id(1)
    nf = pl.num_programs(1)
    rows = nseq * L

    @pl.when(f == 0)
    def _():
        for q in range(nseq):
            nyq = jnp.zeros((1, HY_W), F32)
            for r0 in range(q * L, (q + 1) * L, HY_TILE):
                if conv_a:
                    zt = _short_conv_tile(a_ref, r0, L, wa_ref, ba_ref)
                else:
                    zt = a_ref[r0:r0 + HY_TILE, :]
                z_sc[r0:r0 + HY_TILE, :] = zt
                nyq = nyq + jnp.sum(zt * _alt_sign(zt.shape, r0), axis=0, keepdims=True)
            nyq = nyq * kn_ref[0:1, :]
            for r0 in range(q * L, (q + 1) * L, HY_TILE):
                acc_sc[r0:r0 + HY_TILE, :] = _alt_sign((HY_TILE, HY_W), r0) * nyq

    ka = ka_ref[...]
    ki = ki_ref[...]
    for q in range(nseq):
        z = z_sc[q * L:(q + 1) * L, :].astype(BF16)
        a = jnp.dot(cr_ref[...], z, preferred_element_type=F32)
        b = jnp.dot(sr_ref[...], z, preferred_element_type=F32)
        p = (a * ka + b * ki).astype(BF16)
        qq = (b * ka - a * ki).astype(BF16)
        acc_sc[q * L:(q + 1) * L, :] += (jnp.dot(cc_ref[...], p, preferred_element_type=F32)
                                         + jnp.dot(sc_ref[...], qq, preferred_element_type=F32))

    @pl.when(f == nf - 1)
    def _():
        for r0 in range(0, rows, HY_TILE):
            y = acc_sc[r0:r0 + HY_TILE, :] + z_sc[r0:r0 + HY_TILE, :] * hb_ref[...]
            o_ref[r0:r0 + HY_TILE, :] = (y * _short_conv_tile(x_ref, r0, L, wx_ref, bx_ref)).astype(o_ref.dtype)


def _hyena_conv(a, a_col, x, x_col, row0, n_rows, L, nseq, spectra, filt, wa, ba, wx, bx, hbias, conv_a, dft,
                full_out=False, out_prev=None):
    fb = min(L, 256)
    cmat, smat = dft
    ka, ki, kn = spectra
    rb = nseq * L
    a_off = row0 // rb if a.shape[0] != n_rows else 0
    x_off = row0 // rb
    o_off = row0 // rb if full_out else 0
    extra_specs, extra_args, aliases = [], [], {}
    if out_prev is not None:
        extra_specs, extra_args, aliases = [pl.BlockSpec(memory_space=pl.ANY)], [out_prev], {14: 0}
    return pl.pallas_call(
        functools.partial(_hyena_conv_kernel, L=L, nseq=nseq, conv_a=conv_a),
        grid=(n_rows // rb, L // fb),
        input_output_aliases=aliases,
        in_specs=extra_specs[:0] + [
            pl.BlockSpec((rb, HY_W), lambda i, f: (i + a_off, a_col)),
            pl.BlockSpec((rb, HY_W), lambda i, f: (i + x_off, x_col)),
            pl.BlockSpec((fb, L), lambda i, f: (f, 0)),
            pl.BlockSpec((fb, L), lambda i, f: (f, 0)),
            pl.BlockSpec((L, fb), lambda i, f: (0, f)),
            pl.BlockSpec((L, fb), lambda i, f: (0, f)),
            pl.BlockSpec((fb, HY_W), lambda i, f: (f, filt)),
            pl.BlockSpec((fb, HY_W), lambda i, f: (f, filt)),
            pl.BlockSpec((SUBLANES, HY_W), lambda i, f: (0, filt)),
            pl.BlockSpec((3, HY_W), lambda i, f: (0, 0)),
            pl.BlockSpec((1, HY_W), lambda i, f: (0, 0)),
            pl.BlockSpec((3, HY_W), lambda i, f: (0, 0)),
            pl.BlockSpec((1, HY_W), lambda i, f: (0, 0)),
            pl.BlockSpec((1, HY_W), lambda i, f: (0, 0)),
        ] + extra_specs,
        out_specs=pl.BlockSpec((rb, HY_W), lambda i, f: (i + o_off, 0)),
        out_shape=jax.ShapeDtypeStruct((N_TOK, HY_W), BF16) if full_out else jax.ShapeDtypeStruct((n_rows, HY_W), F32),
        scratch_shapes=[pltpu.VMEM((rb, HY_W), F32), pltpu.VMEM((rb, HY_W), F32)],
        compiler_params=_params(("parallel", "arbitrary")),
        name="hyena_conv",
    )(a, x, cmat, smat, cmat, smat, ka, ki, kn, wa, ba, wx, bx, hbias, *extra_args)


def _hyena_filter_taps(L, f1w, f1b, fr1, f2w, f2b, fr2, f3w):
    hp = lax.Precision.HIGHEST
    pos = jnp.arange(L, dtype=F32)
    t = pos / (L - 1)
    w = 2.0 * math.pi * pos / L
    f = jnp.linspace(1e-4, HY_BANDS - 1, HY_BANDS, dtype=F32)
    wf = w[:, None] * f[None, :]
    feat = jnp.concatenate([t[:, None], jnp.cos(wf), -jnp.sin(wf)], axis=-1)
    h = jnp.sin(fr1 * (jnp.dot(feat, f1w, precision=hp) + f1b))
    h = jnp.sin(fr2 * (jnp.dot(h, f2w, precision=hp) + f2b))
    h = jnp.dot(h, f3w, precision=hp).astype(F32)
    deltas = jnp.linspace(math.log(HY_TARGET) / HY_SLOW, math.log(HY_TARGET) / HY_FAST, HY_W, dtype=F32)
    window = jnp.exp(-t[:, None] * jnp.abs(deltas)[None, :])
    w = HY_W
    parts = [h[:, k * w:(k + 1) * w] * window for k in range(4)]
    colsum = [jnp.sum(jnp.abs(p), axis=0, keepdims=True) for p in parts]
    den = [colsum[0] + colsum[1] + EPS, colsum[2] + colsum[3] + EPS]
    fwd = jnp.concatenate([parts[0] / den[0], parts[2] / den[1]], axis=1)
    bwd = jnp.concatenate([parts[1] / den[0], parts[3] / den[1]], axis=1)
    bwd = jnp.where(pos[:, None] == 0, 0.0, bwd)
    return fwd + bwd, bwd - fwd


def _dot_t0(a, b):
    return lax.dot_general(a, b, (((0,), (0,)), ((), ())), preferred_element_type=F32)


def _dot_t1(a, b):
    return lax.dot_general(a, b, (((1,), (1,)), ((), ())), preferred_element_type=F32)


def _retention_kernel(*refs, L, has_s0, has_prev):
    refs = list(refs)
    q_ref, k_ref, v_ref, g_ref, dl_ref = refs[:5]
    pos = 5
    s0_ref = st_ref = None
    if has_s0:
        s0_ref = refs[pos]
        pos += 1
    if has_prev:
        pos += 1
    y_ref = refs[pos]
    pos += 1
    if not has_s0:
        st_ref = refs[pos]
        pos += 1
    sb_sc, sf_cur, sb_cur = refs[pos:]
    c = RET_CHUNK
    nc = L // c
    kscale = RET_HEAD_DIM ** -0.5
    ri = lax.broadcasted_iota(jnp.int32, (c, c), 0).astype(F32)
    ci = lax.broadcasted_iota(jnp.int32, (c, c), 1).astype(F32)
    diff = ri - ci
    dec = []
    for h in range(RET_HEADS):
        xf = dl_ref[0, h:h + 1, :]
        xb = dl_ref[1, h:h + 1, :]
        lgf = jnp.minimum(xf, 0.0) - jnp.log1p(jnp.exp(-jnp.abs(xf)))
        lgb = jnp.minimum(xb, 0.0) - jnp.log1p(jnp.exp(-jnp.abs(xb)))
        dec.append(dict(
            mask=(jnp.where(diff >= 0, jnp.exp(lgf * jnp.maximum(diff, 0.0)), 0.0)
                  + jnp.where(diff <= 0, jnp.exp(lgb * jnp.maximum(-diff, 0.0)), 0.0)),
            qdec_f=jnp.exp(lgf * (ri + 1.0)), kdec_f=jnp.exp(lgf * (c - 1.0 - ri)),
            qdec_b=jnp.exp(lgb * (c - ri)), kdec_b=jnp.exp(lgb * ri),
            cd_f=jnp.exp(lgf * c), cd_b=jnp.exp(lgb * c)))
        if has_s0:
            sf_cur[h] = s0_ref[0, 0, 0, h]
            sb_cur[h] = s0_ref[0, 0, 1, h]
        else:
            sf_cur[h] = jnp.zeros((RET_HEAD_DIM, RET_HEAD_DIM), F32)
            sb_cur[h] = jnp.zeros((RET_HEAD_DIM, RET_HEAD_DIM), F32)

    def bwd_body(i, carry):
        j = nc - 1 - i
        r0 = pl.multiple_of(j * c, c)
        for h in range(RET_HEADS):
            hs = slice(h * RET_HEAD_DIM, (h + 1) * RET_HEAD_DIM)
            sb = sb_cur[h]
            sb_sc[h * nc + j] = sb
            kc = k_ref[pl.ds(r0, c), hs] * kscale
            vc = v_ref[pl.ds(r0, c), hs]
            sb_cur[h] = sb * dec[h]["cd_b"] + _dot_t0((kc * dec[h]["kdec_b"]).astype(BF16), vc.astype(BF16))
        return carry

    lax.fori_loop(0, nc, bwd_body, 0)

    def fwd_body(j, carry):
        r0 = pl.multiple_of(j * c, c)
        for h in range(RET_HEADS):
            hs = slice(h * RET_HEAD_DIM, (h + 1) * RET_HEAD_DIM)
            dh = dec[h]
            sf = sf_cur[h]
            qc = q_ref[pl.ds(r0, c), hs]
            kc = k_ref[pl.ds(r0, c), hs] * kscale
            vc = v_ref[pl.ds(r0, c), hs].astype(BF16)
            scores = _dot_t1(qc.astype(BF16), kc.astype(BF16)) * dh["mask"]
            o = jnp.dot(scores.astype(BF16), vc, preferred_element_type=F32)
            o = o + jnp.dot((qc * dh["qdec_f"]).astype(BF16), sf.astype(BF16), preferred_element_type=F32)
            o = o + jnp.dot((qc * dh["qdec_b"]).astype(BF16), sb_sc[h * nc + j].astype(BF16),
                            preferred_element_type=F32)
            mu = jnp.mean(o, axis=-1, keepdims=True)
            var = jnp.mean(jnp.square(o - mu), axis=-1, keepdims=True)
            on = (o - mu) * lax.rsqrt(var + EPS)
            gc = g_ref[pl.ds(r0, c), hs]
            y_ref[pl.ds(r0, c), hs] = (gc * (1.0 / (1.0 + jnp.exp(-gc))) * on).astype(y_ref.dtype)
            sf_cur[h] = sf * dh["cd_f"] + _dot_t0((kc * dh["kdec_f"]).astype(BF16), vc)
        return carry

    lax.fori_loop(0, nc, fwd_body, 0)
    if st_ref is not None:
        for h in range(RET_HEADS):
            st_ref[0, 0, 0, h] = sf_cur[h]
            st_ref[0, 0, 1, h] = sb_cur[h]


def _retention(proj, dl, layer_i, row0, nseq, L, s0=None, st_prev=None, y_prev=None):
    off = row0 // L
    has_s0 = s0 is not None
    has_prev = (st_prev is not None) or (y_prev is not None)
    assert not (st_prev is not None and y_prev is not None)
    nc = L // RET_CHUNK
    n_ret = (DEPTH + 1) // 2
    col = lambda j: pl.BlockSpec((L, RET_W), lambda b: (b + off, 3 + j))
    in_specs = [col(0), col(1), col(2), col(3),
                pl.BlockSpec((2, RET_HEADS, LANES), lambda b: (0, 0, 0))]
    args = [proj, proj, proj, proj, dl]
    y_spec = pl.BlockSpec((L, RET_W), lambda b: (b + off, 0))
    y_shape = jax.ShapeDtypeStruct((N_TOK, RET_W), BF16)
    st_block = (1, 1, 2, RET_HEADS, RET_HEAD_DIM, RET_HEAD_DIM)
    st_spec = pl.BlockSpec(st_block, lambda b: (b, layer_i, 0, 0, 0, 0))
    aliases = {}
    if has_s0:
        in_specs.append(st_spec)
        args.append(s0)
        out_specs, out_shape = y_spec, y_shape
    else:
        out_specs = [y_spec, st_spec]
        out_shape = [y_shape, jax.ShapeDtypeStruct((nseq, n_ret) + st_block[2:], F32)]
    if has_prev:
        aliases = {len(args): 1 if st_prev is not None else 0}
        in_specs.append(pl.BlockSpec(memory_space=pl.ANY))
        args.append(st_prev if st_prev is not None else y_prev)
    state = pltpu.VMEM((RET_HEADS, RET_HEAD_DIM, RET_HEAD_DIM), F32)
    return pl.pallas_call(
        functools.partial(_retention_kernel, L=L, has_s0=has_s0, has_prev=has_prev),
        grid=(nseq,),
        in_specs=in_specs,
        out_specs=out_specs,
        out_shape=out_shape,
        input_output_aliases=aliases,
        scratch_shapes=[pltpu.VMEM((RET_HEADS * nc, RET_HEAD_DIM, RET_HEAD_DIM), F32), state, state],
        compiler_params=_params(("parallel",)),
        name="retention",
    )(*args)


@functools.lru_cache(maxsize=None)
def _rope_tables():
    L = DEC_SEQ
    rows = L // GRID_W
    row = np.repeat(np.arange(rows, dtype=np.float64), GRID_W)
    col = np.tile(np.arange(GRID_W, dtype=np.float64), rows)
    quarter = DIFF_HEAD_DIM // 4
    freqs = ROPE_BASE ** (-np.arange(quarter, dtype=np.float64) / quarter)
    j = np.arange(LANES)
    pos = np.where(((j % DIFF_HEAD_DIM) < DIFF_HEAD_DIM // 2)[None, :], row[:, None], col[:, None])
    ang = pos * freqs[j % quarter][None, :]
    cos = np.cos(ang).astype(np.float32)
    sin = np.sin(ang).astype(np.float32)
    first = ((j % (2 * quarter)) < quarter)[None, :]
    sin_a = np.where(first, -sin, 0.0).astype(np.float32)
    sin_b = np.where(first, 0.0, sin).astype(np.float32)
    return jnp.asarray(cos), jnp.asarray(sin_a), jnp.asarray(sin_b)


def _rope_head(x, cos, sin_a, sin_b):
    quarter = DIFF_HEAD_DIM // 4
    up = pltpu.roll(x, LANES - quarter, axis=1)
    dn = pltpu.roll(x, quarter, axis=1)
    return x * cos + up * sin_a + dn * sin_b


def _kv_prep_kernel(q_ref, k_ref, v_ref, cos_ref, sa_ref, sb_ref, qo_ref, ko_ref, vo_ref):
    cos, sa, sb = cos_ref[...], sa_ref[...], sb_ref[...]
    for h in range(DIFF_HEADS):
        hs = slice(h * LANES, (h + 1) * LANES)
        qo_ref[:, hs] = _rope_head(q_ref[:, hs], cos, sa, sb).astype(BF16)
        ko_ref[0, :, hs] = _rope_head(k_ref[:, hs], cos, sa, sb).astype(BF16)
    vo_ref[0] = v_ref[...].astype(BF16)


def _cache_copy_kernel(ck_ref, cv_ref, k_in, v_in, ko_ref, vo_ref):
    del k_in, v_in
    ko_ref[0] = ck_ref[0, 0].astype(BF16)
    vo_ref[0] = cv_ref[0, 0].astype(BF16)


def _sample_qkv(proj, cache_k, cache_v, layer_i, tm=256):
    cos, sa, sb = _rope_tables()
    lk = PAST_LEN + DEC_SEQ
    pblk = N_PROMPT // tm
    nblk = DEC_SEQ // tm
    cblk = PAST_LEN // tm
    tab = pl.BlockSpec((tm, LANES), lambda b, i: (i, 0))
    q, k, v = pl.pallas_call(
        _kv_prep_kernel,
        grid=(DEC_BATCH, nblk),
        in_specs=[
            pl.BlockSpec((tm, D_MODEL), lambda b, i: (pblk + b * nblk + i, 0)),
            pl.BlockSpec((tm, D_MODEL), lambda b, i: (pblk + b * nblk + i, 1)),
            pl.BlockSpec((tm, D_MODEL), lambda b, i: (pblk + b * nblk + i, 2)),
            tab, tab, tab,
        ],
        out_specs=[
            pl.BlockSpec((tm, D_MODEL), lambda b, i: (b * nblk + i, 0)),
            pl.BlockSpec((1, tm, D_MODEL), lambda b, i: (b, cblk + i, 0)),
            pl.BlockSpec((1, tm, D_MODEL), lambda b, i: (b, cblk + i, 0)),
        ],
        out_shape=[
            jax.ShapeDtypeStruct((N_SAMPLE, D_MODEL), BF16),
            jax.ShapeDtypeStruct((DEC_BATCH, lk, D_MODEL), BF16),
            jax.ShapeDtypeStruct((DEC_BATCH, lk, D_MODEL), BF16),
        ],
        compiler_params=_params(("parallel", "parallel")),
        name="rope_qkv",
    )(proj, proj, proj, cos, sa, sb)
    n_att = DEPTH // 2
    ck = cache_k.reshape(DEC_BATCH, n_att, PAST_LEN, D_MODEL)
    cv = cache_v.reshape(DEC_BATCH, n_att, PAST_LEN, D_MODEL)
    k, v = pl.pallas_call(
        _cache_copy_kernel,
        grid=(DEC_BATCH,),
        in_specs=[
            pl.BlockSpec((1, 1, PAST_LEN, D_MODEL), lambda b: (b, layer_i, 0, 0)),
            pl.BlockSpec((1, 1, PAST_LEN, D_MODEL), lambda b: (b, layer_i, 0, 0)),
            pl.BlockSpec(memory_space=pl.ANY),
            pl.BlockSpec(memory_space=pl.ANY),
        ],
        out_specs=[
            pl.BlockSpec((1, PAST_LEN, D_MODEL), lambda b: (b, 0, 0)),
            pl.BlockSpec((1, PAST_LEN, D_MODEL), lambda b: (b, 0, 0)),
        ],
        out_shape=[
            jax.ShapeDtypeStruct((DEC_BATCH, lk, D_MODEL), BF16),
            jax.ShapeDtypeStruct((DEC_BATCH, lk, D_MODEL), BF16),
        ],
        input_output_aliases={2: 0, 3: 1},
        compiler_params=_params(("parallel",)),
        name="cache_prepend",
    )(ck, cv, k, v)
    return q, k, v


def _diff_attn_kernel(*refs, lam_init, batched_kv, cache_out):
    q_ref, k_ref, v_ref, lam_ref, sg_ref = refs[:5]
    if cache_out:
        o_ref, kc_ref, vc_ref = refs[-3:]
        kc_ref[0, 0] = k_ref[...]
        vc_ref[0, 0] = v_ref[...]
    else:
        o_ref = refs[-1]
    lv = lam_ref[...]
    lam = (jnp.exp(jnp.sum(lv[0:1] * lv[1:2], axis=-1, keepdims=True))
           - jnp.exp(jnp.sum(lv[2:3] * lv[3:4], axis=-1, keepdims=True)) + lam_init)
    lane = lax.broadcasted_iota(jnp.int32, (1, LANES), 1)
    m1 = (lane < DIFF_HEAD_DIM).astype(F32)
    m2 = 1.0 - m1
    scale = DIFF_HEAD_DIM ** -0.5
    for h in range(DIFF_HEADS):
        hs = slice(h * LANES, (h + 1) * LANES)
        q = q_ref[:, hs].astype(F32) * scale
        if batched_kv:
            k = k_ref[0, :, hs].astype(BF16)
            v = v_ref[0, :, hs].astype(BF16)
        else:
            k = k_ref[:, hs].astype(BF16)
            v = v_ref[:, hs].astype(BF16)
        v_ext = jnp.concatenate([v, jnp.ones_like(v)], axis=1)
        outs = []
        for m in (m1, m2):
            s = _dot_t1((q * m).astype(BF16), k)
            s = s - jnp.max(s, axis=-1, keepdims=True)
            pv = jnp.dot(jnp.exp(s).astype(BF16), v_ext, preferred_element_type=F32)
            outs.append(pv[:, :LANES] / pv[:, LANES:])
        o = outs[0] - lam * outs[1]
        ms = jnp.mean(o * o, axis=-1, keepdims=True)
        o_ref[:, hs] = (o * lax.rsqrt(ms + EPS) * sg_ref[...] * (1.0 - lam_init)).astype(o_ref.dtype)


def _diff_attention(q, k, v, lam_vec, subln, lam_init, *, nb, lq, lk, tq, q_row0, q_col, kv_cols, batched_kv,
                    out_row0=0, out_prev=None, cache_layer=None, cache_prev=None):
    nq = lq // tq
    qoff = q_row0 // tq
    ooff = out_row0 // tq
    cache_out = cache_layer is not None
    extra_specs, extra_args, aliases = [], [], {}
    if out_prev is not None:
        extra_specs, extra_args, aliases = [pl.BlockSpec(memory_space=pl.ANY)], [out_prev], {5: 0}
    o_spec = pl.BlockSpec((tq, D_MODEL), lambda b, i: (ooff + b * nq + i, 0))
    o_shape = jax.ShapeDtypeStruct((N_TOK, D_MODEL), BF16)
    if cache_out:
        assert not batched_kv and nq == 1 and out_prev is None
        if cache_prev is not None:
            extra_specs = [pl.BlockSpec(memory_space=pl.ANY)] * 2
            extra_args = list(cache_prev)
            aliases = {5: 1, 6: 2}
        c_spec = pl.BlockSpec((1, 1, lk, D_MODEL), lambda b, i: (b, cache_layer, 0, 0))
        c_shape = jax.ShapeDtypeStruct((nb, DEPTH // 2, lk, D_MODEL), F32)
        o_spec, o_shape = [o_spec, c_spec, c_spec], [o_shape, c_shape, c_shape]
    q_spec = pl.BlockSpec((tq, D_MODEL), lambda b, i: (qoff + b * nq + i, q_col))
    if batched_kv:
        k_spec = pl.BlockSpec((1, lk, D_MODEL), lambda b, i: (b, 0, 0))
        v_spec = k_spec
    else:
        k_spec = pl.BlockSpec((lk, D_MODEL), lambda b, i: (b, kv_cols[0]))
        v_spec = pl.BlockSpec((lk, D_MODEL), lambda b, i: (b, kv_cols[1]))
    return pl.pallas_call(
        functools.partial(_diff_attn_kernel, lam_init=lam_init, batched_kv=batched_kv, cache_out=cache_out),
        grid=(nb, nq),
        in_specs=[q_spec, k_spec, v_spec,
                  pl.BlockSpec((4, LANES), lambda b, i: (0, 0)),
                  pl.BlockSpec((1, LANES), lambda b, i: (0, 0))] + extra_specs,
        out_specs=o_spec,
        out_shape=o_shape,
        input_output_aliases=aliases,
        compiler_params=_params(("parallel", "arbitrary")),
        name="diff_attention",
    )(q, k, v, lam_vec, subln, *extra_args)


PACK_W = D_MODEL // 4


def _router_kernel(x_ref, m_ref, g_ref, wr_ref, h_ref, aff_ref):
    h = _normed(x_ref[...], g_ref, m_ref, SH2, SC2)
    hb = h.astype(BF16)
    hf = hb.astype(F32)
    bits = lax.bitcast_convert_type(hf, jnp.int32)
    for p in range(2):
        lo = lax.shift_right_logical(bits[:, p * PACK_W:(p + 1) * PACK_W], 16)
        hi = bits[:, (2 + p) * PACK_W:(3 + p) * PACK_W] & jnp.int32(-65536)
        h_ref[p] = hi | lo
    hl = (h - hf).astype(BF16)
    wr = wr_ref[...]
    wh = wr.astype(BF16)
    wl = (wr - wh.astype(F32)).astype(BF16)
    logits = (jnp.dot(hb, wh, preferred_element_type=F32) + jnp.dot(hl, wh, preferred_element_type=F32)
              + jnp.dot(hb, wl, preferred_element_type=F32))
    lt = jnp.transpose(logits)[0:N_EXPERTS, :]
    lt = lt - jnp.max(lt, axis=0, keepdims=True)
    e = jnp.exp(lt)
    aff_ref[...] = e / jnp.sum(e, axis=0, keepdims=True)


def _router(x, mods_l, g, w_router, tm=512):
    wr = jnp.pad(w_router, ((0, 0), (0, LANES - N_EXPERTS)))
    return pl.pallas_call(
        _router_kernel,
        grid=(N_TOK // tm,),
        in_specs=[
            pl.BlockSpec((tm, D_MODEL), lambda i: (i, 0)),
            pl.BlockSpec((1, 6, D_MODEL), lambda i: (_group_of_block(i, tm), 0, 0)),
            pl.BlockSpec((1, D_MODEL), lambda i: (0, 0)),
            pl.BlockSpec((D_MODEL, LANES), lambda i: (0, 0)),
        ],
        out_specs=[
            pl.BlockSpec((2, tm, PACK_W), lambda i: (0, i, 0)),
            pl.BlockSpec((N_EXPERTS, tm), lambda i: (0, i)),
        ],
        out_shape=[
            jax.ShapeDtypeStruct((2, N_TOK, PACK_W), jnp.int32),
            jax.ShapeDtypeStruct((N_EXPERTS, N_TOK), F32),
        ],
        compiler_params=_params(("parallel",)),
        name="norm_router",
    )(x, mods_l, g.reshape(1, D_MODEL), wr)


SC_WINDOW = 128


def _gather_rows(table, idx):
    n = idx.shape[0]
    mesh = plsc.VectorSubcoreMesh(core_axis_name="core", subcore_axis_name="subcore")

    @pl.kernel(out_type=jax.ShapeDtypeStruct((n, PACK_W), table.dtype), mesh=mesh, scratch_types=[])
    def gather_kernel(t_hbm, i_hbm, o_hbm):
        def body(i_vmem, o_vmem):
            pltpu.sync_copy(t_hbm.at[i_vmem.at[0]], o_vmem)

        pltpu.emit_pipeline(
            body,
            grid=(n // SC_WINDOW,),
            in_specs=[pl.BlockSpec((1, SC_WINDOW), index_map=lambda i: (0, i))],
            out_specs=[pl.BlockSpec((SC_WINDOW, PACK_W), index_map=lambda i: (i, 0))],
            core_axis_name=("core", "subcore"),
            dimension_semantics=(pltpu.PARALLEL,),
        )(i_hbm, o_hbm)

    return gather_kernel(table, idx.reshape(1, n))


FFN_TF = 512
FFN_TR = 768


def _unpack_rows(pa, pb):
    def lo(w):
        return lax.bitcast_convert_type(lax.shift_left(w, 16), F32).astype(BF16)

    def hi(w):
        return lax.bitcast_convert_type(w & jnp.int32(-65536), F32).astype(BF16)

    return jnp.concatenate([lo(pa), lo(pb), hi(pa), hi(pb)], axis=1)


def _expert_ffn_kernel(xs_ref, wg_ref, wu_ref, wd_ref, gate_ref, m_ref, o_ref, xs_sc):
    f = pl.program_id(1)

    @pl.when(f == 0)
    def _():
        for r0 in range(0, CAP_T, FFN_TR):
            xs_sc[r0:r0 + FFN_TR, :] = _unpack_rows(xs_ref[0, 0, r0:r0 + FFN_TR, :], xs_ref[1, 0, r0:r0 + FFN_TR, :])

    wg = wg_ref[0].astype(BF16)
    wu = wu_ref[0].astype(BF16)
    wd = wd_ref[0].astype(BF16)
    for r0 in range(0, CAP_T, FFN_TR):
        rs = slice(r0, r0 + FFN_TR)
        xs = xs_sc[rs, :]
        a = jnp.dot(xs, wg, preferred_element_type=F32)
        u = jnp.dot(xs, wu, preferred_element_type=F32)
        hid = (a * (1.0 / (1.0 + jnp.exp(-a))) * u).astype(BF16)
        y = jnp.dot(hid, wd, preferred_element_type=F32)

        @pl.when(f == 0)
        def _():
            o_ref[0, rs, :] = y

        @pl.when(jnp.logical_and(f > 0, f < pl.num_programs(1) - 1))
        def _():
            o_ref[0, rs, :] += y

        @pl.when(f == pl.num_programs(1) - 1)
        def _():
            gg = gate_ref[0, rs, :]
            scale = gg[:, 0:1] * m_ref[0, G2:G2 + 1, :]
            for g in range(1, N_GROUPS):
                scale = scale + gg[:, g:g + 1] * m_ref[g, G2:G2 + 1, :]
            o_ref[0, rs, :] = (o_ref[0, rs, :] + y) * scale


def _expert_ffn(xs, wg, wu, wd, gate, mods_l, l):
    return pl.pallas_call(
        _expert_ffn_kernel,
        grid=(N_EXPERTS, EXPERT_FF // FFN_TF),
        in_specs=[
            pl.BlockSpec((2, 1, CAP_T, PACK_W), lambda e, f: (0, e, 0, 0)),
            pl.BlockSpec((None, 1, D_MODEL, FFN_TF), lambda e, f: (l, e, 0, f)),
            pl.BlockSpec((None, 1, D_MODEL, FFN_TF), lambda e, f: (l, e, 0, f)),
            pl.BlockSpec((None, 1, FFN_TF, D_MODEL), lambda e, f: (l, e, f, 0)),
            pl.BlockSpec((1, CAP_T, N_GROUPS), lambda e, f: (e, 0, 0)),
            pl.BlockSpec((N_GROUPS, 6, D_MODEL), lambda e, f: (0, 0, 0)),
        ],
        out_specs=pl.BlockSpec((1, CAP_T, D_MODEL), lambda e, f: (e, 0, 0)),
        out_shape=jax.ShapeDtypeStruct((N_EXPERTS, CAP_T, D_MODEL), F32),
        scratch_shapes=[pltpu.VMEM((CAP_T, D_MODEL), BF16)],
        compiler_params=_params(("parallel", "arbitrary")),
        name="expert_ffn",
    )(xs, wg, wu, wd, gate, mods_l)


def _final_norm_kernel(x_ref, g_ref, o_ref):
    x = x_ref[...]
    ms = jnp.mean(x * x, axis=-1, keepdims=True)
    o_ref[...] = x * lax.rsqrt(ms + EPS) * g_ref[...]


def _final_norm(x, g, row0, n_rows, tm=512):
    off = row0 // tm
    return pl.pallas_call(
        _final_norm_kernel,
        grid=(n_rows // tm,),
        in_specs=[pl.BlockSpec((tm, D_MODEL), lambda i: (i + off, 0)),
                  pl.BlockSpec((1, D_MODEL), lambda i: (0, 0))],
        out_specs=pl.BlockSpec((tm, D_MODEL), lambda i: (i, 0)),
        out_shape=jax.ShapeDtypeStruct((n_rows, D_MODEL), F32),
        compiler_params=_params(("parallel",)),
        name="final_norm",
    )(x, g.reshape(1, D_MODEL))


def _even_layer(x, mods_l, i, state_ret, st_prev, dfts, norm1_g, w_in_even, hy_short_w, hy_short_b, hy_f1_w, hy_f1_b,
                hy_freq1, hy_f2_w, hy_f2_b, hy_freq2, hy_f3_w, hy_bias, ret_decay, w_out_even):
    proj = _norm_project(x, mods_l, norm1_g, _weight_bf16(w_in_even, i))
    sw, sbias = hy_short_w[i], hy_short_b[i].reshape(1, 3 * HY_W)
    hb = hy_bias[i]
    y_hy = None
    for (row0, n_rows, L, nseq) in ((0, N_PROMPT, SEQ, 8), (N_PROMPT, N_SAMPLE, DEC_SEQ, 1)):
        s, d = _hyena_filter_taps(L, hy_f1_w[i], hy_f1_b[i], hy_freq1[i], hy_f2_w[i], hy_f2_b[i], hy_freq2[i], hy_f3_w[i])
        spectra = _filter_spectra(s, d, L, dfts[L])
        wv, bv = sw[:, 0:HY_W], sbias[:, 0:HY_W]
        w1, b1 = sw[:, HY_W:2 * HY_W], sbias[:, HY_W:2 * HY_W]
        w2, b2 = sw[:, 2 * HY_W:], sbias[:, 2 * HY_W:]
        z1 = _hyena_conv(proj, 0, proj, 1, row0, n_rows, L, nseq, spectra, 0, wv, bv, w1, b1, hb[0:1], True, dfts[L])
        y_hy = _hyena_conv(z1, 0, proj, 2, row0, n_rows, L, nseq, spectra, 1, wv, bv, w2, b2, hb[1:2], False, dfts[L],
                           full_out=True, out_prev=y_hy)
    dl = jnp.broadcast_to(ret_decay[i].astype(F32)[:, :, None], (2, RET_HEADS, LANES))
    y_ret, st = _retention(proj, dl, i, 0, BATCH, SEQ, st_prev=st_prev)
    y_ret = _retention(proj, dl, i, N_PROMPT, DEC_BATCH, DEC_SEQ, s0=state_ret, y_prev=y_ret)
    wo = _weight_bf16(w_out_even, i)
    x = _project_residual([y_hy, y_ret], [wo[:HY_W], wo[HY_W:]], x, mods_l, G1)
    return x, st


def _odd_layer(x, mods_l, l, i, cache_k, cache_v, kv_prev, norm1_g, w_in_odd, lam_q1, lam_k1, lam_q2, lam_k2, subln_g,
               w_out_odd):
    lam_init = 0.8 - 0.6 * math.exp(-0.3 * l)
    proj = _norm_project(x, mods_l, norm1_g, _weight_bf16(w_in_odd, i))
    lam_vec = jnp.pad(jnp.stack([lam_q1[i], lam_k1[i], lam_q2[i], lam_k2[i]]), ((0, 0), (0, LANES - DIFF_HEAD_DIM)))
    sg = subln_g[i].reshape(1, LANES)
    o_p, kc, vc = _diff_attention(proj, proj, proj, lam_vec, sg, lam_init, nb=BATCH, lq=SEQ, lk=SEQ, tq=SEQ,
                                  q_row0=0, q_col=0, kv_cols=(1, 2), batched_kv=False,
                                  cache_layer=i, cache_prev=kv_prev)
    qs, ks, vs = _sample_qkv(proj, cache_k, cache_v, i)
    o = _diff_attention(qs, ks, vs, lam_vec, sg, lam_init, nb=DEC_BATCH, lq=DEC_SEQ, lk=PAST_LEN + DEC_SEQ, tq=256,
                        q_row0=0, q_col=0, kv_cols=None, batched_kv=True, out_row0=N_PROMPT, out_prev=o_p)
    x = _project_residual([o], [_weight_bf16(w_out_odd, i)], x, mods_l, G1)
    return x, (kc, vc)


def _moe_layer(x, mods_l, l, norm2_g, w_router, wg, wu, wd):
    h, aff = _router(x, mods_l, norm2_g, w_router)
    gate_p, idx_p = lax.top_k(aff[:, :N_PROMPT], CAP_P)
    gate_s, idx_s = lax.top_k(aff[:, N_PROMPT:], CAP_S)
    idx = jnp.concatenate([idx_p, idx_s + N_PROMPT], axis=1)
    gate = jnp.concatenate([gate_p, gate_s], axis=1)
    flat = idx.reshape(-1)
    xs = _gather_rows(h.reshape(2 * N_TOK, PACK_W), jnp.concatenate([flat, flat + N_TOK]))
    xs = xs.reshape(2, N_EXPERTS, CAP_T, PACK_W)
    grp = jnp.where(idx < N_PROMPT, 0, 1 + (idx - N_PROMPT) // DEC_SEQ)
    gate_grp = jnp.where(grp[:, :, None] == jnp.arange(N_GROUPS)[None, None, :], gate[:, :, None], 0.0)
    out = _expert_ffn(xs, wg, wu, wd, gate_grp, mods_l, l)
    return x.at[idx.reshape(-1)].add(out.reshape(-1, D_MODEL))


def kernel(x_prompt, x_sample, state_ret, cache_k, cache_v, c, c_ctx, w_mod, b_mod, norm1_g, norm2_g, w_in_even, hy_short_w, hy_short_b, hy_f1_w, hy_f1_b, hy_freq1, hy_f2_w, hy_f2_b, hy_freq2, hy_f3_w, hy_bias, ret_decay, w_out_even, w_in_odd, lam_q1, lam_k1, lam_q2, lam_k2, subln_g, w_out_odd, moe_router, moe_wg, moe_wu, moe_wd, final_g):
    x = (x_prompt.reshape(N_PROMPT, D_MODEL), x_sample.reshape(N_SAMPLE, D_MODEL))
    mods = _modulation(c, c_ctx, w_mod, b_mod)
    dfts = {L: _dft_bf16(L) for L in (SEQ, DEC_SEQ)}
    state_ret = state_ret.astype(F32)
    st = kv = None
    for l in range(DEPTH):
        i = l // 2
        if l % 2 == 0:
            x, st = _even_layer(x, mods[l], i, state_ret, st, dfts, norm1_g[l], w_in_even, hy_short_w, hy_short_b,
                                hy_f1_w, hy_f1_b, hy_freq1, hy_f2_w, hy_f2_b, hy_freq2, hy_f3_w, hy_bias, ret_decay,
                                w_out_even)
        else:
            x, kv = _odd_layer(x, mods[l], l, i, cache_k, cache_v, kv, norm1_g[l], w_in_odd, lam_q1, lam_k1, lam_q2,
                               lam_k2, subln_g, w_out_odd)
        x = _moe_layer(x, mods[l], l, norm2_g[l], moe_router[l], moe_wg, moe_wu, moe_wd)
    y_prompt = _final_norm(x, final_g, 0, N_PROMPT).reshape(BATCH, SEQ, D_MODEL)
    y_sample = _final_norm(x, final_g, N_PROMPT, N_SAMPLE).reshape(DEC_BATCH, DEC_SEQ, D_MODEL)
    cache_shape = (BATCH, DEPTH // 2, SEQ, DIFF_HEADS, 2 * DIFF_HEAD_DIM)
    return (y_prompt, y_sample, st, kv[0].reshape(cache_shape), kv[1].reshape(cache_shape))
```

```python
import functools
import math

import numpy as np
import jax
import jax.numpy as jnp
from jax import lax
from jax.experimental import pallas as pl
from jax.experimental.pallas import tpu as pltpu
from jax.experimental.pallas import tpu_sc as plsc

F32 = jnp.float32
BF16 = jnp.bfloat16

D_MODEL = 1024
BATCH = 32
SEQ = 256
DEPTH = 4
DEC_BATCH = 2
DEC_SEQ = 2048
PAST_LEN = 256
GRID_W = 64
HY_W = 512
HY_EMB = 33
HY_BANDS = 16
HY_FF = 64
HY_TARGET = 1e-2
HY_FAST = 0.3
HY_SLOW = 1.5
RET_W = 512
RET_HEADS = 4
RET_HEAD_DIM = 128
RET_CHUNK = 128
DIFF_HEADS = 8
DIFF_HEAD_DIM = 64
ROPE_BASE = 10000.0
N_EXPERTS = 16
EC_FACTOR = 2
EXPERT_FF = 1024
EVEN_IN = 3 * HY_W + 4 * RET_W
EPS = 1e-6

N_PROMPT = BATCH * SEQ
N_SAMPLE = DEC_BATCH * DEC_SEQ
N_TOK = N_PROMPT + N_SAMPLE
N_GROUPS = 1 + DEC_BATCH
CAP_P = EC_FACTOR * N_PROMPT // N_EXPERTS
CAP_S = EC_FACTOR * N_SAMPLE // N_EXPERTS
CAP_T = CAP_P + CAP_S

LANES = 128
SUBLANES = 8
VMEM_LIMIT = 56 * 1024 * 1024

SH1, SC1, G1, SH2, SC2, G2 = range(6)


def _params(sem, vmem=VMEM_LIMIT):
    return pltpu.CompilerParams(dimension_semantics=sem, vmem_limit_bytes=vmem)


def _group_of_block(i, tm):
    pb = N_PROMPT // tm
    return jnp.where(i < pb, 0, 1 + (i - pb) // (DEC_SEQ // tm))


MOD_TN = 1024


MOD_UNROLL = 4


def _mod_kernel(cb_ref, w_ref, b_ref, o_ref, a_sc):
    nchunk = MOD_TN // LANES
    cv = cb_ref[...]
    a_sc[...] = cv * (1.0 / (1.0 + jnp.exp(-cv)))

    def body(kb, accs):
        accs = list(accs)
        for u in range(MOD_UNROLL):
            k0 = pl.multiple_of((kb * MOD_UNROLL + u) * SUBLANES, SUBLANES)
            a = [a_sc[r, pl.ds(k0, SUBLANES), :] for r in range(N_GROUPS)]
            for ci in range(nchunk):
                wv = w_ref[0, pl.ds(k0, SUBLANES), ci * LANES:(ci + 1) * LANES]
                for r in range(N_GROUPS):
                    accs[ci * N_GROUPS + r] = accs[ci * N_GROUPS + r] + wv * a[r]
        return tuple(accs)

    init = tuple(jnp.zeros((SUBLANES, LANES), F32) for _ in range(N_GROUPS * nchunk))
    accs = lax.fori_loop(0, D_MODEL // (SUBLANES * MOD_UNROLL), body, init)
    o_ref[...] = jnp.zeros(o_ref.shape, F32)
    for r in range(N_GROUPS):
        for ci in range(nchunk):
            row = jnp.sum(accs[ci * N_GROUPS + r], axis=0, keepdims=True)
            o_ref[0, r:r + 1, ci * LANES:(ci + 1) * LANES] = row + b_ref[0, :, ci * LANES:(ci + 1) * LANES]


def _modulation(c, c_ctx, w_mod, b_mod):
    cond = jnp.concatenate([c_ctx[None, :], c], axis=0)
    cb = jnp.broadcast_to(cond[:, :, None], (N_GROUPS, D_MODEL, LANES))
    out = pl.pallas_call(
        _mod_kernel,
        grid=(DEPTH, 6 * D_MODEL // MOD_TN),
        in_specs=[
            pl.BlockSpec((N_GROUPS, D_MODEL, LANES), lambda l, j: (0, 0, 0)),
            pl.BlockSpec((1, D_MODEL, MOD_TN), lambda l, j: (l, 0, j)),
            pl.BlockSpec((1, 1, MOD_TN), lambda l, j: (l, 0, j)),
        ],
        out_specs=pl.BlockSpec((1, SUBLANES, MOD_TN), lambda l, j: (l, 0, j)),
        out_shape=jax.ShapeDtypeStruct((DEPTH, SUBLANES, 6 * D_MODEL), F32),
        scratch_shapes=[pltpu.VMEM((N_GROUPS, D_MODEL, LANES), F32)],
        compiler_params=_params(("parallel", "parallel")),
        name="ada_mod",
    )(cb, w_mod, b_mod.reshape(DEPTH, 1, 6 * D_MODEL))
    return out[:, :N_GROUPS].reshape(DEPTH, N_GROUPS, 6, D_MODEL)


def _normed(x, g_ref, m_ref, shift, scale):
    ms = jnp.mean(x * x, axis=-1, keepdims=True)
    y = x * lax.rsqrt(ms + EPS) * g_ref[...]
    return y * (1.0 + m_ref[0, scale:scale + 1, :]) + m_ref[0, shift:shift + 1, :]


def _weight_bf16(w, i):
    return w[i].astype(BF16)


def _x_specs(x, tm):
    if not isinstance(x, tuple):
        return [x], [pl.BlockSpec((tm, D_MODEL), lambda i: (i, 0))]
    pb = N_PROMPT // tm
    return list(x), [pl.BlockSpec((tm, D_MODEL), lambda i: (jnp.minimum(i, pb - 1), 0)),
                     pl.BlockSpec((tm, D_MODEL), lambda i: (jnp.maximum(i - pb, 0), 0))]


def _x_block(x_refs, tm):
    if len(x_refs) == 1:
        return x_refs[0][...]
    return jnp.where(pl.program_id(0) < N_PROMPT // tm, x_refs[0][...], x_refs[1][...])


def _norm_mm_kernel(*refs, tn, tm):
    m_ref, g_ref, w_ref, o_ref = refs[-4:]
    h = _normed(_x_block(refs[:-4], tm), g_ref, m_ref, SH1, SC1).astype(BF16)
    for c0 in range(0, o_ref.shape[1], tn):
        o_ref[:, c0:c0 + tn] = jnp.dot(h, w_ref[:, c0:c0 + tn], preferred_element_type=F32)


def _norm_project(x, mods_l, g, w_bf16, tm=512, tn=512):
    nout = w_bf16.shape[1]
    x_args, x_specs = _x_specs(x, tm)
    return pl.pallas_call(
        functools.partial(_norm_mm_kernel, tn=tn, tm=tm),
        grid=(N_TOK // tm,),
        in_specs=x_specs + [
            pl.BlockSpec((1, 6, D_MODEL), lambda i: (_group_of_block(i, tm), 0, 0)),
            pl.BlockSpec((1, D_MODEL), lambda i: (0, 0)),
            pl.BlockSpec((D_MODEL, nout), lambda i: (0, 0)),
        ],
        out_specs=pl.BlockSpec((tm, nout), lambda i: (i, 0)),
        out_shape=jax.ShapeDtypeStruct((N_TOK, nout), F32),
        compiler_params=_params(("parallel",)),
        name="norm_project",
    )(*x_args, mods_l, g.reshape(1, D_MODEL), w_bf16)


def _proj_res_kernel(*refs, n_in, gate, tm):
    a_refs = refs[:n_in]
    w_refs = refs[n_in:2 * n_in]
    x_refs = refs[2 * n_in:-2]
    m_ref, o_ref = refs[-2:]
    acc = None
    for a_ref, w_ref in zip(a_refs, w_refs):
        t = jnp.dot(a_ref[...].astype(BF16), w_ref[...], preferred_element_type=F32)
        acc = t if acc is None else acc + t
    o_ref[...] = _x_block(x_refs, tm) + m_ref[0, gate:gate + 1, :] * acc


def _project_residual(acts, ws_bf16, x, mods_l, gate, tm=512):
    n_in = len(acts)
    x_args, x_specs = _x_specs(x, tm)
    in_specs = [pl.BlockSpec((tm, a.shape[1]), lambda i: (i, 0)) for a in acts]
    in_specs += [pl.BlockSpec(w.shape, lambda i: (0, 0)) for w in ws_bf16]
    in_specs += x_specs + [pl.BlockSpec((1, 6, D_MODEL), lambda i: (_group_of_block(i, tm), 0, 0))]
    return pl.pallas_call(
        functools.partial(_proj_res_kernel, n_in=n_in, gate=gate, tm=tm),
        grid=(N_TOK // tm,),
        in_specs=in_specs,
        out_specs=pl.BlockSpec((tm, D_MODEL), lambda i: (i, 0)),
        out_shape=jax.ShapeDtypeStruct((N_TOK, D_MODEL), F32),
        input_output_aliases={2 * n_in: 0} if len(x_args) == 1 else {},
        compiler_params=_params(("parallel",)),
        name="project_residual",
    )(*acts, *ws_bf16, *x_args, mods_l)


@functools.lru_cache(maxsize=None)
def _dft_mats(L):
    n = 2 * L
    ft = (np.arange(L, dtype=np.int64)[:, None] * np.arange(L, dtype=np.int64)[None, :]) % n
    ang = ft.astype(np.float64) * (2.0 * np.pi / n)
    return np.cos(ang).astype(np.float32), np.sin(ang).astype(np.float32)


def _dft_bf16(L):
    c, s = _dft_mats(L)
    return jnp.asarray(c).astype(BF16), jnp.asarray(s).astype(BF16)


def _alt_sign(shape, row0):
    t = lax.broadcasted_iota(jnp.int32, shape, 0) + row0
    return (1 - 2 * (t & 1)).astype(F32)


def _filter_dft_kernel(s_ref, d_ref, c_ref, sn_ref, ka_ref, ki_ref, kn_ref, *, L, fb):
    f0 = pl.program_id(0) * fb
    n = 2.0 * L
    s = s_ref[...]
    r = jnp.dot(c_ref[...], s, preferred_element_type=F32)
    im = jnp.dot(sn_ref[...], d_ref[...], preferred_element_type=F32)
    fidx = lax.broadcasted_iota(jnp.int32, r.shape, 0) + f0
    scale = jnp.where(fidx == 0, 1.0 / n, 2.0 / n)
    ka_ref[...] = r * scale
    ki_ref[...] = im * (2.0 / n)
    nyq = jnp.sum(s.astype(F32) * _alt_sign(s.shape, 0), axis=0, keepdims=True) * (1.0 / n)
    kn_ref[...] = jnp.broadcast_to(nyq, kn_ref.shape)


def _filter_spectra(s, d, L, dft):
    fb = min(L, 512)
    cmat, smat = dft
    w = 2 * HY_W
    return pl.pallas_call(
        functools.partial(_filter_dft_kernel, L=L, fb=fb),
        grid=(L // fb,),
        in_specs=[
            pl.BlockSpec((L, w), lambda f: (0, 0)),
            pl.BlockSpec((L, w), lambda f: (0, 0)),
            pl.BlockSpec((fb, L), lambda f: (f, 0)),
            pl.BlockSpec((fb, L), lambda f: (f, 0)),
        ],
        out_specs=[
            pl.BlockSpec((fb, w), lambda f: (f, 0)),
            pl.BlockSpec((fb, w), lambda f: (f, 0)),
            pl.BlockSpec((SUBLANES, w), lambda f: (0, 0)),
        ],
        out_shape=[
            jax.ShapeDtypeStruct((L, w), F32),
            jax.ShapeDtypeStruct((L, w), F32),
            jax.ShapeDtypeStruct((SUBLANES, w), F32),
        ],
        compiler_params=_params(("arbitrary",)),
        name="hyena_filter_dft",
    )(s.astype(BF16), d.astype(BF16), cmat, smat)


HY_TILE = 256


def _short_conv_tile(ref, r0, L, w_ref, b_ref):
    t = HY_TILE
    cur = ref[r0:r0 + t, :]
    rid = lax.broadcasted_iota(jnp.int32, cur.shape, 0)
    if r0 % L == 0:
        prev = jnp.where(rid == 0, 0.0, pltpu.roll(cur, 1, axis=0))
    else:
        prev = ref[r0 - 1:r0 - 1 + t, :]
    if (r0 + t) % L == 0:
        nxt = jnp.where(rid == t - 1, 0.0, pltpu.roll(cur, t - 1, axis=0))
    else:
        nxt = ref[r0 + 1:r0 + 1 + t, :]
    return prev * w_ref[0:1, :] + cur * w_ref[1:2, :] + nxt * w_ref[2:3, :] + b_ref[...]


def _hyena_conv_kernel(*refs, L, nseq, conv_a):
    (a_ref, x_ref, cr_ref, sr_ref, cc_ref, sc_ref, ka_ref, ki_ref, kn_ref,
     wa_ref, ba_ref, wx_ref, bx_ref, hb_ref) = refs[:14]
    o_ref, z_sc, acc_sc = refs[-3:]
    f = pl.program_id(1)
    nf = pl.num_programs(1)
    rows = nseq * L

    @pl.when(f == 0)
    def _():
        for q in range(nseq):
            nyq = jnp.zeros((1, HY_W), F32)
            for r0 in range(q * L, (q + 1) * L, HY_TILE):
                if conv_a:
                    zt = _short_conv_tile(a_ref, r0, L, wa_ref, ba_ref)
                else:
                    zt = a_ref[r0:r0 + HY_TILE, :]
                z_sc[r0:r0 + HY_TILE, :] = zt
                nyq = nyq + jnp.sum(zt * _alt_sign(zt.shape, r0), axis=0, keepdims=True)
            nyq = nyq * kn_ref[0:1, :]
            for r0 in range(q * L, (q + 1) * L, HY_TILE):
                acc_sc[r0:r0 + HY_TILE, :] = _alt_sign((HY_TILE, HY_W), r0) * nyq

    ka = ka_ref[...]
    ki = ki_ref[...]
    for q in range(nseq):
        z = z_sc[q * L:(q + 1) * L, :].astype(BF16)
        a = jnp.dot(cr_ref[...], z, preferred_element_type=F32)
        b = jnp.dot(sr_ref[...], z, preferred_element_type=F32)
        p = (a * ka + b * ki).astype(BF16)
        qq = (b * ka - a * ki).astype(BF16)
        acc_sc[q * L:(q + 1) * L, :] += (jnp.dot(cc_ref[...], p, preferred_element_type=F32)
                                         + jnp.dot(sc_ref[...], qq, preferred_element_type=F32))

    @pl.when(f == nf - 1)
    def _():
        for r0 in range(0, rows, HY_TILE):
            y = acc_sc[r0:r0 + HY_TILE, :] + z_sc[r0:r0 + HY_TILE, :] * hb_ref[...]
            o_ref[r0:r0 + HY_TILE, :] = (y * _short_conv_tile(x_ref, r0, L, wx_ref, bx_ref)).astype(o_ref.dtype)


def _hyena_conv(a, a_col, x, x_col, row0, n_rows, L, nseq, spectra, filt, wa, ba, wx, bx, hbias, conv_a, dft,
                full_out=False, out_prev=None):
    fb = min(L, 256)
    cmat, smat = dft
    ka, ki, kn = spectra
    rb = nseq * L
    a_off = row0 // rb if a.shape[0] != n_rows else 0
    x_off = row0 // rb
    o_off = row0 // rb if full_out else 0
    extra_specs, extra_args, aliases = [], [], {}
    if out_prev is not None:
        extra_specs, extra_args, aliases = [pl.BlockSpec(memory_space=pl.ANY)], [out_prev], {14: 0}
    return pl.pallas_call(
        functools.partial(_hyena_conv_kernel, L=L, nseq=nseq, conv_a=conv_a),
        grid=(n_rows // rb, L // fb),
        input_output_aliases=aliases,
        in_specs=extra_specs[:0] + [
            pl.BlockSpec((rb, HY_W), lambda i, f: (i + a_off, a_col)),
            pl.BlockSpec((rb, HY_W), lambda i, f: (i + x_off, x_col)),
            pl.BlockSpec((fb, L), lambda i, f: (f, 0)),
            pl.BlockSpec((fb, L), lambda i, f: (f, 0)),
            pl.BlockSpec((L, fb), lambda i, f: (0, f)),
            pl.BlockSpec((L, fb), lambda i, f: (0, f)),
            pl.BlockSpec((fb, HY_W), lambda i, f: (f, filt)),
            pl.BlockSpec((fb, HY_W), lambda i, f: (f, filt)),
            pl.BlockSpec((SUBLANES, HY_W), lambda i, f: (0, filt)),
            pl.BlockSpec((3, HY_W), lambda i, f: (0, 0)),
            pl.BlockSpec((1, HY_W), lambda i, f: (0, 0)),
            pl.BlockSpec((3, HY_W), lambda i, f: (0, 0)),
            pl.BlockSpec((1, HY_W), lambda i, f: (0, 0)),
            pl.BlockSpec((1, HY_W), lambda i, f: (0, 0)),
        ] + extra_specs,
        out_specs=pl.BlockSpec((rb, HY_W), lambda i, f: (i + o_off, 0)),
        out_shape=jax.ShapeDtypeStruct((N_TOK, HY_W), BF16) if full_out else jax.ShapeDtypeStruct((n_rows, HY_W), F32),
        scratch_shapes=[pltpu.VMEM((rb, HY_W), F32), pltpu.VMEM((rb, HY_W), F32)],
        compiler_params=_params(("parallel", "arbitrary")),
        name="hyena_conv",
    )(a, x, cmat, smat, cmat, smat, ka, ki, kn, wa, ba, wx, bx, hbias, *extra_args)


def _hyena_filter_taps(L, f1w, f1b, fr1, f2w, f2b, fr2, f3w):
    hp = lax.Precision.HIGHEST
    pos = jnp.arange(L, dtype=F32)
    t = pos / (L - 1)
    w = 2.0 * math.pi * pos / L
    f = jnp.linspace(1e-4, HY_BANDS - 1, HY_BANDS, dtype=F32)
    wf = w[:, None] * f[None, :]
    feat = jnp.concatenate([t[:, None], jnp.cos(wf), -jnp.sin(wf)], axis=-1)
    h = jnp.sin(fr1 * (jnp.dot(feat, f1w, precision=hp) + f1b))
    h = jnp.sin(fr2 * (jnp.dot(h, f2w, precision=hp) + f2b))
    h = jnp.dot(h, f3w, precision=hp).astype(F32)
    deltas = jnp.linspace(math.log(HY_TARGET) / HY_SLOW, math.log(HY_TARGET) / HY_FAST, HY_W, dtype=F32)
    window = jnp.exp(-t[:, None] * jnp.abs(deltas)[None, :])
    w = HY_W
    parts = [h[:, k * w:(k + 1) * w] * window for k in range(4)]
    colsum = [jnp.sum(jnp.abs(p), axis=0, keepdims=True) for p in parts]
    den = [colsum[0] + colsum[1] + EPS, colsum[2] + colsum[3] + EPS]
    fwd = jnp.concatenate([parts[0] / den[0], parts[2] / den[1]], axis=1)
    bwd = jnp.concatenate([parts[1] / den[0], parts[3] / den[1]], axis=1)
    bwd = jnp.where(pos[:, None] == 0, 0.0, bwd)
    return fwd + bwd, bwd - fwd


def _dot_t0(a, b):
    return lax.dot_general(a, b, (((0,), (0,)), ((), ())), preferred_element_type=F32)


def _dot_t1(a, b):
    return lax.dot_general(a, b, (((1,), (1,)), ((), ())), preferred_element_type=F32)


def _retention_kernel(*refs, L, has_s0, has_prev):
    refs = list(refs)
    q_ref, k_ref, v_ref, g_ref, dl_ref = refs[:5]
    pos = 5
    s0_ref = st_ref = None
    if has_s0:
        s0_ref = refs[pos]
        pos += 1
    if has_prev:
        pos += 1
    y_ref = refs[pos]
    pos += 1
    if not has_s0:
        st_ref = refs[pos]
        pos += 1
    sb_sc, sf_cur, sb_cur = refs[pos:]
    c = RET_CHUNK
    nc = L // c
    kscale = RET_HEAD_DIM ** -0.5
    ri = lax.broadcasted_iota(jnp.int32, (c, c), 0).astype(F32)
    ci = lax.broadcasted_iota(jnp.int32, (c, c), 1).astype(F32)
    diff = ri - ci
    dec = []
    for h in range(RET_HEADS):
        xf = dl_ref[0, h:h + 1, :]
        xb = dl_ref[1, h:h + 1, :]
        lgf = jnp.minimum(xf, 0.0) - jnp.log1p(jnp.exp(-jnp.abs(xf)))
        lgb = jnp.minimum(xb, 0.0) - jnp.log1p(jnp.exp(-jnp.abs(xb)))
        dec.append(dict(
            mask=(jnp.where(diff >= 0, jnp.exp(lgf * jnp.maximum(diff, 0.0)), 0.0)
                  + jnp.where(diff <= 0, jnp.exp(lgb * jnp.maximum(-diff, 0.0)), 0.0)),
            qdec_f=jnp.exp(lgf * (ri + 1.0)), kdec_f=jnp.exp(lgf * (c - 1.0 - ri)),
            qdec_b=jnp.exp(lgb * (c - ri)), kdec_b=jnp.exp(lgb * ri),
            cd_f=jnp.exp(lgf * c), cd_b=jnp.exp(lgb * c)))
        if has_s0:
            sf_cur[h] = s0_ref[0, 0, 0, h]
            sb_cur[h] = s0_ref[0, 0, 1, h]
        else:
            sf_cur[h] = jnp.zeros((RET_HEAD_DIM, RET_HEAD_DIM), F32)
            sb_cur[h] = jnp.zeros((RET_HEAD_DIM, RET_HEAD_DIM), F32)

    def bwd_body(i, carry):
        j = nc - 1 - i
        r0 = pl.multiple_of(j * c, c)
        for h in range(RET_HEADS):
            hs = slice(h * RET_HEAD_DIM, (h + 1) * RET_HEAD_DIM)
            sb = sb_cur[h]
            sb_sc[h * nc + j] = sb
            kc = k_ref[pl.ds(r0, c), hs] * kscale
            vc = v_ref[pl.ds(r0, c), hs]
            sb_cur[h] = sb * dec[h]["cd_b"] + _dot_t0((kc * dec[h]["kdec_b"]).astype(BF16), vc.astype(BF16))
        return carry

    lax.fori_loop(0, nc, bwd_body, 0)

    def fwd_body(j, carry):
        r0 = pl.multiple_of(j * c, c)
        for h in range(RET_HEADS):
            hs = slice(h * RET_HEAD_DIM, (h + 1) * RET_HEAD_DIM)
            dh = dec[h]
            sf = sf_cur[h]
            qc = q_ref[pl.ds(r0, c), hs]
            kc = k_ref[pl.ds(r0, c), hs] * kscale
            vc = v_ref[pl.ds(r0, c), hs].astype(BF16)
            scores = _dot_t1(qc.astype(BF16), kc.astype(BF16)) * dh["mask"]
            o = jnp.dot(scores.astype(BF16), vc, preferred_element_type=F32)
            o = o + jnp.dot((qc * dh["qdec_f"]).astype(BF16), sf.astype(BF16), preferred_element_type=F32)
            o = o + jnp.dot((qc * dh["qdec_b"]).astype(BF16), sb_sc[h * nc + j].astype(BF16),
                            preferred_element_type=F32)
            mu = jnp.mean(o, axis=-1, keepdims=True)
            var = jnp.mean(jnp.square(o - mu), axis=-1, keepdims=True)
            on = (o - mu) * lax.rsqrt(var + EPS)
            gc = g_ref[pl.ds(r0, c), hs]
            y_ref[pl.ds(r0, c), hs] = (gc * (1.0 / (1.0 + jnp.exp(-gc))) * on).astype(y_ref.dtype)
            sf_cur[h] = sf * dh["cd_f"] + _dot_t0((kc * dh["kdec_f"]).astype(BF16), vc)
        return carry

    lax.fori_loop(0, nc, fwd_body, 0)
    if st_ref is not None:
        for h in range(RET_HEADS):
            st_ref[0, 0, 0, h] = sf_cur[h]
            st_ref[0, 0, 1, h] = sb_cur[h]


def _retention(proj, dl, layer_i, row0, nseq, L, s0=None, st_prev=None, y_prev=None):
    off = row0 // L
    has_s0 = s0 is not None
    has_prev = (st_prev is not None) or (y_prev is not None)
    assert not (st_prev is not None and y_prev is not None)
    nc = L // RET_CHUNK
    n_ret = (DEPTH + 1) // 2
    col = lambda j: pl.BlockSpec((L, RET_W), lambda b: (b + off, 3 + j))
    in_specs = [col(0), col(1), col(2), col(3),
                pl.BlockSpec((2, RET_HEADS, LANES), lambda b: (0, 0, 0))]
    args = [proj, proj, proj, proj, dl]
    y_spec = pl.BlockSpec((L, RET_W), lambda b: (b + off, 0))
    y_shape = jax.ShapeDtypeStruct((N_TOK, RET_W), BF16)
    st_block = (1, 1, 2, RET_HEADS, RET_HEAD_DIM, RET_HEAD_DIM)
    st_spec = pl.BlockSpec(st_block, lambda b: (b, layer_i, 0, 0, 0, 0))
    aliases = {}
    if has_s0:
        in_specs.append(st_spec)
        args.append(s0)
        out_specs, out_shape = y_spec, y_shape
    else:
        out_specs = [y_spec, st_spec]
        out_shape = [y_shape, jax.ShapeDtypeStruct((nseq, n_ret) + st_block[2:], F32)]
    if has_prev:
        aliases = {len(args): 1 if st_prev is not None else 0}
        in_specs.append(pl.BlockSpec(memory_space=pl.ANY))
        args.append(st_prev if st_prev is not None else y_prev)
    state = pltpu.VMEM((RET_HEADS, RET_HEAD_DIM, RET_HEAD_DIM), F32)
    return pl.pallas_call(
        functools.partial(_retention_kernel, L=L, has_s0=has_s0, has_prev=has_prev),
        grid=(nseq,),
        in_specs=in_specs,
        out_specs=out_specs,
        out_shape=out_shape,
        input_output_aliases=aliases,
        scratch_shapes=[pltpu.VMEM((RET_HEADS * nc, RET_HEAD_DIM, RET_HEAD_DIM), F32), state, state],
        compiler_params=_params(("parallel",)),
        name="retention",
    )(*args)


@functools.lru_cache(maxsize=None)
def _rope_tables():
    L = DEC_SEQ
    rows = L // GRID_W
    row = np.repeat(np.arange(rows, dtype=np.float64), GRID_W)
    col = np.tile(np.arange(GRID_W, dtype=np.float64), rows)
    quarter = DIFF_HEAD_DIM // 4
    freqs = ROPE_BASE ** (-np.arange(quarter, dtype=np.float64) / quarter)
    j = np.arange(LANES)
    pos = np.where(((j % DIFF_HEAD_DIM) < DIFF_HEAD_DIM // 2)[None, :], row[:, None], col[:, None])
    ang = pos * freqs[j % quarter][None, :]
    cos = np.cos(ang).astype(np.float32)
    sin = np.sin(ang).astype(np.float32)
    first = ((j % (2 * quarter)) < quarter)[None, :]
    sin_a = np.where(first, -sin, 0.0).astype(np.float32)
    sin_b = np.where(first, 0.0, sin).astype(np.float32)
    return jnp.asarray(cos), jnp.asarray(sin_a), jnp.asarray(sin_b)


def _rope_head(x, cos, sin_a, sin_b):
    quarter = DIFF_HEAD_DIM // 4
    up = pltpu.roll(x, LANES - quarter, axis=1)
    dn = pltpu.roll(x, quarter, axis=1)
    return x * cos + up * sin_a + dn * sin_b


def _kv_prep_kernel(q_ref, k_ref, v_ref, cos_ref, sa_ref, sb_ref, qo_ref, ko_ref, vo_ref):
    cos, sa, sb = cos_ref[...], sa_ref[...], sb_ref[...]
    for h in range(DIFF_HEADS):
        hs = slice(h * LANES, (h + 1) * LANES)
        qo_ref[:, hs] = _rope_head(q_ref[:, hs], cos, sa, sb).astype(BF16)
        ko_ref[0, :, hs] = _rope_head(k_ref[:, hs], cos, sa, sb).astype(BF16)
    vo_ref[0] = v_ref[...].astype(BF16)


def _cache_copy_kernel(ck_ref, cv_ref, k_in, v_in, ko_ref, vo_ref):
    del k_in, v_in
    ko_ref[0] = ck_ref[0, 0].astype(BF16)
    vo_ref[0] = cv_ref[0, 0].astype(BF16)


def _sample_qkv(proj, cache_k, cache_v, layer_i, tm=256):
    cos, sa, sb = _rope_tables()
    lk = PAST_LEN + DEC_SEQ
    pblk = N_PROMPT // tm
    nblk = DEC_SEQ // tm
    cblk = PAST_LEN // tm
    tab = pl.BlockSpec((tm, LANES), lambda b, i: (i, 0))
    q, k, v = pl.pallas_call(
        _kv_prep_kernel,
        grid=(DEC_BATCH, nblk),
        in_specs=[
            pl.BlockSpec((tm, D_MODEL), lambda b, i: (pblk + b * nblk + i, 0)),
            pl.BlockSpec((tm, D_MODEL), lambda b, i: (pblk + b * nblk + i, 1)),
            pl.BlockSpec((tm, D_MODEL), lambda b, i: (pblk + b * nblk + i, 2)),
            tab, tab, tab,
        ],
        out_specs=[
            pl.BlockSpec((tm, D_MODEL), lambda b, i: (b * nblk + i, 0)),
            pl.BlockSpec((1, tm, D_MODEL), lambda b, i: (b, cblk + i, 0)),
            pl.BlockSpec((1, tm, D_MODEL), lambda b, i: (b, cblk + i, 0)),
        ],
        out_shape=[
            jax.ShapeDtypeStruct((N_SAMPLE, D_MODEL), BF16),
            jax.ShapeDtypeStruct((DEC_BATCH, lk, D_MODEL), BF16),
            jax.ShapeDtypeStruct((DEC_BATCH, lk, D_MODEL), BF16),
        ],
        compiler_params=_params(("parallel", "parallel")),
        name="rope_qkv",
    )(proj, proj, proj, cos, sa, sb)
    n_att = DEPTH // 2
    ck = cache_k.reshape(DEC_BATCH, n_att, PAST_LEN, D_MODEL)
    cv = cache_v.reshape(DEC_BATCH, n_att, PAST_LEN, D_MODEL)
    k, v = pl.pallas_call(
        _cache_copy_kernel,
        grid=(DEC_BATCH,),
        in_specs=[
            pl.BlockSpec((1, 1, PAST_LEN, D_MODEL), lambda b: (b, layer_i, 0, 0)),
            pl.BlockSpec((1, 1, PAST_LEN, D_MODEL), lambda b: (b, layer_i, 0, 0)),
            pl.BlockSpec(memory_space=pl.ANY),
            pl.BlockSpec(memory_space=pl.ANY),
        ],
        out_specs=[
            pl.BlockSpec((1, PAST_LEN, D_MODEL), lambda b: (b, 0, 0)),
            pl.BlockSpec((1, PAST_LEN, D_MODEL), lambda b: (b, 0, 0)),
        ],
        out_shape=[
            jax.ShapeDtypeStruct((DEC_BATCH, lk, D_MODEL), BF16),
            jax.ShapeDtypeStruct((DEC_BATCH, lk, D_MODEL), BF16),
        ],
        input_output_aliases={2: 0, 3: 1},
        compiler_params=_params(("parallel",)),
        name="cache_prepend",
    )(ck, cv, k, v)
    return q, k, v


def _diff_attn_kernel(*refs, lam_init, batched_kv, cache_out):
    q_ref, k_ref, v_ref, lam_ref, sg_ref = refs[:5]
    if cache_out:
        o_ref, kc_ref, vc_ref = refs[-3:]
        kc_ref[0, 0] = k_ref[...]
        vc_ref[0, 0] = v_ref[...]
    else:
        o_ref = refs[-1]
    lv = lam_ref[...]
    lam = (jnp.exp(jnp.sum(lv[0:1] * lv[1:2], axis=-1, keepdims=True))
           - jnp.exp(jnp.sum(lv[2:3] * lv[3:4], axis=-1, keepdims=True)) + lam_init)
    lane = lax.broadcasted_iota(jnp.int32, (1, LANES), 1)
    m1 = (lane < DIFF_HEAD_DIM).astype(F32)
    m2 = 1.0 - m1
    scale = DIFF_HEAD_DIM ** -0.5
    for h in range(DIFF_HEADS):
        hs = slice(h * LANES, (h + 1) * LANES)
        q = q_ref[:, hs].astype(F32) * scale
        if batched_kv:
            k = k_ref[0, :, hs].astype(BF16)
            v = v_ref[0, :, hs].astype(BF16)
        else:
            k = k_ref[:, hs].astype(BF16)
            v = v_ref[:, hs].astype(BF16)
        v_ext = jnp.concatenate([v, jnp.ones_like(v)], axis=1)
        outs = []
        for m in (m1, m2):
            s = _dot_t1((q * m).astype(BF16), k)
            s = s - jnp.max(s, axis=-1, keepdims=True)
            pv = jnp.dot(jnp.exp(s).astype(BF16), v_ext, preferred_element_type=F32)
            outs.append(pv[:, :LANES] / pv[:, LANES:])
        o = outs[0] - lam * outs[1]
        ms = jnp.mean(o * o, axis=-1, keepdims=True)
        o_ref[:, hs] = (o * lax.rsqrt(ms + EPS) * sg_ref[...] * (1.0 - lam_init)).astype(o_ref.dtype)


def _diff_attention(q, k, v, lam_vec, subln, lam_init, *, nb, lq, lk, tq, q_row0, q_col, kv_cols, batched_kv,
                    out_row0=0, out_prev=None, cache_layer=None, cache_prev=None):
    nq = lq // tq
    qoff = q_row0 // tq
    ooff = out_row0 // tq
    cache_out = cache_layer is not None
    extra_specs, extra_args, aliases = [], [], {}
    if out_prev is not None:
        extra_specs, extra_args, aliases = [pl.BlockSpec(memory_space=pl.ANY)], [out_prev], {5: 0}
    o_spec = pl.BlockSpec((tq, D_MODEL), lambda b, i: (ooff + b * nq + i, 0))
    o_shape = jax.ShapeDtypeStruct((N_TOK, D_MODEL), BF16)
    if cache_out:
        assert not batched_kv and nq == 1 and out_prev is None
        if cache_prev is not None:
            extra_specs = [pl.BlockSpec(memory_space=pl.ANY)] * 2
            extra_args = list(cache_prev)
            aliases = {5: 1, 6: 2}
        c_spec = pl.BlockSpec((1, 1, lk, D_MODEL), lambda b, i: (b, cache_layer, 0, 0))
        c_shape = jax.ShapeDtypeStruct((nb, DEPTH // 2, lk, D_MODEL), F32)
        o_spec, o_shape = [o_spec, c_spec, c_spec], [o_shape, c_shape, c_shape]
    q_spec = pl.BlockSpec((tq, D_MODEL), lambda b, i: (qoff + b * nq + i, q_col))
    if batched_kv:
        k_spec = pl.BlockSpec((1, lk, D_MODEL), lambda b, i: (b, 0, 0))
        v_spec = k_spec
    else:
        k_spec = pl.BlockSpec((lk, D_MODEL), lambda b, i: (b, kv_cols[0]))
        v_spec = pl.BlockSpec((lk, D_MODEL), lambda b, i: (b, kv_cols[1]))
    return pl.pallas_call(
        functools.partial(_diff_attn_kernel, lam_init=lam_init, batched_kv=batched_kv, cache_out=cache_out),
        grid=(nb, nq),
        in_specs=[q_spec, k_spec, v_spec,
                  pl.BlockSpec((4, LANES), lambda b, i: (0, 0)),
                  pl.BlockSpec((1, LANES), lambda b, i: (0, 0))] + extra_specs,
        out_specs=o_spec,
        out_shape=o_shape,
        input_output_aliases=aliases,
        compiler_params=_params(("parallel", "arbitrary")),
        name="diff_attention",
    )(q, k, v, lam_vec, subln, *extra_args)


PACK_W = D_MODEL // 4


def _router_kernel(x_ref, m_ref, g_ref, wr_ref, h_ref, aff_ref):
    h = _normed(x_ref[...], g_ref, m_ref, SH2, SC2)
    hb = h.astype(BF16)
    hf = hb.astype(F32)
    bits = lax.bitcast_convert_type(hf, jnp.int32)
    for p in range(2):
        lo = lax.shift_right_logical(bits[:, p * PACK_W:(p + 1) * PACK_W], 16)
        hi = bits[:, (2 + p) * PACK_W:(3 + p) * PACK_W] & jnp.int32(-65536)
        h_ref[p] = hi | lo
    hl = (h - hf).astype(BF16)
    wr = wr_ref[...]
    wh = wr.astype(BF16)
    wl = (wr - wh.astype(F32)).astype(BF16)
    logits = (jnp.dot(hb, wh, preferred_element_type=F32) + jnp.dot(hl, wh, preferred_element_type=F32)
              + jnp.dot(hb, wl, preferred_element_type=F32))
    lt = jnp.transpose(logits)[0:N_EXPERTS, :]
    lt = lt - jnp.max(lt, axis=0, keepdims=True)
    e = jnp.exp(lt)
    aff_ref[...] = e / jnp.sum(e, axis=0, keepdims=True)


def _router(x, mods_l, g, w_router, tm=512):
    wr = jnp.pad(w_router, ((0, 0), (0, LANES - N_EXPERTS)))
    return pl.pallas_call(
        _router_kernel,
        grid=(N_TOK // tm,),
        in_specs=[
            pl.BlockSpec((tm, D_MODEL), lambda i: (i, 0)),
            pl.BlockSpec((1, 6, D_MODEL), lambda i: (_group_of_block(i, tm), 0, 0)),
            pl.BlockSpec((1, D_MODEL), lambda i: (0, 0)),
            pl.BlockSpec((D_MODEL, LANES), lambda i: (0, 0)),
        ],
        out_specs=[
            pl.BlockSpec((2, tm, PACK_W), lambda i: (0, i, 0)),
            pl.BlockSpec((N_EXPERTS, tm), lambda i: (0, i)),
        ],
        out_shape=[
            jax.ShapeDtypeStruct((2, N_TOK, PACK_W), jnp.int32),
            jax.ShapeDtypeStruct((N_EXPERTS, N_TOK), F32),
        ],
        compiler_params=_params(("parallel",)),
        name="norm_router",
    )(x, mods_l, g.reshape(1, D_MODEL), wr)


SC_WINDOW = 128


def _gather_rows(table, idx):
    n = idx.shape[0]
    mesh = plsc.VectorSubcoreMesh(core_axis_name="core", subcore_axis_name="subcore")

    @pl.kernel(out_type=jax.ShapeDtypeStruct((n, PACK_W), table.dtype), mesh=mesh, scratch_types=[])
    def gather_kernel(t_hbm, i_hbm, o_hbm):
        def body(i_vmem, o_vmem):
            pltpu.sync_copy(t_hbm.at[i_vmem.at[0]], o_vmem)

        pltpu.emit_pipeline(
            body,
            grid=(n // SC_WINDOW,),
            in_specs=[pl.BlockSpec((1, SC_WINDOW), index_map=lambda i: (0, i))],
            out_specs=[pl.BlockSpec((SC_WINDOW, PACK_W), index_map=lambda i: (i, 0))],
            core_axis_name=("core", "subcore"),
            dimension_semantics=(pltpu.PARALLEL,),
        )(i_hbm, o_hbm)

    return gather_kernel(table, idx.reshape(1, n))


FFN_TF = 512
FFN_TR = 1536


def _unpack_rows(pa, pb):
    def lo(w):
        return lax.bitcast_convert_type(lax.shift_left(w, 16), F32).astype(BF16)

    def hi(w):
        return lax.bitcast_convert_type(w & jnp.int32(-65536), F32).astype(BF16)

    return jnp.concatenate([lo(pa), lo(pb), hi(pa), hi(pb)], axis=1)


def _expert_ffn_kernel(xs_ref, wg_ref, wu_ref, wd_ref, gate_ref, m_ref, o_ref, xs_sc):
    f = pl.program_id(1)

    @pl.when(f == 0)
    def _():
        for r0 in range(0, CAP_T, FFN_TR):
            xs_sc[r0:r0 + FFN_TR, :] = _unpack_rows(xs_ref[0, 0, r0:r0 + FFN_TR, :], xs_ref[1, 0, r0:r0 + FFN_TR, :])

    wg = wg_ref[0].astype(BF16)
    wu = wu_ref[0].astype(BF16)
    wd = wd_ref[0].astype(BF16)
    for r0 in range(0, CAP_T, FFN_TR):
        rs = slice(r0, r0 + FFN_TR)
        xs = xs_sc[rs, :]
        a = jnp.dot(xs, wg, preferred_element_type=F32)
        u = jnp.dot(xs, wu, preferred_element_type=F32)
        hid = (a * (1.0 / (1.0 + jnp.exp(-a))) * u).astype(BF16)
        y = jnp.dot(hid, wd, preferred_element_type=F32)

        @pl.when(f == 0)
        def _():
            o_ref[0, rs, :] = y

        @pl.when(jnp.logical_and(f > 0, f < pl.num_programs(1) - 1))
        def _():
            o_ref[0, rs, :] += y

        @pl.when(f == pl.num_programs(1) - 1)
        def _():
            gg = gate_ref[0, rs, :]
            scale = gg[:, 0:1] * m_ref[0, G2:G2 + 1, :]
            for g in range(1, N_GROUPS):
                scale = scale + gg[:, g:g + 1] * m_ref[g, G2:G2 + 1, :]
            o_ref[0, rs, :] = (o_ref[0, rs, :] + y) * scale


def _expert_ffn(xs, wg, wu, wd, gate, mods_l, l):
    return pl.pallas_call(
        _expert_ffn_kernel,
        grid=(N_EXPERTS, EXPERT_FF // FFN_TF),
        in_specs=[
            pl.BlockSpec((2, 1, CAP_T, PACK_W), lambda e, f: (0, e, 0, 0)),
            pl.BlockSpec((None, 1, D_MODEL, FFN_TF), lambda e, f: (l, e, 0, f)),
            pl.BlockSpec((None, 1, D_MODEL, FFN_TF), lambda e, f: (l, e, 0, f)),
            pl.BlockSpec((None, 1, FFN_TF, D_MODEL), lambda e, f: (l, e, f, 0)),
            pl.BlockSpec((1, CAP_T, N_GROUPS), lambda e, f: (e, 0, 0)),
            pl.BlockSpec((N_GROUPS, 6, D_MODEL), lambda e, f: (0, 0, 0)),
        ],
        out_specs=pl.BlockSpec((1, CAP_T, D_MODEL), lambda e, f: (e, 0, 0)),
        out_shape=jax.ShapeDtypeStruct((N_EXPERTS, CAP_T, D_MODEL), F32),
        scratch_shapes=[pltpu.VMEM((CAP_T, D_MODEL), BF16)],
        compiler_params=_params(("parallel", "arbitrary")),
        name="expert_ffn",
    )(xs, wg, wu, wd, gate, mods_l)


def _final_norm_kernel(x_ref, g_ref, o_ref):
    x = x_ref[...]
    ms = jnp.mean(x * x, axis=-1, keepdims=True)
    o_ref[...] = x * lax.rsqrt(ms + EPS) * g_ref[...]


def _final_norm(x, g, row0, n_rows, tm=512):
    off = row0 // tm
    return pl.pallas_call(
        _final_norm_kernel,
        grid=(n_rows // tm,),
        in_specs=[pl.BlockSpec((tm, D_MODEL), lambda i: (i + off, 0)),
                  pl.BlockSpec((1, D_MODEL), lambda i: (0, 0))],
        out_specs=pl.BlockSpec((tm, D_MODEL), lambda i: (i, 0)),
        out_shape=jax.ShapeDtypeStruct((n_rows, D_MODEL), F32),
        compiler_params=_params(("parallel",)),
        name="final_norm",
    )(x, g.reshape(1, D_MODEL))


def _even_layer(x, mods_l, i, state_ret, st_prev, dfts, norm1_g, w_in_even, hy_short_w, hy_short_b, hy_f1_w, hy_f1_b,
                hy_freq1, hy_f2_w, hy_f2_b, hy_freq2, hy_f3_w, hy_bias, ret_decay, w_out_even):
    proj = _norm_project(x, mods_l, norm1_g, _weight_bf16(w_in_even, i))
    sw, sbias = hy_short_w[i], hy_short_b[i].reshape(1, 3 * HY_W)
    hb = hy_bias[i]
    y_hy = None
    for (row0, n_rows, L, nseq) in ((0, N_PROMPT, SEQ, 8), (N_PROMPT, N_SAMPLE, DEC_SEQ, 1)):
        s, d = _hyena_filter_taps(L, hy_f1_w[i], hy_f1_b[i], hy_freq1[i], hy_f2_w[i], hy_f2_b[i], hy_freq2[i], hy_f3_w[i])
        spectra = _filter_spectra(s, d, L, dfts[L])
        wv, bv = sw[:, 0:HY_W], sbias[:, 0:HY_W]
        w1, b1 = sw[:, HY_W:2 * HY_W], sbias[:, HY_W:2 * HY_W]
        w2, b2 = sw[:, 2 * HY_W:], sbias[:, 2 * HY_W:]
        z1 = _hyena_conv(proj, 0, proj, 1, row0, n_rows, L, nseq, spectra, 0, wv, bv, w1, b1, hb[0:1], True, dfts[L])
        y_hy = _hyena_conv(z1, 0, proj, 2, row0, n_rows, L, nseq, spectra, 1, wv, bv, w2, b2, hb[1:2], False, dfts[L],
                           full_out=True, out_prev=y_hy)
    dl = jnp.broadcast_to(ret_decay[i].astype(F32)[:, :, None], (2, RET_HEADS, LANES))
    y_ret, st = _retention(proj, dl, i, 0, BATCH, SEQ, st_prev=st_prev)
    y_ret = _retention(proj, dl, i, N_PROMPT, DEC_BATCH, DEC_SEQ, s0=state_ret, y_prev=y_ret)
    wo = _weight_bf16(w_out_even, i)
    x = _project_residual([y_hy, y_ret], [wo[:HY_W], wo[HY_W:]], x, mods_l, G1)
    return x, st


def _odd_layer(x, mods_l, l, i, cache_k, cache_v, kv_prev, norm1_g, w_in_odd, lam_q1, lam_k1, lam_q2, lam_k2, subln_g,
               w_out_odd):
    lam_init = 0.8 - 0.6 * math.exp(-0.3 * l)
    proj = _norm_project(x, mods_l, norm1_g, _weight_bf16(w_in_odd, i))
    lam_vec = jnp.pad(jnp.stack([lam_q1[i], lam_k1[i], lam_q2[i], lam_k2[i]]), ((0, 0), (0, LANES - DIFF_HEAD_DIM)))
    sg = subln_g[i].reshape(1, LANES)
    o_p, kc, vc = _diff_attention(proj, proj, proj, lam_vec, sg, lam_init, nb=BATCH, lq=SEQ, lk=SEQ, tq=SEQ,
                                  q_row0=0, q_col=0, kv_cols=(1, 2), batched_kv=False,
                                  cache_layer=i, cache_prev=kv_prev)
    qs, ks, vs = _sample_qkv(proj, cache_k, cache_v, i)
    o = _diff_attention(qs, ks, vs, lam_vec, sg, lam_init, nb=DEC_BATCH, lq=DEC_SEQ, lk=PAST_LEN + DEC_SEQ, tq=256,
                        q_row0=0, q_col=0, kv_cols=None, batched_kv=True, out_row0=N_PROMPT, out_prev=o_p)
    x = _project_residual([o], [_weight_bf16(w_out_odd, i)], x, mods_l, G1)
    return x, (kc, vc)


def _moe_layer(x, mods_l, l, norm2_g, w_router, wg, wu, wd):
    h, aff = _router(x, mods_l, norm2_g, w_router)
    gate_p, idx_p = lax.top_k(aff[:, :N_PROMPT], CAP_P)
    gate_s, idx_s = lax.top_k(aff[:, N_PROMPT:], CAP_S)
    idx = jnp.concatenate([idx_p, idx_s + N_PROMPT], axis=1)
    gate = jnp.concatenate([gate_p, gate_s], axis=1)
    flat = idx.reshape(-1)
    xs = _gather_rows(h.reshape(2 * N_TOK, PACK_W), jnp.concatenate([flat, flat + N_TOK]))
    xs = xs.reshape(2, N_EXPERTS, CAP_T, PACK_W)
    grp = jnp.where(idx < N_PROMPT, 0, 1 + (idx - N_PROMPT) // DEC_SEQ)
    gate_grp = jnp.where(grp[:, :, None] == jnp.arange(N_GROUPS)[None, None, :], gate[:, :, None], 0.0)
    out = _expert_ffn(xs, wg, wu, wd, gate_grp, mods_l, l)
    return x.at[idx.reshape(-1)].add(out.reshape(-1, D_MODEL))


def kernel(x_prompt, x_sample, state_ret, cache_k, cache_v, c, c_ctx, w_mod, b_mod, norm1_g, norm2_g, w_in_even, hy_short_w, hy_short_b, hy_f1_w, hy_f1_b, hy_freq1, hy_f2_w, hy_f2_b, hy_freq2, hy_f3_w, hy_bias, ret_decay, w_out_even, w_in_odd, lam_q1, lam_k1, lam_q2, lam_k2, subln_g, w_out_odd, moe_router, moe_wg, moe_wu, moe_wd, final_g):
    x = (x_prompt.reshape(N_PROMPT, D_MODEL), x_sample.reshape(N_SAMPLE, D_MODEL))
    mods = _modulation(c, c_ctx, w_mod, b_mod)
    dfts = {L: _dft_bf16(L) for L in (SEQ, DEC_SEQ)}
    state_ret = state_ret.astype(F32)
    st = kv = None
    for l in range(DEPTH):
        i = l // 2
        if l % 2 == 0:
            x, st = _even_layer(x, mods[l], i, state_ret, st, dfts, norm1_g[l], w_in_even, hy_short_w, hy_short_b,
                                hy_f1_w, hy_f1_b, hy_freq1, hy_f2_w, hy_f2_b, hy_freq2, hy_f3_w, hy_bias, ret_decay,
                                w_out_even)
        else:
            x, kv = _odd_layer(x, mods[l], l, i, cache_k, cache_v, kv, norm1_g[l], w_in_odd, lam_q1, lam_k1, lam_q2,
                               lam_k2, subln_g, w_out_odd)
        x = _moe_layer(x, mods[l], l, norm2_g[l], moe_router[l], moe_wg, moe_wu, moe_wd)
    y_prompt = _final_norm(x, final_g, 0, N_PROMPT).reshape(BATCH, SEQ, D_MODEL)
    y_sample = _final_norm(x, final_g, N_PROMPT, N_SAMPLE).reshape(DEC_BATCH, DEC_SEQ, D_MODEL)
    cache_shape = (BATCH, DEPTH // 2, SEQ, DIFF_HEADS, 2 * DIFF_HEAD_DIM)
    return (y_prompt, y_sample, st, kv[0].reshape(cache_shape), kv[1].reshape(cache_shape))
```

```python
import functools
import math

import numpy as np
import jax
import jax.numpy as jnp
from jax import lax
from jax.experimental import pallas as pl
from jax.experimental.pallas import tpu as pltpu
from jax.experimental.pallas import tpu_sc as plsc

F32 = jnp.float32
BF16 = jnp.bfloat16

D_MODEL = 1024
BATCH = 32
SEQ = 256
DEPTH = 4
DEC_BATCH = 2
DEC_SEQ = 2048
PAST_LEN = 256
GRID_W = 64
HY_W = 512
HY_EMB = 33
HY_BANDS = 16
HY_FF = 64
HY_TARGET = 1e-2
HY_FAST = 0.3
HY_SLOW = 1.5
RET_W = 512
RET_HEADS = 4
RET_HEAD_DIM = 128
RET_CHUNK = 128
DIFF_HEADS = 8
DIFF_HEAD_DIM = 64
ROPE_BASE = 10000.0
N_EXPERTS = 16
EC_FACTOR = 2
EXPERT_FF = 1024
EVEN_IN = 3 * HY_W + 4 * RET_W
EPS = 1e-6

N_PROMPT = BATCH * SEQ
N_SAMPLE = DEC_BATCH * DEC_SEQ
N_TOK = N_PROMPT + N_SAMPLE
N_GROUPS = 1 + DEC_BATCH
CAP_P = EC_FACTOR * N_PROMPT // N_EXPERTS
CAP_S = EC_FACTOR * N_SAMPLE // N_EXPERTS
CAP_T = CAP_P + CAP_S

LANES = 128
SUBLANES = 8
VMEM_LIMIT = 56 * 1024 * 1024

SH1, SC1, G1, SH2, SC2, G2 = range(6)


def _params(sem, vmem=VMEM_LIMIT):
    return pltpu.CompilerParams(dimension_semantics=sem, vmem_limit_bytes=vmem)


def _group_of_block(i, tm):
    pb = N_PROMPT // tm
    return jnp.where(i < pb, 0, 1 + (i - pb) // (DEC_SEQ // tm))


MOD_TN = 1024


MOD_UNROLL = 4


def _mod_kernel(cb_ref, w_ref, b_ref, o_ref, a_sc):
    nchunk = MOD_TN // LANES
    cv = cb_ref[...]
    a_sc[...] = cv * (1.0 / (1.0 + jnp.exp(-cv)))

    def body(kb, accs):
        accs = list(accs)
        for u in range(MOD_UNROLL):
            k0 = pl.multiple_of((kb * MOD_UNROLL + u) * SUBLANES, SUBLANES)
            a = [a_sc[r, pl.ds(k0, SUBLANES), :] for r in range(N_GROUPS)]
            for ci in range(nchunk):
                wv = w_ref[0, pl.ds(k0, SUBLANES), ci * LANES:(ci + 1) * LANES]
                for r in range(N_GROUPS):
                    accs[ci * N_GROUPS + r] = accs[ci * N_GROUPS + r] + wv * a[r]
        return tuple(accs)

    init = tuple(jnp.zeros((SUBLANES, LANES), F32) for _ in range(N_GROUPS * nchunk))
    accs = lax.fori_loop(0, D_MODEL // (SUBLANES * MOD_UNROLL), body, init)
    o_ref[...] = jnp.zeros(o_ref.shape, F32)
    for r in range(N_GROUPS):
        for ci in range(nchunk):
            row = jnp.sum(accs[ci * N_GROUPS + r], axis=0, keepdims=True)
            o_ref[0, r:r + 1, ci * LANES:(ci + 1) * LANES] = row + b_ref[0, :, ci * LANES:(ci + 1) * LANES]


def _modulation(c, c_ctx, w_mod, b_mod):
    cond = jnp.concatenate([c_ctx[None, :], c], axis=0)
    cb = jnp.broadcast_to(cond[:, :, None], (N_GROUPS, D_MODEL, LANES))
    out = pl.pallas_call(
        _mod_kernel,
        grid=(DEPTH, 6 * D_MODEL // MOD_TN),
        in_specs=[
            pl.BlockSpec((N_GROUPS, D_MODEL, LANES), lambda l, j: (0, 0, 0)),
            pl.BlockSpec((1, D_MODEL, MOD_TN), lambda l, j: (l, 0, j)),
            pl.BlockSpec((1, 1, MOD_TN), lambda l, j: (l, 0, j)),
        ],
        out_specs=pl.BlockSpec((1, SUBLANES, MOD_TN), lambda l, j: (l, 0, j)),
        out_shape=jax.ShapeDtypeStruct((DEPTH, SUBLANES, 6 * D_MODEL), F32),
        scratch_shapes=[pltpu.VMEM((N_GROUPS, D_MODEL, LANES), F32)],
        compiler_params=_params(("parallel", "parallel")),
        name="ada_mod",
    )(cb, w_mod, b_mod.reshape(DEPTH, 1, 6 * D_MODEL))
    return out[:, :N_GROUPS].reshape(DEPTH, N_GROUPS, 6, D_MODEL)


def _normed(x, g_ref, m_ref, shift, scale):
    ms = jnp.mean(x * x, axis=-1, keepdims=True)
    y = x * lax.rsqrt(ms + EPS) * g_ref[...]
    return y * (1.0 + m_ref[0, scale:scale + 1, :]) + m_ref[0, shift:shift + 1, :]


def _weight_bf16(w, i):
    return w[i].astype(BF16)


def _x_specs(x, tm):
    if not isinstance(x, tuple):
        return [x], [pl.BlockSpec((tm, D_MODEL), lambda i: (i, 0))]
    pb = N_PROMPT // tm
    return list(x), [pl.BlockSpec((tm, D_MODEL), lambda i: (jnp.minimum(i, pb - 1), 0)),
                     pl.BlockSpec((tm, D_MODEL), lambda i: (jnp.maximum(i - pb, 0), 0))]


def _x_block(x_refs, tm):
    if len(x_refs) == 1:
        return x_refs[0][...]
    return jnp.where(pl.program_id(0) < N_PROMPT // tm, x_refs[0][...], x_refs[1][...])


def _norm_mm_kernel(*refs, tn, tm):
    m_ref, g_ref, w_ref, o_ref = refs[-4:]
    h = _normed(_x_block(refs[:-4], tm), g_ref, m_ref, SH1, SC1).astype(BF16)
    for c0 in range(0, o_ref.shape[1], tn):
        o_ref[:, c0:c0 + tn] = jnp.dot(h, w_ref[:, c0:c0 + tn], preferred_element_type=F32)


def _norm_project(x, mods_l, g, w_bf16, tm=512, tn=512):
    nout = w_bf16.shape[1]
    x_args, x_specs = _x_specs(x, tm)
    return pl.pallas_call(
        functools.partial(_norm_mm_kernel, tn=tn, tm=tm),
        grid=(N_TOK // tm,),
        in_specs=x_specs + [
            pl.BlockSpec((1, 6, D_MODEL), lambda i: (_group_of_block(i, tm), 0, 0)),
            pl.BlockSpec((1, D_MODEL), lambda i: (0, 0)),
            pl.BlockSpec((D_MODEL, nout), lambda i: (0, 0)),
        ],
        out_specs=pl.BlockSpec((tm, nout), lambda i: (i, 0)),
        out_shape=jax.ShapeDtypeStruct((N_TOK, nout), F32),
        compiler_params=_params(("parallel",)),
        name="norm_project",
    )(*x_args, mods_l, g.reshape(1, D_MODEL), w_bf16)


def _proj_res_kernel(*refs, n_in, n_x, gate, tm):
    a_refs = refs[:n_in]
    w_refs = refs[n_in:2 * n_in]
    x_refs = refs[2 * n_in:2 * n_in + n_x]
    m_ref, g2_ref, wr_ref, o_ref, h_ref, aff_ref = refs[2 * n_in + n_x:]
    acc = None
    for a_ref, w_ref in zip(a_refs, w_refs):
        t = jnp.dot(a_ref[...].astype(BF16), w_ref[...], preferred_element_type=F32)
        acc = t if acc is None else acc + t
    x_new = _x_block(x_refs, tm) + m_ref[0, gate:gate + 1, :] * acc
    o_ref[...] = x_new
    _route_block(x_new, m_ref, g2_ref, wr_ref, h_ref, aff_ref)


def _project_residual(acts, ws_bf16, x, mods_l, gate, norm2_g, w_router, tm=512):
    n_in = len(acts)
    x_args, x_specs = _x_specs(x, tm)
    wr = jnp.pad(w_router, ((0, 0), (0, LANES - N_EXPERTS)))
    in_specs = [pl.BlockSpec((tm, a.shape[1]), lambda i: (i, 0)) for a in acts]
    in_specs += [pl.BlockSpec(w.shape, lambda i: (0, 0)) for w in ws_bf16]
    in_specs += x_specs + [
        pl.BlockSpec((1, 6, D_MODEL), lambda i: (_group_of_block(i, tm), 0, 0)),
        pl.BlockSpec((1, D_MODEL), lambda i: (0, 0)),
        pl.BlockSpec((D_MODEL, LANES), lambda i: (0, 0)),
    ]
    return pl.pallas_call(
        functools.partial(_proj_res_kernel, n_in=n_in, n_x=len(x_args), gate=gate, tm=tm),
        grid=(N_TOK // tm,),
        in_specs=in_specs,
        out_specs=[
            pl.BlockSpec((tm, D_MODEL), lambda i: (i, 0)),
            pl.BlockSpec((2, tm, PACK_W), lambda i: (0, i, 0)),
            pl.BlockSpec((N_EXPERTS, tm), lambda i: (0, i)),
        ],
        out_shape=[
            jax.ShapeDtypeStruct((N_TOK, D_MODEL), F32),
            jax.ShapeDtypeStruct((2, N_TOK, PACK_W), jnp.int32),
            jax.ShapeDtypeStruct((N_EXPERTS, N_TOK), F32),
        ],
        input_output_aliases={2 * n_in: 0} if len(x_args) == 1 else {},
        compiler_params=_params(("parallel",)),
        name="project_residual_route",
    )(*acts, *ws_bf16, *x_args, mods_l, norm2_g.reshape(1, D_MODEL), wr)


@functools.lru_cache(maxsize=None)
def _dft_mats(L):
    n = 2 * L
    ft = (np.arange(L, dtype=np.int64)[:, None] * np.arange(L, dtype=np.int64)[None, :]) % n
    ang = ft.astype(np.float64) * (2.0 * np.pi / n)
    return np.cos(ang).astype(np.float32), np.sin(ang).astype(np.float32)


def _dft_bf16(L):
    c, s = _dft_mats(L)
    return jnp.asarray(c).astype(BF16), jnp.asarray(s).astype(BF16)


def _alt_sign(shape, row0):
    t = lax.broadcasted_iota(jnp.int32, shape, 0) + row0
    return (1 - 2 * (t & 1)).astype(F32)


def _filter_dft_kernel(s_ref, d_ref, c_ref, sn_ref, ka_ref, ki_ref, kn_ref, *, L, fb):
    f0 = pl.program_id(0) * fb
    n = 2.0 * L
    s = s_ref[...]
    r = jnp.dot(c_ref[...], s, preferred_element_type=F32)
    im = jnp.dot(sn_ref[...], d_ref[...], preferred_element_type=F32)
    fidx = lax.broadcasted_iota(jnp.int32, r.shape, 0) + f0
    scale = jnp.where(fidx == 0, 1.0 / n, 2.0 / n)
    ka_ref[...] = r * scale
    ki_ref[...] = im * (2.0 / n)
    nyq = jnp.sum(s.astype(F32) * _alt_sign(s.shape, 0), axis=0, keepdims=True) * (1.0 / n)
    kn_ref[...] = jnp.broadcast_to(nyq, kn_ref.shape)


def _filter_spectra(s, d, L, dft):
    fb = min(L, 512)
    cmat, smat = dft
    w = 2 * HY_W
    return pl.pallas_call(
        functools.partial(_filter_dft_kernel, L=L, fb=fb),
        grid=(L // fb,),
        in_specs=[
            pl.BlockSpec((L, w), lambda f: (0, 0)),
            pl.BlockSpec((L, w), lambda f: (0, 0)),
            pl.BlockSpec((fb, L), lambda f: (f, 0)),
            pl.BlockSpec((fb, L), lambda f: (f, 0)),
        ],
        out_specs=[
            pl.BlockSpec((fb, w), lambda f: (f, 0)),
            pl.BlockSpec((fb, w), lambda f: (f, 0)),
            pl.BlockSpec((SUBLANES, w), lambda f: (0, 0)),
        ],
        out_shape=[
            jax.ShapeDtypeStruct((L, w), F32),
            jax.ShapeDtypeStruct((L, w), F32),
            jax.ShapeDtypeStruct((SUBLANES, w), F32),
        ],
        compiler_params=_params(("arbitrary",)),
        name="hyena_filter_dft",
    )(s.astype(BF16), d.astype(BF16), cmat, smat)


HY_TILE = 256


def _short_conv_tile(ref, r0, L, w_ref, b_ref):
    t = HY_TILE
    cur = ref[r0:r0 + t, :]
    rid = lax.broadcasted_iota(jnp.int32, cur.shape, 0)
    if r0 % L == 0:
        prev = jnp.where(rid == 0, 0.0, pltpu.roll(cur, 1, axis=0))
    else:
        prev = ref[r0 - 1:r0 - 1 + t, :]
    if (r0 + t) % L == 0:
        nxt = jnp.where(rid == t - 1, 0.0, pltpu.roll(cur, t - 1, axis=0))
    else:
        nxt = ref[r0 + 1:r0 + 1 + t, :]
    return prev * w_ref[0:1, :] + cur * w_ref[1:2, :] + nxt * w_ref[2:3, :] + b_ref[...]


def _hyena_conv_kernel(*refs, L, nseq, conv_a):
    (a_ref, x_ref, cr_ref, sr_ref, cc_ref, sc_ref, ka_ref, ki_ref, kn_ref,
     wa_ref, ba_ref, wx_ref, bx_ref, hb_ref) = refs[:14]
    o_ref, z_sc, acc_sc = refs[-3:]
    f = pl.program_id(1)
    nf = pl.num_programs(1)
    rows = nseq * L

    @pl.when(f == 0)
    def _():
        for q in range(nseq):
            nyq = jnp.zeros((1, HY_W), F32)
            for r0 in range(q * L, (q + 1) * L, HY_TILE):
                if conv_a:
                    zt = _short_conv_tile(a_ref, r0, L, wa_ref, ba_ref)
                else:
                    zt = a_ref[r0:r0 + HY_TILE, :]
                z_sc[r0:r0 + HY_TILE, :] = zt
                nyq = nyq + jnp.sum(zt * _alt_sign(zt.shape, r0), axis=0, keepdims=True)
            nyq = nyq * kn_ref[0:1, :]
            for r0 in range(q * L, (q + 1) * L, HY_TILE):
                acc_sc[r0:r0 + HY_TILE, :] = _alt_sign((HY_TILE, HY_W), r0) * nyq

    ka = ka_ref[...]
    ki = ki_ref[...]
    for q in range(nseq):
        z = z_sc[q * L:(q + 1) * L, :].astype(BF16)
        a = jnp.dot(cr_ref[...], z, preferred_element_type=F32)
        b = jnp.dot(sr_ref[...], z, preferred_element_type=F32)
        p = (a * ka + b * ki).astype(BF16)
        qq = (b * ka - a * ki).astype(BF16)
        acc_sc[q * L:(q + 1) * L, :] += (jnp.dot(cc_ref[...], p, preferred_element_type=F32)
                                         + jnp.dot(sc_ref[...], qq, preferred_element_type=F32))

    @pl.when(f == nf - 1)
    def _():
        for r0 in range(0, rows, HY_TILE):
            y = acc_sc[r0:r0 + HY_TILE, :] + z_sc[r0:r0 + HY_TILE, :] * hb_ref[...]
            o_ref[r0:r0 + HY_TILE, :] = (y * _short_conv_tile(x_ref, r0, L, wx_ref, bx_ref)).astype(o_ref.dtype)


def _hyena_conv(a, a_col, x, x_col, row0, n_rows, L, nseq, spectra, filt, wa, ba, wx, bx, hbias, conv_a, dft,
                full_out=False, out_prev=None):
    fb = min(L, 256)
    cmat, smat = dft
    ka, ki, kn = spectra
    rb = nseq * L
    a_off = row0 // rb if a.shape[0] != n_rows else 0
    x_off = row0 // rb
    o_off = row0 // rb if full_out else 0
    extra_specs, extra_args, aliases = [], [], {}
    if out_prev is not None:
        extra_specs, extra_args, aliases = [pl.BlockSpec(memory_space=pl.ANY)], [out_prev], {14: 0}
    return pl.pallas_call(
        functools.partial(_hyena_conv_kernel, L=L, nseq=nseq, conv_a=conv_a),
        grid=(n_rows // rb, L // fb),
        input_output_aliases=aliases,
        in_specs=extra_specs[:0] + [
            pl.BlockSpec((rb, HY_W), lambda i, f: (i + a_off, a_col)),
            pl.BlockSpec((rb, HY_W), lambda i, f: (i + x_off, x_col)),
            pl.BlockSpec((fb, L), lambda i, f: (f, 0)),
            pl.BlockSpec((fb, L), lambda i, f: (f, 0)),
            pl.BlockSpec((L, fb), lambda i, f: (0, f)),
            pl.BlockSpec((L, fb), lambda i, f: (0, f)),
            pl.BlockSpec((fb, HY_W), lambda i, f: (f, filt)),
            pl.BlockSpec((fb, HY_W), lambda i, f: (f, filt)),
            pl.BlockSpec((SUBLANES, HY_W), lambda i, f: (0, filt)),
            pl.BlockSpec((3, HY_W), lambda i, f: (0, 0)),
            pl.BlockSpec((1, HY_W), lambda i, f: (0, 0)),
            pl.BlockSpec((3, HY_W), lambda i, f: (0, 0)),
            pl.BlockSpec((1, HY_W), lambda i, f: (0, 0)),
            pl.BlockSpec((1, HY_W), lambda i, f: (0, 0)),
        ] + extra_specs,
        out_specs=pl.BlockSpec((rb, HY_W), lambda i, f: (i + o_off, 0)),
        out_shape=jax.ShapeDtypeStruct((N_TOK, HY_W), BF16) if full_out else jax.ShapeDtypeStruct((n_rows, HY_W), F32),
        scratch_shapes=[pltpu.VMEM((rb, HY_W), F32), pltpu.VMEM((rb, HY_W), F32)],
        compiler_params=_params(("parallel", "arbitrary")),
        name="hyena_conv",
    )(a, x, cmat, smat, cmat, smat, ka, ki, kn, wa, ba, wx, bx, hbias, *extra_args)


def _hyena_filter_taps(L, f1w, f1b, fr1, f2w, f2b, fr2, f3w):
    hp = lax.Precision.HIGHEST
    pos = jnp.arange(L, dtype=F32)
    t = pos / (L - 1)
    w = 2.0 * math.pi * pos / L
    f = jnp.linspace(1e-4, HY_BANDS - 1, HY_BANDS, dtype=F32)
    wf = w[:, None] * f[None, :]
    feat = jnp.concatenate([t[:, None], jnp.cos(wf), -jnp.sin(wf)], axis=-1)
    h = jnp.sin(fr1 * (jnp.dot(feat, f1w, precision=hp) + f1b))
    h = jnp.sin(fr2 * (jnp.dot(h, f2w, precision=hp) + f2b))
    h = jnp.dot(h, f3w, precision=hp).astype(F32)
    deltas = jnp.linspace(math.log(HY_TARGET) / HY_SLOW, math.log(HY_TARGET) / HY_FAST, HY_W, dtype=F32)
    window = jnp.exp(-t[:, None] * jnp.abs(deltas)[None, :])
    w = HY_W
    parts = [h[:, k * w:(k + 1) * w] * window for k in range(4)]
    colsum = [jnp.sum(jnp.abs(p), axis=0, keepdims=True) for p in parts]
    den = [colsum[0] + colsum[1] + EPS, colsum[2] + colsum[3] + EPS]
    fwd = jnp.concatenate([parts[0] / den[0], parts[2] / den[1]], axis=1)
    bwd = jnp.concatenate([parts[1] / den[0], parts[3] / den[1]], axis=1)
    bwd = jnp.where(pos[:, None] == 0, 0.0, bwd)
    return fwd + bwd, bwd - fwd


def _dot_t0(a, b):
    return lax.dot_general(a, b, (((0,), (0,)), ((), ())), preferred_element_type=F32)


def _dot_t1(a, b):
    return lax.dot_general(a, b, (((1,), (1,)), ((), ())), preferred_element_type=F32)


def _retention_kernel(*refs, L, has_s0, has_prev):
    refs = list(refs)
    q_ref, k_ref, v_ref, g_ref, dl_ref = refs[:5]
    pos = 5
    s0_ref = st_ref = None
    if has_s0:
        s0_ref = refs[pos]
        pos += 1
    if has_prev:
        pos += 1
    y_ref = refs[pos]
    pos += 1
    if not has_s0:
        st_ref = refs[pos]
        pos += 1
    sb_sc, sf_cur, sb_cur = refs[pos:]
    c = RET_CHUNK
    nc = L // c
    kscale = RET_HEAD_DIM ** -0.5
    ri = lax.broadcasted_iota(jnp.int32, (c, c), 0).astype(F32)
    ci = lax.broadcasted_iota(jnp.int32, (c, c), 1).astype(F32)
    diff = ri - ci
    dec = []
    for h in range(RET_HEADS):
        xf = dl_ref[0, h:h + 1, :]
        xb = dl_ref[1, h:h + 1, :]
        lgf = jnp.minimum(xf, 0.0) - jnp.log1p(jnp.exp(-jnp.abs(xf)))
        lgb = jnp.minimum(xb, 0.0) - jnp.log1p(jnp.exp(-jnp.abs(xb)))
        dec.append(dict(
            mask=(jnp.where(diff >= 0, jnp.exp(lgf * jnp.maximum(diff, 0.0)), 0.0)
                  + jnp.where(diff <= 0, jnp.exp(lgb * jnp.maximum(-diff, 0.0)), 0.0)),
            qdec_f=jnp.exp(lgf * (ri + 1.0)), kdec_f=jnp.exp(lgf * (c - 1.0 - ri)),
            qdec_b=jnp.exp(lgb * (c - ri)), kdec_b=jnp.exp(lgb * ri),
            cd_f=jnp.exp(lgf * c), cd_b=jnp.exp(lgb * c)))
        if has_s0:
            sf_cur[h] = s0_ref[0, 0, 0, h]
            sb_cur[h] = s0_ref[0, 0, 1, h]
        else:
            sf_cur[h] = jnp.zeros((RET_HEAD_DIM, RET_HEAD_DIM), F32)
            sb_cur[h] = jnp.zeros((RET_HEAD_DIM, RET_HEAD_DIM), F32)

    def bwd_body(i, carry):
        j = nc - 1 - i
        r0 = pl.multiple_of(j * c, c)
        for h in range(RET_HEADS):
            hs = slice(h * RET_HEAD_DIM, (h + 1) * RET_HEAD_DIM)
            sb = sb_cur[h]
            sb_sc[h * nc + j] = sb
            kc = k_ref[pl.ds(r0, c), hs] * kscale
            vc = v_ref[pl.ds(r0, c), hs]
            sb_cur[h] = sb * dec[h]["cd_b"] + _dot_t0((kc * dec[h]["kdec_b"]).astype(BF16), vc.astype(BF16))
        return carry

    lax.fori_loop(0, nc, bwd_body, 0)

    def fwd_body(j, carry):
        r0 = pl.multiple_of(j * c, c)
        for h in range(RET_HEADS):
            hs = slice(h * RET_HEAD_DIM, (h + 1) * RET_HEAD_DIM)
            dh = dec[h]
            sf = sf_cur[h]
            qc = q_ref[pl.ds(r0, c), hs]
            kc = k_ref[pl.ds(r0, c), hs] * kscale
            vc = v_ref[pl.ds(r0, c), hs].astype(BF16)
            scores = _dot_t1(qc.astype(BF16), kc.astype(BF16)) * dh["mask"]
            o = jnp.dot(scores.astype(BF16), vc, preferred_element_type=F32)
            o = o + jnp.dot((qc * dh["qdec_f"]).astype(BF16), sf.astype(BF16), preferred_element_type=F32)
            o = o + jnp.dot((qc * dh["qdec_b"]).astype(BF16), sb_sc[h * nc + j].astype(BF16),
                            preferred_element_type=F32)
            mu = jnp.mean(o, axis=-1, keepdims=True)
            var = jnp.mean(jnp.square(o - mu), axis=-1, keepdims=True)
            on = (o - mu) * lax.rsqrt(var + EPS)
            gc = g_ref[pl.ds(r0, c), hs]
            y_ref[pl.ds(r0, c), hs] = (gc * (1.0 / (1.0 + jnp.exp(-gc))) * on).astype(y_ref.dtype)
            sf_cur[h] = sf * dh["cd_f"] + _dot_t0((kc * dh["kdec_f"]).astype(BF16), vc)
        return carry

    lax.fori_loop(0, nc, fwd_body, 0)
    if st_ref is not None:
        for h in range(RET_HEADS):
            st_ref[0, 0, 0, h] = sf_cur[h]
            st_ref[0, 0, 1, h] = sb_cur[h]


def _retention(proj, dl, layer_i, row0, nseq, L, s0=None, st_prev=None, y_prev=None):
    off = row0 // L
    has_s0 = s0 is not None
    has_prev = (st_prev is not None) or (y_prev is not None)
    assert not (st_prev is not None and y_prev is not None)
    nc = L // RET_CHUNK
    n_ret = (DEPTH + 1) // 2
    col = lambda j: pl.BlockSpec((L, RET_W), lambda b: (b + off, 3 + j))
    in_specs = [col(0), col(1), col(2), col(3),
                pl.BlockSpec((2, RET_HEADS, LANES), lambda b: (0, 0, 0))]
    args = [proj, proj, proj, proj, dl]
    y_spec = pl.BlockSpec((L, RET_W), lambda b: (b + off, 0))
    y_shape = jax.ShapeDtypeStruct((N_TOK, RET_W), BF16)
    st_block = (1, 1, 2, RET_HEADS, RET_HEAD_DIM, RET_HEAD_DIM)
    st_spec = pl.BlockSpec(st_block, lambda b: (b, layer_i, 0, 0, 0, 0))
    aliases = {}
    if has_s0:
        in_specs.append(st_spec)
        args.append(s0)
        out_specs, out_shape = y_spec, y_shape
    else:
        out_specs = [y_spec, st_spec]
        out_shape = [y_shape, jax.ShapeDtypeStruct((nseq, n_ret) + st_block[2:], F32)]
    if has_prev:
        aliases = {len(args): 1 if st_prev is not None else 0}
        in_specs.append(pl.BlockSpec(memory_space=pl.ANY))
        args.append(st_prev if st_prev is not None else y_prev)
    state = pltpu.VMEM((RET_HEADS, RET_HEAD_DIM, RET_HEAD_DIM), F32)
    return pl.pallas_call(
        functools.partial(_retention_kernel, L=L, has_s0=has_s0, has_prev=has_prev),
        grid=(nseq,),
        in_specs=in_specs,
        out_specs=out_specs,
        out_shape=out_shape,
        input_output_aliases=aliases,
        scratch_shapes=[pltpu.VMEM((RET_HEADS * nc, RET_HEAD_DIM, RET_HEAD_DIM), F32), state, state],
        compiler_params=_params(("parallel",)),
        name="retention",
    )(*args)


@functools.lru_cache(maxsize=None)
def _rope_tables():
    L = DEC_SEQ
    rows = L // GRID_W
    row = np.repeat(np.arange(rows, dtype=np.float64), GRID_W)
    col = np.tile(np.arange(GRID_W, dtype=np.float64), rows)
    quarter = DIFF_HEAD_DIM // 4
    freqs = ROPE_BASE ** (-np.arange(quarter, dtype=np.float64) / quarter)
    j = np.arange(LANES)
    pos = np.where(((j % DIFF_HEAD_DIM) < DIFF_HEAD_DIM // 2)[None, :], row[:, None], col[:, None])
    ang = pos * freqs[j % quarter][None, :]
    cos = np.cos(ang).astype(np.float32)
    sin = np.sin(ang).astype(np.float32)
    first = ((j % (2 * quarter)) < quarter)[None, :]
    sin_a = np.where(first, -sin, 0.0).astype(np.float32)
    sin_b = np.where(first, 0.0, sin).astype(np.float32)
    return jnp.asarray(cos), jnp.asarray(sin_a), jnp.asarray(sin_b)


def _rope_head(x, cos, sin_a, sin_b):
    quarter = DIFF_HEAD_DIM // 4
    up = pltpu.roll(x, LANES - quarter, axis=1)
    dn = pltpu.roll(x, quarter, axis=1)
    return x * cos + up * sin_a + dn * sin_b


def _kv_prep_kernel(q_ref, k_ref, v_ref, cos_ref, sa_ref, sb_ref, qo_ref, ko_ref, vo_ref):
    cos, sa, sb = cos_ref[...], sa_ref[...], sb_ref[...]
    for h in range(DIFF_HEADS):
        hs = slice(h * LANES, (h + 1) * LANES)
        qo_ref[:, hs] = _rope_head(q_ref[:, hs], cos, sa, sb).astype(BF16)
        ko_ref[0, :, hs] = _rope_head(k_ref[:, hs], cos, sa, sb).astype(BF16)
    vo_ref[0] = v_ref[...].astype(BF16)


def _cache_copy_kernel(ck_ref, cv_ref, k_in, v_in, ko_ref, vo_ref):
    del k_in, v_in
    ko_ref[0] = ck_ref[0, 0].astype(BF16)
    vo_ref[0] = cv_ref[0, 0].astype(BF16)


def _sample_qkv(proj, cache_k, cache_v, layer_i, tm=256):
    cos, sa, sb = _rope_tables()
    lk = PAST_LEN + DEC_SEQ
    pblk = N_PROMPT // tm
    nblk = DEC_SEQ // tm
    cblk = PAST_LEN // tm
    tab = pl.BlockSpec((tm, LANES), lambda b, i: (i, 0))
    q, k, v = pl.pallas_call(
        _kv_prep_kernel,
        grid=(DEC_BATCH, nblk),
        in_specs=[
            pl.BlockSpec((tm, D_MODEL), lambda b, i: (pblk + b * nblk + i, 0)),
            pl.BlockSpec((tm, D_MODEL), lambda b, i: (pblk + b * nblk + i, 1)),
            pl.BlockSpec((tm, D_MODEL), lambda b, i: (pblk + b * nblk + i, 2)),
            tab, tab, tab,
        ],
        out_specs=[
            pl.BlockSpec((tm, D_MODEL), lambda b, i: (b * nblk + i, 0)),
            pl.BlockSpec((1, tm, D_MODEL), lambda b, i: (b, cblk + i, 0)),
            pl.BlockSpec((1, tm, D_MODEL), lambda b, i: (b, cblk + i, 0)),
        ],
        out_shape=[
            jax.ShapeDtypeStruct((N_SAMPLE, D_MODEL), BF16),
            jax.ShapeDtypeStruct((DEC_BATCH, lk, D_MODEL), BF16),
            jax.ShapeDtypeStruct((DEC_BATCH, lk, D_MODEL), BF16),
        ],
        compiler_params=_params(("parallel", "parallel")),
        name="rope_qkv",
    )(proj, proj, proj, cos, sa, sb)
    n_att = DEPTH // 2
    ck = cache_k.reshape(DEC_BATCH, n_att, PAST_LEN, D_MODEL)
    cv = cache_v.reshape(DEC_BATCH, n_att, PAST_LEN, D_MODEL)
    k, v = pl.pallas_call(
        _cache_copy_kernel,
        grid=(DEC_BATCH,),
        in_specs=[
            pl.BlockSpec((1, 1, PAST_LEN, D_MODEL), lambda b: (b, layer_i, 0, 0)),
            pl.BlockSpec((1, 1, PAST_LEN, D_MODEL), lambda b: (b, layer_i, 0, 0)),
            pl.BlockSpec(memory_space=pl.ANY),
            pl.BlockSpec(memory_space=pl.ANY),
        ],
        out_specs=[
            pl.BlockSpec((1, PAST_LEN, D_MODEL), lambda b: (b, 0, 0)),
            pl.BlockSpec((1, PAST_LEN, D_MODEL), lambda b: (b, 0, 0)),
        ],
        out_shape=[
            jax.ShapeDtypeStruct((DEC_BATCH, lk, D_MODEL), BF16),
            jax.ShapeDtypeStruct((DEC_BATCH, lk, D_MODEL), BF16),
        ],
        input_output_aliases={2: 0, 3: 1},
        compiler_params=_params(("parallel",)),
        name="cache_prepend",
    )(ck, cv, k, v)
    return q, k, v


def _diff_attn_kernel(*refs, lam_init, batched_kv, cache_out):
    q_ref, k_ref, v_ref, lam_ref, sg_ref = refs[:5]
    if cache_out:
        o_ref, kc_ref, vc_ref = refs[-3:]
        kc_ref[0, 0] = k_ref[...]
        vc_ref[0, 0] = v_ref[...]
    else:
        o_ref = refs[-1]
    lv = lam_ref[...]
    lam = (jnp.exp(jnp.sum(lv[0:1] * lv[1:2], axis=-1, keepdims=True))
           - jnp.exp(jnp.sum(lv[2:3] * lv[3:4], axis=-1, keepdims=True)) + lam_init)
    lane = lax.broadcasted_iota(jnp.int32, (1, LANES), 1)
    m1 = (lane < DIFF_HEAD_DIM).astype(F32)
    m2 = 1.0 - m1
    scale = DIFF_HEAD_DIM ** -0.5
    for h in range(DIFF_HEADS):
        hs = slice(h * LANES, (h + 1) * LANES)
        q = q_ref[:, hs].astype(F32) * scale
        if batched_kv:
            k = k_ref[0, :, hs].astype(BF16)
            v = v_ref[0, :, hs].astype(BF16)
        else:
            k = k_ref[:, hs].astype(BF16)
            v = v_ref[:, hs].astype(BF16)
        v_ext = jnp.concatenate([v, jnp.ones_like(v)], axis=1)
        outs = []
        for m in (m1, m2):
            s = _dot_t1((q * m).astype(BF16), k)
            s = s - jnp.max(s, axis=-1, keepdims=True)
            pv = jnp.dot(jnp.exp(s).astype(BF16), v_ext, preferred_element_type=F32)
            outs.append(pv[:, :LANES] / pv[:, LANES:])
        o = outs[0] - lam * outs[1]
        ms = jnp.mean(o * o, axis=-1, keepdims=True)
        o_ref[:, hs] = (o * lax.rsqrt(ms + EPS) * sg_ref[...] * (1.0 - lam_init)).astype(o_ref.dtype)


def _diff_attention(q, k, v, lam_vec, subln, lam_init, *, nb, lq, lk, tq, q_row0, q_col, kv_cols, batched_kv,
                    out_row0=0, out_prev=None, cache_layer=None, cache_prev=None):
    nq = lq // tq
    qoff = q_row0 // tq
    ooff = out_row0 // tq
    cache_out = cache_layer is not None
    extra_specs, extra_args, aliases = [], [], {}
    if out_prev is not None:
        extra_specs, extra_args, aliases = [pl.BlockSpec(memory_space=pl.ANY)], [out_prev], {5: 0}
    o_spec = pl.BlockSpec((tq, D_MODEL), lambda b, i: (ooff + b * nq + i, 0))
    o_shape = jax.ShapeDtypeStruct((N_TOK, D_MODEL), BF16)
    if cache_out:
        assert not batched_kv and nq == 1 and out_prev is None
        if cache_prev is not None:
            extra_specs = [pl.BlockSpec(memory_space=pl.ANY)] * 2
            extra_args = list(cache_prev)
            aliases = {5: 1, 6: 2}
        c_spec = pl.BlockSpec((1, 1, lk, D_MODEL), lambda b, i: (b, cache_layer, 0, 0))
        c_shape = jax.ShapeDtypeStruct((nb, DEPTH // 2, lk, D_MODEL), F32)
        o_spec, o_shape = [o_spec, c_spec, c_spec], [o_shape, c_shape, c_shape]
    q_spec = pl.BlockSpec((tq, D_MODEL), lambda b, i: (qoff + b * nq + i, q_col))
    if batched_kv:
        k_spec = pl.BlockSpec((1, lk, D_MODEL), lambda b, i: (b, 0, 0))
        v_spec = k_spec
    else:
        k_spec = pl.BlockSpec((lk, D_MODEL), lambda b, i: (b, kv_cols[0]))
        v_spec = pl.BlockSpec((lk, D_MODEL), lambda b, i: (b, kv_cols[1]))
    return pl.pallas_call(
        functools.partial(_diff_attn_kernel, lam_init=lam_init, batched_kv=batched_kv, cache_out=cache_out),
        grid=(nb, nq),
        in_specs=[q_spec, k_spec, v_spec,
                  pl.BlockSpec((4, LANES), lambda b, i: (0, 0)),
                  pl.BlockSpec((1, LANES), lambda b, i: (0, 0))] + extra_specs,
        out_specs=o_spec,
        out_shape=o_shape,
        input_output_aliases=aliases,
        compiler_params=_params(("parallel", "arbitrary")),
        name="diff_attention",
    )(q, k, v, lam_vec, subln, *extra_args)


PACK_W = D_MODEL // 4


def _route_block(x, m_ref, g_ref, wr_ref, h_ref, aff_ref):
    h = _normed(x, g_ref, m_ref, SH2, SC2)
    hb = h.astype(BF16)
    hf = hb.astype(F32)
    bits = lax.bitcast_convert_type(hf, jnp.int32)
    for p in range(2):
        lo = lax.shift_right_logical(bits[:, p * PACK_W:(p + 1) * PACK_W], 16)
        hi = bits[:, (2 + p) * PACK_W:(3 + p) * PACK_W] & jnp.int32(-65536)
        h_ref[p] = hi | lo
    hl = (h - hf).astype(BF16)
    wr = wr_ref[...]
    wh = wr.astype(BF16)
    wl = (wr - wh.astype(F32)).astype(BF16)
    logits = (jnp.dot(hb, wh, preferred_element_type=F32) + jnp.dot(hl, wh, preferred_element_type=F32)
              + jnp.dot(hb, wl, preferred_element_type=F32))
    lt = jnp.transpose(logits)[0:N_EXPERTS, :]
    lt = lt - jnp.max(lt, axis=0, keepdims=True)
    e = jnp.exp(lt)
    aff_ref[...] = e / jnp.sum(e, axis=0, keepdims=True)


SC_WINDOW = 128


def _gather_rows(table, idx):
    n = idx.shape[0]
    mesh = plsc.VectorSubcoreMesh(core_axis_name="core", subcore_axis_name="subcore")

    @pl.kernel(out_type=jax.ShapeDtypeStruct((n, PACK_W), table.dtype), mesh=mesh, scratch_types=[])
    def gather_kernel(t_hbm, i_hbm, o_hbm):
        def body(i_vmem, o_vmem):
            pltpu.sync_copy(t_hbm.at[i_vmem.at[0]], o_vmem)

        pltpu.emit_pipeline(
            body,
            grid=(n // SC_WINDOW,),
            in_specs=[pl.BlockSpec((1, SC_WINDOW), index_map=lambda i: (0, i))],
            out_specs=[pl.BlockSpec((SC_WINDOW, PACK_W), index_map=lambda i: (i, 0))],
            core_axis_name=("core", "subcore"),
            dimension_semantics=(pltpu.PARALLEL,),
        )(i_hbm, o_hbm)

    return gather_kernel(table, idx.reshape(1, n))


FFN_TF = 512
FFN_TR = 768


def _unpack_rows(pa, pb):
    def lo(w):
        return lax.bitcast_convert_type(lax.shift_left(w, 16), F32).astype(BF16)

    def hi(w):
        return lax.bitcast_convert_type(w & jnp.int32(-65536), F32).astype(BF16)

    return jnp.concatenate([lo(pa), lo(pb), hi(pa), hi(pb)], axis=1)


def _expert_ffn_kernel(xs_ref, wg_ref, wu_ref, wd_ref, gate_ref, m_ref, o_ref, xs_sc):
    f = pl.program_id(1)

    @pl.when(f == 0)
    def _():
        for r0 in range(0, CAP_T, FFN_TR):
            xs_sc[r0:r0 + FFN_TR, :] = _unpack_rows(xs_ref[0, 0, r0:r0 + FFN_TR, :], xs_ref[1, 0, r0:r0 + FFN_TR, :])

    wg = wg_ref[0].astype(BF16)
    wu = wu_ref[0].astype(BF16)
    wd = wd_ref[0].astype(BF16)
    for r0 in range(0, CAP_T, FFN_TR):
        rs = slice(r0, r0 + FFN_TR)
        xs = xs_sc[rs, :]
        a = jnp.dot(xs, wg, preferred_element_type=F32)
        u = jnp.dot(xs, wu, preferred_element_type=F32)
        hid = (a * (1.0 / (1.0 + jnp.exp(-a))) * u).astype(BF16)
        y = jnp.dot(hid, wd, preferred_element_type=F32)

        @pl.when(f == 0)
        def _():
            o_ref[0, rs, :] = y

        @pl.when(jnp.logical_and(f > 0, f < pl.num_programs(1) - 1))
        def _():
            o_ref[0, rs, :] += y

        @pl.when(f == pl.num_programs(1) - 1)
        def _():
            gg = gate_ref[0, rs, :]
            scale = gg[:, 0:1] * m_ref[0, G2:G2 + 1, :]
            for g in range(1, N_GROUPS):
                scale = scale + gg[:, g:g + 1] * m_ref[g, G2:G2 + 1, :]
            o_ref[0, rs, :] = (o_ref[0, rs, :] + y) * scale


def _expert_ffn(xs, wg, wu, wd, gate, mods_l, l):
    return pl.pallas_call(
        _expert_ffn_kernel,
        grid=(N_EXPERTS, EXPERT_FF // FFN_TF),
        in_specs=[
            pl.BlockSpec((2, 1, CAP_T, PACK_W), lambda e, f: (0, e, 0, 0)),
            pl.BlockSpec((None, 1, D_MODEL, FFN_TF), lambda e, f: (l, e, 0, f)),
            pl.BlockSpec((None, 1, D_MODEL, FFN_TF), lambda e, f: (l, e, 0, f)),
            pl.BlockSpec((None, 1, FFN_TF, D_MODEL), lambda e, f: (l, e, f, 0)),
            pl.BlockSpec((1, CAP_T, N_GROUPS), lambda e, f: (e, 0, 0)),
            pl.BlockSpec((N_GROUPS, 6, D_MODEL), lambda e, f: (0, 0, 0)),
        ],
        out_specs=pl.BlockSpec((1, CAP_T, D_MODEL), lambda e, f: (e, 0, 0)),
        out_shape=jax.ShapeDtypeStruct((N_EXPERTS, CAP_T, D_MODEL), F32),
        scratch_shapes=[pltpu.VMEM((CAP_T, D_MODEL), BF16)],
        compiler_params=_params(("parallel", "arbitrary")),
        name="expert_ffn",
    )(xs, wg, wu, wd, gate, mods_l)


def _final_norm_kernel(x_ref, g_ref, o_ref):
    x = x_ref[...]
    ms = jnp.mean(x * x, axis=-1, keepdims=True)
    o_ref[...] = x * lax.rsqrt(ms + EPS) * g_ref[...]


def _final_norm(x, g, row0, n_rows, tm=512):
    off = row0 // tm
    return pl.pallas_call(
        _final_norm_kernel,
        grid=(n_rows // tm,),
        in_specs=[pl.BlockSpec((tm, D_MODEL), lambda i: (i + off, 0)),
                  pl.BlockSpec((1, D_MODEL), lambda i: (0, 0))],
        out_specs=pl.BlockSpec((tm, D_MODEL), lambda i: (i, 0)),
        out_shape=jax.ShapeDtypeStruct((n_rows, D_MODEL), F32),
        compiler_params=_params(("parallel",)),
        name="final_norm",
    )(x, g.reshape(1, D_MODEL))


def _even_layer(x, mods_l, i, state_ret, st_prev, dfts, norm1_g, w_in_even, hy_short_w, hy_short_b, hy_f1_w, hy_f1_b,
                hy_freq1, hy_f2_w, hy_f2_b, hy_freq2, hy_f3_w, hy_bias, ret_decay, w_out_even, norm2_g, w_router):
    proj = _norm_project(x, mods_l, norm1_g, _weight_bf16(w_in_even, i))
    sw, sbias = hy_short_w[i], hy_short_b[i].reshape(1, 3 * HY_W)
    hb = hy_bias[i]
    y_hy = None
    for (row0, n_rows, L, nseq) in ((0, N_PROMPT, SEQ, 8), (N_PROMPT, N_SAMPLE, DEC_SEQ, 1)):
        s, d = _hyena_filter_taps(L, hy_f1_w[i], hy_f1_b[i], hy_freq1[i], hy_f2_w[i], hy_f2_b[i], hy_freq2[i], hy_f3_w[i])
        spectra = _filter_spectra(s, d, L, dfts[L])
        wv, bv = sw[:, 0:HY_W], sbias[:, 0:HY_W]
        w1, b1 = sw[:, HY_W:2 * HY_W], sbias[:, HY_W:2 * HY_W]
        w2, b2 = sw[:, 2 * HY_W:], sbias[:, 2 * HY_W:]
        z1 = _hyena_conv(proj, 0, proj, 1, row0, n_rows, L, nseq, spectra, 0, wv, bv, w1, b1, hb[0:1], True, dfts[L])
        y_hy = _hyena_conv(z1, 0, proj, 2, row0, n_rows, L, nseq, spectra, 1, wv, bv, w2, b2, hb[1:2], False, dfts[L],
                           full_out=True, out_prev=y_hy)
    dl = jnp.broadcast_to(ret_decay[i].astype(F32)[:, :, None], (2, RET_HEADS, LANES))
    y_ret, st = _retention(proj, dl, i, 0, BATCH, SEQ, st_prev=st_prev)
    y_ret = _retention(proj, dl, i, N_PROMPT, DEC_BATCH, DEC_SEQ, s0=state_ret, y_prev=y_ret)
    wo = _weight_bf16(w_out_even, i)
    routed = _project_residual([y_hy, y_ret], [wo[:HY_W], wo[HY_W:]], x, mods_l, G1, norm2_g, w_router)
    return routed, st


def _odd_layer(x, mods_l, l, i, cache_k, cache_v, kv_prev, norm1_g, w_in_odd, lam_q1, lam_k1, lam_q2, lam_k2, subln_g,
               w_out_odd, norm2_g, w_router):
    lam_init = 0.8 - 0.6 * math.exp(-0.3 * l)
    proj = _norm_project(x, mods_l, norm1_g, _weight_bf16(w_in_odd, i))
    lam_vec = jnp.pad(jnp.stack([lam_q1[i], lam_k1[i], lam_q2[i], lam_k2[i]]), ((0, 0), (0, LANES - DIFF_HEAD_DIM)))
    sg = subln_g[i].reshape(1, LANES)
    o_p, kc, vc = _diff_attention(proj, proj, proj, lam_vec, sg, lam_init, nb=BATCH, lq=SEQ, lk=SEQ, tq=SEQ,
                                  q_row0=0, q_col=0, kv_cols=(1, 2), batched_kv=False,
                                  cache_layer=i, cache_prev=kv_prev)
    qs, ks, vs = _sample_qkv(proj, cache_k, cache_v, i)
    o = _diff_attention(qs, ks, vs, lam_vec, sg, lam_init, nb=DEC_BATCH, lq=DEC_SEQ, lk=PAST_LEN + DEC_SEQ, tq=256,
                        q_row0=0, q_col=0, kv_cols=None, batched_kv=True, out_row0=N_PROMPT, out_prev=o_p)
    routed = _project_residual([o], [_weight_bf16(w_out_odd, i)], x, mods_l, G1, norm2_g, w_router)
    return routed, (kc, vc)


def _moe_layer(routed, mods_l, l, wg, wu, wd):
    x, h, aff = routed
    gate_p, idx_p = lax.top_k(aff[:, :N_PROMPT], CAP_P)
    gate_s, idx_s = lax.top_k(aff[:, N_PROMPT:], CAP_S)
    idx = jnp.concatenate([idx_p, idx_s + N_PROMPT], axis=1)
    gate = jnp.concatenate([gate_p, gate_s], axis=1)
    flat = idx.reshape(-1)
    xs = _gather_rows(h.reshape(2 * N_TOK, PACK_W), jnp.concatenate([flat, flat + N_TOK]))
    xs = xs.reshape(2, N_EXPERTS, CAP_T, PACK_W)
    grp = jnp.where(idx < N_PROMPT, 0, 1 + (idx - N_PROMPT) // DEC_SEQ)
    gate_grp = jnp.where(grp[:, :, None] == jnp.arange(N_GROUPS)[None, None, :], gate[:, :, None], 0.0)
    out = _expert_ffn(xs, wg, wu, wd, gate_grp, mods_l, l)
    return x.at[idx.reshape(-1)].add(out.reshape(-1, D_MODEL))


def kernel(x_prompt, x_sample, state_ret, cache_k, cache_v, c, c_ctx, w_mod, b_mod, norm1_g, norm2_g, w_in_even, hy_short_w, hy_short_b, hy_f1_w, hy_f1_b, hy_freq1, hy_f2_w, hy_f2_b, hy_freq2, hy_f3_w, hy_bias, ret_decay, w_out_even, w_in_odd, lam_q1, lam_k1, lam_q2, lam_k2, subln_g, w_out_odd, moe_router, moe_wg, moe_wu, moe_wd, final_g):
    x = (x_prompt.reshape(N_PROMPT, D_MODEL), x_sample.reshape(N_SAMPLE, D_MODEL))
    mods = _modulation(c, c_ctx, w_mod, b_mod)
    dfts = {L: _dft_bf16(L) for L in (SEQ, DEC_SEQ)}
    state_ret = state_ret.astype(F32)
    st = kv = None
    for l in range(DEPTH):
        i = l // 2
        if l % 2 == 0:
            routed, st = _even_layer(x, mods[l], i, state_ret, st, dfts, norm1_g[l], w_in_even, hy_short_w, hy_short_b,
                                     hy_f1_w, hy_f1_b, hy_freq1, hy_f2_w, hy_f2_b, hy_freq2, hy_f3_w, hy_bias,
                                     ret_decay, w_out_even, norm2_g[l], moe_router[l])
        else:
            routed, kv = _odd_layer(x, mods[l], l, i, cache_k, cache_v, kv, norm1_g[l], w_in_odd, lam_q1, lam_k1,
                                    lam_q2, lam_k2, subln_g, w_out_odd, norm2_g[l], moe_router[l])
        x = _moe_layer(routed, mods[l], l, moe_wg, moe_wu, moe_wd)
    y_prompt = _final_norm(x, final_g, 0, N_PROMPT).reshape(BATCH, SEQ, D_MODEL)
    y_sample = _final_norm(x, final_g, N_PROMPT, N_SAMPLE).reshape(DEC_BATCH, DEC_SEQ, D_MODEL)
    cache_shape = (BATCH, DEPTH // 2, SEQ, DIFF_HEADS, 2 * DIFF_HEAD_DIM)
    return (y_prompt, y_sample, st, kv[0].reshape(cache_shape), kv[1].reshape(cache_shape))
```

```python
import functools
import math

import numpy as np
import jax
import jax.numpy as jnp
from jax import lax
from jax.experimental import pallas as pl
from jax.experimental.pallas import tpu as pltpu
from jax.experimental.pallas import tpu_sc as plsc

F32 = jnp.float32
BF16 = jnp.bfloat16

D_MODEL = 1024
BATCH = 32
SEQ = 256
DEPTH = 4
DEC_BATCH = 2
DEC_SEQ = 2048
PAST_LEN = 256
GRID_W = 64
HY_W = 512
HY_EMB = 33
HY_BANDS = 16
HY_FF = 64
HY_TARGET = 1e-2
HY_FAST = 0.3
HY_SLOW = 1.5
RET_W = 512
RET_HEADS = 4
RET_HEAD_DIM = 128
RET_CHUNK = 128
DIFF_HEADS = 8
DIFF_HEAD_DIM = 64
ROPE_BASE = 10000.0
N_EXPERTS = 16
EC_FACTOR = 2
EXPERT_FF = 1024
EVEN_IN = 3 * HY_W + 4 * RET_W
EPS = 1e-6

N_PROMPT = BATCH * SEQ
N_SAMPLE = DEC_BATCH * DEC_SEQ
N_TOK = N_PROMPT + N_SAMPLE
N_GROUPS = 1 + DEC_BATCH
CAP_P = EC_FACTOR * N_PROMPT // N_EXPERTS
CAP_S = EC_FACTOR * N_SAMPLE // N_EXPERTS
CAP_T = CAP_P + CAP_S

LANES = 128
SUBLANES = 8
VMEM_LIMIT = 56 * 1024 * 1024

SH1, SC1, G1, SH2, SC2, G2 = range(6)


def _params(sem, vmem=VMEM_LIMIT):
    return pltpu.CompilerParams(dimension_semantics=sem, vmem_limit_bytes=vmem)


def _group_of_block(i, tm):
    pb = N_PROMPT // tm
    return jnp.where(i < pb, 0, 1 + (i - pb) // (DEC_SEQ // tm))


MOD_TN = 1024


MOD_UNROLL = 4


def _mod_kernel(cb_ref, w_ref, b_ref, o_ref, a_sc):
    nchunk = MOD_TN // LANES
    cv = cb_ref[...]
    a_sc[...] = cv * (1.0 / (1.0 + jnp.exp(-cv)))

    def body(kb, accs):
        accs = list(accs)
        for u in range(MOD_UNROLL):
            k0 = pl.multiple_of((kb * MOD_UNROLL + u) * SUBLANES, SUBLANES)
            a = [a_sc[r, pl.ds(k0, SUBLANES), :] for r in range(N_GROUPS)]
            for ci in range(nchunk):
                wv = w_ref[0, pl.ds(k0, SUBLANES), ci * LANES:(ci + 1) * LANES]
                for r in range(N_GROUPS):
                    accs[ci * N_GROUPS + r] = accs[ci * N_GROUPS + r] + wv * a[r]
        return tuple(accs)

    init = tuple(jnp.zeros((SUBLANES, LANES), F32) for _ in range(N_GROUPS * nchunk))
    accs = lax.fori_loop(0, D_MODEL // (SUBLANES * MOD_UNROLL), body, init)
    o_ref[...] = jnp.zeros(o_ref.shape, F32)
    for r in range(N_GROUPS):
        for ci in range(nchunk):
            row = jnp.sum(accs[ci * N_GROUPS + r], axis=0, keepdims=True)
            o_ref[0, r:r + 1, ci * LANES:(ci + 1) * LANES] = row + b_ref[0, :, ci * LANES:(ci + 1) * LANES]


def _modulation(c, c_ctx, w_mod, b_mod):
    cond = jnp.concatenate([c_ctx[None, :], c], axis=0)
    cb = jnp.broadcast_to(cond[:, :, None], (N_GROUPS, D_MODEL, LANES))
    out = pl.pallas_call(
        _mod_kernel,
        grid=(DEPTH, 6 * D_MODEL // MOD_TN),
        in_specs=[
            pl.BlockSpec((N_GROUPS, D_MODEL, LANES), lambda l, j: (0, 0, 0)),
            pl.BlockSpec((1, D_MODEL, MOD_TN), lambda l, j: (l, 0, j)),
            pl.BlockSpec((1, 1, MOD_TN), lambda l, j: (l, 0, j)),
        ],
        out_specs=pl.BlockSpec((1, SUBLANES, MOD_TN), lambda l, j: (l, 0, j)),
        out_shape=jax.ShapeDtypeStruct((DEPTH, SUBLANES, 6 * D_MODEL), F32),
        scratch_shapes=[pltpu.VMEM((N_GROUPS, D_MODEL, LANES), F32)],
        compiler_params=_params(("parallel", "parallel")),
        name="ada_mod",
    )(cb, w_mod, b_mod.reshape(DEPTH, 1, 6 * D_MODEL))
    return out[:, :N_GROUPS].reshape(DEPTH, N_GROUPS, 6, D_MODEL)


def _normed(x, g_ref, m_ref, shift, scale):
    ms = jnp.mean(x * x, axis=-1, keepdims=True)
    y = x * lax.rsqrt(ms + EPS) * g_ref[...]
    return y * (1.0 + m_ref[0, scale:scale + 1, :]) + m_ref[0, shift:shift + 1, :]


def _weight_bf16(w, i):
    return w[i].astype(BF16)


def _x_specs(x, tm):
    if not isinstance(x, tuple):
        return [x], [pl.BlockSpec((tm, D_MODEL), lambda i: (i, 0))]
    pb = N_PROMPT // tm
    return list(x), [pl.BlockSpec((tm, D_MODEL), lambda i: (jnp.minimum(i, pb - 1), 0)),
                     pl.BlockSpec((tm, D_MODEL), lambda i: (jnp.maximum(i - pb, 0), 0))]


def _x_block(x_refs, tm):
    if len(x_refs) == 1:
        return x_refs[0][...]
    return jnp.where(pl.program_id(0) < N_PROMPT // tm, x_refs[0][...], x_refs[1][...])


def _norm_mm_kernel(*refs, tn, tm):
    m_ref, g_ref, w_ref, o_ref = refs[-4:]
    h = _normed(_x_block(refs[:-4], tm), g_ref, m_ref, SH1, SC1).astype(BF16)
    for c0 in range(0, o_ref.shape[1], tn):
        o_ref[:, c0:c0 + tn] = jnp.dot(h, w_ref[:, c0:c0 + tn], preferred_element_type=F32)


def _norm_project(x, mods_l, g, w_bf16, tm=512, tn=512):
    nout = w_bf16.shape[1]
    x_args, x_specs = _x_specs(x, tm)
    return pl.pallas_call(
        functools.partial(_norm_mm_kernel, tn=tn, tm=tm),
        grid=(N_TOK // tm,),
        in_specs=x_specs + [
            pl.BlockSpec((1, 6, D_MODEL), lambda i: (_group_of_block(i, tm), 0, 0)),
            pl.BlockSpec((1, D_MODEL), lambda i: (0, 0)),
            pl.BlockSpec((D_MODEL, nout), lambda i: (0, 0)),
        ],
        out_specs=pl.BlockSpec((tm, nout), lambda i: (i, 0)),
        out_shape=jax.ShapeDtypeStruct((N_TOK, nout), F32),
        compiler_params=_params(("parallel",)),
        name="norm_project",
    )(*x_args, mods_l, g.reshape(1, D_MODEL), w_bf16)


def _proj_res_kernel(*refs, n_in, gate, tm):
    a_refs = refs[:n_in]
    w_refs = refs[n_in:2 * n_in]
    x_refs = refs[2 * n_in:-2]
    m_ref, o_ref = refs[-2:]
    acc = None
    for a_ref, w_ref in zip(a_refs, w_refs):
        t = jnp.dot(a_ref[...].astype(BF16), w_ref[...], preferred_element_type=F32)
        acc = t if acc is None else acc + t
    o_ref[...] = _x_block(x_refs, tm) + m_ref[0, gate:gate + 1, :] * acc


def _project_residual(acts, ws_bf16, x, mods_l, gate, tm=512):
    n_in = len(acts)
    x_args, x_specs = _x_specs(x, tm)
    in_specs = [pl.BlockSpec((tm, a.shape[1]), lambda i: (i, 0)) for a in acts]
    in_specs += [pl.BlockSpec(w.shape, lambda i: (0, 0)) for w in ws_bf16]
    in_specs += x_specs + [pl.BlockSpec((1, 6, D_MODEL), lambda i: (_group_of_block(i, tm), 0, 0))]
    return pl.pallas_call(
        functools.partial(_proj_res_kernel, n_in=n_in, gate=gate, tm=tm),
        grid=(N_TOK // tm,),
        in_specs=in_specs,
        out_specs=pl.BlockSpec((tm, D_MODEL), lambda i: (i, 0)),
        out_shape=jax.ShapeDtypeStruct((N_TOK, D_MODEL), F32),
        input_output_aliases={2 * n_in: 0} if len(x_args) == 1 else {},
        compiler_params=_params(("parallel",)),
        name="project_residual",
    )(*acts, *ws_bf16, *x_args, mods_l)


@functools.lru_cache(maxsize=None)
def _dft_mats(L):
    n = 2 * L
    ft = (np.arange(L, dtype=np.int64)[:, None] * np.arange(L, dtype=np.int64)[None, :]) % n
    ang = ft.astype(np.float64) * (2.0 * np.pi / n)
    return np.cos(ang).astype(np.float32), np.sin(ang).astype(np.float32)


def _dft_bf16(L):
    c, s = _dft_mats(L)
    return jnp.asarray(c).astype(BF16), jnp.asarray(s).astype(BF16)


def _alt_sign(shape, row0):
    t = lax.broadcasted_iota(jnp.int32, shape, 0) + row0
    return (1 - 2 * (t & 1)).astype(F32)


def _filter_dft_kernel(s_ref, d_ref, c_ref, sn_ref, ka_ref, ki_ref, kn_ref, *, L, fb):
    f0 = pl.program_id(0) * fb
    n = 2.0 * L
    s = s_ref[...]
    r = jnp.dot(c_ref[...], s, preferred_element_type=F32)
    im = jnp.dot(sn_ref[...], d_ref[...], preferred_element_type=F32)
    fidx = lax.broadcasted_iota(jnp.int32, r.shape, 0) + f0
    scale = jnp.where(fidx == 0, 1.0 / n, 2.0 / n)
    ka_ref[...] = r * scale
    ki_ref[...] = im * (2.0 / n)
    nyq = jnp.sum(s.astype(F32) * _alt_sign(s.shape, 0), axis=0, keepdims=True) * (1.0 / n)
    kn_ref[...] = jnp.broadcast_to(nyq, kn_ref.shape)


def _filter_spectra(s, d, L, dft):
    fb = min(L, 512)
    cmat, smat = dft
    w = 2 * HY_W
    return pl.pallas_call(
        functools.partial(_filter_dft_kernel, L=L, fb=fb),
        grid=(L // fb,),
        in_specs=[
            pl.BlockSpec((L, w), lambda f: (0, 0)),
            pl.BlockSpec((L, w), lambda f: (0, 0)),
            pl.BlockSpec((fb, L), lambda f: (f, 0)),
            pl.BlockSpec((fb, L), lambda f: (f, 0)),
        ],
        out_specs=[
            pl.BlockSpec((fb, w), lambda f: (f, 0)),
            pl.BlockSpec((fb, w), lambda f: (f, 0)),
            pl.BlockSpec((SUBLANES, w), lambda f: (0, 0)),
        ],
        out_shape=[
            jax.ShapeDtypeStruct((L, w), F32),
            jax.ShapeDtypeStruct((L, w), F32),
            jax.ShapeDtypeStruct((SUBLANES, w), F32),
        ],
        compiler_params=_params(("arbitrary",)),
        name="hyena_filter_dft",
    )(s.astype(BF16), d.astype(BF16), cmat, smat)


HY_TILE = 256


def _short_conv_tile(ref, r0, L, w_ref, b_ref):
    t = HY_TILE
    cur = ref[r0:r0 + t, :]
    rid = lax.broadcasted_iota(jnp.int32, cur.shape, 0)
    if r0 % L == 0:
        prev = jnp.where(rid == 0, 0.0, pltpu.roll(cur, 1, axis=0))
    else:
        prev = ref[r0 - 1:r0 - 1 + t, :]
    if (r0 + t) % L == 0:
        nxt = jnp.where(rid == t - 1, 0.0, pltpu.roll(cur, t - 1, axis=0))
    else:
        nxt = ref[r0 + 1:r0 + 1 + t, :]
    return prev * w_ref[0:1, :] + cur * w_ref[1:2, :] + nxt * w_ref[2:3, :] + b_ref[...]


def _hyena_conv_kernel(*refs, L, nseq, conv_a):
    (a_ref, x_ref, cr_ref, sr_ref, cc_ref, sc_ref, ka_ref, ki_ref, kn_ref,
     wa_ref, ba_ref, wx_ref, bx_ref, hb_ref) = refs[:14]
    o_ref, z_sc, acc_sc = refs[-3:]
    f = pl.program_id(1)
    nf = pl.num_programs(1)
    rows = nseq * L

    @pl.when(f == 0)
    def _():
        for q in range(nseq):
            nyq = jnp.zeros((1, HY_W), F32)
            for r0 in range(q * L, (q + 1) * L, HY_TILE):
                if conv_a:
                    zt = _short_conv_tile(a_ref, r0, L, wa_ref, ba_ref)
                else:
                    zt = a_ref[r0:r0 + HY_TILE, :]
                z_sc[r0:r0 + HY_TILE, :] = zt
                nyq = nyq + jnp.sum(zt * _alt_sign(zt.shape, r0), axis=0, keepdims=True)
            nyq = nyq * kn_ref[0:1, :]
            for r0 in range(q * L, (q + 1) * L, HY_TILE):
                acc_sc[r0:r0 + HY_TILE, :] = _alt_sign((HY_TILE, HY_W), r0) * nyq

    ka = ka_ref[...]
    ki = ki_ref[...]
    for q in range(nseq):
        z = z_sc[q * L:(q + 1) * L, :].astype(BF16)
        a = jnp.dot(cr_ref[...], z, preferred_element_type=F32)
        b = jnp.dot(sr_ref[...], z, preferred_element_type=F32)
        p = (a * ka + b * ki).astype(BF16)
        qq = (b * ka - a * ki).astype(BF16)
        acc_sc[q * L:(q + 1) * L, :] += (jnp.dot(cc_ref[...], p, preferred_element_type=F32)
                                         + jnp.dot(sc_ref[...], qq, preferred_element_type=F32))

    @pl.when(f == nf - 1)
    def _():
        for r0 in range(0, rows, HY_TILE):
            y = acc_sc[r0:r0 + HY_TILE, :] + z_sc[r0:r0 + HY_TILE, :] * hb_ref[...]
            o_ref[r0:r0 + HY_TILE, :] = (y * _short_conv_tile(x_ref, r0, L, wx_ref, bx_ref)).astype(o_ref.dtype)


def _hyena_conv(a, a_col, x, x_col, row0, n_rows, L, nseq, spectra, filt, wa, ba, wx, bx, hbias, conv_a, dft,
                full_out=False, out_prev=None):
    fb = min(L, 256)
    cmat, smat = dft
    ka, ki, kn = spectra
    rb = nseq * L
    a_off = row0 // rb if a.shape[0] != n_rows else 0
    x_off = row0 // rb
    o_off = row0 // rb if full_out else 0
    extra_specs, extra_args, aliases = [], [], {}
    if out_prev is not None:
        extra_specs, extra_args, aliases = [pl.BlockSpec(memory_space=pl.ANY)], [out_prev], {14: 0}
    return pl.pallas_call(
        functools.partial(_hyena_conv_kernel, L=L, nseq=nseq, conv_a=conv_a),
        grid=(n_rows // rb, L // fb),
        input_output_aliases=aliases,
        in_specs=extra_specs[:0] + [
            pl.BlockSpec((rb, HY_W), lambda i, f: (i + a_off, a_col)),
            pl.BlockSpec((rb, HY_W), lambda i, f: (i + x_off, x_col)),
            pl.BlockSpec((fb, L), lambda i, f: (f, 0)),
            pl.BlockSpec((fb, L), lambda i, f: (f, 0)),
            pl.BlockSpec((L, fb), lambda i, f: (0, f)),
            pl.BlockSpec((L, fb), lambda i, f: (0, f)),
            pl.BlockSpec((fb, HY_W), lambda i, f: (f, filt)),
            pl.BlockSpec((fb, HY_W), lambda i, f: (f, filt)),
            pl.BlockSpec((SUBLANES, HY_W), lambda i, f: (0, filt)),
            pl.BlockSpec((3, HY_W), lambda i, f: (0, 0)),
            pl.BlockSpec((1, HY_W), lambda i, f: (0, 0)),
            pl.BlockSpec((3, HY_W), lambda i, f: (0, 0)),
            pl.BlockSpec((1, HY_W), lambda i, f: (0, 0)),
            pl.BlockSpec((1, HY_W), lambda i, f: (0, 0)),
        ] + extra_specs,
        out_specs=pl.BlockSpec((rb, HY_W), lambda i, f: (i + o_off, 0)),
        out_shape=jax.ShapeDtypeStruct((N_TOK, HY_W), BF16) if full_out else jax.ShapeDtypeStruct((n_rows, HY_W), F32),
        scratch_shapes=[pltpu.VMEM((rb, HY_W), F32), pltpu.VMEM((rb, HY_W), F32)],
        compiler_params=_params(("parallel", "arbitrary")),
        name="hyena_conv",
    )(a, x, cmat, smat, cmat, smat, ka, ki, kn, wa, ba, wx, bx, hbias, *extra_args)


def _hyena_filter_taps(L, f1w, f1b, fr1, f2w, f2b, fr2, f3w):
    hp = lax.Precision.HIGHEST
    pos = jnp.arange(L, dtype=F32)
    t = pos / (L - 1)
    w = 2.0 * math.pi * pos / L
    f = jnp.linspace(1e-4, HY_BANDS - 1, HY_BANDS, dtype=F32)
    wf = w[:, None] * f[None, :]
    feat = jnp.concatenate([t[:, None], jnp.cos(wf), -jnp.sin(wf)], axis=-1)
    h = jnp.sin(fr1 * (jnp.dot(feat, f1w, precision=hp) + f1b))
    h = jnp.sin(fr2 * (jnp.dot(h, f2w, precision=hp) + f2b))
    h = jnp.dot(h, f3w, precision=hp).astype(F32)
    deltas = jnp.linspace(math.log(HY_TARGET) / HY_SLOW, math.log(HY_TARGET) / HY_FAST, HY_W, dtype=F32)
    window = jnp.exp(-t[:, None] * jnp.abs(deltas)[None, :])
    w = HY_W
    parts = [h[:, k * w:(k + 1) * w] * window for k in range(4)]
    colsum = [jnp.sum(jnp.abs(p), axis=0, keepdims=True) for p in parts]
    den = [colsum[0] + colsum[1] + EPS, colsum[2] + colsum[3] + EPS]
    fwd = jnp.concatenate([parts[0] / den[0], parts[2] / den[1]], axis=1)
    bwd = jnp.concatenate([parts[1] / den[0], parts[3] / den[1]], axis=1)
    bwd = jnp.where(pos[:, None] == 0, 0.0, bwd)
    return fwd + bwd, bwd - fwd


def _dot_t0(a, b):
    return lax.dot_general(a, b, (((0,), (0,)), ((), ())), preferred_element_type=F32)


def _dot_t1(a, b):
    return lax.dot_general(a, b, (((1,), (1,)), ((), ())), preferred_element_type=F32)


def _retention_kernel(*refs, L, has_s0, has_prev):
    refs = list(refs)
    q_ref, k_ref, v_ref, g_ref, dl_ref = refs[:5]
    pos = 5
    s0_ref = st_ref = None
    if has_s0:
        s0_ref = refs[pos]
        pos += 1
    if has_prev:
        pos += 1
    y_ref = refs[pos]
    pos += 1
    if not has_s0:
        st_ref = refs[pos]
        pos += 1
    sb_sc, sf_cur, sb_cur = refs[pos:]
    c = RET_CHUNK
    nc = L // c
    kscale = RET_HEAD_DIM ** -0.5
    ri = lax.broadcasted_iota(jnp.int32, (c, c), 0).astype(F32)
    ci = lax.broadcasted_iota(jnp.int32, (c, c), 1).astype(F32)
    diff = ri - ci
    dec = []
    for h in range(RET_HEADS):
        xf = dl_ref[0, h:h + 1, :]
        xb = dl_ref[1, h:h + 1, :]
        lgf = jnp.minimum(xf, 0.0) - jnp.log1p(jnp.exp(-jnp.abs(xf)))
        lgb = jnp.minimum(xb, 0.0) - jnp.log1p(jnp.exp(-jnp.abs(xb)))
        dec.append(dict(
            mask=(jnp.where(diff >= 0, jnp.exp(lgf * jnp.maximum(diff, 0.0)), 0.0)
                  + jnp.where(diff <= 0, jnp.exp(lgb * jnp.maximum(-diff, 0.0)), 0.0)),
            qdec_f=jnp.exp(lgf * (ri + 1.0)), kdec_f=jnp.exp(lgf * (c - 1.0 - ri)),
            qdec_b=jnp.exp(lgb * (c - ri)), kdec_b=jnp.exp(lgb * ri),
            cd_f=jnp.exp(lgf * c), cd_b=jnp.exp(lgb * c)))
        if has_s0:
            sf_cur[h] = s0_ref[0, 0, 0, h]
            sb_cur[h] = s0_ref[0, 0, 1, h]
        else:
            sf_cur[h] = jnp.zeros((RET_HEAD_DIM, RET_HEAD_DIM), F32)
            sb_cur[h] = jnp.zeros((RET_HEAD_DIM, RET_HEAD_DIM), F32)

    def bwd_body(i, carry):
        j = nc - 1 - i
        r0 = pl.multiple_of(j * c, c)
        for h in range(RET_HEADS):
            hs = slice(h * RET_HEAD_DIM, (h + 1) * RET_HEAD_DIM)
            sb = sb_cur[h]
            sb_sc[h * nc + j] = sb
            kc = k_ref[pl.ds(r0, c), hs] * kscale
            vc = v_ref[pl.ds(r0, c), hs]
            sb_cur[h] = sb * dec[h]["cd_b"] + _dot_t0((kc * dec[h]["kdec_b"]).astype(BF16), vc.astype(BF16))
        return carry

    lax.fori_loop(0, nc, bwd_body, 0)

    def fwd_body(j, carry):
        r0 = pl.multiple_of(j * c, c)
        for h in range(RET_HEADS):
            hs = slice(h * RET_HEAD_DIM, (h + 1) * RET_HEAD_DIM)
            dh = dec[h]
            sf = sf_cur[h]
            qc = q_ref[pl.ds(r0, c), hs]
            kc = k_ref[pl.ds(r0, c), hs] * kscale
            vc = v_ref[pl.ds(r0, c), hs].astype(BF16)
            scores = _dot_t1(qc.astype(BF16), kc.astype(BF16)) * dh["mask"]
            o = jnp.dot(scores.astype(BF16), vc, preferred_element_type=F32)
            o = o + jnp.dot((qc * dh["qdec_f"]).astype(BF16), sf.astype(BF16), preferred_element_type=F32)
            o = o + jnp.dot((qc * dh["qdec_b"]).astype(BF16), sb_sc[h * nc + j].astype(BF16),
                            preferred_element_type=F32)
            mu = jnp.mean(o, axis=-1, keepdims=True)
            var = jnp.mean(jnp.square(o - mu), axis=-1, keepdims=True)
            on = (o - mu) * lax.rsqrt(var + EPS)
            gc = g_ref[pl.ds(r0, c), hs]
            y_ref[pl.ds(r0, c), hs] = (gc * (1.0 / (1.0 + jnp.exp(-gc))) * on).astype(y_ref.dtype)
            sf_cur[h] = sf * dh["cd_f"] + _dot_t0((kc * dh["kdec_f"]).astype(BF16), vc)
        return carry

    lax.fori_loop(0, nc, fwd_body, 0)
    if st_ref is not None:
        for h in range(RET_HEADS):
            st_ref[0, 0, 0, h] = sf_cur[h]
            st_ref[0, 0, 1, h] = sb_cur[h]


def _retention(proj, dl, layer_i, row0, nseq, L, s0=None, st_prev=None, y_prev=None):
    off = row0 // L
    has_s0 = s0 is not None
    has_prev = (st_prev is not None) or (y_prev is not None)
    assert not (st_prev is not None and y_prev is not None)
    nc = L // RET_CHUNK
    n_ret = (DEPTH + 1) // 2
    col = lambda j: pl.BlockSpec((L, RET_W), lambda b: (b + off, 3 + j))
    in_specs = [col(0), col(1), col(2), col(3),
                pl.BlockSpec((2, RET_HEADS, LANES), lambda b: (0, 0, 0))]
    args = [proj, proj, proj, proj, dl]
    y_spec = pl.BlockSpec((L, RET_W), lambda b: (b + off, 0))
    y_shape = jax.ShapeDtypeStruct((N_TOK, RET_W), BF16)
    st_block = (1, 1, 2, RET_HEADS, RET_HEAD_DIM, RET_HEAD_DIM)
    st_spec = pl.BlockSpec(st_block, lambda b: (b, layer_i, 0, 0, 0, 0))
    aliases = {}
    if has_s0:
        in_specs.append(st_spec)
        args.append(s0)
        out_specs, out_shape = y_spec, y_shape
    else:
        out_specs = [y_spec, st_spec]
        out_shape = [y_shape, jax.ShapeDtypeStruct((nseq, n_ret) + st_block[2:], F32)]
    if has_prev:
        aliases = {len(args): 1 if st_prev is not None else 0}
        in_specs.append(pl.BlockSpec(memory_space=pl.ANY))
        args.append(st_prev if st_prev is not None else y_prev)
    state = pltpu.VMEM((RET_HEADS, RET_HEAD_DIM, RET_HEAD_DIM), F32)
    return pl.pallas_call(
        functools.partial(_retention_kernel, L=L, has_s0=has_s0, has_prev=has_prev),
        grid=(nseq,),
        in_specs=in_specs,
        out_specs=out_specs,
        out_shape=out_shape,
        input_output_aliases=aliases,
        scratch_shapes=[pltpu.VMEM((RET_HEADS * nc, RET_HEAD_DIM, RET_HEAD_DIM), F32), state, state],
        compiler_params=_params(("parallel",)),
        name="retention",
    )(*args)


@functools.lru_cache(maxsize=None)
def _rope_tables():
    L = DEC_SEQ
    rows = L // GRID_W
    row = np.repeat(np.arange(rows, dtype=np.float64), GRID_W)
    col = np.tile(np.arange(GRID_W, dtype=np.float64), rows)
    quarter = DIFF_HEAD_DIM // 4
    freqs = ROPE_BASE ** (-np.arange(quarter, dtype=np.float64) / quarter)
    j = np.arange(LANES)
    pos = np.where(((j % DIFF_HEAD_DIM) < DIFF_HEAD_DIM // 2)[None, :], row[:, None], col[:, None])
    ang = pos * freqs[j % quarter][None, :]
    cos = np.cos(ang).astype(np.float32)
    sin = np.sin(ang).astype(np.float32)
    first = ((j % (2 * quarter)) < quarter)[None, :]
    sin_a = np.where(first, -sin, 0.0).astype(np.float32)
    sin_b = np.where(first, 0.0, sin).astype(np.float32)
    return jnp.asarray(cos), jnp.asarray(sin_a), jnp.asarray(sin_b)


def _rope_head(x, cos, sin_a, sin_b):
    quarter = DIFF_HEAD_DIM // 4
    up = pltpu.roll(x, LANES - quarter, axis=1)
    dn = pltpu.roll(x, quarter, axis=1)
    return x * cos + up * sin_a + dn * sin_b


def _kv_prep_kernel(q_ref, k_ref, v_ref, cos_ref, sa_ref, sb_ref, qo_ref, ko_ref, vo_ref):
    cos, sa, sb = cos_ref[...], sa_ref[...], sb_ref[...]
    for h in range(DIFF_HEADS):
        hs = slice(h * LANES, (h + 1) * LANES)
        qo_ref[:, hs] = _rope_head(q_ref[:, hs], cos, sa, sb).astype(BF16)
        ko_ref[0, :, hs] = _rope_head(k_ref[:, hs], cos, sa, sb).astype(BF16)
    vo_ref[0] = v_ref[...].astype(BF16)


def _cache_copy_kernel(ck_ref, cv_ref, k_in, v_in, ko_ref, vo_ref):
    del k_in, v_in
    ko_ref[0] = ck_ref[0, 0].astype(BF16)
    vo_ref[0] = cv_ref[0, 0].astype(BF16)


def _sample_qkv(proj, cache_k, cache_v, layer_i, tm=256):
    cos, sa, sb = _rope_tables()
    lk = PAST_LEN + DEC_SEQ
    pblk = N_PROMPT // tm
    nblk = DEC_SEQ // tm
    cblk = PAST_LEN // tm
    tab = pl.BlockSpec((tm, LANES), lambda b, i: (i, 0))
    q, k, v = pl.pallas_call(
        _kv_prep_kernel,
        grid=(DEC_BATCH, nblk),
        in_specs=[
            pl.BlockSpec((tm, D_MODEL), lambda b, i: (pblk + b * nblk + i, 0)),
            pl.BlockSpec((tm, D_MODEL), lambda b, i: (pblk + b * nblk + i, 1)),
            pl.BlockSpec((tm, D_MODEL), lambda b, i: (pblk + b * nblk + i, 2)),
            tab, tab, tab,
        ],
        out_specs=[
            pl.BlockSpec((tm, D_MODEL), lambda b, i: (b * nblk + i, 0)),
            pl.BlockSpec((1, tm, D_MODEL), lambda b, i: (b, cblk + i, 0)),
            pl.BlockSpec((1, tm, D_MODEL), lambda b, i: (b, cblk + i, 0)),
        ],
        out_shape=[
            jax.ShapeDtypeStruct((N_SAMPLE, D_MODEL), BF16),
            jax.ShapeDtypeStruct((DEC_BATCH, lk, D_MODEL), BF16),
            jax.ShapeDtypeStruct((DEC_BATCH, lk, D_MODEL), BF16),
        ],
        compiler_params=_params(("parallel", "parallel")),
        name="rope_qkv",
    )(proj, proj, proj, cos, sa, sb)
    n_att = DEPTH // 2
    ck = cache_k.reshape(DEC_BATCH, n_att, PAST_LEN, D_MODEL)
    cv = cache_v.reshape(DEC_BATCH, n_att, PAST_LEN, D_MODEL)
    k, v = pl.pallas_call(
        _cache_copy_kernel,
        grid=(DEC_BATCH,),
        in_specs=[
            pl.BlockSpec((1, 1, PAST_LEN, D_MODEL), lambda b: (b, layer_i, 0, 0)),
            pl.BlockSpec((1, 1, PAST_LEN, D_MODEL), lambda b: (b, layer_i, 0, 0)),
            pl.BlockSpec(memory_space=pl.ANY),
            pl.BlockSpec(memory_space=pl.ANY),
        ],
        out_specs=[
            pl.BlockSpec((1, PAST_LEN, D_MODEL), lambda b: (b, 0, 0)),
            pl.BlockSpec((1, PAST_LEN, D_MODEL), lambda b: (b, 0, 0)),
        ],
        out_shape=[
            jax.ShapeDtypeStruct((DEC_BATCH, lk, D_MODEL), BF16),
            jax.ShapeDtypeStruct((DEC_BATCH, lk, D_MODEL), BF16),
        ],
        input_output_aliases={2: 0, 3: 1},
        compiler_params=_params(("parallel",)),
        name="cache_prepend",
    )(ck, cv, k, v)
    return q, k, v


def _diff_attn_kernel(*refs, lam_init, batched_kv, cache_out):
    q_ref, k_ref, v_ref, lam_ref, sg_ref = refs[:5]
    if cache_out:
        o_ref, kc_ref, vc_ref = refs[-3:]
        kc_ref[0, 0] = k_ref[...]
        vc_ref[0, 0] = v_ref[...]
    else:
        o_ref = refs[-1]
    lv = lam_ref[...]
    lam = (jnp.exp(jnp.sum(lv[0:1] * lv[1:2], axis=-1, keepdims=True))
           - jnp.exp(jnp.sum(lv[2:3] * lv[3:4], axis=-1, keepdims=True)) + lam_init)
    lane = lax.broadcasted_iota(jnp.int32, (1, LANES), 1)
    m1 = (lane < DIFF_HEAD_DIM).astype(F32)
    m2 = 1.0 - m1
    scale = DIFF_HEAD_DIM ** -0.5
    for h in range(DIFF_HEADS):
        hs = slice(h * LANES, (h + 1) * LANES)
        q = q_ref[:, hs].astype(F32) * scale
        if batched_kv:
            k = k_ref[0, :, hs].astype(BF16)
            v = v_ref[0, :, hs].astype(BF16)
        else:
            k = k_ref[:, hs].astype(BF16)
            v = v_ref[:, hs].astype(BF16)
        v_ext = jnp.concatenate([v, jnp.ones_like(v)], axis=1)
        outs = []
        for m in (m1, m2):
            s = _dot_t1((q * m).astype(BF16), k)
            s = s - jnp.max(s, axis=-1, keepdims=True)
            pv = jnp.dot(jnp.exp(s).astype(BF16), v_ext, preferred_element_type=F32)
            outs.append(pv[:, :LANES] / pv[:, LANES:])
        o = outs[0] - lam * outs[1]
        ms = jnp.mean(o * o, axis=-1, keepdims=True)
        o_ref[:, hs] = (o * lax.rsqrt(ms + EPS) * sg_ref[...] * (1.0 - lam_init)).astype(o_ref.dtype)


def _diff_attention(q, k, v, lam_vec, subln, lam_init, *, nb, lq, lk, tq, q_row0, q_col, kv_cols, batched_kv,
                    out_row0=0, out_prev=None, cache_layer=None, cache_prev=None):
    nq = lq // tq
    qoff = q_row0 // tq
    ooff = out_row0 // tq
    cache_out = cache_layer is not None
    extra_specs, extra_args, aliases = [], [], {}
    if out_prev is not None:
        extra_specs, extra_args, aliases = [pl.BlockSpec(memory_space=pl.ANY)], [out_prev], {5: 0}
    o_spec = pl.BlockSpec((tq, D_MODEL), lambda b, i: (ooff + b * nq + i, 0))
    o_shape = jax.ShapeDtypeStruct((N_TOK, D_MODEL), BF16)
    if cache_out:
        assert not batched_kv and nq == 1 and out_prev is None
        if cache_prev is not None:
            extra_specs = [pl.BlockSpec(memory_space=pl.ANY)] * 2
            extra_args = list(cache_prev)
            aliases = {5: 1, 6: 2}
        c_spec = pl.BlockSpec((1, 1, lk, D_MODEL), lambda b, i: (b, cache_layer, 0, 0))
        c_shape = jax.ShapeDtypeStruct((nb, DEPTH // 2, lk, D_MODEL), F32)
        o_spec, o_shape = [o_spec, c_spec, c_spec], [o_shape, c_shape, c_shape]
    q_spec = pl.BlockSpec((tq, D_MODEL), lambda b, i: (qoff + b * nq + i, q_col))
    if batched_kv:
        k_spec = pl.BlockSpec((1, lk, D_MODEL), lambda b, i: (b, 0, 0))
        v_spec = k_spec
    else:
        k_spec = pl.BlockSpec((lk, D_MODEL), lambda b, i: (b, kv_cols[0]))
        v_spec = pl.BlockSpec((lk, D_MODEL), lambda b, i: (b, kv_cols[1]))
    return pl.pallas_call(
        functools.partial(_diff_attn_kernel, lam_init=lam_init, batched_kv=batched_kv, cache_out=cache_out),
        grid=(nb, nq),
        in_specs=[q_spec, k_spec, v_spec,
                  pl.BlockSpec((4, LANES), lambda b, i: (0, 0)),
                  pl.BlockSpec((1, LANES), lambda b, i: (0, 0))] + extra_specs,
        out_specs=o_spec,
        out_shape=o_shape,
        input_output_aliases=aliases,
        compiler_params=_params(("parallel", "arbitrary")),
        name="diff_attention",
    )(q, k, v, lam_vec, subln, *extra_args)


PACK_W = D_MODEL // 4


def _router_kernel(x_ref, m_ref, g_ref, wr_ref, h_ref, aff_ref):
    h = _normed(x_ref[...], g_ref, m_ref, SH2, SC2)
    hb = h.astype(BF16)
    hf = hb.astype(F32)
    bits = lax.bitcast_convert_type(hf, jnp.int32)
    for p in range(2):
        lo = lax.shift_right_logical(bits[:, p * PACK_W:(p + 1) * PACK_W], 16)
        hi = bits[:, (2 + p) * PACK_W:(3 + p) * PACK_W] & jnp.int32(-65536)
        h_ref[p] = hi | lo
    hl = (h - hf).astype(BF16)
    wr = wr_ref[...]
    wh = wr.astype(BF16)
    wl = (wr - wh.astype(F32)).astype(BF16)
    logits = (jnp.dot(hb, wh, preferred_element_type=F32) + jnp.dot(hl, wh, preferred_element_type=F32)
              + jnp.dot(hb, wl, preferred_element_type=F32))
    lt = jnp.transpose(logits)[0:N_EXPERTS, :]
    lt = lt - jnp.max(lt, axis=0, keepdims=True)
    e = jnp.exp(lt)
    aff_ref[...] = e / jnp.sum(e, axis=0, keepdims=True)


def _router(x, mods_l, g, w_router, tm=512):
    wr = jnp.pad(w_router, ((0, 0), (0, LANES - N_EXPERTS)))
    return pl.pallas_call(
        _router_kernel,
        grid=(N_TOK // tm,),
        in_specs=[
            pl.BlockSpec((tm, D_MODEL), lambda i: (i, 0)),
            pl.BlockSpec((1, 6, D_MODEL), lambda i: (_group_of_block(i, tm), 0, 0)),
            pl.BlockSpec((1, D_MODEL), lambda i: (0, 0)),
            pl.BlockSpec((D_MODEL, LANES), lambda i: (0, 0)),
        ],
        out_specs=[
            pl.BlockSpec((2, tm, PACK_W), lambda i: (0, i, 0)),
            pl.BlockSpec((N_EXPERTS, tm), lambda i: (0, i)),
        ],
        out_shape=[
            jax.ShapeDtypeStruct((2, N_TOK, PACK_W), jnp.int32),
            jax.ShapeDtypeStruct((N_EXPERTS, N_TOK), F32),
        ],
        compiler_params=_params(("parallel",)),
        name="norm_router",
    )(x, mods_l, g.reshape(1, D_MODEL), wr)


SC_WINDOW = 128


def _gather_rows(table, idx):
    n = idx.shape[0]
    mesh = plsc.VectorSubcoreMesh(core_axis_name="core", subcore_axis_name="subcore")

    @pl.kernel(out_type=jax.ShapeDtypeStruct((n, PACK_W), table.dtype), mesh=mesh, scratch_types=[])
    def gather_kernel(t_hbm, i_hbm, o_hbm):
        def body(i_vmem, o_vmem):
            pltpu.sync_copy(t_hbm.at[i_vmem.at[0]], o_vmem)

        pltpu.emit_pipeline(
            body,
            grid=(n // SC_WINDOW,),
            in_specs=[pl.BlockSpec((1, SC_WINDOW), index_map=lambda i: (0, i))],
            out_specs=[pl.BlockSpec((SC_WINDOW, PACK_W), index_map=lambda i: (i, 0))],
            core_axis_name=("core", "subcore"),
            dimension_semantics=(pltpu.PARALLEL,),
        )(i_hbm, o_hbm)

    return gather_kernel(table, idx.reshape(1, n))


FFN_TF = 512
FFN_TR = 768


def _unpack_rows(pa, pb):
    def lo(w):
        return lax.bitcast_convert_type(lax.shift_left(w, 16), F32).astype(BF16)

    def hi(w):
        return lax.bitcast_convert_type(w & jnp.int32(-65536), F32).astype(BF16)

    return jnp.concatenate([lo(pa), lo(pb), hi(pa), hi(pb)], axis=1)


def _expert_ffn_kernel(xs_ref, wg_ref, wu_ref, wd_ref, gate_ref, m_ref, o_ref, xs_sc):
    f = pl.program_id(1)

    @pl.when(f == 0)
    def _():
        for r0 in range(0, CAP_T, FFN_TR):
            xs_sc[r0:r0 + FFN_TR, :] = _unpack_rows(xs_ref[0, 0, r0:r0 + FFN_TR, :], xs_ref[1, 0, r0:r0 + FFN_TR, :])

    wg = wg_ref[0].astype(BF16)
    wu = wu_ref[0].astype(BF16)
    wd = wd_ref[0].astype(BF16)
    for r0 in range(0, CAP_T, FFN_TR):
        rs = slice(r0, r0 + FFN_TR)
        xs = xs_sc[rs, :]
        a = jnp.dot(xs, wg, preferred_element_type=F32)
        u = jnp.dot(xs, wu, preferred_element_type=F32)
        hid = (a * (1.0 / (1.0 + jnp.exp(-a))) * u).astype(BF16)
        y = jnp.dot(hid, wd, preferred_element_type=F32)

        @pl.when(f == 0)
        def _():
            o_ref[0, rs, :] = y

        @pl.when(jnp.logical_and(f > 0, f < pl.num_programs(1) - 1))
        def _():
            o_ref[0, rs, :] += y

        @pl.when(f == pl.num_programs(1) - 1)
        def _():
            gg = gate_ref[0, rs, :]
            scale = gg[:, 0:1] * m_ref[0, G2:G2 + 1, :]
            for g in range(1, N_GROUPS):
                scale = scale + gg[:, g:g + 1] * m_ref[g, G2:G2 + 1, :]
            o_ref[0, rs, :] = (o_ref[0, rs, :] + y) * scale


def _expert_ffn(xs, wg, wu, wd, gate, mods_l, l):
    return pl.pallas_call(
        _expert_ffn_kernel,
        grid=(N_EXPERTS, EXPERT_FF // FFN_TF),
        in_specs=[
            pl.BlockSpec((2, 1, CAP_T, PACK_W), lambda e, f: (0, e, 0, 0)),
            pl.BlockSpec((None, 1, D_MODEL, FFN_TF), lambda e, f: (l, e, 0, f)),
            pl.BlockSpec((None, 1, D_MODEL, FFN_TF), lambda e, f: (l, e, 0, f)),
            pl.BlockSpec((None, 1, FFN_TF, D_MODEL), lambda e, f: (l, e, f, 0)),
            pl.BlockSpec((1, CAP_T, N_GROUPS), lambda e, f: (e, 0, 0)),
            pl.BlockSpec((N_GROUPS, 6, D_MODEL), lambda e, f: (0, 0, 0)),
        ],
        out_specs=pl.BlockSpec((1, CAP_T, D_MODEL), lambda e, f: (e, 0, 0)),
        out_shape=jax.ShapeDtypeStruct((N_EXPERTS, CAP_T, D_MODEL), F32),
        scratch_shapes=[pltpu.VMEM((CAP_T, D_MODEL), BF16)],
        compiler_params=_params(("parallel", "arbitrary")),
        name="expert_ffn",
    )(xs, wg, wu, wd, gate, mods_l)


def _final_norm_kernel(x_ref, g_ref, o_ref):
    x = x_ref[...]
    ms = jnp.mean(x * x, axis=-1, keepdims=True)
    o_ref[...] = x * lax.rsqrt(ms + EPS) * g_ref[...]


def _final_norm(x, g, row0, n_rows, tm=512):
    off = row0 // tm
    return pl.pallas_call(
        _final_norm_kernel,
        grid=(n_rows // tm,),
        in_specs=[pl.BlockSpec((tm, D_MODEL), lambda i: (i + off, 0)),
                  pl.BlockSpec((1, D_MODEL), lambda i: (0, 0))],
        out_specs=pl.BlockSpec((tm, D_MODEL), lambda i: (i, 0)),
        out_shape=jax.ShapeDtypeStruct((n_rows, D_MODEL), F32),
        compiler_params=_params(("parallel",)),
        name="final_norm",
    )(x, g.reshape(1, D_MODEL))


def _even_layer(x, mods_l, i, state_ret, st_prev, dfts, norm1_g, w_in_even, hy_short_w, hy_short_b, hy_f1_w, hy_f1_b,
                hy_freq1, hy_f2_w, hy_f2_b, hy_freq2, hy_f3_w, hy_bias, ret_decay, w_out_even):
    proj = _norm_project(x, mods_l, norm1_g, _weight_bf16(w_in_even, i))
    sw, sbias = hy_short_w[i], hy_short_b[i].reshape(1, 3 * HY_W)
    hb = hy_bias[i]
    y_hy = None
    for (row0, n_rows, L, nseq) in ((0, N_PROMPT, SEQ, 8), (N_PROMPT, N_SAMPLE, DEC_SEQ, 1)):
        s, d = _hyena_filter_taps(L, hy_f1_w[i], hy_f1_b[i], hy_freq1[i], hy_f2_w[i], hy_f2_b[i], hy_freq2[i], hy_f3_w[i])
        spectra = _filter_spectra(s, d, L, dfts[L])
        wv, bv = sw[:, 0:HY_W], sbias[:, 0:HY_W]
        w1, b1 = sw[:, HY_W:2 * HY_W], sbias[:, HY_W:2 * HY_W]
        w2, b2 = sw[:, 2 * HY_W:], sbias[:, 2 * HY_W:]
        z1 = _hyena_conv(proj, 0, proj, 1, row0, n_rows, L, nseq, spectra, 0, wv, bv, w1, b1, hb[0:1], True, dfts[L])
        y_hy = _hyena_conv(z1, 0, proj, 2, row0, n_rows, L, nseq, spectra, 1, wv, bv, w2, b2, hb[1:2], False, dfts[L],
                           full_out=True, out_prev=y_hy)
    dl = jnp.broadcast_to(ret_decay[i].astype(F32)[:, :, None], (2, RET_HEADS, LANES))
    y_ret, st = _retention(proj, dl, i, 0, BATCH, SEQ, st_prev=st_prev)
    y_ret = _retention(proj, dl, i, N_PROMPT, DEC_BATCH, DEC_SEQ, s0=state_ret, y_prev=y_ret)
    wo = _weight_bf16(w_out_even, i)
    x = _project_residual([y_hy, y_ret], [wo[:HY_W], wo[HY_W:]], x, mods_l, G1)
    return x, st


def _odd_layer(x, mods_l, l, i, cache_k, cache_v, kv_prev, norm1_g, w_in_odd, lam_q1, lam_k1, lam_q2, lam_k2, subln_g,
               w_out_odd):
    lam_init = 0.8 - 0.6 * math.exp(-0.3 * l)
    proj = _norm_project(x, mods_l, norm1_g, _weight_bf16(w_in_odd, i))
    lam_vec = jnp.pad(jnp.stack([lam_q1[i], lam_k1[i], lam_q2[i], lam_k2[i]]), ((0, 0), (0, LANES - DIFF_HEAD_DIM)))
    sg = subln_g[i].reshape(1, LANES)
    o_p, kc, vc = _diff_attention(proj, proj, proj, lam_vec, sg, lam_init, nb=BATCH, lq=SEQ, lk=SEQ, tq=SEQ,
                                  q_row0=0, q_col=0, kv_cols=(1, 2), batched_kv=False,
                                  cache_layer=i, cache_prev=kv_prev)
    qs, ks, vs = _sample_qkv(proj, cache_k, cache_v, i)
    o = _diff_attention(qs, ks, vs, lam_vec, sg, lam_init, nb=DEC_BATCH, lq=DEC_SEQ, lk=PAST_LEN + DEC_SEQ, tq=512,
                        q_row0=0, q_col=0, kv_cols=None, batched_kv=True, out_row0=N_PROMPT, out_prev=o_p)
    x = _project_residual([o], [_weight_bf16(w_out_odd, i)], x, mods_l, G1)
    return x, (kc, vc)


def _moe_layer(x, mods_l, l, norm2_g, w_router, wg, wu, wd):
    h, aff = _router(x, mods_l, norm2_g, w_router)
    gate_p, idx_p = lax.top_k(aff[:, :N_PROMPT], CAP_P)
    gate_s, idx_s = lax.top_k(aff[:, N_PROMPT:], CAP_S)
    idx = jnp.concatenate([idx_p, idx_s + N_PROMPT], axis=1)
    gate = jnp.concatenate([gate_p, gate_s], axis=1)
    flat = idx.reshape(-1)
    xs = _gather_rows(h.reshape(2 * N_TOK, PACK_W), jnp.concatenate([flat, flat + N_TOK]))
    xs = xs.reshape(2, N_EXPERTS, CAP_T, PACK_W)
    grp = jnp.where(idx < N_PROMPT, 0, 1 + (idx - N_PROMPT) // DEC_SEQ)
    gate_grp = jnp.where(grp[:, :, None] == jnp.arange(N_GROUPS)[None, None, :], gate[:, :, None], 0.0)
    out = _expert_ffn(xs, wg, wu, wd, gate_grp, mods_l, l)
    return x.at[idx.reshape(-1)].add(out.reshape(-1, D_MODEL))


def kernel(x_prompt, x_sample, state_ret, cache_k, cache_v, c, c_ctx, w_mod, b_mod, norm1_g, norm2_g, w_in_even, hy_short_w, hy_short_b, hy_f1_w, hy_f1_b, hy_freq1, hy_f2_w, hy_f2_b, hy_freq2, hy_f3_w, hy_bias, ret_decay, w_out_even, w_in_odd, lam_q1, lam_k1, lam_q2, lam_k2, subln_g, w_out_odd, moe_router, moe_wg, moe_wu, moe_wd, final_g):
    x = (x_prompt.reshape(N_PROMPT, D_MODEL), x_sample.reshape(N_SAMPLE, D_MODEL))
    mods = _modulation(c, c_ctx, w_mod, b_mod)
    dfts = {L: _dft_bf16(L) for L in (SEQ, DEC_SEQ)}
    state_ret = state_ret.astype(F32)
    st = kv = None
    for l in range(DEPTH):
        i = l // 2
        if l % 2 == 0:
            x, st = _even_layer(x, mods[l], i, state_ret, st, dfts, norm1_g[l], w_in_even, hy_short_w, hy_short_b,
                                hy_f1_w, hy_f1_b, hy_freq1, hy_f2_w, hy_f2_b, hy_freq2, hy_f3_w, hy_bias, ret_decay,
                                w_out_even)
        else:
            x, kv = _odd_layer(x, mods[l], l, i, cache_k, cache_v, kv, norm1_g[l], w_in_odd, lam_q1, lam_k1, lam_q2,
                               lam_k2, subln_g, w_out_odd)
        x = _moe_layer(x, mods[l], l, norm2_g[l], moe_router[l], moe_wg, moe_wu, moe_wd)
    y_prompt = _final_norm(x, final_g, 0, N_PROMPT).reshape(BATCH, SEQ, D_MODEL)
    y_sample = _final_norm(x, final_g, N_PROMPT, N_SAMPLE).reshape(DEC_BATCH, DEC_SEQ, D_MODEL)
    cache_shape = (BATCH, DEPTH // 2, SEQ, DIFF_HEADS, 2 * DIFF_HEAD_DIM)
    return (y_prompt, y_sample, st, kv[0].reshape(cache_shape), kv[1].reshape(cache_shape))
```

```python
import functools
import math

import numpy as np
import jax
import jax.numpy as jnp
from jax import lax
from jax.experimental import pallas as pl
from jax.experimental.pallas import tpu as pltpu
from jax.experimental.pallas import tpu_sc as plsc

F32 = jnp.float32
BF16 = jnp.bfloat16

D_MODEL = 1024
BATCH = 32
SEQ = 256
DEPTH = 4
DEC_BATCH = 2
DEC_SEQ = 2048
PAST_LEN = 256
GRID_W = 64
HY_W = 512
HY_EMB = 33
HY_BANDS = 16
HY_FF = 64
HY_TARGET = 1e-2
HY_FAST = 0.3
HY_SLOW = 1.5
RET_W = 512
RET_HEADS = 4
RET_HEAD_DIM = 128
RET_CHUNK = 128
DIFF_HEADS = 8
DIFF_HEAD_DIM = 64
ROPE_BASE = 10000.0
N_EXPERTS = 16
EC_FACTOR = 2
EXPERT_FF = 1024
EVEN_IN = 3 * HY_W + 4 * RET_W
EPS = 1e-6

N_PROMPT = BATCH * SEQ
N_SAMPLE = DEC_BATCH * DEC_SEQ
N_TOK = N_PROMPT + N_SAMPLE
N_GROUPS = 1 + DEC_BATCH
CAP_P = EC_FACTOR * N_PROMPT // N_EXPERTS
CAP_S = EC_FACTOR * N_SAMPLE // N_EXPERTS
CAP_T = CAP_P + CAP_S

LANES = 128
SUBLANES = 8
VMEM_LIMIT = 56 * 1024 * 1024

SH1, SC1, G1, SH2, SC2, G2 = range(6)


def _params(sem, vmem=VMEM_LIMIT):
    return pltpu.CompilerParams(dimension_semantics=sem, vmem_limit_bytes=vmem)


def _group_of_block(i, tm):
    pb = N_PROMPT // tm
    return jnp.where(i < pb, 0, 1 + (i - pb) // (DEC_SEQ // tm))


MOD_TN = 1024


MOD_UNROLL = 4


def _mod_kernel(cb_ref, w_ref, b_ref, o_ref, a_sc):
    nchunk = MOD_TN // LANES
    cv = cb_ref[...]
    a_sc[...] = cv * (1.0 / (1.0 + jnp.exp(-cv)))

    def body(kb, accs):
        accs = list(accs)
        for u in range(MOD_UNROLL):
            k0 = pl.multiple_of((kb * MOD_UNROLL + u) * SUBLANES, SUBLANES)
            a = [a_sc[r, pl.ds(k0, SUBLANES), :] for r in range(N_GROUPS)]
            for ci in range(nchunk):
                wv = w_ref[0, pl.ds(k0, SUBLANES), ci * LANES:(ci + 1) * LANES]
                for r in range(N_GROUPS):
                    accs[ci * N_GROUPS + r] = accs[ci * N_GROUPS + r] + wv * a[r]
        return tuple(accs)

    init = tuple(jnp.zeros((SUBLANES, LANES), F32) for _ in range(N_GROUPS * nchunk))
    accs = lax.fori_loop(0, D_MODEL // (SUBLANES * MOD_UNROLL), body, init)
    o_ref[...] = jnp.zeros(o_ref.shape, F32)
    for r in range(N_GROUPS):
        for ci in range(nchunk):
            row = jnp.sum(accs[ci * N_GROUPS + r], axis=0, keepdims=True)
            o_ref[0, r:r + 1, ci * LANES:(ci + 1) * LANES] = row + b_ref[0, :, ci * LANES:(ci + 1) * LANES]


def _modulation(c, c_ctx, w_mod, b_mod):
    cond = jnp.concatenate([c_ctx[None, :], c], axis=0)
    cb = jnp.broadcast_to(cond[:, :, None], (N_GROUPS, D_MODEL, LANES))
    out = pl.pallas_call(
        _mod_kernel,
        grid=(DEPTH, 6 * D_MODEL // MOD_TN),
        in_specs=[
            pl.BlockSpec((N_GROUPS, D_MODEL, LANES), lambda l, j: (0, 0, 0)),
            pl.BlockSpec((1, D_MODEL, MOD_TN), lambda l, j: (l, 0, j)),
            pl.BlockSpec((1, 1, MOD_TN), lambda l, j: (l, 0, j)),
        ],
        out_specs=pl.BlockSpec((1, SUBLANES, MOD_TN), lambda l, j: (l, 0, j)),
        out_shape=jax.ShapeDtypeStruct((DEPTH, SUBLANES, 6 * D_MODEL), F32),
        scratch_shapes=[pltpu.VMEM((N_GROUPS, D_MODEL, LANES), F32)],
        compiler_params=_params(("parallel", "parallel")),
        name="ada_mod",
    )(cb, w_mod, b_mod.reshape(DEPTH, 1, 6 * D_MODEL))
    return out[:, :N_GROUPS].reshape(DEPTH, N_GROUPS, 6, D_MODEL)


def _normed(x, g_ref, m_ref, shift, scale):
    ms = jnp.mean(x * x, axis=-1, keepdims=True)
    y = x * lax.rsqrt(ms + EPS) * g_ref[...]
    return y * (1.0 + m_ref[0, scale:scale + 1, :]) + m_ref[0, shift:shift + 1, :]


def _weight_bf16(w, i):
    return w[i].astype(BF16)


def _x_specs(x, tm):
    if not isinstance(x, tuple):
        return [x], [pl.BlockSpec((tm, D_MODEL), lambda i: (i, 0))]
    pb = N_PROMPT // tm
    return list(x), [pl.BlockSpec((tm, D_MODEL), lambda i: (jnp.minimum(i, pb - 1), 0)),
                     pl.BlockSpec((tm, D_MODEL), lambda i: (jnp.maximum(i - pb, 0), 0))]


def _x_block(x_refs, tm):
    if len(x_refs) == 1:
        return x_refs[0][...]
    return jnp.where(pl.program_id(0) < N_PROMPT // tm, x_refs[0][...], x_refs[1][...])


def _norm_mm_kernel(*refs, tn, tm):
    m_ref, g_ref, w_ref, o_ref = refs[-4:]
    h = _normed(_x_block(refs[:-4], tm), g_ref, m_ref, SH1, SC1).astype(BF16)
    for c0 in range(0, o_ref.shape[1], tn):
        o_ref[:, c0:c0 + tn] = jnp.dot(h, w_ref[:, c0:c0 + tn], preferred_element_type=F32)


def _norm_project(x, mods_l, g, w_bf16, tm=512, tn=512):
    nout = w_bf16.shape[1]
    x_args, x_specs = _x_specs(x, tm)
    return pl.pallas_call(
        functools.partial(_norm_mm_kernel, tn=tn, tm=tm),
        grid=(N_TOK // tm,),
        in_specs=x_specs + [
            pl.BlockSpec((1, 6, D_MODEL), lambda i: (_group_of_block(i, tm), 0, 0)),
            pl.BlockSpec((1, D_MODEL), lambda i: (0, 0)),
            pl.BlockSpec((D_MODEL, nout), lambda i: (0, 0)),
        ],
        out_specs=pl.BlockSpec((tm, nout), lambda i: (i, 0)),
        out_shape=jax.ShapeDtypeStruct((N_TOK, nout), F32),
        compiler_params=_params(("parallel",)),
        name="norm_project",
    )(*x_args, mods_l, g.reshape(1, D_MODEL), w_bf16)


def _proj_res_kernel(*refs, n_in, gate, tm):
    a_refs = refs[:n_in]
    w_refs = refs[n_in:2 * n_in]
    x_refs = refs[2 * n_in:-2]
    m_ref, o_ref = refs[-2:]
    acc = None
    for a_ref, w_ref in zip(a_refs, w_refs):
        t = jnp.dot(a_ref[...].astype(BF16), w_ref[...], preferred_element_type=F32)
        acc = t if acc is None else acc + t
    o_ref[...] = _x_block(x_refs, tm) + m_ref[0, gate:gate + 1, :] * acc


def _project_residual(acts, ws_bf16, x, mods_l, gate, tm=512):
    n_in = len(acts)
    x_args, x_specs = _x_specs(x, tm)
    in_specs = [pl.BlockSpec((tm, a.shape[1]), lambda i: (i, 0)) for a in acts]
    in_specs += [pl.BlockSpec(w.shape, lambda i: (0, 0)) for w in ws_bf16]
    in_specs += x_specs + [pl.BlockSpec((1, 6, D_MODEL), lambda i: (_group_of_block(i, tm), 0, 0))]
    return pl.pallas_call(
        functools.partial(_proj_res_kernel, n_in=n_in, gate=gate, tm=tm),
        grid=(N_TOK // tm,),
        in_specs=in_specs,
        out_specs=pl.BlockSpec((tm, D_MODEL), lambda i: (i, 0)),
        out_shape=jax.ShapeDtypeStruct((N_TOK, D_MODEL), F32),
        input_output_aliases={2 * n_in: 0} if len(x_args) == 1 else {},
        compiler_params=_params(("parallel",)),
        name="project_residual",
    )(*acts, *ws_bf16, *x_args, mods_l)


@functools.lru_cache(maxsize=None)
def _dft_mats(L):
    n = 2 * L
    ft = (np.arange(L, dtype=np.int64)[:, None] * np.arange(L, dtype=np.int64)[None, :]) % n
    ang = ft.astype(np.float64) * (2.0 * np.pi / n)
    return np.cos(ang).astype(np.float32), np.sin(ang).astype(np.float32)


def _dft_bf16(L):
    c, s = _dft_mats(L)
    return jnp.asarray(c).astype(BF16), jnp.asarray(s).astype(BF16)


def _alt_sign(shape, row0):
    t = lax.broadcasted_iota(jnp.int32, shape, 0) + row0
    return (1 - 2 * (t & 1)).astype(F32)


def _filter_dft_kernel(s_ref, d_ref, c_ref, sn_ref, ka_ref, ki_ref, kn_ref, *, L, fb):
    f0 = pl.program_id(0) * fb
    n = 2.0 * L
    s = s_ref[...]
    r = jnp.dot(c_ref[...], s, preferred_element_type=F32)
    im = jnp.dot(sn_ref[...], d_ref[...], preferred_element_type=F32)
    fidx = lax.broadcasted_iota(jnp.int32, r.shape, 0) + f0
    scale = jnp.where(fidx == 0, 1.0 / n, 2.0 / n)
    ka_ref[...] = r * scale
    ki_ref[...] = im * (2.0 / n)
    nyq = jnp.sum(s.astype(F32) * _alt_sign(s.shape, 0), axis=0, keepdims=True) * (1.0 / n)
    kn_ref[...] = jnp.broadcast_to(nyq, kn_ref.shape)


def _filter_spectra(s, d, L, dft):
    fb = min(L, 512)
    cmat, smat = dft
    w = 2 * HY_W
    return pl.pallas_call(
        functools.partial(_filter_dft_kernel, L=L, fb=fb),
        grid=(L // fb,),
        in_specs=[
            pl.BlockSpec((L, w), lambda f: (0, 0)),
            pl.BlockSpec((L, w), lambda f: (0, 0)),
            pl.BlockSpec((fb, L), lambda f: (f, 0)),
            pl.BlockSpec((fb, L), lambda f: (f, 0)),
        ],
        out_specs=[
            pl.BlockSpec((fb, w), lambda f: (f, 0)),
            pl.BlockSpec((fb, w), lambda f: (f, 0)),
            pl.BlockSpec((SUBLANES, w), lambda f: (0, 0)),
        ],
        out_shape=[
            jax.ShapeDtypeStruct((L, w), F32),
            jax.ShapeDtypeStruct((L, w), F32),
            jax.ShapeDtypeStruct((SUBLANES, w), F32),
        ],
        compiler_params=_params(("arbitrary",)),
        name="hyena_filter_dft",
    )(s.astype(BF16), d.astype(BF16), cmat, smat)


HY_TILE = 256


def _short_conv_tile(ref, r0, L, w_ref, b_ref):
    t = HY_TILE
    cur = ref[r0:r0 + t, :]
    rid = lax.broadcasted_iota(jnp.int32, cur.shape, 0)
    if r0 % L == 0:
        prev = jnp.where(rid == 0, 0.0, pltpu.roll(cur, 1, axis=0))
    else:
        prev = ref[r0 - 1:r0 - 1 + t, :]
    if (r0 + t) % L == 0:
        nxt = jnp.where(rid == t - 1, 0.0, pltpu.roll(cur, t - 1, axis=0))
    else:
        nxt = ref[r0 + 1:r0 + 1 + t, :]
    return prev * w_ref[0:1, :] + cur * w_ref[1:2, :] + nxt * w_ref[2:3, :] + b_ref[...]


def _hyena_conv_kernel(*refs, L, nseq, conv_a):
    (a_ref, x_ref, cr_ref, sr_ref, cc_ref, sc_ref, ka_ref, ki_ref, kn_ref,
     wa_ref, ba_ref, wx_ref, bx_ref, hb_ref) = refs[:14]
    o_ref, z_sc, acc_sc = refs[-3:]
    f = pl.program_id(1)
    nf = pl.num_programs(1)
    rows = nseq * L

    @pl.when(f == 0)
    def _():
        for q in range(nseq):
            nyq = jnp.zeros((1, HY_W), F32)
            for r0 in range(q * L, (q + 1) * L, HY_TILE):
                if conv_a:
                    zt = _short_conv_tile(a_ref, r0, L, wa_ref, ba_ref)
                else:
                    zt = a_ref[r0:r0 + HY_TILE, :]
                z_sc[r0:r0 + HY_TILE, :] = zt
                nyq = nyq + jnp.sum(zt * _alt_sign(zt.shape, r0), axis=0, keepdims=True)
            nyq = nyq * kn_ref[0:1, :]
            for r0 in range(q * L, (q + 1) * L, HY_TILE):
                acc_sc[r0:r0 + HY_TILE, :] = _alt_sign((HY_TILE, HY_W), r0) * nyq

    ka = ka_ref[...]
    ki = ki_ref[...]
    for q in range(nseq):
        z = z_sc[q * L:(q + 1) * L, :].astype(BF16)
        a = jnp.dot(cr_ref[...], z, preferred_element_type=F32)
        b = jnp.dot(sr_ref[...], z, preferred_element_type=F32)
        p = (a * ka + b * ki).astype(BF16)
        qq = (b * ka - a * ki).astype(BF16)
        acc_sc[q * L:(q + 1) * L, :] += (jnp.dot(cc_ref[...], p, preferred_element_type=F32)
                                         + jnp.dot(sc_ref[...], qq, preferred_element_type=F32))

    @pl.when(f == nf - 1)
    def _():
        for r0 in range(0, rows, HY_TILE):
            y = acc_sc[r0:r0 + HY_TILE, :] + z_sc[r0:r0 + HY_TILE, :] * hb_ref[...]
            o_ref[r0:r0 + HY_TILE, :] = (y * _short_conv_tile(x_ref, r0, L, wx_ref, bx_ref)).astype(o_ref.dtype)


def _hyena_conv(a, a_col, x, x_col, row0, n_rows, L, nseq, spectra, filt, wa, ba, wx, bx, hbias, conv_a, dft,
                full_out=False, out_prev=None):
    fb = min(L, 256)
    cmat, smat = dft
    ka, ki, kn = spectra
    rb = nseq * L
    a_off = row0 // rb if a.shape[0] != n_rows else 0
    x_off = row0 // rb
    o_off = row0 // rb if full_out else 0
    extra_specs, extra_args, aliases = [], [], {}
    if out_prev is not None:
        extra_specs, extra_args, aliases = [pl.BlockSpec(memory_space=pl.ANY)], [out_prev], {14: 0}
    return pl.pallas_call(
        functools.partial(_hyena_conv_kernel, L=L, nseq=nseq, conv_a=conv_a),
        grid=(n_rows // rb, L // fb),
        input_output_aliases=aliases,
        in_specs=extra_specs[:0] + [
            pl.BlockSpec((rb, HY_W), lambda i, f: (i + a_off, a_col)),
            pl.BlockSpec((rb, HY_W), lambda i, f: (i + x_off, x_col)),
            pl.BlockSpec((fb, L), lambda i, f: (f, 0)),
            pl.BlockSpec((fb, L), lambda i, f: (f, 0)),
            pl.BlockSpec((L, fb), lambda i, f: (0, f)),
            pl.BlockSpec((L, fb), lambda i, f: (0, f)),
            pl.BlockSpec((fb, HY_W), lambda i, f: (f, filt)),
            pl.BlockSpec((fb, HY_W), lambda i, f: (f, filt)),
            pl.BlockSpec((SUBLANES, HY_W), lambda i, f: (0, filt)),
            pl.BlockSpec((3, HY_W), lambda i, f: (0, 0)),
            pl.BlockSpec((1, HY_W), lambda i, f: (0, 0)),
            pl.BlockSpec((3, HY_W), lambda i, f: (0, 0)),
            pl.BlockSpec((1, HY_W), lambda i, f: (0, 0)),
            pl.BlockSpec((1, HY_W), lambda i, f: (0, 0)),
        ] + extra_specs,
        out_specs=pl.BlockSpec((rb, HY_W), lambda i, f: (i + o_off, 0)),
        out_shape=jax.ShapeDtypeStruct((N_TOK, HY_W), BF16) if full_out else jax.ShapeDtypeStruct((n_rows, HY_W), F32),
        scratch_shapes=[pltpu.VMEM((rb, HY_W), F32), pltpu.VMEM((rb, HY_W), F32)],
        compiler_params=_params(("parallel", "arbitrary")),
        name="hyena_conv",
    )(a, x, cmat, smat, cmat, smat, ka, ki, kn, wa, ba, wx, bx, hbias, *extra_args)


def _hyena_filter_taps(L, f1w, f1b, fr1, f2w, f2b, fr2, f3w):
    hp = lax.Precision.HIGHEST
    pos = jnp.arange(L, dtype=F32)
    t = pos / (L - 1)
    w = 2.0 * math.pi * pos / L
    f = jnp.linspace(1e-4, HY_BANDS - 1, HY_BANDS, dtype=F32)
    wf = w[:, None] * f[None, :]
    feat = jnp.concatenate([t[:, None], jnp.cos(wf), -jnp.sin(wf)], axis=-1)
    h = jnp.sin(fr1 * (jnp.dot(feat, f1w, precision=hp) + f1b))
    h = jnp.sin(fr2 * (jnp.dot(h, f2w, precision=hp) + f2b))
    h = jnp.dot(h, f3w, precision=hp).astype(F32)
    deltas = jnp.linspace(math.log(HY_TARGET) / HY_SLOW, math.log(HY_TARGET) / HY_FAST, HY_W, dtype=F32)
    window = jnp.exp(-t[:, None] * jnp.abs(deltas)[None, :])
    w = HY_W
    parts = [h[:, k * w:(k + 1) * w] * window for k in range(4)]
    colsum = [jnp.sum(jnp.abs(p), axis=0, keepdims=True) for p in parts]
    den = [colsum[0] + colsum[1] + EPS, colsum[2] + colsum[3] + EPS]
    fwd = jnp.concatenate([parts[0] / den[0], parts[2] / den[1]], axis=1)
    bwd = jnp.concatenate([parts[1] / den[0], parts[3] / den[1]], axis=1)
    bwd = jnp.where(pos[:, None] == 0, 0.0, bwd)
    return fwd + bwd, bwd - fwd


def _dot_t0(a, b):
    return lax.dot_general(a, b, (((0,), (0,)), ((), ())), preferred_element_type=F32)


def _dot_t1(a, b):
    return lax.dot_general(a, b, (((1,), (1,)), ((), ())), preferred_element_type=F32)


def _retention_kernel(*refs, L, has_s0, has_prev):
    refs = list(refs)
    q_ref, k_ref, v_ref, g_ref, dl_ref = refs[:5]
    pos = 5
    s0_ref = st_ref = None
    if has_s0:
        s0_ref = refs[pos]
        pos += 1
    if has_prev:
        pos += 1
    y_ref = refs[pos]
    pos += 1
    if not has_s0:
        st_ref = refs[pos]
        pos += 1
    sb_sc, sf_cur, sb_cur = refs[pos:]
    c = RET_CHUNK
    nc = L // c
    kscale = RET_HEAD_DIM ** -0.5
    ri = lax.broadcasted_iota(jnp.int32, (c, c), 0).astype(F32)
    ci = lax.broadcasted_iota(jnp.int32, (c, c), 1).astype(F32)
    diff = ri - ci
    dec = []
    for h in range(RET_HEADS):
        xf = dl_ref[0, h:h + 1, :]
        xb = dl_ref[1, h:h + 1, :]
        lgf = jnp.minimum(xf, 0.0) - jnp.log1p(jnp.exp(-jnp.abs(xf)))
        lgb = jnp.minimum(xb, 0.0) - jnp.log1p(jnp.exp(-jnp.abs(xb)))
        dec.append(dict(
            mask=(jnp.where(diff >= 0, jnp.exp(lgf * jnp.maximum(diff, 0.0)), 0.0)
                  + jnp.where(diff <= 0, jnp.exp(lgb * jnp.maximum(-diff, 0.0)), 0.0)),
            qdec_f=jnp.exp(lgf * (ri + 1.0)), kdec_f=jnp.exp(lgf * (c - 1.0 - ri)),
            qdec_b=jnp.exp(lgb * (c - ri)), kdec_b=jnp.exp(lgb * ri),
            cd_f=jnp.exp(lgf * c), cd_b=jnp.exp(lgb * c)))
        if has_s0:
            sf_cur[h] = s0_ref[0, 0, 0, h]
            sb_cur[h] = s0_ref[0, 0, 1, h]
        else:
            sf_cur[h] = jnp.zeros((RET_HEAD_DIM, RET_HEAD_DIM), F32)
            sb_cur[h] = jnp.zeros((RET_HEAD_DIM, RET_HEAD_DIM), F32)

    def bwd_body(i, carry):
        j = nc - 1 - i
        r0 = pl.multiple_of(j * c, c)
        for h in range(RET_HEADS):
            hs = slice(h * RET_HEAD_DIM, (h + 1) * RET_HEAD_DIM)
            sb = sb_cur[h]
            sb_sc[h * nc + j] = sb
            kc = k_ref[pl.ds(r0, c), hs] * kscale
            vc = v_ref[pl.ds(r0, c), hs]
            sb_cur[h] = sb * dec[h]["cd_b"] + _dot_t0((kc * dec[h]["kdec_b"]).astype(BF16), vc.astype(BF16))
        return carry

    lax.fori_loop(0, nc, bwd_body, 0)

    def fwd_body(j, carry):
        r0 = pl.multiple_of(j * c, c)
        for h in range(RET_HEADS):
            hs = slice(h * RET_HEAD_DIM, (h + 1) * RET_HEAD_DIM)
            dh = dec[h]
            sf = sf_cur[h]
            qc = q_ref[pl.ds(r0, c), hs]
            kc = k_ref[pl.ds(r0, c), hs] * kscale
            vc = v_ref[pl.ds(r0, c), hs].astype(BF16)
            scores = _dot_t1(qc.astype(BF16), kc.astype(BF16)) * dh["mask"]
            o = jnp.dot(scores.astype(BF16), vc, preferred_element_type=F32)
            o = o + jnp.dot((qc * dh["qdec_f"]).astype(BF16), sf.astype(BF16), preferred_element_type=F32)
            o = o + jnp.dot((qc * dh["qdec_b"]).astype(BF16), sb_sc[h * nc + j].astype(BF16),
                            preferred_element_type=F32)
            mu = jnp.mean(o, axis=-1, keepdims=True)
            var = jnp.mean(jnp.square(o - mu), axis=-1, keepdims=True)
            on = (o - mu) * lax.rsqrt(var + EPS)
            gc = g_ref[pl.ds(r0, c), hs]
            y_ref[pl.ds(r0, c), hs] = (gc * (1.0 / (1.0 + jnp.exp(-gc))) * on).astype(y_ref.dtype)
            sf_cur[h] = sf * dh["cd_f"] + _dot_t0((kc * dh["kdec_f"]).astype(BF16), vc)
        return carry

    lax.fori_loop(0, nc, fwd_body, 0)
    if st_ref is not None:
        for h in range(RET_HEADS):
            st_ref[0, 0, 0, h] = sf_cur[h]
            st_ref[0, 0, 1, h] = sb_cur[h]


def _retention(proj, dl, layer_i, row0, nseq, L, s0=None, st_prev=None, y_prev=None):
    off = row0 // L
    has_s0 = s0 is not None
    has_prev = (st_prev is not None) or (y_prev is not None)
    assert not (st_prev is not None and y_prev is not None)
    nc = L // RET_CHUNK
    n_ret = (DEPTH + 1) // 2
    col = lambda j: pl.BlockSpec((L, RET_W), lambda b: (b + off, 3 + j))
    in_specs = [col(0), col(1), col(2), col(3),
                pl.BlockSpec((2, RET_HEADS, LANES), lambda b: (0, 0, 0))]
    args = [proj, proj, proj, proj, dl]
    y_spec = pl.BlockSpec((L, RET_W), lambda b: (b + off, 0))
    y_shape = jax.ShapeDtypeStruct((N_TOK, RET_W), BF16)
    st_block = (1, 1, 2, RET_HEADS, RET_HEAD_DIM, RET_HEAD_DIM)
    st_spec = pl.BlockSpec(st_block, lambda b: (b, layer_i, 0, 0, 0, 0))
    aliases = {}
    if has_s0:
        in_specs.append(st_spec)
        args.append(s0)
        out_specs, out_shape = y_spec, y_shape
    else:
        out_specs = [y_spec, st_spec]
        out_shape = [y_shape, jax.ShapeDtypeStruct((nseq, n_ret) + st_block[2:], F32)]
    if has_prev:
        aliases = {len(args): 1 if st_prev is not None else 0}
        in_specs.append(pl.BlockSpec(memory_space=pl.ANY))
        args.append(st_prev if st_prev is not None else y_prev)
    state = pltpu.VMEM((RET_HEADS, RET_HEAD_DIM, RET_HEAD_DIM), F32)
    return pl.pallas_call(
        functools.partial(_retention_kernel, L=L, has_s0=has_s0, has_prev=has_prev),
        grid=(nseq,),
        in_specs=in_specs,
        out_specs=out_specs,
        out_shape=out_shape,
        input_output_aliases=aliases,
        scratch_shapes=[pltpu.VMEM((RET_HEADS * nc, RET_HEAD_DIM, RET_HEAD_DIM), F32), state, state],
        compiler_params=_params(("parallel",)),
        name="retention",
    )(*args)


@functools.lru_cache(maxsize=None)
def _rope_tables():
    L = DEC_SEQ
    rows = L // GRID_W
    row = np.repeat(np.arange(rows, dtype=np.float64), GRID_W)
    col = np.tile(np.arange(GRID_W, dtype=np.float64), rows)
    quarter = DIFF_HEAD_DIM // 4
    freqs = ROPE_BASE ** (-np.arange(quarter, dtype=np.float64) / quarter)
    j = np.arange(LANES)
    pos = np.where(((j % DIFF_HEAD_DIM) < DIFF_HEAD_DIM // 2)[None, :], row[:, None], col[:, None])
    ang = pos * freqs[j % quarter][None, :]
    cos = np.cos(ang).astype(np.float32)
    sin = np.sin(ang).astype(np.float32)
    first = ((j % (2 * quarter)) < quarter)[None, :]
    sin_a = np.where(first, -sin, 0.0).astype(np.float32)
    sin_b = np.where(first, 0.0, sin).astype(np.float32)
    return jnp.asarray(cos), jnp.asarray(sin_a), jnp.asarray(sin_b)


def _rope_head(x, cos, sin_a, sin_b):
    quarter = DIFF_HEAD_DIM // 4
    up = pltpu.roll(x, LANES - quarter, axis=1)
    dn = pltpu.roll(x, quarter, axis=1)
    return x * cos + up * sin_a + dn * sin_b


def _kv_prep_kernel(q_ref, k_ref, v_ref, cos_ref, sa_ref, sb_ref, qo_ref, ko_ref, vo_ref):
    cos, sa, sb = cos_ref[...], sa_ref[...], sb_ref[...]
    for h in range(DIFF_HEADS):
        hs = slice(h * LANES, (h + 1) * LANES)
        qo_ref[:, hs] = _rope_head(q_ref[:, hs], cos, sa, sb).astype(BF16)
        ko_ref[0, :, hs] = _rope_head(k_ref[:, hs], cos, sa, sb).astype(BF16)
    vo_ref[0] = v_ref[...].astype(BF16)


def _cache_copy_kernel(ck_ref, cv_ref, k_in, v_in, ko_ref, vo_ref):
    del k_in, v_in
    ko_ref[0] = ck_ref[0, 0].astype(BF16)
    vo_ref[0] = cv_ref[0, 0].astype(BF16)


def _sample_qkv(proj, cache_k, cache_v, layer_i, tm=256):
    cos, sa, sb = _rope_tables()
    lk = PAST_LEN + DEC_SEQ
    pblk = N_PROMPT // tm
    nblk = DEC_SEQ // tm
    cblk = PAST_LEN // tm
    tab = pl.BlockSpec((tm, LANES), lambda b, i: (i, 0))
    q, k, v = pl.pallas_call(
        _kv_prep_kernel,
        grid=(DEC_BATCH, nblk),
        in_specs=[
            pl.BlockSpec((tm, D_MODEL), lambda b, i: (pblk + b * nblk + i, 0)),
            pl.BlockSpec((tm, D_MODEL), lambda b, i: (pblk + b * nblk + i, 1)),
            pl.BlockSpec((tm, D_MODEL), lambda b, i: (pblk + b * nblk + i, 2)),
            tab, tab, tab,
        ],
        out_specs=[
            pl.BlockSpec((tm, D_MODEL), lambda b, i: (b * nblk + i, 0)),
            pl.BlockSpec((1, tm, D_MODEL), lambda b, i: (b, cblk + i, 0)),
            pl.BlockSpec((1, tm, D_MODEL), lambda b, i: (b, cblk + i, 0)),
        ],
        out_shape=[
            jax.ShapeDtypeStruct((N_SAMPLE, D_MODEL), BF16),
            jax.ShapeDtypeStruct((DEC_BATCH, lk, D_MODEL), BF16),
            jax.ShapeDtypeStruct((DEC_BATCH, lk, D_MODEL), BF16),
        ],
        compiler_params=_params(("parallel", "parallel")),
        name="rope_qkv",
    )(proj, proj, proj, cos, sa, sb)
    n_att = DEPTH // 2
    ck = cache_k.reshape(DEC_BATCH, n_att, PAST_LEN, D_MODEL)
    cv = cache_v.reshape(DEC_BATCH, n_att, PAST_LEN, D_MODEL)
    k, v = pl.pallas_call(
        _cache_copy_kernel,
        grid=(DEC_BATCH,),
        in_specs=[
            pl.BlockSpec((1, 1, PAST_LEN, D_MODEL), lambda b: (b, layer_i, 0, 0)),
            pl.BlockSpec((1, 1, PAST_LEN, D_MODEL), lambda b: (b, layer_i, 0, 0)),
            pl.BlockSpec(memory_space=pl.ANY),
            pl.BlockSpec(memory_space=pl.ANY),
        ],
        out_specs=[
            pl.BlockSpec((1, PAST_LEN, D_MODEL), lambda b: (b, 0, 0)),
            pl.BlockSpec((1, PAST_LEN, D_MODEL), lambda b: (b, 0, 0)),
        ],
        out_shape=[
            jax.ShapeDtypeStruct((DEC_BATCH, lk, D_MODEL), BF16),
            jax.ShapeDtypeStruct((DEC_BATCH, lk, D_MODEL), BF16),
        ],
        input_output_aliases={2: 0, 3: 1},
        compiler_params=_params(("parallel",)),
        name="cache_prepend",
    )(ck, cv, k, v)
    return q, k, v


def _diff_attn_kernel(*refs, lam_init, batched_kv, cache_out):
    q_ref, k_ref, v_ref, lam_ref, sg_ref = refs[:5]
    if cache_out:
        o_ref, kc_ref, vc_ref = refs[-3:]
        kc_ref[0, 0] = k_ref[...]
        vc_ref[0, 0] = v_ref[...]
    else:
        o_ref = refs[-1]
    lv = lam_ref[...]
    lam = (jnp.exp(jnp.sum(lv[0:1] * lv[1:2], axis=-1, keepdims=True))
           - jnp.exp(jnp.sum(lv[2:3] * lv[3:4], axis=-1, keepdims=True)) + lam_init)
    lane = lax.broadcasted_iota(jnp.int32, (1, LANES), 1)
    m1 = (lane < DIFF_HEAD_DIM).astype(F32)
    m2 = 1.0 - m1
    scale = DIFF_HEAD_DIM ** -0.5
    for h in range(q_ref.shape[1] // LANES):
        hs = slice(h * LANES, (h + 1) * LANES)
        q = q_ref[:, hs].astype(F32) * scale
        if batched_kv:
            k = k_ref[0, :, hs].astype(BF16)
            v = v_ref[0, :, hs].astype(BF16)
        else:
            k = k_ref[:, hs].astype(BF16)
            v = v_ref[:, hs].astype(BF16)
        v_ext = jnp.concatenate([v, jnp.ones_like(v)], axis=1)
        outs = []
        for m in (m1, m2):
            s = _dot_t1((q * m).astype(BF16), k)
            s = s - jnp.max(s, axis=-1, keepdims=True)
            pv = jnp.dot(jnp.exp(s).astype(BF16), v_ext, preferred_element_type=F32)
            outs.append(pv[:, :LANES] / pv[:, LANES:])
        o = outs[0] - lam * outs[1]
        ms = jnp.mean(o * o, axis=-1, keepdims=True)
        o_ref[:, hs] = (o * lax.rsqrt(ms + EPS) * sg_ref[...] * (1.0 - lam_init)).astype(o_ref.dtype)


def _diff_attention(q, k, v, lam_vec, subln, lam_init, *, nb, lq, lk, tq, q_row0, q_col, kv_cols, batched_kv,
                    out_row0=0, out_prev=None, cache_layer=None, cache_prev=None):
    nq = lq // tq
    qoff = q_row0 // tq
    ooff = out_row0 // tq
    cache_out = cache_layer is not None
    extra_specs, extra_args, aliases = [], [], {}
    if out_prev is not None:
        extra_specs, extra_args, aliases = [pl.BlockSpec(memory_space=pl.ANY)], [out_prev], {5: 0}
    o_spec = pl.BlockSpec((tq, D_MODEL), lambda b, i: (ooff + b * nq + i, 0))
    o_shape = jax.ShapeDtypeStruct((N_TOK, D_MODEL), BF16)
    if cache_out:
        assert not batched_kv and nq == 1 and out_prev is None
        if cache_prev is not None:
            extra_specs = [pl.BlockSpec(memory_space=pl.ANY)] * 2
            extra_args = list(cache_prev)
            aliases = {5: 1, 6: 2}
        c_spec = pl.BlockSpec((1, 1, lk, D_MODEL), lambda b, i: (b, cache_layer, 0, 0))
        c_shape = jax.ShapeDtypeStruct((nb, DEPTH // 2, lk, D_MODEL), F32)
        o_spec, o_shape = [o_spec, c_spec, c_spec], [o_shape, c_shape, c_shape]
    q_spec = pl.BlockSpec((tq, D_MODEL), lambda b, i: (qoff + b * nq + i, q_col))
    if batched_kv:
        k_spec = pl.BlockSpec((1, lk, D_MODEL), lambda b, i: (b, 0, 0))
        v_spec = k_spec
    else:
        k_spec = pl.BlockSpec((lk, D_MODEL), lambda b, i: (b, kv_cols[0]))
        v_spec = pl.BlockSpec((lk, D_MODEL), lambda b, i: (b, kv_cols[1]))
    return pl.pallas_call(
        functools.partial(_diff_attn_kernel, lam_init=lam_init, batched_kv=batched_kv, cache_out=cache_out),
        grid=(nb, nq),
        in_specs=[q_spec, k_spec, v_spec,
                  pl.BlockSpec((4, LANES), lambda b, i: (0, 0)),
                  pl.BlockSpec((1, LANES), lambda b, i: (0, 0))] + extra_specs,
        out_specs=o_spec,
        out_shape=o_shape,
        input_output_aliases=aliases,
        compiler_params=_params(("parallel", "arbitrary")),
        name="diff_attention",
    )(q, k, v, lam_vec, subln, *extra_args)


def _diff_attention_per_head(q, k, v, lam_vec, subln, lam_init, *, nb, lq, lk, tq, out_row0, out_prev):
    nq = lq // tq
    ooff = out_row0 // tq
    kv_spec = pl.BlockSpec((1, lk, LANES), lambda b, h, i: (b, 0, h))
    return pl.pallas_call(
        functools.partial(_diff_attn_kernel, lam_init=lam_init, batched_kv=True, cache_out=False),
        grid=(nb, DIFF_HEADS, nq),
        in_specs=[pl.BlockSpec((tq, LANES), lambda b, h, i: (b * nq + i, h)), kv_spec, kv_spec,
                  pl.BlockSpec((4, LANES), lambda b, h, i: (0, 0)),
                  pl.BlockSpec((1, LANES), lambda b, h, i: (0, 0)),
                  pl.BlockSpec(memory_space=pl.ANY)],
        out_specs=pl.BlockSpec((tq, LANES), lambda b, h, i: (ooff + b * nq + i, h)),
        out_shape=jax.ShapeDtypeStruct((N_TOK, D_MODEL), BF16),
        input_output_aliases={5: 0},
        compiler_params=_params(("parallel", "parallel", "arbitrary")),
        name="diff_attention_head",
    )(q, k, v, lam_vec, subln, out_prev)


PACK_W = D_MODEL // 4


def _router_kernel(x_ref, m_ref, g_ref, wr_ref, h_ref, aff_ref):
    h = _normed(x_ref[...], g_ref, m_ref, SH2, SC2)
    hb = h.astype(BF16)
    hf = hb.astype(F32)
    bits = lax.bitcast_convert_type(hf, jnp.int32)
    for p in range(2):
        lo = lax.shift_right_logical(bits[:, p * PACK_W:(p + 1) * PACK_W], 16)
        hi = bits[:, (2 + p) * PACK_W:(3 + p) * PACK_W] & jnp.int32(-65536)
        h_ref[p] = hi | lo
    hl = (h - hf).astype(BF16)
    wr = wr_ref[...]
    wh = wr.astype(BF16)
    wl = (wr - wh.astype(F32)).astype(BF16)
    logits = (jnp.dot(hb, wh, preferred_element_type=F32) + jnp.dot(hl, wh, preferred_element_type=F32)
              + jnp.dot(hb, wl, preferred_element_type=F32))
    lt = jnp.transpose(logits)[0:N_EXPERTS, :]
    lt = lt - jnp.max(lt, axis=0, keepdims=True)
    e = jnp.exp(lt)
    aff_ref[...] = e / jnp.sum(e, axis=0, keepdims=True)


def _router(x, mods_l, g, w_router, tm=512):
    wr = jnp.pad(w_router, ((0, 0), (0, LANES - N_EXPERTS)))
    return pl.pallas_call(
        _router_kernel,
        grid=(N_TOK // tm,),
        in_specs=[
            pl.BlockSpec((tm, D_MODEL), lambda i: (i, 0)),
            pl.BlockSpec((1, 6, D_MODEL), lambda i: (_group_of_block(i, tm), 0, 0)),
            pl.BlockSpec((1, D_MODEL), lambda i: (0, 0)),
            pl.BlockSpec((D_MODEL, LANES), lambda i: (0, 0)),
        ],
        out_specs=[
            pl.BlockSpec((2, tm, PACK_W), lambda i: (0, i, 0)),
            pl.BlockSpec((N_EXPERTS, tm), lambda i: (0, i)),
        ],
        out_shape=[
            jax.ShapeDtypeStruct((2, N_TOK, PACK_W), jnp.int32),
            jax.ShapeDtypeStruct((N_EXPERTS, N_TOK), F32),
        ],
        compiler_params=_params(("parallel",)),
        name="norm_router",
    )(x, mods_l, g.reshape(1, D_MODEL), wr)


SC_WINDOW = 128


def _gather_rows(table, idx):
    n = idx.shape[0]
    mesh = plsc.VectorSubcoreMesh(core_axis_name="core", subcore_axis_name="subcore")

    @pl.kernel(out_type=jax.ShapeDtypeStruct((n, PACK_W), table.dtype), mesh=mesh, scratch_types=[])
    def gather_kernel(t_hbm, i_hbm, o_hbm):
        def body(i_vmem, o_vmem):
            pltpu.sync_copy(t_hbm.at[i_vmem.at[0]], o_vmem)

        pltpu.emit_pipeline(
            body,
            grid=(n // SC_WINDOW,),
            in_specs=[pl.BlockSpec((1, SC_WINDOW), index_map=lambda i: (0, i))],
            out_specs=[pl.BlockSpec((SC_WINDOW, PACK_W), index_map=lambda i: (i, 0))],
            core_axis_name=("core", "subcore"),
            dimension_semantics=(pltpu.PARALLEL,),
        )(i_hbm, o_hbm)

    return gather_kernel(table, idx.reshape(1, n))


FFN_TF = 512
FFN_TR = 768


def _unpack_rows(pa, pb):
    def lo(w):
        return lax.bitcast_convert_type(lax.shift_left(w, 16), F32).astype(BF16)

    def hi(w):
        return lax.bitcast_convert_type(w & jnp.int32(-65536), F32).astype(BF16)

    return jnp.concatenate([lo(pa), lo(pb), hi(pa), hi(pb)], axis=1)


def _expert_ffn_kernel(xs_ref, wg_ref, wu_ref, wd_ref, gate_ref, m_ref, o_ref, xs_sc):
    f = pl.program_id(1)

    @pl.when(f == 0)
    def _():
        for r0 in range(0, CAP_T, FFN_TR):
            xs_sc[r0:r0 + FFN_TR, :] = _unpack_rows(xs_ref[0, 0, r0:r0 + FFN_TR, :], xs_ref[1, 0, r0:r0 + FFN_TR, :])

    wg = wg_ref[0].astype(BF16)
    wu = wu_ref[0].astype(BF16)
    wd = wd_ref[0].astype(BF16)
    for r0 in range(0, CAP_T, FFN_TR):
        rs = slice(r0, r0 + FFN_TR)
        xs = xs_sc[rs, :]
        a = jnp.dot(xs, wg, preferred_element_type=F32)
        u = jnp.dot(xs, wu, preferred_element_type=F32)
        hid = (a * (1.0 / (1.0 + jnp.exp(-a))) * u).astype(BF16)
        y = jnp.dot(hid, wd, preferred_element_type=F32)

        @pl.when(f == 0)
        def _():
            o_ref[0, rs, :] = y

        @pl.when(jnp.logical_and(f > 0, f < pl.num_programs(1) - 1))
        def _():
            o_ref[0, rs, :] += y

        @pl.when(f == pl.num_programs(1) - 1)
        def _():
            gg = gate_ref[0, rs, :]
            scale = gg[:, 0:1] * m_ref[0, G2:G2 + 1, :]
            for g in range(1, N_GROUPS):
                scale = scale + gg[:, g:g + 1] * m_ref[g, G2:G2 + 1, :]
            o_ref[0, rs, :] = (o_ref[0, rs, :] + y) * scale


def _expert_ffn(xs, wg, wu, wd, gate, mods_l, l):
    return pl.pallas_call(
        _expert_ffn_kernel,
        grid=(N_EXPERTS, EXPERT_FF // FFN_TF),
        in_specs=[
            pl.BlockSpec((2, 1, CAP_T, PACK_W), lambda e, f: (0, e, 0, 0)),
            pl.BlockSpec((None, 1, D_MODEL, FFN_TF), lambda e, f: (l, e, 0, f)),
            pl.BlockSpec((None, 1, D_MODEL, FFN_TF), lambda e, f: (l, e, 0, f)),
            pl.BlockSpec((None, 1, FFN_TF, D_MODEL), lambda e, f: (l, e, f, 0)),
            pl.BlockSpec((1, CAP_T, N_GROUPS), lambda e, f: (e, 0, 0)),
            pl.BlockSpec((N_GROUPS, 6, D_MODEL), lambda e, f: (0, 0, 0)),
        ],
        out_specs=pl.BlockSpec((1, CAP_T, D_MODEL), lambda e, f: (e, 0, 0)),
        out_shape=jax.ShapeDtypeStruct((N_EXPERTS, CAP_T, D_MODEL), F32),
        scratch_shapes=[pltpu.VMEM((CAP_T, D_MODEL), BF16)],
        compiler_params=_params(("parallel", "arbitrary")),
        name="expert_ffn",
    )(xs, wg, wu, wd, gate, mods_l)


def _final_norm_kernel(x_ref, g_ref, o_ref):
    x = x_ref[...]
    ms = jnp.mean(x * x, axis=-1, keepdims=True)
    o_ref[...] = x * lax.rsqrt(ms + EPS) * g_ref[...]


def _final_norm(x, g, row0, n_rows, tm=512):
    off = row0 // tm
    return pl.pallas_call(
        _final_norm_kernel,
        grid=(n_rows // tm,),
        in_specs=[pl.BlockSpec((tm, D_MODEL), lambda i: (i + off, 0)),
                  pl.BlockSpec((1, D_MODEL), lambda i: (0, 0))],
        out_specs=pl.BlockSpec((tm, D_MODEL), lambda i: (i, 0)),
        out_shape=jax.ShapeDtypeStruct((n_rows, D_MODEL), F32),
        compiler_params=_params(("parallel",)),
        name="final_norm",
    )(x, g.reshape(1, D_MODEL))


def _even_layer(x, mods_l, i, state_ret, st_prev, dfts, norm1_g, w_in_even, hy_short_w, hy_short_b, hy_f1_w, hy_f1_b,
                hy_freq1, hy_f2_w, hy_f2_b, hy_freq2, hy_f3_w, hy_bias, ret_decay, w_out_even):
    proj = _norm_project(x, mods_l, norm1_g, _weight_bf16(w_in_even, i))
    sw, sbias = hy_short_w[i], hy_short_b[i].reshape(1, 3 * HY_W)
    hb = hy_bias[i]
    y_hy = None
    for (row0, n_rows, L, nseq) in ((0, N_PROMPT, SEQ, 8), (N_PROMPT, N_SAMPLE, DEC_SEQ, 1)):
        s, d = _hyena_filter_taps(L, hy_f1_w[i], hy_f1_b[i], hy_freq1[i], hy_f2_w[i], hy_f2_b[i], hy_freq2[i], hy_f3_w[i])
        spectra = _filter_spectra(s, d, L, dfts[L])
        wv, bv = sw[:, 0:HY_W], sbias[:, 0:HY_W]
        w1, b1 = sw[:, HY_W:2 * HY_W], sbias[:, HY_W:2 * HY_W]
        w2, b2 = sw[:, 2 * HY_W:], sbias[:, 2 * HY_W:]
        z1 = _hyena_conv(proj, 0, proj, 1, row0, n_rows, L, nseq, spectra, 0, wv, bv, w1, b1, hb[0:1], True, dfts[L])
        y_hy = _hyena_conv(z1, 0, proj, 2, row0, n_rows, L, nseq, spectra, 1, wv, bv, w2, b2, hb[1:2], False, dfts[L],
                           full_out=True, out_prev=y_hy)
    dl = jnp.broadcast_to(ret_decay[i].astype(F32)[:, :, None], (2, RET_HEADS, LANES))
    y_ret, st = _retention(proj, dl, i, 0, BATCH, SEQ, st_prev=st_prev)
    y_ret = _retention(proj, dl, i, N_PROMPT, DEC_BATCH, DEC_SEQ, s0=state_ret, y_prev=y_ret)
    wo = _weight_bf16(w_out_even, i)
    x = _project_residual([y_hy, y_ret], [wo[:HY_W], wo[HY_W:]], x, mods_l, G1)
    return x, st


def _odd_layer(x, mods_l, l, i, cache_k, cache_v, kv_prev, norm1_g, w_in_odd, lam_q1, lam_k1, lam_q2, lam_k2, subln_g,
               w_out_odd):
    lam_init = 0.8 - 0.6 * math.exp(-0.3 * l)
    proj = _norm_project(x, mods_l, norm1_g, _weight_bf16(w_in_odd, i))
    lam_vec = jnp.pad(jnp.stack([lam_q1[i], lam_k1[i], lam_q2[i], lam_k2[i]]), ((0, 0), (0, LANES - DIFF_HEAD_DIM)))
    sg = subln_g[i].reshape(1, LANES)
    o_p, kc, vc = _diff_attention(proj, proj, proj, lam_vec, sg, lam_init, nb=BATCH, lq=SEQ, lk=SEQ, tq=SEQ,
                                  q_row0=0, q_col=0, kv_cols=(1, 2), batched_kv=False,
                                  cache_layer=i, cache_prev=kv_prev)
    qs, ks, vs = _sample_qkv(proj, cache_k, cache_v, i)
    o = _diff_attention_per_head(qs, ks, vs, lam_vec, sg, lam_init, nb=DEC_BATCH, lq=DEC_SEQ, lk=PAST_LEN + DEC_SEQ,
                                 tq=512, out_row0=N_PROMPT, out_prev=o_p)
    x = _project_residual([o], [_weight_bf16(w_out_odd, i)], x, mods_l, G1)
    return x, (kc, vc)


def _moe_layer(x, mods_l, l, norm2_g, w_router, wg, wu, wd):
    h, aff = _router(x, mods_l, norm2_g, w_router)
    gate_p, idx_p = lax.top_k(aff[:, :N_PROMPT], CAP_P)
    gate_s, idx_s = lax.top_k(aff[:, N_PROMPT:], CAP_S)
    idx = jnp.concatenate([idx_p, idx_s + N_PROMPT], axis=1)
    gate = jnp.concatenate([gate_p, gate_s], axis=1)
    flat = idx.reshape(-1)
    xs = _gather_rows(h.reshape(2 * N_TOK, PACK_W), jnp.concatenate([flat, flat + N_TOK]))
    xs = xs.reshape(2, N_EXPERTS, CAP_T, PACK_W)
    grp = jnp.where(idx < N_PROMPT, 0, 1 + (idx - N_PROMPT) // DEC_SEQ)
    gate_grp = jnp.where(grp[:, :, None] == jnp.arange(N_GROUPS)[None, None, :], gate[:, :, None], 0.0)
    out = _expert_ffn(xs, wg, wu, wd, gate_grp, mods_l, l)
    return x.at[idx.reshape(-1)].add(out.reshape(-1, D_MODEL))


def kernel(x_prompt, x_sample, state_ret, cache_k, cache_v, c, c_ctx, w_mod, b_mod, norm1_g, norm2_g, w_in_even, hy_short_w, hy_short_b, hy_f1_w, hy_f1_b, hy_freq1, hy_f2_w, hy_f2_b, hy_freq2, hy_f3_w, hy_bias, ret_decay, w_out_even, w_in_odd, lam_q1, lam_k1, lam_q2, lam_k2, subln_g, w_out_odd, moe_router, moe_wg, moe_wu, moe_wd, final_g):
    x = (x_prompt.reshape(N_PROMPT, D_MODEL), x_sample.reshape(N_SAMPLE, D_MODEL))
    mods = _modulation(c, c_ctx, w_mod, b_mod)
    dfts = {L: _dft_bf16(L) for L in (SEQ, DEC_SEQ)}
    state_ret = state_ret.astype(F32)
    st = kv = None
    for l in range(DEPTH):
        i = l // 2
        if l % 2 == 0:
            x, st = _even_layer(x, mods[l], i, state_ret, st, dfts, norm1_g[l], w_in_even, hy_short_w, hy_short_b,
                                hy_f1_w, hy_f1_b, hy_freq1, hy_f2_w, hy_f2_b, hy_freq2, hy_f3_w, hy_bias, ret_decay,
                                w_out_even)
        else:
            x, kv = _odd_layer(x, mods[l], l, i, cache_k, cache_v, kv, norm1_g[l], w_in_odd, lam_q1, lam_k1, lam_q2,
                               lam_k2, subln_g, w_out_odd)
        x = _moe_layer(x, mods[l], l, norm2_g[l], moe_router[l], moe_wg, moe_wu, moe_wd)
    y_prompt = _final_norm(x, final_g, 0, N_PROMPT).reshape(BATCH, SEQ, D_MODEL)
    y_sample = _final_norm(x, final_g, N_PROMPT, N_SAMPLE).reshape(DEC_BATCH, DEC_SEQ, D_MODEL)
    cache_shape = (BATCH, DEPTH // 2, SEQ, DIFF_HEADS, 2 * DIFF_HEAD_DIM)
    return (y_prompt, y_sample, st, kv[0].reshape(cache_shape), kv[1].reshape(cache_shape))
```
